```python
import jax, jax.numpy as jnp
from jax import lax
import numpy as np

D_MODEL = 2048
BATCH = 4
SEQ = 2048
DEPTH = 2
DEC_BATCH = 8
DEC_SEQ = 8
PAST_LEN = 16384
PAGE_SIZE = 128

CONV_WIDTH = D_MODEL // 4
CONV_K = 3
ATT_WIDTH = D_MODEL // 4
ATT_HEADS_PER_GROUP = 4
ATT_HEAD_DIM = ATT_WIDTH // ATT_HEADS_PER_GROUP
DIL_GROUPS = ((128, 1), (512, 4), (2048, 16))
N_DIL = len(DIL_GROUPS)
N_ATT_HEADS = N_DIL * ATT_HEADS_PER_GROUP
POOL_WIDTH = D_MODEL - CONV_WIDTH - ATT_WIDTH
POOL_WINDOWS = (2, 4, 8, 16)
POOL_GROUP = POOL_WIDTH // len(POOL_WINDOWS)
POOL_BUF = max(POOL_WINDOWS) - 1
A_COLS = 3 * CONV_WIDTH
B_COLS = N_DIL * 3 * ATT_WIDTH
IN_COLS = A_COLS + B_COLS + POOL_WIDTH
D_FF = ((8 * D_MODEL // 3 + 127) // 128) * 128
FFN_CONV_K = 3
RMS_EPS = 1e-6

kernel_name = "hymba_conv_dilattn_pool_decoder_step"


def rms_norm(x, g):
    xf = x.astype(jnp.float32)
    y = xf * lax.rsqrt(jnp.mean(xf * xf, axis=-1, keepdims=True) + RMS_EPS)
    return (y * g.astype(jnp.float32)).astype(x.dtype)


def alibi_slopes():
    h = np.arange(1, N_ATT_HEADS + 1, dtype=np.float32)
    return jnp.asarray(np.power(np.float32(2.0), -8.0 * h / N_ATT_HEADS), dtype=jnp.float32)


def causal_dwconv(u, buf, w):
    k = w.shape[0]
    t = u.shape[1]
    ext = jnp.concatenate([buf.astype(u.dtype), u], axis=1)
    y = ext[:, 0:t] * w[0]
    for j in range(1, k):
        y = y + ext[:, j:j + t] * w[j]
    return y, ext[:, ext.shape[1] - (k - 1):]


def dilated_attention(q, k_ext, v_ext, start, window, dilation, slopes):
    t = q.shape[1]
    offs = jnp.arange(window // dilation + 1, dtype=jnp.int32)
    qf = q.astype(jnp.float32) * (ATT_HEAD_DIM ** -0.5)
    q_pos = start + jnp.arange(t, dtype=jnp.int32)

    def rows(ext, j):
        return lax.dynamic_slice_in_dim(ext, window - j * dilation, t, axis=1).astype(jnp.float32)

    def score(j):
        dist = j * dilation
        s = jnp.einsum("bthd,bthd->bth", qf, rows(k_ext, j)) - slopes * dist.astype(jnp.float32)
        ok = (q_pos - dist) >= 0
        return jnp.where(ok[None, :, None], s, -jnp.inf)

    s = lax.map(score, offs)
    m = jnp.max(s, axis=0)
    p = jnp.exp(s - m)
    l = jnp.sum(p, axis=0)

    def acc_step(acc, inp):
        j, pj = inp
        return acc + pj[..., None] * rows(v_ext, j), None

    acc, _ = lax.scan(acc_step, jnp.zeros(qf.shape, jnp.float32), (offs, p))
    return acc / l[..., None], m + jnp.log(l)


def multiscale_pool(u, buf, start, w_pool, scale):
    b, t, _ = u.shape
    ext = jnp.concatenate([buf.astype(u.dtype), u], axis=1).astype(jnp.float32)
    cs = jnp.pad(jnp.cumsum(ext, axis=1), ((0, 0), (1, 0), (0, 0)))
    pos = start + jnp.arange(t, dtype=jnp.int32)
    means = []
    for gi, w in enumerate(POOL_WINDOWS):
        sl = slice(gi * POOL_GROUP, (gi + 1) * POOL_GROUP)
        hi = cs[:, POOL_BUF + 1:POOL_BUF + 1 + t, sl]
        lo = cs[:, POOL_BUF + 1 - w:POOL_BUF + 1 - w + t, sl]
        cnt = jnp.minimum(w, pos + 1).astype(jnp.float32)
        means.append((hi - lo) / cnt[None, :, None])
    d = (jnp.concatenate(means, axis=-1) - u.astype(jnp.float32)).reshape(b, t, len(POOL_WINDOWS), POOL_GROUP)
    z = jnp.einsum("btgc,gcd->btgd", d, w_pool.astype(jnp.float32)).reshape(b, t, POOL_WIDTH)
    z = z * scale.astype(jnp.float32)
    return z.astype(u.dtype), ext[:, ext.shape[1] - POOL_BUF:].astype(u.dtype)


def decoder_layer(x, bufs, params, slopes, start):
    conv_buf, kv_bufs, pool_buf, ffn_buf = bufs
    (norm1, w_in, conv_a_w, w_out, pool_w, pool_scale, norm2, w_gate, w_up, ffn_conv_w, w_down) = params
    b, t, _ = x.shape
    h = rms_norm(x, norm1)
    proj = h @ w_in

    xa = proj[..., 0:CONV_WIDTH]
    gate_b = proj[..., CONV_WIDTH:2 * CONV_WIDTH]
    gate_c = proj[..., 2 * CONV_WIDTH:3 * CONV_WIDTH]
    cu, new_conv = causal_dwconv(gate_c * xa, conv_buf, conv_a_w)
    ya = gate_b * cu

    att = proj[..., A_COLS:A_COLS + B_COLS].reshape(b, t, N_DIL, 3, ATT_HEADS_PER_GROUP, ATT_HEAD_DIM)
    outs, lses, new_kv = [], [], []
    for g, (win, dil) in enumerate(DIL_GROUPS):
        q = att[:, :, g, 0]
        kv = att[:, :, g, 1:3]
        buf = kv_bufs[g].astype(kv.dtype)
        pad = jnp.zeros((b, win - buf.shape[1], 2, ATT_HEADS_PER_GROUP, ATT_HEAD_DIM), kv.dtype)
        ext = jnp.concatenate([pad, buf, kv], axis=1)
        o, lse = dilated_attention(q, ext[:, :, 0], ext[:, :, 1], start, win, dil,
                                   slopes[g * ATT_HEADS_PER_GROUP:(g + 1) * ATT_HEADS_PER_GROUP])
        outs.append(o)
        lses.append(lse)
        keep = min(win, start + t)
        new_kv.append(ext[:, win + t - keep:])
    alpha = jax.nn.softmax(jnp.stack(lses), axis=0)
    yb = jnp.sum(alpha[..., None] * jnp.stack(outs), axis=0).reshape(b, t, ATT_WIDTH).astype(x.dtype)

    yc, new_pool = multiscale_pool(proj[..., A_COLS + B_COLS:], pool_buf, start, pool_w, pool_scale)

    x = x + jnp.concatenate([ya, yb, yc], axis=-1) @ w_out

    h2 = rms_norm(x, norm2)
    gc, new_ffn = causal_dwconv(h2 @ w_gate, ffn_buf, ffn_conv_w)
    x = x + (jax.nn.silu(gc) * (h2 @ w_up)) @ w_down
    return x, (new_kv[0], new_kv[1], new_kv[2], new_conv, new_pool, new_ffn)


def setup_inputs(seed: int = 0) -> dict:
    key = jax.random.key(seed)
    ks = jax.random.split(key, 24)
    f32 = jnp.float32

    def nrm(k, shape, s):
        return jax.random.normal(k, shape, f32) * s

    hh, dh = ATT_HEADS_PER_GROUP, ATT_HEAD_DIM
    return {
        "x_prompt": nrm(ks[0], (BATCH, SEQ, D_MODEL), 1.0),
        "x_sample": nrm(ks[1], (DEC_BATCH, DEC_SEQ, D_MODEL), 1.0),
        "cache_kv_w128": nrm(ks[2], (DEPTH, DEC_BATCH, min(DIL_GROUPS[0][0], PAST_LEN), 2, hh, dh), 1.0),
        "cache_kv_w512": nrm(ks[3], (DEPTH, DEC_BATCH, min(DIL_GROUPS[1][0], PAST_LEN), 2, hh, dh), 1.0),
        "cache_kv_w2048": nrm(ks[4], (DEPTH, DEC_BATCH, min(DIL_GROUPS[2][0], PAST_LEN), 2, hh, dh), 1.0),
        "state_conv_a": nrm(ks[5], (DEPTH, DEC_BATCH, CONV_K - 1, CONV_WIDTH), 1.0),
        "state_pool": nrm(ks[6], (DEPTH, DEC_BATCH, POOL_BUF, POOL_WIDTH), 1.0),
        "state_ffn_conv": nrm(ks[7], (DEPTH, DEC_BATCH, FFN_CONV_K - 1, D_FF), 1.0),
        "norm1": 1.0 + nrm(ks[8], (DEPTH, D_MODEL), 0.05),
        "w_in": nrm(ks[9], (DEPTH, D_MODEL, IN_COLS), D_MODEL ** -0.5),
        "conv_a_w": nrm(ks[10], (DEPTH, CONV_K, CONV_WIDTH), CONV_K ** -0.5),
        "w_out": nrm(ks[11], (DEPTH, D_MODEL, D_MODEL), D_MODEL ** -0.5),
        "pool_w": nrm(ks[12], (DEPTH, len(POOL_WINDOWS), POOL_GROUP, POOL_GROUP), POOL_GROUP ** -0.5),
        "pool_scale": 1.0 + nrm(ks[13], (DEPTH, POOL_WIDTH), 0.1),
        "norm2": 1.0 + nrm(ks[14], (DEPTH, D_MODEL), 0.05),
        "w_gate": nrm(ks[15], (DEPTH, D_MODEL, D_FF), D_MODEL ** -0.5),
        "w_up": nrm(ks[16], (DEPTH, D_MODEL, D_FF), D_MODEL ** -0.5),
        "ffn_conv_w": nrm(ks[17], (DEPTH, FFN_CONV_K, D_FF), FFN_CONV_K ** -0.5),
        "w_down": nrm(ks[18], (DEPTH, D_FF, D_MODEL), D_FF ** -0.5),
        "final_norm": 1.0 + nrm(ks[19], (D_MODEL,), 0.05),
    }


def reference(x_prompt, x_sample, cache_kv_w128, cache_kv_w512, cache_kv_w2048, state_conv_a, state_pool,
              state_ffn_conv, norm1, w_in, conv_a_w, w_out, pool_w, pool_scale, norm2, w_gate, w_up,
              ffn_conv_w, w_down, final_norm):
    slopes = alibi_slopes()
    kv_caches = (cache_kv_w128, cache_kv_w512, cache_kv_w2048)
    hh, dh = ATT_HEADS_PER_GROUP, ATT_HEAD_DIM
    xp, xs = x_prompt, x_sample
    bp, dt = xp.shape[0], xp.dtype
    new_p, new_s = [], []
    for l in range(DEPTH):
        params = (norm1[l], w_in[l], conv_a_w[l], w_out[l], pool_w[l], pool_scale[l], norm2[l],
                  w_gate[l], w_up[l], ffn_conv_w[l], w_down[l])
        p_bufs = (jnp.zeros((bp, CONV_K - 1, CONV_WIDTH), dt),
                  tuple(jnp.zeros((bp, 0, 2, hh, dh), dt) for _ in DIL_GROUPS),
                  jnp.zeros((bp, POOL_BUF, POOL_WIDTH), dt),
                  jnp.zeros((bp, FFN_CONV_K - 1, D_FF), dt))
        xp, sp = decoder_layer(xp, p_bufs, params, slopes, 0)
        s_bufs = (state_conv_a[l], tuple(c[l] for c in kv_caches), state_pool[l], state_ffn_conv[l])
        xs, ss = decoder_layer(xs, s_bufs, params, slopes, PAST_LEN)
        new_p.append(sp)
        new_s.append(ss)

    def stk(lst, i):
        return jnp.stack([s[i] for s in lst])

    y_prompt = rms_norm(xp, final_norm)
    y_sample = rms_norm(xs, final_norm)
    return (y_prompt, y_sample,
            stk(new_p, 0), stk(new_s, 0), stk(new_p, 1), stk(new_s, 1), stk(new_p, 2), stk(new_s, 2),
            stk(new_p, 3), stk(new_s, 3), stk(new_p, 4), stk(new_s, 4), stk(new_p, 5), stk(new_s, 5))
```

```python
import functools

import numpy as np
import jax
import jax.numpy as jnp
from jax import lax
from jax.experimental import pallas as pl
from jax.experimental.pallas import tpu as pltpu

D_MODEL = 2048
DEPTH = 2
PAST_LEN = 16384
CONV_WIDTH = 512
ATT_WIDTH = 512
HEADS = 4
HEAD_DIM = 128
DIL_GROUPS = ((128, 1), (512, 4), (2048, 16))
POOL_WIDTH = 1024
POOL_WINDOWS = (2, 4, 8, 16)
POOL_GROUP = 256
POOL_BUF = 15
IN_COLS = 7168
D_FF = 5504
RMS_EPS = 1e-6
ATT_SCALE = HEAD_DIM ** -0.5

COL_TILE = 512
IN_TILE_ORDER = (4, 5, 7, 8, 10, 11, 3, 6, 9, 0, 1, 2, 12, 13)
Q_TILE0 = 6
A_TILE0 = 9
U_TILE0 = 12
FF_TILE = 512
D_FF_PAD = 5632
CONV_HALO = 8
POOL_HALO = 16
VMEM_LIMIT = 56 * 1024 * 1024

_BF16 = jnp.bfloat16
_F32 = jnp.float32


def _alibi_slopes():
    h = np.arange(1, 3 * HEADS + 1, dtype=np.float32)
    return [float(v) for v in np.power(np.float32(2.0), -8.0 * h / (3 * HEADS))]


SLOPES = _alibi_slopes()


def _rms_norm(x, g):
    return (x * lax.rsqrt(jnp.mean(x * x, axis=-1, keepdims=True) + RMS_EPS)) * g


def _inproj_kernel(x_ref, g_ref, w_ref, proj_ref, kv0_ref, kv1_ref, kv2_ref, h_s):
    j = pl.program_id(1)

    @pl.when(j == 0)
    def _():
        h_s[...] = _rms_norm(x_ref[...], g_ref[...]).astype(_BF16)

    acc = jnp.dot(h_s[...], w_ref[...], preferred_element_type=_F32)
    proj_ref[...] = acc.astype(_BF16)
    for g, ref in enumerate((kv0_ref, kv1_ref, kv2_ref)):
        @pl.when((j == 2 * g) | (j == 2 * g + 1))
        def _(ref=ref):
            ref[...] = acc


def _inproj(x2d, g, w, tm):
    m = x2d.shape[0]

    def kv_spec(grp):
        return pl.BlockSpec((tm, COL_TILE), lambda i, j: (i, jnp.clip(j - 2 * grp, 0, 1)))

    return pl.pallas_call(
        _inproj_kernel,
        grid=(m // tm, IN_COLS // COL_TILE),
        in_specs=[pl.BlockSpec((tm, D_MODEL), lambda i, j: (i, 0)),
                  pl.BlockSpec((1, D_MODEL), lambda i, j: (0, 0)),
                  pl.BlockSpec((D_MODEL, COL_TILE), lambda i, j: (0, j))],
        out_specs=[pl.BlockSpec((tm, COL_TILE), lambda i, j: (i, j)), kv_spec(0), kv_spec(1), kv_spec(2)],
        out_shape=[jax.ShapeDtypeStruct((m, IN_COLS), _BF16)] +
                  [jax.ShapeDtypeStruct((m, 2 * ATT_WIDTH), _F32)] * 3,
        scratch_shapes=[pltpu.VMEM((tm, D_MODEL), _BF16)],
        compiler_params=pltpu.CompilerParams(dimension_semantics=("parallel", "arbitrary"),
                                             vmem_limit_bytes=VMEM_LIMIT),
        name="inproj",
    )(x2d, g.reshape(1, D_MODEL), w)


QB = 128


def _attn_tile(q, k, v, distm, slope_step):
    s = lax.dot_general(q, k, (((1,), (1,)), ((), ())), preferred_element_type=_F32)
    s = s * ATT_SCALE - distm * slope_step
    m = jnp.max(s, axis=1, keepdims=True)
    p = jnp.exp(s - m)
    l = jnp.sum(p, axis=1, keepdims=True)
    acc = jnp.dot(p.astype(_BF16), v, preferred_element_type=_F32)
    return acc, m, l


def _attn_prompt_kernel(q0, k0, v0, q1, k1, v1, q2, k2, v2, o_ref, acc_s, m_s, l_s, *, seq):
    step = pl.program_id(1)
    n_steps = 1 + DIL_GROUPS[1][1] + DIL_GROUPS[2][1]

    row0 = lax.broadcasted_iota(jnp.int32, (QB, QB), 0)
    col0 = lax.broadcasted_iota(jnp.int32, (QB, QB), 1)
    dist_first = jnp.where(col0 <= row0, (row0 - col0).astype(_F32), jnp.inf)
    row1 = lax.broadcasted_iota(jnp.int32, (QB, 2 * QB), 0)
    col1 = lax.broadcasted_iota(jnp.int32, (QB, 2 * QB), 1)
    d1 = QB + row1 - col1
    dist_next = jnp.where((d1 >= 0) & (d1 <= QB), d1.astype(_F32), jnp.inf)

    def unit(qr, kr, vr, grp, res, first):
        dil = DIL_GROUPS[grp][1]
        n_blk = seq // dil // QB

        def block(i, is_first_block):
            for h in range(HEADS):
                hs = slice(h * HEAD_DIM, (h + 1) * HEAD_DIM)
                if is_first_block:
                    q, k, v, dm = qr[0:QB, hs], kr[0:QB, hs], vr[0:QB, hs], dist_first
                else:
                    qs = pl.ds(pl.multiple_of(i * QB, QB), QB)
                    ks = pl.ds(pl.multiple_of(i * QB - QB, QB), 2 * QB)
                    q, k, v, dm = qr[qs, hs], kr[ks, hs], vr[ks, hs], dist_next
                acc, m, l = _attn_tile(q, k, v, dm, SLOPES[grp * HEADS + h] * dil)
                if first:
                    rows = slice(0, QB) if is_first_block else pl.ds(pl.multiple_of(i * QB, QB), QB)
                    acc_s[h, rows, :] = acc
                    m_s[h, rows, :] = jnp.broadcast_to(m, (QB, HEAD_DIM))
                    l_s[h, rows, :] = jnp.broadcast_to(l, (QB, HEAD_DIM))
                else:
                    rows = pl.ds(res + dil * QB * i, QB, stride=dil)
                    m_old, l_old, a_old = m_s[h, rows, :], l_s[h, rows, :], acc_s[h, rows, :]
                    m_new = jnp.maximum(m_old, m)
                    w_old = jnp.exp(m_old - m_new)
                    w_new = jnp.exp(m - m_new)
                    acc_s[h, rows, :] = a_old * w_old + acc * w_new
                    l_s[h, rows, :] = l_old * w_old + l * w_new
                    m_s[h, rows, :] = m_new

        block(0, True)
        if n_blk > 1:
            def body(i, carry):
                block(i, False)
                return carry
            lax.fori_loop(1, n_blk, body, 0)

    n1 = DIL_GROUPS[1][1]

    @pl.when(step == 0)
    def _():
        unit(q0, k0, v0, 0, 0, True)

    @pl.when((step >= 1) & (step <= n1))
    def _():
        unit(q1, k1, v1, 1, step - 1, False)

    @pl.when(step > n1)
    def _():
        unit(q2, k2, v2, 2, step - 1 - n1, False)

    @pl.when(step == n_steps - 1)
    def _():
        for h in range(HEADS):
            o_ref[:, h * HEAD_DIM:(h + 1) * HEAD_DIM] = (acc_s[h] / l_s[h]).astype(_BF16)


def _attn_prompt(proj, batch, seq):
    tiles = IN_COLS // COL_TILE
    n1, n2 = DIL_GROUPS[1][1], DIL_GROUPS[2][1]
    in_specs, args = [], []
    for grp, (_, dil) in enumerate(DIL_GROUPS):
        view = proj.reshape(batch, seq // dil, dil * IN_COLS)
        first_step = (0, 1, 1 + n1)[grp]
        for tile in (Q_TILE0 + grp, 2 * grp, 2 * grp + 1):
            def imap(b, s, tile=tile, dil=dil, first_step=first_step):
                return (b, 0, jnp.clip(s - first_step, 0, dil - 1) * tiles + tile)
            in_specs.append(pl.BlockSpec((None, seq // dil, COL_TILE), imap))
            args.append(view)
    return pl.pallas_call(
        functools.partial(_attn_prompt_kernel, seq=seq),
        grid=(batch, 1 + n1 + n2),
        in_specs=in_specs,
        out_specs=pl.BlockSpec((None, seq, ATT_WIDTH), lambda b, s: (b, 0, 0)),
        out_shape=jax.ShapeDtypeStruct((batch, seq, ATT_WIDTH), _BF16),
        scratch_shapes=[pltpu.VMEM((HEADS, seq, HEAD_DIM), _F32)] * 3,
        compiler_params=pltpu.CompilerParams(dimension_semantics=("parallel", "arbitrary"),
                                             vmem_limit_bytes=VMEM_LIMIT),
        name="attn_prompt",
    )(*args)


def _attn_sample_kernel(proj_ref, n0, n1, n2, c0, c1, c2, yb_ref, o0, o1, o2, *, tdec):
    news, caches, outs = (n0, n1, n2), (c0, c1, c2), (o0, o1, o2)
    for h in range(HEADS):
        pieces = []
        for grp, (win, dil) in enumerate(DIL_GROUPS):
            slope = SLOPES[grp * HEADS + h]
            q = proj_ref[:, (Q_TILE0 + grp) * COL_TILE + h * HEAD_DIM:(Q_TILE0 + grp) * COL_TILE + (h + 1) * HEAD_DIM]
            ks, vs = slice(h * HEAD_DIM, (h + 1) * HEAD_DIM), slice(ATT_WIDTH + h * HEAD_DIM, ATT_WIDTH + (h + 1) * HEAD_DIM)
            for ref, n_keys, base in ((caches[grp], win, win), (news[grp], tdec, 0)):
                k = ref[:, ks].astype(_BF16)
                v = ref[:, vs].astype(_BF16)
                s = lax.dot_general(q, k, (((1,), (1,)), ((), ())), preferred_element_type=_F32)
                t = lax.broadcasted_iota(jnp.int32, (tdec, n_keys), 0)
                c = lax.broadcasted_iota(jnp.int32, (tdec, n_keys), 1)
                dist = base + t - c
                ok = (dist >= 0) & (dist <= win) & ((dist & (dil - 1)) == 0)
                s = jnp.where(ok, s * ATT_SCALE - slope * dist.astype(_F32), -jnp.inf)
                pieces.append((s, v))
        m = functools.reduce(jnp.maximum, [jnp.max(s, axis=1, keepdims=True) for s, _ in pieces])
        l = jnp.zeros((tdec, 1), _F32)
        acc = jnp.zeros((tdec, HEAD_DIM), _F32)
        for s, v in pieces:
            p = jnp.exp(s - m)
            l = l + jnp.sum(p, axis=1, keepdims=True)
            acc = acc + jnp.dot(p.astype(_BF16), v, preferred_element_type=_F32)
        yb_ref[:, h * HEAD_DIM:(h + 1) * HEAD_DIM] = (acc / l).astype(_BF16)
    for grp, (win, _) in enumerate(DIL_GROUPS):
        outs[grp][0:win - tdec, :] = caches[grp][tdec:win, :]
        outs[grp][win - tdec:win, :] = news[grp][...]


def _attn_sample(proj, kv_new, caches, batch, tdec):
    in_specs = [pl.BlockSpec((None, tdec, IN_COLS), lambda b: (b, 0, 0))]
    in_specs += [pl.BlockSpec((None, tdec, 2 * ATT_WIDTH), lambda b: (b, 0, 0))] * 3
    in_specs += [pl.BlockSpec((None, win, 2 * ATT_WIDTH), lambda b: (b, 0, 0)) for win, _ in DIL_GROUPS]
    out_specs = [pl.BlockSpec((None, tdec, ATT_WIDTH), lambda b: (b, 0, 0))]
    out_specs += [pl.BlockSpec((None, win, 2 * ATT_WIDTH), lambda b: (b, 0, 0)) for win, _ in DIL_GROUPS]
    out_shape = [jax.ShapeDtypeStruct((batch, tdec, ATT_WIDTH), _BF16)]
    out_shape += [jax.ShapeDtypeStruct((batch, win, 2 * ATT_WIDTH), _F32) for win, _ in DIL_GROUPS]
    return pl.pallas_call(
        functools.partial(_attn_sample_kernel, tdec=tdec),
        grid=(batch,),
        in_specs=in_specs, out_specs=out_specs, out_shape=out_shape,
        compiler_params=pltpu.CompilerParams(dimension_semantics=("parallel",), vmem_limit_bytes=VMEM_LIMIT),
        name="attn_sample",
    )(proj.reshape(batch, tdec, IN_COLS), *[k.reshape(batch, tdec, 2 * ATT_WIDTH) for k in kv_new], *caches)


def _mixer_kernel(pa_ref, pu_ref, yb_ref, x_ref, cs0_ref, ps0_ref, cw_ref, pw_ref, psc_ref, wo_ref,
                  out_ref, cs_out, ps_out, ext_a, ext_u, ycat, *, nb, tt, start):
    ti = pl.program_id(1)
    rows = nb * tt

    @pl.when(ti == 0)
    def _():
        ext_a[:, 0:CONV_HALO, :] = cs0_ref[...]
        ext_u[:, 0:POOL_HALO, :] = ps0_ref[...]

    pa = pa_ref[...].astype(_F32)
    xa, gate_b, gate_c = (pa[:, :, k * CONV_WIDTH:(k + 1) * CONV_WIDTH] for k in range(3))
    prod = gate_c * xa
    ext_a[:, CONV_HALO:CONV_HALO + tt, :] = prod
    cw = cw_ref[...]
    cu = (cw[0] * ext_a[:, CONV_HALO - 2:CONV_HALO - 2 + tt, :]
          + cw[1] * ext_a[:, CONV_HALO - 1:CONV_HALO - 1 + tt, :] + cw[2] * prod)
    ycat[:, 0:CONV_WIDTH] = (gate_b * cu).reshape(rows, CONV_WIDTH).astype(_BF16)
    ycat[:, CONV_WIDTH:CONV_WIDTH + ATT_WIDTH] = (
        yb_ref[...].astype(_F32).reshape(rows, ATT_WIDTH).astype(_BF16))

    u = pu_ref[...].astype(_F32)
    ext_u[:, POOL_HALO:POOL_HALO + tt, :] = u
    pos = start + ti * tt + lax.broadcasted_iota(jnp.int32, (1, tt, 1), 1)
    for gi, w in enumerate(POOL_WINDOWS):
        cols = slice(gi * POOL_GROUP, (gi + 1) * POOL_GROUP)
        ug = u[:, :, cols]
        tot = ug
        for k in range(1, w):
            tot = tot + ext_u[:, POOL_HALO - k:POOL_HALO - k + tt, cols]
        cnt = jnp.minimum(w, pos + 1).astype(_F32)
        diff = (tot / cnt - ug).reshape(rows, POOL_GROUP).astype(_BF16)
        z = jnp.dot(diff, pw_ref[gi], preferred_element_type=_F32) * psc_ref[:, cols]
        c0 = CONV_WIDTH + ATT_WIDTH + gi * POOL_GROUP
        ycat[:, c0:c0 + POOL_GROUP] = z.astype(_BF16)

    mixed = jnp.dot(ycat[...], wo_ref[...], preferred_element_type=_F32)
    out_ref[...] = x_ref[...] + mixed.reshape(nb, tt, D_MODEL)

    tail_a = ext_a[:, tt:tt + CONV_HALO, :]
    tail_u = ext_u[:, tt:tt + POOL_HALO, :]
    ext_a[:, 0:CONV_HALO, :] = tail_a
    ext_u[:, 0:POOL_HALO, :] = tail_u
    cs_out[...] = tail_a
    ps_out[...] = tail_u


def _mixer(proj, yb, x, conv_state, pool_state, conv_w, pool_w, pool_scale, w_out, nb, tt, start):
    batch, seq, _ = x.shape
    proj3 = proj.reshape(batch, seq, IN_COLS)
    a_blk = A_TILE0 * COL_TILE // (3 * CONV_WIDTH)
    u_blk = U_TILE0 * COL_TILE // POOL_WIDTH
    const2 = lambda b, t: (0, 0)
    return pl.pallas_call(
        functools.partial(_mixer_kernel, nb=nb, tt=tt, start=start),
        grid=(batch // nb, seq // tt),
        in_specs=[pl.BlockSpec((nb, tt, 3 * CONV_WIDTH), lambda b, t: (b, t, a_blk)),
                  pl.BlockSpec((nb, tt, POOL_WIDTH), lambda b, t: (b, t, u_blk)),
                  pl.BlockSpec((nb, tt, ATT_WIDTH), lambda b, t: (b, t, 0)),
                  pl.BlockSpec((nb, tt, D_MODEL), lambda b, t: (b, t, 0)),
                  pl.BlockSpec((nb, CONV_HALO, CONV_WIDTH), lambda b, t: (b, 0, 0)),
                  pl.BlockSpec((nb, POOL_HALO, POOL_WIDTH), lambda b, t: (b, 0, 0)),
                  pl.BlockSpec((3, CONV_WIDTH), const2),
                  pl.BlockSpec((len(POOL_WINDOWS), POOL_GROUP, POOL_GROUP), lambda b, t: (0, 0, 0)),
                  pl.BlockSpec((1, POOL_WIDTH), const2),
                  pl.BlockSpec((D_MODEL, D_MODEL), const2)],
        out_specs=[pl.BlockSpec((nb, tt, D_MODEL), lambda b, t: (b, t, 0)),
                   pl.BlockSpec((nb, CONV_HALO, CONV_WIDTH), lambda b, t: (b, 0, 0)),
                   pl.BlockSpec((nb, POOL_HALO, POOL_WIDTH), lambda b, t: (b, 0, 0))],
        out_shape=[jax.ShapeDtypeStruct((batch, seq, D_MODEL), _F32),
                   jax.ShapeDtypeStruct((batch, CONV_HALO, CONV_WIDTH), _F32),
                   jax.ShapeDtypeStruct((batch, POOL_HALO, POOL_WIDTH), _F32)],
        scratch_shapes=[pltpu.VMEM((nb, tt + CONV_HALO, CONV_WIDTH), _F32),
                        pltpu.VMEM((nb, tt + POOL_HALO, POOL_WIDTH), _F32),
                        pltpu.VMEM((nb * tt, D_MODEL), _BF16)],
        compiler_params=pltpu.CompilerParams(dimension_semantics=("parallel", "arbitrary"),
                                             vmem_limit_bytes=VMEM_LIMIT),
        name="mixer",
    )(proj3, proj3, yb, x, conv_state, pool_state, conv_w, pool_w, pool_scale.reshape(1, POOL_WIDTH), w_out)


def _ffn_kernel(x_ref, n2_ref, wg_ref, wu_ref, cw_ref, wd_ref, st0_ref, fn_ref,
                out_ref, st_out, h_s, ext_s, carry_s, *, nb, tt, final):
    ti = pl.program_id(1)
    f = pl.program_id(2)
    n_f = pl.num_programs(2)
    rows = nb * tt

    @pl.when(f == 0)
    def _():
        x = x_ref[...]
        h_s[...] = _rms_norm(x, n2_ref[...]).reshape(rows, D_MODEL).astype(_BF16)
        out_ref[...] = x

    @pl.when(ti == 0)
    def _():
        ext_s[:, 0:CONV_HALO, :] = st0_ref[...]

    @pl.when(ti > 0)
    def _():
        ext_s[:, 0:CONV_HALO, :] = carry_s[f]

    h = h_s[...]
    gate = jnp.dot(h, wg_ref[...], preferred_element_type=_F32).reshape(nb, tt, FF_TILE)
    up = jnp.dot(h, wu_ref[...], preferred_element_type=_F32).reshape(nb, tt, FF_TILE)
    ext_s[:, CONV_HALO:CONV_HALO + tt, :] = gate
    cw = cw_ref[...]
    gc = (cw[0] * ext_s[:, CONV_HALO - 2:CONV_HALO - 2 + tt, :]
          + cw[1] * ext_s[:, CONV_HALO - 1:CONV_HALO - 1 + tt, :] + cw[2] * gate)
    tail = ext_s[:, tt:tt + CONV_HALO, :]
    carry_s[f] = tail
    st_out[...] = tail
    act = (jax.nn.silu(gc) * up).reshape(rows, FF_TILE).astype(_BF16)
    out_ref[...] += jnp.dot(act, wd_ref[...], preferred_element_type=_F32).reshape(nb, tt, D_MODEL)

    if final:
        @pl.when(f == n_f - 1)
        def _():
            out_ref[...] = _rms_norm(out_ref[...], fn_ref[...])


def _ffn(x, norm2, w_gate, w_up, conv_w, w_down, state, final_norm, nb, tt, final):
    batch, seq, _ = x.shape
    n_f = D_FF_PAD // FF_TILE
    vec = lambda b, t, f: (0, 0)
    return pl.pallas_call(
        functools.partial(_ffn_kernel, nb=nb, tt=tt, final=final),
        grid=(batch // nb, seq // tt, n_f),
        in_specs=[pl.BlockSpec((nb, tt, D_MODEL), lambda b, t, f: (b, t, 0)),
                  pl.BlockSpec((1, D_MODEL), vec),
                  pl.BlockSpec((D_MODEL, FF_TILE), lambda b, t, f: (0, f)),
                  pl.BlockSpec((D_MODEL, FF_TILE), lambda b, t, f: (0, f)),
                  pl.BlockSpec((3, FF_TILE), lambda b, t, f: (0, f)),
                  pl.BlockSpec((FF_TILE, D_MODEL), lambda b, t, f: (f, 0)),
                  pl.BlockSpec((nb, CONV_HALO, FF_TILE), lambda b, t, f: (b, 0, f)),
                  pl.BlockSpec((1, D_MODEL), vec)],
        out_specs=[pl.BlockSpec((nb, tt, D_MODEL), lambda b, t, f: (b, t, 0)),
                   pl.BlockSpec((nb, None, CONV_HALO, FF_TILE), lambda b, t, f: (b, t, 0, f))],
        out_shape=[jax.ShapeDtypeStruct((batch, seq, D_MODEL), _F32),
                   jax.ShapeDtypeStruct((batch, seq // tt, CONV_HALO, D_FF_PAD), _F32)],
        scratch_shapes=[pltpu.VMEM((nb * tt, D_MODEL), _BF16),
                        pltpu.VMEM((nb, tt + CONV_HALO, FF_TILE), _F32),
                        pltpu.VMEM((n_f, nb, CONV_HALO, FF_TILE), _F32)],
        compiler_params=pltpu.CompilerParams(dimension_semantics=("parallel", "arbitrary", "arbitrary"),
                                             vmem_limit_bytes=VMEM_LIMIT),
        name="ffn",
    )(x, norm2.reshape(1, D_MODEL), w_gate, w_up, conv_w, w_down, state, final_norm.reshape(1, D_MODEL))


def _pad_rows_front(a, rows):
    return jnp.pad(a, ((0, 0), (rows - a.shape[1], 0), (0, 0)))


def _layer(x, states, weights, final_norm, *, nb, tt, tm, start, final, prompt):
    (w_in, conv_a_w, w_out, pool_w, pool_scale, w_gate, w_up, ffn_conv_w, w_down, norm1, norm2) = weights
    conv_state, kv_caches, pool_state, ffn_state = states
    batch, seq, _ = x.shape
    proj, kv0, kv1, kv2 = _inproj(x.reshape(batch * seq, D_MODEL), norm1, w_in, tm)
    kv_new = (kv0, kv1, kv2)
    if prompt:
        yb = _attn_prompt(proj, batch, seq)
        new_kv = tuple(kv.reshape(batch, seq, 2 * ATT_WIDTH)[:, seq - win:] for kv, (win, _) in zip(kv_new, DIL_GROUPS))
    else:
        yb, *new_kv = _attn_sample(proj, kv_new, kv_caches, batch, seq)
    x1, conv_out, pool_out = _mixer(proj, yb, x, conv_state, pool_state, conv_a_w, pool_w, pool_scale, w_out,
                                    nb, tt, start)
    x2, ffn_out = _ffn(x1, norm2, w_gate, w_up, ffn_conv_w, w_down, ffn_state, final_norm, nb, tt, final)
    new_states = tuple(kv.reshape(batch, -1, 2, HEADS, HEAD_DIM) for kv in new_kv) + (
        conv_out[:, CONV_HALO - 2:], pool_out[:, POOL_HALO - POOL_BUF:], ffn_out[:, -1, CONV_HALO - 2:, :D_FF])
    return x2, new_states


def kernel(x_prompt, x_sample, cache_kv_w128, cache_kv_w512, cache_kv_w2048, state_conv_a, state_pool, state_ffn_conv, norm1, w_in, conv_a_w, w_out, pool_w, pool_scale, norm2, w_gate, w_up, ffn_conv_w, w_down, final_norm):
    bp, sp, _ = x_prompt.shape
    bs, ss, _ = x_sample.shape
    caches = (cache_kv_w128, cache_kv_w512, cache_kv_w2048)
    ff_pad = D_FF_PAD - D_FF
    xp, xs = x_prompt, x_sample
    new_p, new_s = [], []
    for l in range(DEPTH):
        w_in_l = jnp.concatenate([w_in[l][:, t * COL_TILE:(t + 1) * COL_TILE] for t in IN_TILE_ORDER],
                                 axis=1).astype(_BF16)
        weights = (w_in_l, conv_a_w[l], w_out[l].astype(_BF16), pool_w[l].astype(_BF16), pool_scale[l],
                   jnp.pad(w_gate[l], ((0, 0), (0, ff_pad))).astype(_BF16),
                   jnp.pad(w_up[l], ((0, 0), (0, ff_pad))).astype(_BF16),
                   jnp.pad(ffn_conv_w[l], ((0, 0), (0, ff_pad))),
                   jnp.pad(w_down[l], ((0, ff_pad), (0, 0))).astype(_BF16),
                   norm1[l], norm2[l])
        final = l == DEPTH - 1
        p_states = (jnp.zeros((bp, CONV_HALO, CONV_WIDTH), _F32), None,
                    jnp.zeros((bp, POOL_HALO, POOL_WIDTH), _F32), jnp.zeros((bp, CONV_HALO, D_FF_PAD), _F32))
        xp, st = _layer(xp, p_states, weights, final_norm, nb=1, tt=512, tm=512, start=0, final=final, prompt=True)
        new_p.append(st)
        s_states = (_pad_rows_front(state_conv_a[l], CONV_HALO),
                    tuple(c[l].reshape(bs, c.shape[2], 2 * ATT_WIDTH) for c in caches),
                    _pad_rows_front(state_pool[l], POOL_HALO),
                    jnp.pad(state_ffn_conv[l], ((0, 0), (CONV_HALO - 2, 0), (0, ff_pad))))
        xs, st = _layer(xs, s_states, weights, final_norm, nb=bs, tt=ss, tm=bs * ss, start=PAST_LEN, final=final,
                        prompt=False)
        new_s.append(st)

    def stk(lst, i):
        return jnp.stack([s[i] for s in lst])

    out = [xp, xs]
    for i in range(6):
        out += [stk(new_p, i), stk(new_s, i)]
    return tuple(out)
```

```python
import functools

import numpy as np
import jax
import jax.numpy as jnp
from jax import lax
from jax.experimental import pallas as pl
from jax.experimental.pallas import tpu as pltpu

D_MODEL = 2048
DEPTH = 2
PAST_LEN = 16384
CONV_WIDTH = 512
ATT_WIDTH = 512
HEADS = 4
HEAD_DIM = 128
DIL_GROUPS = ((128, 1), (512, 4), (2048, 16))
POOL_WIDTH = 1024
POOL_WINDOWS = (2, 4, 8, 16)
POOL_GROUP = 256
POOL_BUF = 15
IN_COLS = 7168
D_FF = 5504
RMS_EPS = 1e-6
ATT_SCALE = HEAD_DIM ** -0.5

COL_TILE = 512
IN_TILES = IN_COLS // COL_TILE
Q_TILE = (3, 6, 9)
U_TILE0 = 12
NAT_TILES = 8
KV_ROWS = 2 * HEADS
SUB_ROWS = 512
FF_TILE = 512
FF_TILES = -(-D_FF // FF_TILE)
FF_LAST = D_FF - (FF_TILES - 1) * FF_TILE
CONV_HALO = 8
POOL_HALO = 16
VMEM_LIMIT = 56 * 1024 * 1024

_BF16 = jnp.bfloat16
_F32 = jnp.float32


def _alibi_slopes():
    h = np.arange(1, 3 * HEADS + 1, dtype=np.float32)
    return [float(v) for v in np.power(np.float32(2.0), -8.0 * h / (3 * HEADS))]


SLOPES = _alibi_slopes()


def _rms_norm(x, g):
    return (x * lax.rsqrt(jnp.mean(x * x, axis=-1, keepdims=True) + RMS_EPS)) * g


def _head_cols(h, base=0):
    return slice(base + h * HEAD_DIM, base + (h + 1) * HEAD_DIM)


def _inproj_sample_kernel(x_ref, g_ref, w_ref, proj_ref, kv0_ref, kv1_ref, kv2_ref, h_s):
    j = pl.program_id(1)

    @pl.when(j == 0)
    def _():
        h_s[...] = _rms_norm(x_ref[...], g_ref[...]).astype(_BF16)

    acc = jnp.dot(h_s[...], w_ref[...], preferred_element_type=_F32)
    proj_ref[...] = acc.astype(_BF16)
    for g, ref in enumerate((kv0_ref, kv1_ref, kv2_ref)):
        @pl.when((j == Q_TILE[g] + 1) | (j == Q_TILE[g] + 2))
        def _(ref=ref):
            ref[...] = acc


def _inproj_sample(x2d, g, w, layer):
    m = x2d.shape[0]

    def kv_spec(grp):
        return pl.BlockSpec((m, COL_TILE), lambda i, j: (i, jnp.clip(j - Q_TILE[grp] - 1, 0, 1)))

    return pl.pallas_call(
        _inproj_sample_kernel,
        grid=(1, IN_TILES),
        in_specs=[pl.BlockSpec((m, D_MODEL), lambda i, j: (i, 0)),
                  pl.BlockSpec((1, D_MODEL), lambda i, j: (0, 0)),
                  pl.BlockSpec((None, D_MODEL, COL_TILE), lambda i, j: (layer, 0, j))],
        out_specs=[pl.BlockSpec((m, COL_TILE), lambda i, j: (i, j)), kv_spec(0), kv_spec(1), kv_spec(2)],
        out_shape=[jax.ShapeDtypeStruct((m, IN_COLS), _BF16)] +
                  [jax.ShapeDtypeStruct((m, 2 * ATT_WIDTH), _F32)] * 3,
        scratch_shapes=[pltpu.VMEM((m, D_MODEL), _BF16)],
        compiler_params=pltpu.CompilerParams(dimension_semantics=("parallel", "arbitrary"),
                                             vmem_limit_bytes=VMEM_LIMIT),
        name="inproj_sample",
    )(x2d, g.reshape(1, D_MODEL), w)


def _inproj_prompt_kernel(*refs, tm, tiles_per_seq, aliased):
    n_in = 6 if aliased else 3
    x_ref, g_ref, w_ref = refs[:3]
    nat_ref, g1_ref, g2_ref, kv0_ref, kv1_ref, kv2_ref, h_s, de_s = refs[n_in:]
    kv_refs = (kv0_ref, kv1_ref, kv2_ref)
    res_refs = (None, g1_ref, g2_ref)
    j = pl.program_id(1)
    is_last_tile = (pl.program_id(0) % tiles_per_seq) == tiles_per_seq - 1

    @pl.when(j == 0)
    def _():
        h_s[...] = _rms_norm(x_ref[...], g_ref[...]).astype(_BF16)

    w = w_ref[...]
    for c in range(tm // SUB_ROWS):
        r0 = c * SUB_ROWS
        acc = jnp.dot(h_s[r0:r0 + SUB_ROWS, :], w, preferred_element_type=_F32)

        @pl.when((j < Q_TILE[1]) | (j >= U_TILE0))
        def _():
            nat_ref[r0:r0 + SUB_ROWS, :] = acc.astype(_BF16)

        for grp, (win, dil) in enumerate(DIL_GROUPS):
            keep = min(win, tm)
            lo = max(r0, tm - keep)
            if lo < r0 + SUB_ROWS:
                n = r0 + SUB_ROWS - lo
                for part in range(2):
                    cond = j == Q_TILE[grp] + 1 + part
                    if win < tiles_per_seq * tm:
                        cond = cond & is_last_tile

                    @pl.when(cond)
                    def _(grp=grp, part=part, lo=lo, n=n, keep=keep):
                        for h in range(HEADS):
                            dst = pl.ds((lo - (tm - keep)) * KV_ROWS + part * HEADS + h, n, stride=KV_ROWS)
                            kv_refs[grp][dst, :] = acc[lo - r0:lo - r0 + n, _head_cols(h)]

            if dil > 1:
                @pl.when((j >= Q_TILE[grp]) & (j <= Q_TILE[grp] + 2))
                def _(grp=grp, dil=dil):
                    for h in range(HEADS):
                        de_s[h] = acc[:, _head_cols(h)]
                    n = SUB_ROWS // dil
                    for res in range(dil):
                        for h in range(HEADS):
                            res_refs[grp][res, c * n:(c + 1) * n, _head_cols(h)] = (
                                de_s[h, pl.ds(res, n, stride=dil), :].astype(_BF16))


def _inproj_prompt(x2d, g, w, batch, seq, tm, layer, prev_kv):
    m = x2d.shape[0]
    tps = seq // tm
    aliased = prev_kv is not None
    (w0, _), (w1, d1), (w2, d2) = DIL_GROUPS
    assert w2 == seq and w0 <= tm and w1 <= tm and tm % SUB_ROWS == 0

    in_specs = [pl.BlockSpec((tm, D_MODEL), lambda i, j: (i, 0)),
                pl.BlockSpec((1, D_MODEL), lambda i, j: (0, 0)),
                pl.BlockSpec((None, D_MODEL, COL_TILE), lambda i, j: (layer, 0, j))]
    args = [x2d, g.reshape(1, D_MODEL), w]
    if aliased:
        in_specs += [pl.BlockSpec(memory_space=pl.ANY)] * 3
        args += list(prev_kv)

    def res_spec(grp, dil):
        return pl.BlockSpec((None, dil, tm // dil, COL_TILE),
                            lambda i, j: (i // tps, 0, i % tps, jnp.clip(j - Q_TILE[grp], 0, 2)))

    out_specs = [
        pl.BlockSpec((tm, COL_TILE), lambda i, j: (i, jnp.clip(j, 0, Q_TILE[1] - 1) + jnp.clip(j - U_TILE0 + 1, 0, 2))),
        res_spec(1, d1), res_spec(2, d2),
        pl.BlockSpec((None, None, w0 * KV_ROWS, HEAD_DIM), lambda i, j: (layer, i // tps, 0, 0)),
        pl.BlockSpec((None, None, w1 * KV_ROWS, HEAD_DIM), lambda i, j: (layer, i // tps, 0, 0)),
        pl.BlockSpec((None, None, tm * KV_ROWS, HEAD_DIM), lambda i, j: (layer, i // tps, i % tps, 0)),
    ]
    out_shape = [
        jax.ShapeDtypeStruct((m, NAT_TILES * COL_TILE), _BF16),
        jax.ShapeDtypeStruct((batch, d1, seq // d1, 3 * ATT_WIDTH), _BF16),
        jax.ShapeDtypeStruct((batch, d2, seq // d2, 3 * ATT_WIDTH), _BF16),
    ] + [jax.ShapeDtypeStruct((DEPTH, batch, win * KV_ROWS, HEAD_DIM), _F32) for win, _ in DIL_GROUPS]
    return pl.pallas_call(
        functools.partial(_inproj_prompt_kernel, tm=tm, tiles_per_seq=tps, aliased=aliased),
        grid=(m // tm, IN_TILES),
        in_specs=in_specs, out_specs=out_specs, out_shape=out_shape,
        scratch_shapes=[pltpu.VMEM((tm, D_MODEL), _BF16), pltpu.VMEM((HEADS, SUB_ROWS, HEAD_DIM), _F32)],
        input_output_aliases={3: 3, 4: 4, 5: 5} if aliased else {},
        compiler_params=pltpu.CompilerParams(dimension_semantics=("arbitrary", "arbitrary"),
                                             vmem_limit_bytes=VMEM_LIMIT),
        name="inproj_prompt",
    )(*args)


QB = 128


def _attn_tile(q, k, v, distm, slope_step):
    s = lax.dot_general(q, k, (((1,), (1,)), ((), ())), preferred_element_type=_F32)
    s = s * ATT_SCALE - distm * slope_step
    m = jnp.max(s, axis=1, keepdims=True)
    p = jnp.exp(s - m)
    l = jnp.sum(p, axis=1, keepdims=True)
    acc = jnp.dot(p.astype(_BF16), v, preferred_element_type=_F32)
    return acc, m, l


def _attn_prompt_kernel(q0, k0, v0, q1, k1, v1, q2, k2, v2, o_ref, acc_s, m_s, l_s, *, seq):
    step = pl.program_id(1)
    n_steps = 1 + DIL_GROUPS[1][1] + DIL_GROUPS[2][1]

    row0 = lax.broadcasted_iota(jnp.int32, (QB, QB), 0)
    col0 = lax.broadcasted_iota(jnp.int32, (QB, QB), 1)
    dist_first = jnp.where(col0 <= row0, (row0 - col0).astype(_F32), jnp.inf)
    row1 = lax.broadcasted_iota(jnp.int32, (QB, 2 * QB), 0)
    col1 = lax.broadcasted_iota(jnp.int32, (QB, 2 * QB), 1)
    d1 = QB + row1 - col1
    dist_next = jnp.where((d1 >= 0) & (d1 <= QB), d1.astype(_F32), jnp.inf)

    def unit(qr, kr, vr, grp, res, first):
        dil = DIL_GROUPS[grp][1]
        n_blk = seq // dil // QB

        def block(i, is_first_block):
            for h in range(HEADS):
                hs = _head_cols(h)
                if is_first_block:
                    q, k, v, dm = qr[0:QB, hs], kr[0:QB, hs], vr[0:QB, hs], dist_first
                else:
                    qs = pl.ds(pl.multiple_of(i * QB, QB), QB)
                    ks = pl.ds(pl.multiple_of(i * QB - QB, QB), 2 * QB)
                    q, k, v, dm = qr[qs, hs], kr[ks, hs], vr[ks, hs], dist_next
                acc, m, l = _attn_tile(q, k, v, dm, SLOPES[grp * HEADS + h] * dil)
                if first:
                    rows = slice(0, QB) if is_first_block else pl.ds(pl.multiple_of(i * QB, QB), QB)
                    acc_s[h, rows, :] = acc
                    m_s[h, rows, :] = jnp.broadcast_to(m, (QB, HEAD_DIM))
                    l_s[h, rows, :] = jnp.broadcast_to(l, (QB, HEAD_DIM))
                else:
                    rows = pl.ds(res + dil * QB * i, QB, stride=dil)
                    m_old, l_old, a_old = m_s[h, rows, :], l_s[h, rows, :], acc_s[h, rows, :]
                    m_new = jnp.maximum(m_old, m)
                    w_old = jnp.exp(m_old - m_new)
                    w_new = jnp.exp(m - m_new)
                    acc_s[h, rows, :] = a_old * w_old + acc * w_new
                    l_s[h, rows, :] = l_old * w_old + l * w_new
                    m_s[h, rows, :] = m_new

        block(0, True)
        if n_blk > 1:
            def body(i, carry):
                block(i, False)
                return carry
            lax.fori_loop(1, n_blk, body, 0)

    n1 = DIL_GROUPS[1][1]

    @pl.when(step == 0)
    def _():
        unit(q0, k0, v0, 0, 0, True)

    @pl.when((step >= 1) & (step <= n1))
    def _():
        unit(q1, k1, v1, 1, step - 1, False)

    @pl.when(step > n1)
    def _():
        unit(q2, k2, v2, 2, step - 1 - n1, False)

    @pl.when(step == n_steps - 1)
    def _():
        for h in range(HEADS):
            o_ref[:, _head_cols(h)] = (acc_s[h] / l_s[h]).astype(_BF16)


def _attn_prompt(nat, qkv1, qkv2, batch, seq):
    n1, n2 = DIL_GROUPS[1][1], DIL_GROUPS[2][1]
    nat3 = nat.reshape(batch, seq, NAT_TILES * COL_TILE)
    in_specs = [pl.BlockSpec((None, seq, COL_TILE), lambda b, s, t=t: (b, 0, Q_TILE[0] + t)) for t in range(3)]
    args = [nat3] * 3
    for arr, dil, first_step in ((qkv1, n1, 1), (qkv2, n2, 1 + n1)):
        for t in range(3):
            def imap(b, s, t=t, dil=dil, first_step=first_step):
                return (b, jnp.clip(s - first_step, 0, dil - 1), 0, t)
            in_specs.append(pl.BlockSpec((None, None, seq // dil, COL_TILE), imap))
            args.append(arr)
    return pl.pallas_call(
        functools.partial(_attn_prompt_kernel, seq=seq),
        grid=(batch, 1 + n1 + n2),
        in_specs=in_specs,
        out_specs=pl.BlockSpec((None, seq, ATT_WIDTH), lambda b, s: (b, 0, 0)),
        out_shape=jax.ShapeDtypeStruct((batch, seq, ATT_WIDTH), _BF16),
        scratch_shapes=[pltpu.VMEM((HEADS, seq, HEAD_DIM), _F32)] * 3,
        compiler_params=pltpu.CompilerParams(dimension_semantics=("parallel", "arbitrary"),
                                             vmem_limit_bytes=VMEM_LIMIT),
        name="attn_prompt",
    )(*args)


def _attn_sample_kernel(*refs, tdec, aliased):
    n_in = 10 if aliased else 7
    proj_ref, n0, n1, n2, c0, c1, c2 = refs[:7]
    yb_ref, o0, o1, o2 = refs[n_in:]
    news, caches, outs = (n0, n1, n2), (c0, c1, c2), (o0, o1, o2)
    for h in range(HEADS):
        pieces = []
        for grp, (win, dil) in enumerate(DIL_GROUPS):
            slope = SLOPES[grp * HEADS + h]
            q = proj_ref[:, _head_cols(h, Q_TILE[grp] * COL_TILE)]
            k_new = news[grp][:, _head_cols(h)]
            v_new = news[grp][:, _head_cols(h, ATT_WIDTH)]
            k_old = caches[grp][pl.ds(h, win, stride=KV_ROWS), :]
            v_old = caches[grp][pl.ds(HEADS + h, win, stride=KV_ROWS), :]
            for k, v, n_keys, base in ((k_old, v_old, win, win), (k_new, v_new, tdec, 0)):
                s = lax.dot_general(q, k.astype(_BF16), (((1,), (1,)), ((), ())), preferred_element_type=_F32)
                t = lax.broadcasted_iota(jnp.int32, (tdec, n_keys), 0)
                c = lax.broadcasted_iota(jnp.int32, (tdec, n_keys), 1)
                dist = base + t - c
                ok = (dist >= 0) & (dist <= win) & ((dist & (dil - 1)) == 0)
                s = jnp.where(ok, s * ATT_SCALE - slope * dist.astype(_F32), -jnp.inf)
                pieces.append((s, v.astype(_BF16)))
        m = functools.reduce(jnp.maximum, [jnp.max(s, axis=1, keepdims=True) for s, _ in pieces])
        l = jnp.zeros((tdec, 1), _F32)
        acc = jnp.zeros((tdec, HEAD_DIM), _F32)
        for s, v in pieces:
            p = jnp.exp(s - m)
            l = l + jnp.sum(p, axis=1, keepdims=True)
            acc = acc + jnp.dot(p.astype(_BF16), v, preferred_element_type=_F32)
        yb_ref[:, _head_cols(h)] = (acc / l).astype(_BF16)
    for grp, (win, _) in enumerate(DIL_GROUPS):
        kept = (win - tdec) * KV_ROWS
        outs[grp][0:kept, :] = caches[grp][tdec * KV_ROWS:win * KV_ROWS, :]
        for part in range(2):
            for h in range(HEADS):
                outs[grp][pl.ds(kept + part * HEADS + h, tdec, stride=KV_ROWS), :] = (
                    news[grp][:, _head_cols(h, part * ATT_WIDTH)])


def _attn_sample(proj, kv_new, caches, batch, tdec, layer, prev_out):
    aliased = prev_out is not None
    in_specs = [pl.BlockSpec((None, tdec, IN_COLS), lambda b: (b, 0, 0))]
    in_specs += [pl.BlockSpec((None, tdec, 2 * ATT_WIDTH), lambda b: (b, 0, 0))] * 3
    cache_specs = [pl.BlockSpec((None, None, win * KV_ROWS, HEAD_DIM), lambda b: (layer, b, 0, 0)) for win, _ in DIL_GROUPS]
    in_specs += cache_specs
    args = [proj.reshape(batch, tdec, IN_COLS)] + [k.reshape(batch, tdec, 2 * ATT_WIDTH) for k in kv_new] + list(caches)
    if aliased:
        in_specs += [pl.BlockSpec(memory_space=pl.ANY)] * 3
        args += list(prev_out)
    out_specs = [pl.BlockSpec((None, tdec, ATT_WIDTH), lambda b: (b, 0, 0))] + cache_specs
    out_shape = [jax.ShapeDtypeStruct((batch, tdec, ATT_WIDTH), _BF16)]
    out_shape += [jax.ShapeDtypeStruct((DEPTH, batch, win * KV_ROWS, HEAD_DIM), _F32) for win, _ in DIL_GROUPS]
    return pl.pallas_call(
        functools.partial(_attn_sample_kernel, tdec=tdec, aliased=aliased),
        grid=(batch,),
        in_specs=in_specs, out_specs=out_specs, out_shape=out_shape,
        input_output_aliases={7: 1, 8: 2, 9: 3} if aliased else {},
        compiler_params=pltpu.CompilerParams(dimension_semantics=("arbitrary",), vmem_limit_bytes=VMEM_LIMIT),
        name="attn_sample",
    )(*args)


def _mixer_kernel(pa_ref, pu_ref, yb_ref, x_ref, cs0_ref, ps0_ref, cw_ref, pw_ref, psc_ref, wo_ref,
                  out_ref, cs_out, ps_out, ext_a, ext_u, ycat, *, nb, tt, start):
    ti = pl.program_id(1)
    rows = nb * tt

    @pl.when(ti == 0)
    def _():
        ext_a[:, 0:CONV_HALO, :] = cs0_ref[...]
        ext_u[:, 0:POOL_HALO, :] = ps0_ref[...]

    pa = pa_ref[...].astype(_F32)
    xa, gate_b, gate_c = (pa[:, :, k * CONV_WIDTH:(k + 1) * CONV_WIDTH] for k in range(3))
    prod = gate_c * xa
    ext_a[:, CONV_HALO:CONV_HALO + tt, :] = prod
    cw = cw_ref[...]
    cu = (cw[0] * ext_a[:, CONV_HALO - 2:CONV_HALO - 2 + tt, :]
          + cw[1] * ext_a[:, CONV_HALO - 1:CONV_HALO - 1 + tt, :] + cw[2] * prod)
    ycat[:, 0:CONV_WIDTH] = (gate_b * cu).reshape(rows, CONV_WIDTH).astype(_BF16)
    ycat[:, CONV_WIDTH:CONV_WIDTH + ATT_WIDTH] = (
        yb_ref[...].astype(_F32).reshape(rows, ATT_WIDTH).astype(_BF16))

    u = pu_ref[...].astype(_F32)
    ext_u[:, POOL_HALO:POOL_HALO + tt, :] = u
    pos = start + ti * tt + lax.broadcasted_iota(jnp.int32, (1, tt, 1), 1)
    for gi, w in enumerate(POOL_WINDOWS):
        cols = slice(gi * POOL_GROUP, (gi + 1) * POOL_GROUP)
        ug = u[:, :, cols]
        tot = ug
        for k in range(1, w):
            tot = tot + ext_u[:, POOL_HALO - k:POOL_HALO - k + tt, cols]
        cnt = jnp.minimum(w, pos + 1).astype(_F32)
        diff = (tot / cnt - ug).reshape(rows, POOL_GROUP).astype(_BF16)
        z = jnp.dot(diff, pw_ref[gi], preferred_element_type=_F32) * psc_ref[:, cols]
        c0 = CONV_WIDTH + ATT_WIDTH + gi * POOL_GROUP
        ycat[:, c0:c0 + POOL_GROUP] = z.astype(_BF16)

    mixed = jnp.dot(ycat[...], wo_ref[...], preferred_element_type=_F32)
    out_ref[...] = x_ref[...] + mixed.reshape(nb, tt, D_MODEL)

    tail_a = ext_a[:, tt:tt + CONV_HALO, :]
    tail_u = ext_u[:, tt:tt + POOL_HALO, :]
    ext_a[:, 0:CONV_HALO, :] = tail_a
    ext_u[:, 0:POOL_HALO, :] = tail_u
    cs_out[...] = tail_a
    ps_out[...] = tail_u


def _mixer(proj, yb, x, conv_state, pool_state, conv_w, pool_w, pool_scale, w_out, layer, nb, tt, start, u_tile0):
    batch, seq, _ = x.shape
    proj3 = proj.reshape(batch, seq, proj.shape[-1])
    u_blk = u_tile0 * COL_TILE // POOL_WIDTH
    const2 = lambda b, t: (0, 0)
    return pl.pallas_call(
        functools.partial(_mixer_kernel, nb=nb, tt=tt, start=start),
        grid=(batch // nb, seq // tt),
        in_specs=[pl.BlockSpec((nb, tt, 3 * CONV_WIDTH), lambda b, t: (b, t, 0)),
                  pl.BlockSpec((nb, tt, POOL_WIDTH), lambda b, t: (b, t, u_blk)),
                  pl.BlockSpec((nb, tt, ATT_WIDTH), lambda b, t: (b, t, 0)),
                  pl.BlockSpec((nb, tt, D_MODEL), lambda b, t: (b, t, 0)),
                  pl.BlockSpec((nb, CONV_HALO, CONV_WIDTH), lambda b, t: (b, 0, 0)),
                  pl.BlockSpec((nb, POOL_HALO, POOL_WIDTH), lambda b, t: (b, 0, 0)),
                  pl.BlockSpec((3, CONV_WIDTH), const2),
                  pl.BlockSpec((None, len(POOL_WINDOWS), POOL_GROUP, POOL_GROUP), lambda b, t: (layer, 0, 0, 0)),
                  pl.BlockSpec((1, POOL_WIDTH), const2),
                  pl.BlockSpec((None, D_MODEL, D_MODEL), lambda b, t: (layer, 0, 0))],
        out_specs=[pl.BlockSpec((nb, tt, D_MODEL), lambda b, t: (b, t, 0)),
                   pl.BlockSpec((nb, CONV_HALO, CONV_WIDTH), lambda b, t: (b, 0, 0)),
                   pl.BlockSpec((nb, POOL_HALO, POOL_WIDTH), lambda b, t: (b, 0, 0))],
        out_shape=[jax.ShapeDtypeStruct((batch, seq, D_MODEL), _F32),
                   jax.ShapeDtypeStruct((batch, CONV_HALO, CONV_WIDTH), _F32),
                   jax.ShapeDtypeStruct((batch, POOL_HALO, POOL_WIDTH), _F32)],
        scratch_shapes=[pltpu.VMEM((nb, tt + CONV_HALO, CONV_WIDTH), _F32),
                        pltpu.VMEM((nb, tt + POOL_HALO, POOL_WIDTH), _F32),
                        pltpu.VMEM((nb * tt, D_MODEL), _BF16)],
        compiler_params=pltpu.CompilerParams(dimension_semantics=("parallel", "arbitrary"),
                                             vmem_limit_bytes=VMEM_LIMIT),
        name="mixer",
    )(proj3, proj3, yb, x, conv_state, pool_state, conv_w, pool_w, pool_scale.reshape(1, POOL_WIDTH), w_out)


def _ffn_kernel(x_ref, n2_ref, wg_ref, wu_ref, cw_ref, wd_ref, st0_ref, fn_ref,
                out_ref, st_out, h_s, ext_s, carry_s, *, nb, tt, final):
    ti = pl.program_id(1)
    f = pl.program_id(2)
    rows = nb * tt

    @pl.when(f == 0)
    def _():
        x = x_ref[...]
        h_s[...] = _rms_norm(x, n2_ref[...]).reshape(rows, D_MODEL).astype(_BF16)
        out_ref[...] = x

    def tile(width):
        cols = slice(0, width)

        @pl.when(ti == 0)
        def _():
            ext_s[:, 0:CONV_HALO, cols] = st0_ref[:, :, cols]

        @pl.when(ti > 0)
        def _():
            ext_s[:, 0:CONV_HALO, cols] = carry_s[f, :, :, cols]

        h = h_s[...]
        gate = jnp.dot(h, wg_ref[:, cols], preferred_element_type=_F32).reshape(nb, tt, width)
        up = jnp.dot(h, wu_ref[:, cols], preferred_element_type=_F32).reshape(nb, tt, width)
        ext_s[:, CONV_HALO:CONV_HALO + tt, cols] = gate
        cw = cw_ref[:, cols]
        gc = (cw[0] * ext_s[:, CONV_HALO - 2:CONV_HALO - 2 + tt, cols]
              + cw[1] * ext_s[:, CONV_HALO - 1:CONV_HALO - 1 + tt, cols] + cw[2] * gate)
        tail = ext_s[:, tt:tt + CONV_HALO, cols]
        carry_s[f, :, :, cols] = tail
        st_out[:, :, cols] = tail
        act = (jax.nn.silu(gc) * up).reshape(rows, width).astype(_BF16)
        out_ref[...] += jnp.dot(act, wd_ref[cols, :], preferred_element_type=_F32).reshape(nb, tt, D_MODEL)

    @pl.when(f < FF_TILES - 1)
    def _():
        tile(FF_TILE)

    @pl.when(f == FF_TILES - 1)
    def _():
        tile(FF_LAST)
        if final:
            out_ref[...] = _rms_norm(out_ref[...], fn_ref[...])


def _ffn(x, norm2, w_gate, w_up, conv_w, w_down, state, final_norm, layer, nb, tt, final):
    batch, seq, _ = x.shape
    vec = lambda b, t, f: (0, 0)
    return pl.pallas_call(
        functools.partial(_ffn_kernel, nb=nb, tt=tt, final=final),
        grid=(batch // nb, seq // tt, FF_TILES),
        in_specs=[pl.BlockSpec((nb, tt, D_MODEL), lambda b, t, f: (b, t, 0)),
                  pl.BlockSpec((1, D_MODEL), vec),
                  pl.BlockSpec((None, D_MODEL, FF_TILE), lambda b, t, f: (layer, 0, f)),
                  pl.BlockSpec((None, D_MODEL, FF_TILE), lambda b, t, f: (layer, 0, f)),
                  pl.BlockSpec((3, FF_TILE), lambda b, t, f: (0, f)),
                  pl.BlockSpec((None, FF_TILE, D_MODEL), lambda b, t, f: (layer, f, 0)),
                  pl.BlockSpec((nb, CONV_HALO, FF_TILE), lambda b, t, f: (b, 0, f)),
                  pl.BlockSpec((1, D_MODEL), vec)],
        out_specs=[pl.BlockSpec((nb, tt, D_MODEL), lambda b, t, f: (b, t, 0)),
                   pl.BlockSpec((nb, None, CONV_HALO, FF_TILE), lambda b, t, f: (b, t, 0, f))],
        out_shape=[jax.ShapeDtypeStruct((batch, seq, D_MODEL), _F32),
                   jax.ShapeDtypeStruct((batch, seq // tt, CONV_HALO, D_FF), _F32)],
        scratch_shapes=[pltpu.VMEM((nb * tt, D_MODEL), _BF16),
                        pltpu.VMEM((nb, tt + CONV_HALO, FF_TILE), _F32),
                        pltpu.VMEM((FF_TILES, nb, CONV_HALO, FF_TILE), _F32)],
        compiler_params=pltpu.CompilerParams(dimension_semantics=("parallel", "arbitrary", "arbitrary"),
                                             vmem_limit_bytes=VMEM_LIMIT),
        name="ffn",
    )(x, norm2.reshape(1, D_MODEL), w_gate, w_up, conv_w, w_down, state, final_norm.reshape(1, D_MODEL))


def _pad_rows_front(a, rows):
    return jnp.pad(a, ((0, 0), (rows - a.shape[1], 0), (0, 0)))


def kernel(x_prompt, x_sample, cache_kv_w128, cache_kv_w512, cache_kv_w2048, state_conv_a, state_pool, state_ffn_conv, norm1, w_in, conv_a_w, w_out, pool_w, pool_scale, norm2, w_gate, w_up, ffn_conv_w, w_down, final_norm):
    bp, sp, _ = x_prompt.shape
    bs, ss, _ = x_sample.shape
    caches = tuple(c.reshape(DEPTH, bs, win * KV_ROWS, HEAD_DIM)
                   for c, (win, _) in zip((cache_kv_w128, cache_kv_w512, cache_kv_w2048), DIL_GROUPS))
    zeros_conv = jnp.zeros((bp, CONV_HALO, CONV_WIDTH), _F32)
    zeros_pool = jnp.zeros((bp, POOL_HALO, POOL_WIDTH), _F32)
    zeros_ffn = jnp.zeros((bp, CONV_HALO, D_FF), _F32)
    tm = tt = 512
    w_in_b, w_out_b, pool_w_b = w_in.astype(_BF16), w_out.astype(_BF16), pool_w.astype(_BF16)
    w_gate_b, w_up_b, w_down_b = w_gate.astype(_BF16), w_up.astype(_BF16), w_down.astype(_BF16)
    xp, xs = x_prompt, x_sample
    kv_p = kv_s = None
    small_p, small_s = [], []
    for l in range(DEPTH):
        final = l == DEPTH - 1

        nat, qkv1, qkv2, *kv_p = _inproj_prompt(xp.reshape(bp * sp, D_MODEL), norm1[l], w_in_b, bp, sp, tm, l, kv_p)
        yb = _attn_prompt(nat, qkv1, qkv2, bp, sp)
        xp, conv_o, pool_o = _mixer(nat, yb, xp, zeros_conv, zeros_pool, conv_a_w[l], pool_w_b, pool_scale[l], w_out_b,
                                    l, 1, tt, 0, NAT_TILES - 2)
        xp, ffn_o = _ffn(xp, norm2[l], w_gate_b, w_up_b, ffn_conv_w[l], w_down_b, zeros_ffn, final_norm,
                         l, 1, tt, final)
        small_p.append((conv_o[:, CONV_HALO - 2:], pool_o[:, POOL_HALO - POOL_BUF:], ffn_o[:, -1, CONV_HALO - 2:]))

        proj, *kv_new = _inproj_sample(xs.reshape(bs * ss, D_MODEL), norm1[l], w_in_b, l)
        yb, *kv_s = _attn_sample(proj, kv_new, caches, bs, ss, l, kv_s)
        xs, conv_o, pool_o = _mixer(proj, yb, xs, _pad_rows_front(state_conv_a[l], CONV_HALO),
                                    _pad_rows_front(state_pool[l], POOL_HALO), conv_a_w[l], pool_w_b, pool_scale[l],
                                    w_out_b, l, bs, ss, PAST_LEN, U_TILE0)
        xs, ffn_o = _ffn(xs, norm2[l], w_gate_b, w_up_b, ffn_conv_w[l], w_down_b,
                         _pad_rows_front(state_ffn_conv[l], CONV_HALO), final_norm, l, bs, ss, final)
        small_s.append((conv_o[:, CONV_HALO - 2:], pool_o[:, POOL_HALO - POOL_BUF:], ffn_o[:, -1, CONV_HALO - 2:]))

    out = [xp, xs]
    for g, (win, _) in enumerate(DIL_GROUPS):
        out += [kv_p[g].reshape(DEPTH, bp, win, 2, HEADS, HEAD_DIM), kv_s[g].reshape(DEPTH, bs, win, 2, HEADS, HEAD_DIM)]
    for i in range(3):
        out += [jnp.stack([s[i] for s in small_p]), jnp.stack([s[i] for s in small_s])]
    return tuple(out)
```

```python
import functools

import numpy as np
import jax
import jax.numpy as jnp
from jax import lax
from jax.experimental import pallas as pl
from jax.experimental.pallas import tpu as pltpu

D_MODEL = 2048
DEPTH = 2
PAST_LEN = 16384
CONV_WIDTH = 512
ATT_WIDTH = 512
HEADS = 4
HEAD_DIM = 128
DIL_GROUPS = ((128, 1), (512, 4), (2048, 16))
POOL_WIDTH = 1024
POOL_WINDOWS = (2, 4, 8, 16)
POOL_GROUP = 256
POOL_BUF = 15
IN_COLS = 7168
D_FF = 5504
RMS_EPS = 1e-6
ATT_SCALE = HEAD_DIM ** -0.5

COL_TILE = 512
IN_TILES = IN_COLS // COL_TILE
Q_TILE = (3, 6, 9)
U_TILE0 = 12
NAT_TILES = 8
KV_ROWS = 2 * HEADS
SUB_ROWS = 512
FF_TILE = 512
FF_TILES = -(-D_FF // FF_TILE)
FF_LAST = D_FF - (FF_TILES - 1) * FF_TILE
CONV_HALO = 8
POOL_HALO = 16
VMEM_LIMIT = 56 * 1024 * 1024

_BF16 = jnp.bfloat16
_F32 = jnp.float32


def _alibi_slopes():
    h = np.arange(1, 3 * HEADS + 1, dtype=np.float32)
    return [float(v) for v in np.power(np.float32(2.0), -8.0 * h / (3 * HEADS))]


SLOPES = _alibi_slopes()


def _rms_norm(x, g):
    return (x * lax.rsqrt(jnp.mean(x * x, axis=-1, keepdims=True) + RMS_EPS)) * g


def _head_cols(h, base=0):
    return slice(base + h * HEAD_DIM, base + (h + 1) * HEAD_DIM)


def _inproj_sample_kernel(x_ref, g_ref, w_ref, proj_ref, kv0_ref, kv1_ref, kv2_ref, h_s):
    j = pl.program_id(1)

    @pl.when(j == 0)
    def _():
        h_s[...] = _rms_norm(x_ref[...], g_ref[...]).astype(_BF16)

    acc = jnp.dot(h_s[...], w_ref[...], preferred_element_type=_F32)
    proj_ref[...] = acc.astype(_BF16)
    for g, ref in enumerate((kv0_ref, kv1_ref, kv2_ref)):
        @pl.when((j == Q_TILE[g] + 1) | (j == Q_TILE[g] + 2))
        def _(ref=ref):
            ref[...] = acc


def _inproj_sample(x2d, g, w, layer):
    m = x2d.shape[0]

    def kv_spec(grp):
        return pl.BlockSpec((m, COL_TILE), lambda i, j: (i, jnp.clip(j - Q_TILE[grp] - 1, 0, 1)))

    return pl.pallas_call(
        _inproj_sample_kernel,
        grid=(1, IN_TILES),
        in_specs=[pl.BlockSpec((m, D_MODEL), lambda i, j: (i, 0)),
                  pl.BlockSpec((1, D_MODEL), lambda i, j: (0, 0)),
                  pl.BlockSpec((None, D_MODEL, COL_TILE), lambda i, j: (layer, 0, j))],
        out_specs=[pl.BlockSpec((m, COL_TILE), lambda i, j: (i, j)), kv_spec(0), kv_spec(1), kv_spec(2)],
        out_shape=[jax.ShapeDtypeStruct((m, IN_COLS), _BF16)] +
                  [jax.ShapeDtypeStruct((m, 2 * ATT_WIDTH), _F32)] * 3,
        scratch_shapes=[pltpu.VMEM((m, D_MODEL), _BF16)],
        compiler_params=pltpu.CompilerParams(dimension_semantics=("parallel", "arbitrary"),
                                             vmem_limit_bytes=VMEM_LIMIT),
        name="inproj_sample",
    )(x2d, g.reshape(1, D_MODEL), w)


def _inproj_prompt_kernel(*refs, tm, aliased):
    n_in = 6 if aliased else 3
    x_ref, g_ref, w_ref = refs[:3]
    nat_ref, g1_ref, g2_ref, kv0_ref, kv1_ref, kv2_ref, h_s, de_s = refs[n_in:]
    kv_refs = (kv0_ref, kv1_ref, kv2_ref)
    res_refs = (None, g1_ref, g2_ref)
    j = pl.program_id(1)

    @pl.when(j == 0)
    def _():
        h_s[...] = _rms_norm(x_ref[...], g_ref[...]).astype(_BF16)

    def column_tile(grp, part):
        w = w_ref[...]
        for c in range(tm // SUB_ROWS):
            r0 = c * SUB_ROWS
            acc = jnp.dot(h_s[r0:r0 + SUB_ROWS, :], w, preferred_element_type=_F32)
            dil = 1 if grp is None else DIL_GROUPS[grp][1]
            if dil == 1:
                nat_ref[r0:r0 + SUB_ROWS, :] = acc.astype(_BF16)
            else:
                for h in range(HEADS):
                    de_s[c, h] = acc[:, _head_cols(h)]
                n = SUB_ROWS // dil
                for res in range(dil):
                    for h in range(HEADS):
                        res_refs[grp][res, c * n:(c + 1) * n, _head_cols(h)] = (
                            de_s[c, h, pl.ds(res, n, stride=dil), :].astype(_BF16))
            if part is not None:
                keep = min(DIL_GROUPS[grp][0], tm)
                lo = max(r0, tm - keep)
                n = r0 + SUB_ROWS - lo
                if n > 0:
                    for h in range(HEADS):
                        dst = pl.ds((lo - (tm - keep)) * KV_ROWS + part * HEADS + h, n, stride=KV_ROWS)
                        kv_refs[grp][dst, :] = acc[lo - r0:lo - r0 + n, _head_cols(h)]

    @pl.when((j <= Q_TILE[0]) | (j >= U_TILE0))
    def _():
        column_tile(None, None)

    for grp in range(len(DIL_GROUPS)):
        for t, part in enumerate((None, 0, 1)):
            if grp == 0 and part is None:
                continue

            @pl.when(j == Q_TILE[grp] + t)
            def _(grp=grp, part=part):
                column_tile(grp, part)


def _inproj_prompt(x2d, g, w, batch, seq, tm, layer, prev_kv):
    m = x2d.shape[0]
    tps = seq // tm
    aliased = prev_kv is not None
    (w0, _), (w1, d1), (w2, d2) = DIL_GROUPS
    assert w2 == seq and w0 <= tm and w1 <= tm and tm % SUB_ROWS == 0

    in_specs = [pl.BlockSpec((tm, D_MODEL), lambda i, j: (i, 0)),
                pl.BlockSpec((1, D_MODEL), lambda i, j: (0, 0)),
                pl.BlockSpec((None, D_MODEL, COL_TILE), lambda i, j: (layer, 0, j))]
    args = [x2d, g.reshape(1, D_MODEL), w]
    if aliased:
        in_specs += [pl.BlockSpec(memory_space=pl.ANY)] * 3
        args += list(prev_kv)

    def res_spec(grp, dil):
        return pl.BlockSpec((None, dil, tm // dil, COL_TILE),
                            lambda i, j: (i // tps, 0, i % tps, jnp.clip(j - Q_TILE[grp], 0, 2)))

    out_specs = [
        pl.BlockSpec((tm, COL_TILE), lambda i, j: (i, jnp.clip(j, 0, Q_TILE[1] - 1) + jnp.clip(j - U_TILE0 + 1, 0, 2))),
        res_spec(1, d1), res_spec(2, d2),
        pl.BlockSpec((None, None, w0 * KV_ROWS, HEAD_DIM), lambda i, j: (layer, i // tps, 0, 0)),
        pl.BlockSpec((None, None, w1 * KV_ROWS, HEAD_DIM), lambda i, j: (layer, i // tps, 0, 0)),
        pl.BlockSpec((None, None, tm * KV_ROWS, HEAD_DIM), lambda i, j: (layer, i // tps, i % tps, 0)),
    ]
    out_shape = [
        jax.ShapeDtypeStruct((m, NAT_TILES * COL_TILE), _BF16),
        jax.ShapeDtypeStruct((batch, d1, seq // d1, 3 * ATT_WIDTH), _BF16),
        jax.ShapeDtypeStruct((batch, d2, seq // d2, 3 * ATT_WIDTH), _BF16),
    ] + [jax.ShapeDtypeStruct((DEPTH, batch, win * KV_ROWS, HEAD_DIM), _F32) for win, _ in DIL_GROUPS]
    return pl.pallas_call(
        functools.partial(_inproj_prompt_kernel, tm=tm, aliased=aliased),
        grid=(m // tm, IN_TILES),
        in_specs=in_specs, out_specs=out_specs, out_shape=out_shape,
        scratch_shapes=[pltpu.VMEM((tm, D_MODEL), _BF16),
                        pltpu.VMEM((tm // SUB_ROWS, HEADS, SUB_ROWS, HEAD_DIM), _F32)],
        input_output_aliases={3: 3, 4: 4, 5: 5} if aliased else {},
        compiler_params=pltpu.CompilerParams(dimension_semantics=("arbitrary", "arbitrary"),
                                             vmem_limit_bytes=VMEM_LIMIT),
        name="inproj_prompt",
    )(*args)


QB = 128
G0_UNROLL = 3
G2_RES_PER_STEP = 4


def _attn_prompt_kernel(q0, k0, v0, q1, k1, v1, q2, k2, v2, o_ref, acc_s, m_s, l_s, shuf_s, *, seq):
    step = pl.program_id(1)
    n1 = DIL_GROUPS[1][1]
    n2_steps = DIL_GROUPS[2][1] // G2_RES_PER_STEP

    row0 = lax.broadcasted_iota(jnp.int32, (QB, QB), 0)
    col0 = lax.broadcasted_iota(jnp.int32, (QB, QB), 1)
    dist_first = jnp.where(col0 <= row0, (row0 - col0).astype(_F32), jnp.inf)
    row1 = lax.broadcasted_iota(jnp.int32, (QB, 2 * QB), 0)
    col1 = lax.broadcasted_iota(jnp.int32, (QB, 2 * QB), 1)
    d1 = QB + row1 - col1
    dist_next = jnp.where((d1 >= 0) & (d1 <= QB), d1.astype(_F32), jnp.inf)
    ones = jnp.ones((2 * QB, HEAD_DIM), _BF16)

    seg = seq // n1
    chunk = QB // n1
    assert n1 == 4 and DIL_GROUPS[2][1] == 16

    def blocks(specs):
        tiles = []
        for slot, (qr, kr, vr, grp, res, i, is_first_block) in enumerate(specs):
            dil = DIL_GROUPS[grp][1]
            for h in range(HEADS):
                hs = _head_cols(h)
                slope_step = SLOPES[grp * HEADS + h] * dil
                if is_first_block:
                    q, k, v = qr[0:QB, hs], kr[0:QB, hs], vr[0:QB, hs]
                    bias, one = dist_first * slope_step, ones[0:QB]
                else:
                    qs = pl.ds(pl.multiple_of(i * QB, QB), QB)
                    ks = pl.ds(pl.multiple_of(i * QB - QB, QB), 2 * QB)
                    q, k, v = qr[qs, hs], kr[ks, hs], vr[ks, hs]
                    bias, one = dist_next * slope_step, ones
                tiles.append((q, k, jnp.concatenate([v, one], axis=1), bias, grp, res, i, is_first_block, slot, h))
        scores = [lax.dot_general(t[0], t[1], (((1,), (1,)), ((), ())), preferred_element_type=_F32) for t in tiles]
        probs, maxes = [], []
        for t, s in zip(tiles, scores):
            s = s * ATT_SCALE - t[3]
            m = jnp.max(s, axis=1, keepdims=True)
            probs.append(jnp.exp(s - m).astype(_BF16))
            maxes.append(jnp.broadcast_to(m, (QB, HEAD_DIM)))
        outs = [jnp.dot(p, t[2], preferred_element_type=_F32) for t, p in zip(tiles, probs)]
        for (_, _, _, _, grp, res, i, is_first_block, slot, h), m, acc_l in zip(tiles, maxes, outs):
            acc, l = acc_l[:, :HEAD_DIM], acc_l[:, HEAD_DIM:]
            if grp == 0:
                for a, (val, ref) in enumerate(((acc, acc_s), (m, m_s), (l, l_s))):
                    tmp = shuf_s.at[(slot * HEADS + h) * 3 + a]
                    tmp[...] = val
                    for r4 in range(n1):
                        dst = r4 * seg + i * chunk
                        dst = pl.ds(dst if is_first_block else pl.multiple_of(dst, chunk), chunk)
                        ref[h, dst, :] = tmp[pl.ds(r4, chunk, stride=n1), :]
            else:
                if grp == 1:
                    rows = pl.ds(pl.multiple_of(res * seg + i * QB, QB), QB)
                else:
                    rows = pl.ds((res & (n1 - 1)) * seg + (res >> 2), QB, stride=n1)
                m_old, l_old, a_old = m_s[h, rows, :], l_s[h, rows, :], acc_s[h, rows, :]
                m_new = jnp.maximum(m_old, m)
                w_old = jnp.exp(m_old - m_new)
                w_new = jnp.exp(m - m_new)
                acc_s[h, rows, :] = a_old * w_old + acc * w_new
                l_s[h, rows, :] = l_old * w_old + l * w_new
                m_s[h, rows, :] = m_new

    @pl.when(step == 0)
    def _():
        n_blk = seq // QB
        assert (n_blk - 1) % G0_UNROLL == 0
        blocks([(q0, k0, v0, 0, 0, 0, True)])

        def body(it, carry):
            blocks([(q0, k0, v0, 0, 0, 1 + it * G0_UNROLL + u, False) for u in range(G0_UNROLL)])
            return carry
        lax.fori_loop(0, (n_blk - 1) // G0_UNROLL, body, 0)

    @pl.when((step >= 1) & (step <= n1))
    def _():
        blocks([(q1, k1, v1, 1, step - 1, i, i == 0) for i in range(seg // QB)])

    @pl.when(step > n1)
    def _():
        assert seq // DIL_GROUPS[2][1] == QB
        blocks([(q2.at[r], k2.at[r], v2.at[r], 2, (step - 1 - n1) * G2_RES_PER_STEP + r, 0, True)
                for r in range(G2_RES_PER_STEP)])

    @pl.when(step == n1 + n2_steps)
    def _():
        def body(c, carry):
            for h in range(HEADS):
                tmp = shuf_s.at[h]
                for r4 in range(n1):
                    src = pl.ds(pl.multiple_of(r4 * seg + c * chunk, chunk), chunk)
                    tmp[pl.ds(r4, chunk, stride=n1), :] = acc_s[h, src, :] / l_s[h, src, :]
                o_ref[pl.ds(pl.multiple_of(c * QB, QB), QB), _head_cols(h)] = tmp[...].astype(_BF16)
            return carry
        lax.fori_loop(0, seq // QB, body, 0)


def _attn_prompt(nat, qkv1, qkv2, batch, seq):
    n1, n2 = DIL_GROUPS[1][1], DIL_GROUPS[2][1]
    n2_steps = n2 // G2_RES_PER_STEP
    nat3 = nat.reshape(batch, seq, NAT_TILES * COL_TILE)
    in_specs = [pl.BlockSpec((None, seq, COL_TILE), lambda b, s, t=t: (b, 0, Q_TILE[0] + t)) for t in range(3)]
    in_specs += [pl.BlockSpec((None, None, seq // n1, COL_TILE),
                              lambda b, s, t=t: (b, jnp.clip(s - 1, 0, n1 - 1), 0, t)) for t in range(3)]
    in_specs += [pl.BlockSpec((None, G2_RES_PER_STEP, seq // n2, COL_TILE),
                              lambda b, s, t=t: (b, jnp.clip(s - 1 - n1, 0, n2_steps - 1), 0, t)) for t in range(3)]
    args = [nat3] * 3 + [qkv1] * 3 + [qkv2] * 3
    return pl.pallas_call(
        functools.partial(_attn_prompt_kernel, seq=seq),
        grid=(batch, 1 + n1 + n2_steps),
        in_specs=in_specs,
        out_specs=pl.BlockSpec((None, seq, ATT_WIDTH), lambda b, s: (b, 0, 0)),
        out_shape=jax.ShapeDtypeStruct((batch, seq, ATT_WIDTH), _BF16),
        scratch_shapes=[pltpu.VMEM((HEADS, seq, HEAD_DIM), _F32)] * 3
                       + [pltpu.VMEM((G0_UNROLL * HEADS * 3, QB, HEAD_DIM), _F32)],
        compiler_params=pltpu.CompilerParams(dimension_semantics=("parallel", "arbitrary"),
                                             vmem_limit_bytes=VMEM_LIMIT),
        name="attn_prompt",
    )(*args)


def _attn_sample_kernel(*refs, tdec, aliased):
    n_in = 10 if aliased else 7
    proj_ref, n0, n1, n2, c0, c1, c2 = refs[:7]
    yb_ref, o0, o1, o2 = refs[n_in:]
    news, caches, outs = (n0, n1, n2), (c0, c1, c2), (o0, o1, o2)
    for h in range(HEADS):
        pieces = []
        for grp, (win, dil) in enumerate(DIL_GROUPS):
            slope = SLOPES[grp * HEADS + h]
            q = proj_ref[:, _head_cols(h, Q_TILE[grp] * COL_TILE)]
            k_new = news[grp][:, _head_cols(h)]
            v_new = news[grp][:, _head_cols(h, ATT_WIDTH)]
            k_old = caches[grp][pl.ds(h, win, stride=KV_ROWS), :]
            v_old = caches[grp][pl.ds(HEADS + h, win, stride=KV_ROWS), :]
            for k, v, n_keys, base in ((k_old, v_old, win, win), (k_new, v_new, tdec, 0)):
                s = lax.dot_general(q, k.astype(_BF16), (((1,), (1,)), ((), ())), preferred_element_type=_F32)
                t = lax.broadcasted_iota(jnp.int32, (tdec, n_keys), 0)
                c = lax.broadcasted_iota(jnp.int32, (tdec, n_keys), 1)
                dist = base + t - c
                ok = (dist >= 0) & (dist <= win) & ((dist & (dil - 1)) == 0)
                s = jnp.where(ok, s * ATT_SCALE - slope * dist.astype(_F32), -jnp.inf)
                pieces.append((s, v.astype(_BF16)))
        m = functools.reduce(jnp.maximum, [jnp.max(s, axis=1, keepdims=True) for s, _ in pieces])
        l = jnp.zeros((tdec, 1), _F32)
        acc = jnp.zeros((tdec, HEAD_DIM), _F32)
        for s, v in pieces:
            p = jnp.exp(s - m)
            l = l + jnp.sum(p, axis=1, keepdims=True)
            acc = acc + jnp.dot(p.astype(_BF16), v, preferred_element_type=_F32)
        yb_ref[:, _head_cols(h)] = (acc / l).astype(_BF16)
    for grp, (win, _) in enumerate(DIL_GROUPS):
        kept = (win - tdec) * KV_ROWS
        outs[grp][0:kept, :] = caches[grp][tdec * KV_ROWS:win * KV_ROWS, :]
        for part in range(2):
            for h in range(HEADS):
                outs[grp][pl.ds(kept + part * HEADS + h, tdec, stride=KV_ROWS), :] = (
                    news[grp][:, _head_cols(h, part * ATT_WIDTH)])


def _attn_sample(proj, kv_new, caches, batch, tdec, layer, prev_out):
    aliased = prev_out is not None
    in_specs = [pl.BlockSpec((None, tdec, IN_COLS), lambda b: (b, 0, 0))]
    in_specs += [pl.BlockSpec((None, tdec, 2 * ATT_WIDTH), lambda b: (b, 0, 0))] * 3
    cache_specs = [pl.BlockSpec((None, None, win * KV_ROWS, HEAD_DIM), lambda b: (layer, b, 0, 0)) for win, _ in DIL_GROUPS]
    in_specs += cache_specs
    args = [proj.reshape(batch, tdec, IN_COLS)] + [k.reshape(batch, tdec, 2 * ATT_WIDTH) for k in kv_new] + list(caches)
    if aliased:
        in_specs += [pl.BlockSpec(memory_space=pl.ANY)] * 3
        args += list(prev_out)
    out_specs = [pl.BlockSpec((None, tdec, ATT_WIDTH), lambda b: (b, 0, 0))] + cache_specs
    out_shape = [jax.ShapeDtypeStruct((batch, tdec, ATT_WIDTH), _BF16)]
    out_shape += [jax.ShapeDtypeStruct((DEPTH, batch, win * KV_ROWS, HEAD_DIM), _F32) for win, _ in DIL_GROUPS]
    return pl.pallas_call(
        functools.partial(_attn_sample_kernel, tdec=tdec, aliased=aliased),
        grid=(batch,),
        in_specs=in_specs, out_specs=out_specs, out_shape=out_shape,
        input_output_aliases={7: 1, 8: 2, 9: 3} if aliased else {},
        compiler_params=pltpu.CompilerParams(dimension_semantics=("arbitrary",), vmem_limit_bytes=VMEM_LIMIT),
        name="attn_sample",
    )(*args)


def _mixer_kernel(pa_ref, pu_ref, yb_ref, x_ref, cs0_ref, ps0_ref, cw_ref, pw_ref, psc_ref, wo_ref,
                  out_ref, cs_out, ps_out, ext_a, ext_u, ycat, *, nb, tt, start):
    ti = pl.program_id(1)
    rows = nb * tt

    @pl.when(ti == 0)
    def _():
        ext_a[:, 0:CONV_HALO, :] = cs0_ref[...]
        ext_u[:, 0:POOL_HALO, :] = ps0_ref[...]

    pa = pa_ref[...].astype(_F32)
    xa, gate_b, gate_c = (pa[:, :, k * CONV_WIDTH:(k + 1) * CONV_WIDTH] for k in range(3))
    prod = gate_c * xa
    ext_a[:, CONV_HALO:CONV_HALO + tt, :] = prod
    cw = cw_ref[...]
    cu = (cw[0] * ext_a[:, CONV_HALO - 2:CONV_HALO - 2 + tt, :]
          + cw[1] * ext_a[:, CONV_HALO - 1:CONV_HALO - 1 + tt, :] + cw[2] * prod)
    ycat[:, 0:CONV_WIDTH] = (gate_b * cu).reshape(rows, CONV_WIDTH).astype(_BF16)
    ycat[:, CONV_WIDTH:CONV_WIDTH + ATT_WIDTH] = (
        yb_ref[...].astype(_F32).reshape(rows, ATT_WIDTH).astype(_BF16))

    u = pu_ref[...].astype(_F32)
    ext_u[:, POOL_HALO:POOL_HALO + tt, :] = u
    pos = start + ti * tt + lax.broadcasted_iota(jnp.int32, (1, tt, 1), 1)
    for gi, w in enumerate(POOL_WINDOWS):
        cols = slice(gi * POOL_GROUP, (gi + 1) * POOL_GROUP)
        ug = u[:, :, cols]
        tot = ug
        for k in range(1, w):
            tot = tot + ext_u[:, POOL_HALO - k:POOL_HALO - k + tt, cols]
        cnt = jnp.minimum(w, pos + 1).astype(_F32)
        diff = (tot / cnt - ug).reshape(rows, POOL_GROUP).astype(_BF16)
        z = jnp.dot(diff, pw_ref[gi], preferred_element_type=_F32) * psc_ref[:, cols]
        c0 = CONV_WIDTH + ATT_WIDTH + gi * POOL_GROUP
        ycat[:, c0:c0 + POOL_GROUP] = z.astype(_BF16)

    mixed = jnp.dot(ycat[...], wo_ref[...], preferred_element_type=_F32)
    out_ref[...] = x_ref[...] + mixed.reshape(nb, tt, D_MODEL)

    tail_a = ext_a[:, tt:tt + CONV_HALO, :]
    tail_u = ext_u[:, tt:tt + POOL_HALO, :]
    ext_a[:, 0:CONV_HALO, :] = tail_a
    ext_u[:, 0:POOL_HALO, :] = tail_u
    cs_out[...] = tail_a
    ps_out[...] = tail_u


def _mixer(proj, yb, x, conv_state, pool_state, conv_w, pool_w, pool_scale, w_out, layer, nb, tt, start, u_tile0):
    batch, seq, _ = x.shape
    proj3 = proj.reshape(batch, seq, proj.shape[-1])
    u_blk = u_tile0 * COL_TILE // POOL_WIDTH
    const2 = lambda b, t: (0, 0)
    return pl.pallas_call(
        functools.partial(_mixer_kernel, nb=nb, tt=tt, start=start),
        grid=(batch // nb, seq // tt),
        in_specs=[pl.BlockSpec((nb, tt, 3 * CONV_WIDTH), lambda b, t: (b, t, 0)),
                  pl.BlockSpec((nb, tt, POOL_WIDTH), lambda b, t: (b, t, u_blk)),
                  pl.BlockSpec((nb, tt, ATT_WIDTH), lambda b, t: (b, t, 0)),
                  pl.BlockSpec((nb, tt, D_MODEL), lambda b, t: (b, t, 0)),
                  pl.BlockSpec((nb, CONV_HALO, CONV_WIDTH), lambda b, t: (b, 0, 0)),
                  pl.BlockSpec((nb, POOL_HALO, POOL_WIDTH), lambda b, t: (b, 0, 0)),
                  pl.BlockSpec((3, CONV_WIDTH), const2),
                  pl.BlockSpec((None, len(POOL_WINDOWS), POOL_GROUP, POOL_GROUP), lambda b, t: (layer, 0, 0, 0)),
                  pl.BlockSpec((1, POOL_WIDTH), const2),
                  pl.BlockSpec((None, D_MODEL, D_MODEL), lambda b, t: (layer, 0, 0))],
        out_specs=[pl.BlockSpec((nb, tt, D_MODEL), lambda b, t: (b, t, 0)),
                   pl.BlockSpec((nb, CONV_HALO, CONV_WIDTH), lambda b, t: (b, 0, 0)),
                   pl.BlockSpec((nb, POOL_HALO, POOL_WIDTH), lambda b, t: (b, 0, 0))],
        out_shape=[jax.ShapeDtypeStruct((batch, seq, D_MODEL), _F32),
                   jax.ShapeDtypeStruct((batch, CONV_HALO, CONV_WIDTH), _F32),
                   jax.ShapeDtypeStruct((batch, POOL_HALO, POOL_WIDTH), _F32)],
        scratch_shapes=[pltpu.VMEM((nb, tt + CONV_HALO, CONV_WIDTH), _F32),
                        pltpu.VMEM((nb, tt + POOL_HALO, POOL_WIDTH), _F32),
                        pltpu.VMEM((nb * tt, D_MODEL), _BF16)],
        compiler_params=pltpu.CompilerParams(dimension_semantics=("parallel", "arbitrary"),
                                             vmem_limit_bytes=VMEM_LIMIT),
        name="mixer",
    )(proj3, proj3, yb, x, conv_state, pool_state, conv_w, pool_w, pool_scale.reshape(1, POOL_WIDTH), w_out)


def _ffn_kernel(x_ref, n2_ref, wg_ref, wu_ref, cw_ref, wd_ref, st0_ref, fn_ref,
                out_ref, st_out, h_s, ext_s, carry_s, *, nb, tt, final):
    ti = pl.program_id(1)
    f = pl.program_id(2)
    rows = nb * tt

    @pl.when(f == 0)
    def _():
        x = x_ref[...]
        h_s[...] = _rms_norm(x, n2_ref[...]).reshape(rows, D_MODEL).astype(_BF16)
        out_ref[...] = x

    def tile(width):
        cols = slice(0, width)

        @pl.when(ti == 0)
        def _():
            ext_s[:, 0:CONV_HALO, cols] = st0_ref[:, :, cols]

        @pl.when(ti > 0)
        def _():
            ext_s[:, 0:CONV_HALO, cols] = carry_s[f, :, :, cols]

        h = h_s[...]
        gate = jnp.dot(h, wg_ref[:, cols], preferred_element_type=_F32).reshape(nb, tt, width)
        up = jnp.dot(h, wu_ref[:, cols], preferred_element_type=_F32).reshape(nb, tt, width)
        ext_s[:, CONV_HALO:CONV_HALO + tt, cols] = gate
        cw = cw_ref[:, cols]
        gc = (cw[0] * ext_s[:, CONV_HALO - 2:CONV_HALO - 2 + tt, cols]
              + cw[1] * ext_s[:, CONV_HALO - 1:CONV_HALO - 1 + tt, cols] + cw[2] * gate)
        tail = ext_s[:, tt:tt + CONV_HALO, cols]
        carry_s[f, :, :, cols] = tail
        st_out[:, :, cols] = tail
        act = (jax.nn.silu(gc) * up).reshape(rows, width).astype(_BF16)
        out_ref[...] += jnp.dot(act, wd_ref[cols, :], preferred_element_type=_F32).reshape(nb, tt, D_MODEL)

    @pl.when(f < FF_TILES - 1)
    def _():
        tile(FF_TILE)

    @pl.when(f == FF_TILES - 1)
    def _():
        tile(FF_LAST)
        if final:
            out_ref[...] = _rms_norm(out_ref[...], fn_ref[...])


def _ffn(x, norm2, w_gate, w_up, conv_w, w_down, state, final_norm, layer, nb, tt, final):
    batch, seq, _ = x.shape
    vec = lambda b, t, f: (0, 0)
    return pl.pallas_call(
        functools.partial(_ffn_kernel, nb=nb, tt=tt, final=final),
        grid=(batch // nb, seq // tt, FF_TILES),
        in_specs=[pl.BlockSpec((nb, tt, D_MODEL), lambda b, t, f: (b, t, 0)),
                  pl.BlockSpec((1, D_MODEL), vec),
                  pl.BlockSpec((None, D_MODEL, FF_TILE), lambda b, t, f: (layer, 0, f)),
                  pl.BlockSpec((None, D_MODEL, FF_TILE), lambda b, t, f: (layer, 0, f)),
                  pl.BlockSpec((3, FF_TILE), lambda b, t, f: (0, f)),
                  pl.BlockSpec((None, FF_TILE, D_MODEL), lambda b, t, f: (layer, f, 0)),
                  pl.BlockSpec((nb, CONV_HALO, FF_TILE), lambda b, t, f: (b, 0, f)),
                  pl.BlockSpec((1, D_MODEL), vec)],
        out_specs=[pl.BlockSpec((nb, tt, D_MODEL), lambda b, t, f: (b, t, 0)),
                   pl.BlockSpec((nb, None, CONV_HALO, FF_TILE), lambda b, t, f: (b, t, 0, f))],
        out_shape=[jax.ShapeDtypeStruct((batch, seq, D_MODEL), _F32),
                   jax.ShapeDtypeStruct((batch, seq // tt, CONV_HALO, D_FF), _F32)],
        scratch_shapes=[pltpu.VMEM((nb * tt, D_MODEL), _BF16),
                        pltpu.VMEM((nb, tt + CONV_HALO, FF_TILE), _F32),
                        pltpu.VMEM((FF_TILES, nb, CONV_HALO, FF_TILE), _F32)],
        compiler_params=pltpu.CompilerParams(dimension_semantics=("parallel", "arbitrary", "arbitrary"),
                                             vmem_limit_bytes=VMEM_LIMIT),
        name="ffn",
    )(x, norm2.reshape(1, D_MODEL), w_gate, w_up, conv_w, w_down, state, final_norm.reshape(1, D_MODEL))


def _pad_rows_front(a, rows):
    return jnp.pad(a, ((0, 0), (rows - a.shape[1], 0), (0, 0)))


def kernel(x_prompt, x_sample, cache_kv_w128, cache_kv_w512, cache_kv_w2048, state_conv_a, state_pool, state_ffn_conv, norm1, w_in, conv_a_w, w_out, pool_w, pool_scale, norm2, w_gate, w_up, ffn_conv_w, w_down, final_norm):
    bp, sp, _ = x_prompt.shape
    bs, ss, _ = x_sample.shape
    caches = tuple(c.reshape(DEPTH, bs, win * KV_ROWS, HEAD_DIM)
                   for c, (win, _) in zip((cache_kv_w128, cache_kv_w512, cache_kv_w2048), DIL_GROUPS))
    zeros_conv = jnp.zeros((bp, CONV_HALO, CONV_WIDTH), _F32)
    zeros_pool = jnp.zeros((bp, POOL_HALO, POOL_WIDTH), _F32)
    zeros_ffn = jnp.zeros((bp, CONV_HALO, D_FF), _F32)
    tm, tt = 1024, 512
    w_in_b, w_out_b, pool_w_b = w_in.astype(_BF16), w_out.astype(_BF16), pool_w.astype(_BF16)
    w_gate_b, w_up_b, w_down_b = w_gate.astype(_BF16), w_up.astype(_BF16), w_down.astype(_BF16)
    xp, xs = x_prompt, x_sample
    kv_p = kv_s = None
    small_p, small_s = [], []
    for l in range(DEPTH):
        final = l == DEPTH - 1

        nat, qkv1, qkv2, *kv_p = _inproj_prompt(xp.reshape(bp * sp, D_MODEL), norm1[l], w_in_b, bp, sp, tm, l, kv_p)
        yb = _attn_prompt(nat, qkv1, qkv2, bp, sp)
        xp, conv_o, pool_o = _mixer(nat, yb, xp, zeros_conv, zeros_pool, conv_a_w[l], pool_w_b, pool_scale[l], w_out_b,
                                    l, 1, tt, 0, NAT_TILES - 2)
        xp, ffn_o = _ffn(xp, norm2[l], w_gate_b, w_up_b, ffn_conv_w[l], w_down_b, zeros_ffn, final_norm,
                         l, 1, tt, final)
        small_p.append((conv_o[:, CONV_HALO - 2:], pool_o[:, POOL_HALO - POOL_BUF:], ffn_o[:, -1, CONV_HALO - 2:]))

        proj, *kv_new = _inproj_sample(xs.reshape(bs * ss, D_MODEL), norm1[l], w_in_b, l)
        yb, *kv_s = _attn_sample(proj, kv_new, caches, bs, ss, l, kv_s)
        xs, conv_o, pool_o = _mixer(proj, yb, xs, _pad_rows_front(state_conv_a[l], CONV_HALO),
                                    _pad_rows_front(state_pool[l], POOL_HALO), conv_a_w[l], pool_w_b, pool_scale[l],
                                    w_out_b, l, bs, ss, PAST_LEN, U_TILE0)
        xs, ffn_o = _ffn(xs, norm2[l], w_gate_b, w_up_b, ffn_conv_w[l], w_down_b,
                         _pad_rows_front(state_ffn_conv[l], CONV_HALO), final_norm, l, bs, ss, final)
        small_s.append((conv_o[:, CONV_HALO - 2:], pool_o[:, POOL_HALO - POOL_BUF:], ffn_o[:, -1, CONV_HALO - 2:]))

    out = [xp, xs]
    for g, (win, _) in enumerate(DIL_GROUPS):
        out += [kv_p[g].reshape(DEPTH, bp, win, 2, HEADS, HEAD_DIM), kv_s[g].reshape(DEPTH, bs, win, 2, HEADS, HEAD_DIM)]
    for i in range(3):
        out += [jnp.stack([s[i] for s in small_p]), jnp.stack([s[i] for s in small_s])]
    return tuple(out)
```

```python
import functools

import numpy as np
import jax
import jax.numpy as jnp
from jax import lax
from jax.experimental import pallas as pl
from jax.experimental.pallas import tpu as pltpu

D_MODEL = 2048
DEPTH = 2
PAST_LEN = 16384
CONV_WIDTH = 512
ATT_WIDTH = 512
HEADS = 4
HEAD_DIM = 128
DIL_GROUPS = ((128, 1), (512, 4), (2048, 16))
POOL_WIDTH = 1024
POOL_WINDOWS = (2, 4, 8, 16)
POOL_GROUP = 256
POOL_BUF = 15
IN_COLS = 7168
D_FF = 5504
RMS_EPS = 1e-6
ATT_SCALE = HEAD_DIM ** -0.5

COL_TILE = 512
IN_TILES = IN_COLS // COL_TILE
Q_TILE = (3, 6, 9)
U_TILE0 = 12
NAT_TILES = 8
KV_ROWS = 2 * HEADS
SUB_ROWS = 512
FF_TILE = 512
FF_TILES = -(-D_FF // FF_TILE)
FF_LAST = D_FF - (FF_TILES - 1) * FF_TILE
FFN_SUB_ROWS = 256
CONV_HALO = 8
POOL_HALO = 16
VMEM_LIMIT = 56 * 1024 * 1024

_BF16 = jnp.bfloat16
_F32 = jnp.float32


def _alibi_slopes():
    h = np.arange(1, 3 * HEADS + 1, dtype=np.float32)
    return [float(v) for v in np.power(np.float32(2.0), -8.0 * h / (3 * HEADS))]


SLOPES = _alibi_slopes()


def _rms_norm(x, g):
    return (x * lax.rsqrt(jnp.mean(x * x, axis=-1, keepdims=True) + RMS_EPS)) * g


def _head_cols(h, base=0):
    return slice(base + h * HEAD_DIM, base + (h + 1) * HEAD_DIM)


def _inproj_sample_kernel(x_ref, g_ref, w_ref, proj_ref, kv0_ref, kv1_ref, kv2_ref, wt_ref, h_s):
    j = pl.program_id(1)

    @pl.when(j == 0)
    def _():
        h_s[...] = _rms_norm(x_ref[...], g_ref[...]).astype(_BF16)

    w = w_ref[...].astype(_BF16)
    wt_ref[...] = w
    acc = jnp.dot(h_s[...], w, preferred_element_type=_F32)
    proj_ref[...] = acc.astype(_BF16)
    for g, ref in enumerate((kv0_ref, kv1_ref, kv2_ref)):
        @pl.when((j == Q_TILE[g] + 1) | (j == Q_TILE[g] + 2))
        def _(ref=ref):
            ref[...] = acc


def _inproj_sample(x2d, g, w, layer):
    m = x2d.shape[0]

    def kv_spec(grp):
        return pl.BlockSpec((m, COL_TILE), lambda i, j: (i, jnp.clip(j - Q_TILE[grp] - 1, 0, 1)))

    return pl.pallas_call(
        _inproj_sample_kernel,
        grid=(1, IN_TILES),
        in_specs=[pl.BlockSpec((m, D_MODEL), lambda i, j: (i, 0)),
                  pl.BlockSpec((1, D_MODEL), lambda i, j: (0, 0)),
                  pl.BlockSpec((None, D_MODEL, COL_TILE), lambda i, j: (layer, 0, j))],
        out_specs=[pl.BlockSpec((m, COL_TILE), lambda i, j: (i, j)), kv_spec(0), kv_spec(1), kv_spec(2),
                   pl.BlockSpec((None, D_MODEL, COL_TILE), lambda i, j: (j, 0, 0))],
        out_shape=[jax.ShapeDtypeStruct((m, IN_COLS), _BF16)] +
                  [jax.ShapeDtypeStruct((m, 2 * ATT_WIDTH), _F32)] * 3 +
                  [jax.ShapeDtypeStruct((IN_TILES, D_MODEL, COL_TILE), _BF16)],
        scratch_shapes=[pltpu.VMEM((m, D_MODEL), _BF16)],
        compiler_params=pltpu.CompilerParams(dimension_semantics=("parallel", "arbitrary"),
                                             vmem_limit_bytes=VMEM_LIMIT),
        name="inproj_sample",
    )(x2d, g.reshape(1, D_MODEL), w)


def _inproj_prompt_kernel(*refs, tm, aliased):
    n_in = 6 if aliased else 3
    x_ref, g_ref, w_ref = refs[:3]
    nat_ref, g1_ref, g2_ref, kv0_ref, kv1_ref, kv2_ref, h_s, de_s = refs[n_in:]
    kv_refs = (kv0_ref, kv1_ref, kv2_ref)
    res_refs = (None, g1_ref, g2_ref)
    j = pl.program_id(1)

    @pl.when(j == 0)
    def _():
        h_s[...] = _rms_norm(x_ref[...], g_ref[...]).astype(_BF16)

    def column_tile(grp, part):
        w = w_ref[...]
        for c in range(tm // SUB_ROWS):
            r0 = c * SUB_ROWS
            acc = jnp.dot(h_s[r0:r0 + SUB_ROWS, :], w, preferred_element_type=_F32)
            dil = 1 if grp is None else DIL_GROUPS[grp][1]
            if dil == 1:
                nat_ref[r0:r0 + SUB_ROWS, :] = acc.astype(_BF16)
            else:
                for h in range(HEADS):
                    de_s[c, h] = acc[:, _head_cols(h)]
                n = SUB_ROWS // dil
                for res in range(dil):
                    for h in range(HEADS):
                        res_refs[grp][res, c * n:(c + 1) * n, _head_cols(h)] = (
                            de_s[c, h, pl.ds(res, n, stride=dil), :].astype(_BF16))
            if part is not None:
                keep = min(DIL_GROUPS[grp][0], tm)
                lo = max(r0, tm - keep)
                n = r0 + SUB_ROWS - lo
                if n > 0:
                    for h in range(HEADS):
                        dst = pl.ds((lo - (tm - keep)) * KV_ROWS + part * HEADS + h, n, stride=KV_ROWS)
                        kv_refs[grp][dst, :] = acc[lo - r0:lo - r0 + n, _head_cols(h)]

    @pl.when((j <= Q_TILE[0]) | (j >= U_TILE0))
    def _():
        column_tile(None, None)

    for grp in range(len(DIL_GROUPS)):
        for t, part in enumerate((None, 0, 1)):
            if grp == 0 and part is None:
                continue

            @pl.when(j == Q_TILE[grp] + t)
            def _(grp=grp, part=part):
                column_tile(grp, part)


def _inproj_prompt(x2d, g, w, batch, seq, tm, layer, prev_kv):
    m = x2d.shape[0]
    tps = seq // tm
    aliased = prev_kv is not None
    (w0, _), (w1, d1), (w2, d2) = DIL_GROUPS
    assert w2 == seq and w0 <= tm and w1 <= tm and tm % SUB_ROWS == 0

    in_specs = [pl.BlockSpec((tm, D_MODEL), lambda i, j: (i, 0)),
                pl.BlockSpec((1, D_MODEL), lambda i, j: (0, 0)),
                pl.BlockSpec((None, D_MODEL, COL_TILE), lambda i, j: (j, 0, 0))]
    args = [x2d, g.reshape(1, D_MODEL), w]
    if aliased:
        in_specs += [pl.BlockSpec(memory_space=pl.ANY)] * 3
        args += list(prev_kv)

    def res_spec(grp, dil):
        return pl.BlockSpec((None, dil, tm // dil, COL_TILE),
                            lambda i, j: (i // tps, 0, i % tps, jnp.clip(j - Q_TILE[grp], 0, 2)))

    out_specs = [
        pl.BlockSpec((tm, COL_TILE), lambda i, j: (i, jnp.clip(j, 0, Q_TILE[1] - 1) + jnp.clip(j - U_TILE0 + 1, 0, 2))),
        res_spec(1, d1), res_spec(2, d2),
        pl.BlockSpec((None, None, w0 * KV_ROWS, HEAD_DIM), lambda i, j: (layer, i // tps, 0, 0)),
        pl.BlockSpec((None, None, w1 * KV_ROWS, HEAD_DIM), lambda i, j: (layer, i // tps, 0, 0)),
        pl.BlockSpec((None, None, tm * KV_ROWS, HEAD_DIM), lambda i, j: (layer, i // tps, i % tps, 0)),
    ]
    out_shape = [
        jax.ShapeDtypeStruct((m, NAT_TILES * COL_TILE), _BF16),
        jax.ShapeDtypeStruct((batch, d1, seq // d1, 3 * ATT_WIDTH), _BF16),
        jax.ShapeDtypeStruct((batch, d2, seq // d2, 3 * ATT_WIDTH), _BF16),
    ] + [jax.ShapeDtypeStruct((DEPTH, batch, win * KV_ROWS, HEAD_DIM), _F32) for win, _ in DIL_GROUPS]
    return pl.pallas_call(
        functools.partial(_inproj_prompt_kernel, tm=tm, aliased=aliased),
        grid=(m // tm, IN_TILES),
        in_specs=in_specs, out_specs=out_specs, out_shape=out_shape,
        scratch_shapes=[pltpu.VMEM((tm, D_MODEL), _BF16),
                        pltpu.VMEM((tm // SUB_ROWS, HEADS, SUB_ROWS, HEAD_DIM), _F32)],
        input_output_aliases={3: 3, 4: 4, 5: 5} if aliased else {},
        compiler_params=pltpu.CompilerParams(dimension_semantics=("arbitrary", "arbitrary"),
                                             vmem_limit_bytes=VMEM_LIMIT),
        name="inproj_prompt",
    )(*args)


QB = 128
G0_UNROLL = 3
G2_RES_PER_STEP = 4


def _attn_prompt_kernel(q0, k0, v0, q1, k1, v1, q2, k2, v2, o_ref, acc_s, m_s, l_s, shuf_s, *, seq):
    step = pl.program_id(1)
    n1 = DIL_GROUPS[1][1]
    n2_steps = DIL_GROUPS[2][1] // G2_RES_PER_STEP

    row0 = lax.broadcasted_iota(jnp.int32, (QB, QB), 0)
    col0 = lax.broadcasted_iota(jnp.int32, (QB, QB), 1)
    dist_first = jnp.where(col0 <= row0, (row0 - col0).astype(_F32), jnp.inf)
    row1 = lax.broadcasted_iota(jnp.int32, (QB, 2 * QB), 0)
    col1 = lax.broadcasted_iota(jnp.int32, (QB, 2 * QB), 1)
    d1 = QB + row1 - col1
    dist_next = jnp.where((d1 >= 0) & (d1 <= QB), d1.astype(_F32), jnp.inf)
    ones = jnp.ones((2 * QB, HEAD_DIM), _BF16)

    seg = seq // n1
    chunk = QB // n1
    assert n1 == 4 and DIL_GROUPS[2][1] == 16

    def blocks(specs):
        tiles = []
        for slot, (qr, kr, vr, grp, res, i, is_first_block) in enumerate(specs):
            dil = DIL_GROUPS[grp][1]
            for h in range(HEADS):
                hs = _head_cols(h)
                slope_step = SLOPES[grp * HEADS + h] * dil
                if is_first_block:
                    q, k, v = qr[0:QB, hs], kr[0:QB, hs], vr[0:QB, hs]
                    bias, one = dist_first * slope_step, ones[0:QB]
                else:
                    qs = pl.ds(pl.multiple_of(i * QB, QB), QB)
                    ks = pl.ds(pl.multiple_of(i * QB - QB, QB), 2 * QB)
                    q, k, v = qr[qs, hs], kr[ks, hs], vr[ks, hs]
                    bias, one = dist_next * slope_step, ones
                tiles.append((q, k, jnp.concatenate([v, one], axis=1), bias, grp, res, i, is_first_block, slot, h))
        scores = [lax.dot_general(t[0], t[1], (((1,), (1,)), ((), ())), preferred_element_type=_F32) for t in tiles]
        probs, maxes = [], []
        for t, s in zip(tiles, scores):
            s = s * ATT_SCALE - t[3]
            m = jnp.max(s, axis=1, keepdims=True)
            probs.append(jnp.exp(s - m).astype(_BF16))
            maxes.append(jnp.broadcast_to(m, (QB, HEAD_DIM)))
        outs = [jnp.dot(p, t[2], preferred_element_type=_F32) for t, p in zip(tiles, probs)]
        for (_, _, _, _, grp, res, i, is_first_block, slot, h), m, acc_l in zip(tiles, maxes, outs):
            acc, l = acc_l[:, :HEAD_DIM], acc_l[:, HEAD_DIM:]
            if grp == 0:
                for a, (val, ref) in enumerate(((acc, acc_s), (m, m_s), (l, l_s))):
                    tmp = shuf_s.at[(slot * HEADS + h) * 3 + a]
                    tmp[...] = val
                    for r4 in range(n1):
                        dst = r4 * seg + i * chunk
                        dst = pl.ds(dst if is_first_block else pl.multiple_of(dst, chunk), chunk)
                        ref[h, dst, :] = tmp[pl.ds(r4, chunk, stride=n1), :]
            else:
                if grp == 1:
                    rows = pl.ds(pl.multiple_of(res * seg + i * QB, QB), QB)
                else:
                    rows = pl.ds((res & (n1 - 1)) * seg + (res >> 2), QB, stride=n1)
                m_old, l_old, a_old = m_s[h, rows, :], l_s[h, rows, :], acc_s[h, rows, :]
                m_new = jnp.maximum(m_old, m)
                w_old = jnp.exp(m_old - m_new)
                w_new = jnp.exp(m - m_new)
                acc_s[h, rows, :] = a_old * w_old + acc * w_new
                l_s[h, rows, :] = l_old * w_old + l * w_new
                m_s[h, rows, :] = m_new

    @pl.when(step == 0)
    def _():
        n_blk = seq // QB
        assert (n_blk - 1) % G0_UNROLL == 0
        blocks([(q0, k0, v0, 0, 0, 0, True)])

        def body(it, carry):
            blocks([(q0, k0, v0, 0, 0, 1 + it * G0_UNROLL + u, False) for u in range(G0_UNROLL)])
            return carry
        lax.fori_loop(0, (n_blk - 1) // G0_UNROLL, body, 0)

    @pl.when((step >= 1) & (step <= n1))
    def _():
        blocks([(q1, k1, v1, 1, step - 1, i, i == 0) for i in range(seg // QB)])

    @pl.when(step > n1)
    def _():
        assert seq // DIL_GROUPS[2][1] == QB
        blocks([(q2.at[r], k2.at[r], v2.at[r], 2, (step - 1 - n1) * G2_RES_PER_STEP + r, 0, True)
                for r in range(G2_RES_PER_STEP)])

    @pl.when(step == n1 + n2_steps)
    def _():
        def body(c, carry):
            for h in range(HEADS):
                tmp = shuf_s.at[h]
                for r4 in range(n1):
                    src = pl.ds(pl.multiple_of(r4 * seg + c * chunk, chunk), chunk)
                    tmp[pl.ds(r4, chunk, stride=n1), :] = acc_s[h, src, :] / l_s[h, src, :]
                o_ref[pl.ds(pl.multiple_of(c * QB, QB), QB), _head_cols(h)] = tmp[...].astype(_BF16)
            return carry
        lax.fori_loop(0, seq // QB, body, 0)


def _attn_prompt(nat, qkv1, qkv2, batch, seq):
    n1, n2 = DIL_GROUPS[1][1], DIL_GROUPS[2][1]
    n2_steps = n2 // G2_RES_PER_STEP
    nat3 = nat.reshape(batch, seq, NAT_TILES * COL_TILE)
    in_specs = [pl.BlockSpec((None, seq, COL_TILE), lambda b, s, t=t: (b, 0, Q_TILE[0] + t)) for t in range(3)]
    in_specs += [pl.BlockSpec((None, None, seq // n1, COL_TILE),
                              lambda b, s, t=t: (b, jnp.clip(s - 1, 0, n1 - 1), 0, t)) for t in range(3)]
    in_specs += [pl.BlockSpec((None, G2_RES_PER_STEP, seq // n2, COL_TILE),
                              lambda b, s, t=t: (b, jnp.clip(s - 1 - n1, 0, n2_steps - 1), 0, t)) for t in range(3)]
    args = [nat3] * 3 + [qkv1] * 3 + [qkv2] * 3
    return pl.pallas_call(
        functools.partial(_attn_prompt_kernel, seq=seq),
        grid=(batch, 1 + n1 + n2_steps),
        in_specs=in_specs,
        out_specs=pl.BlockSpec((None, seq, ATT_WIDTH), lambda b, s: (b, 0, 0)),
        out_shape=jax.ShapeDtypeStruct((batch, seq, ATT_WIDTH), _BF16),
        scratch_shapes=[pltpu.VMEM((HEADS, seq, HEAD_DIM), _F32)] * 3
                       + [pltpu.VMEM((G0_UNROLL * HEADS * 3, QB, HEAD_DIM), _F32)],
        compiler_params=pltpu.CompilerParams(dimension_semantics=("parallel", "arbitrary"),
                                             vmem_limit_bytes=VMEM_LIMIT),
        name="attn_prompt",
    )(*args)


def _attn_sample_kernel(*refs, tdec, aliased):
    n_in = 10 if aliased else 7
    proj_ref, n0, n1, n2, c0, c1, c2 = refs[:7]
    yb_ref, o0, o1, o2 = refs[n_in:]
    news, caches, outs = (n0, n1, n2), (c0, c1, c2), (o0, o1, o2)
    for h in range(HEADS):
        pieces = []
        for grp, (win, dil) in enumerate(DIL_GROUPS):
            slope = SLOPES[grp * HEADS + h]
            q = proj_ref[:, _head_cols(h, Q_TILE[grp] * COL_TILE)]
            k_new = news[grp][:, _head_cols(h)]
            v_new = news[grp][:, _head_cols(h, ATT_WIDTH)]
            k_old = caches[grp][pl.ds(h, win, stride=KV_ROWS), :]
            v_old = caches[grp][pl.ds(HEADS + h, win, stride=KV_ROWS), :]
            for k, v, n_keys, base in ((k_old, v_old, win, win), (k_new, v_new, tdec, 0)):
                s = lax.dot_general(q, k.astype(_BF16), (((1,), (1,)), ((), ())), preferred_element_type=_F32)
                t = lax.broadcasted_iota(jnp.int32, (tdec, n_keys), 0)
                c = lax.broadcasted_iota(jnp.int32, (tdec, n_keys), 1)
                dist = base + t - c
                ok = (dist >= 0) & (dist <= win) & ((dist & (dil - 1)) == 0)
                s = jnp.where(ok, s * ATT_SCALE - slope * dist.astype(_F32), -jnp.inf)
                pieces.append((s, v.astype(_BF16)))
        m = functools.reduce(jnp.maximum, [jnp.max(s, axis=1, keepdims=True) for s, _ in pieces])
        l = jnp.zeros((tdec, 1), _F32)
        acc = jnp.zeros((tdec, HEAD_DIM), _F32)
        for s, v in pieces:
            p = jnp.exp(s - m)
            l = l + jnp.sum(p, axis=1, keepdims=True)
            acc = acc + jnp.dot(p.astype(_BF16), v, preferred_element_type=_F32)
        yb_ref[:, _head_cols(h)] = (acc / l).astype(_BF16)
    for grp, (win, _) in enumerate(DIL_GROUPS):
        kept = (win - tdec) * KV_ROWS
        outs[grp][0:kept, :] = caches[grp][tdec * KV_ROWS:win * KV_ROWS, :]
        for part in range(2):
            for h in range(HEADS):
                outs[grp][pl.ds(kept + part * HEADS + h, tdec, stride=KV_ROWS), :] = (
                    news[grp][:, _head_cols(h, part * ATT_WIDTH)])


def _attn_sample(proj, kv_new, caches, batch, tdec, layer, prev_out):
    aliased = prev_out is not None
    in_specs = [pl.BlockSpec((None, tdec, IN_COLS), lambda b: (b, 0, 0))]
    in_specs += [pl.BlockSpec((None, tdec, 2 * ATT_WIDTH), lambda b: (b, 0, 0))] * 3
    cache_specs = [pl.BlockSpec((None, None, win * KV_ROWS, HEAD_DIM), lambda b: (layer, b, 0, 0)) for win, _ in DIL_GROUPS]
    in_specs += cache_specs
    args = [proj.reshape(batch, tdec, IN_COLS)] + [k.reshape(batch, tdec, 2 * ATT_WIDTH) for k in kv_new] + list(caches)
    if aliased:
        in_specs += [pl.BlockSpec(memory_space=pl.ANY)] * 3
        args += list(prev_out)
    out_specs = [pl.BlockSpec((None, tdec, ATT_WIDTH), lambda b: (b, 0, 0))] + cache_specs
    out_shape = [jax.ShapeDtypeStruct((batch, tdec, ATT_WIDTH), _BF16)]
    out_shape += [jax.ShapeDtypeStruct((DEPTH, batch, win * KV_ROWS, HEAD_DIM), _F32) for win, _ in DIL_GROUPS]
    return pl.pallas_call(
        functools.partial(_attn_sample_kernel, tdec=tdec, aliased=aliased),
        grid=(batch,),
        in_specs=in_specs, out_specs=out_specs, out_shape=out_shape,
        input_output_aliases={7: 1, 8: 2, 9: 3} if aliased else {},
        compiler_params=pltpu.CompilerParams(dimension_semantics=("arbitrary",), vmem_limit_bytes=VMEM_LIMIT),
        name="attn_sample",
    )(*args)


def _mixer_kernel(pa_ref, pu_ref, yb_ref, x_ref, cs0_ref, ps0_ref, cw_ref, pw_ref, psc_ref, wo_ref,
                  out_ref, cs_out, ps_out, ext_a, ext_u, ycat, *, nb, tt, start):
    ti = pl.program_id(1)
    rows = nb * tt

    @pl.when(ti == 0)
    def _():
        ext_a[:, 0:CONV_HALO, :] = cs0_ref[...]
        ext_u[:, 0:POOL_HALO, :] = ps0_ref[...]

    pa = pa_ref[...].astype(_F32)
    xa, gate_b, gate_c = (pa[:, :, k * CONV_WIDTH:(k + 1) * CONV_WIDTH] for k in range(3))
    prod = gate_c * xa
    ext_a[:, CONV_HALO:CONV_HALO + tt, :] = prod
    cw = cw_ref[...]
    cu = (cw[0] * ext_a[:, CONV_HALO - 2:CONV_HALO - 2 + tt, :]
          + cw[1] * ext_a[:, CONV_HALO - 1:CONV_HALO - 1 + tt, :] + cw[2] * prod)
    ycat[:, 0:CONV_WIDTH] = (gate_b * cu).reshape(rows, CONV_WIDTH).astype(_BF16)
    ycat[:, CONV_WIDTH:CONV_WIDTH + ATT_WIDTH] = (
        yb_ref[...].astype(_F32).reshape(rows, ATT_WIDTH).astype(_BF16))

    u = pu_ref[...].astype(_F32)
    ext_u[:, POOL_HALO:POOL_HALO + tt, :] = u
    pos = start + ti * tt + lax.broadcasted_iota(jnp.int32, (1, tt, 1), 1)
    for gi, w in enumerate(POOL_WINDOWS):
        cols = slice(gi * POOL_GROUP, (gi + 1) * POOL_GROUP)
        ug = u[:, :, cols]
        tot = ug
        for k in range(1, w):
            tot = tot + ext_u[:, POOL_HALO - k:POOL_HALO - k + tt, cols]
        cnt = jnp.minimum(w, pos + 1).astype(_F32)
        diff = (tot / cnt - ug).reshape(rows, POOL_GROUP).astype(_BF16)
        z = jnp.dot(diff, pw_ref[gi], preferred_element_type=_F32) * psc_ref[:, cols]
        c0 = CONV_WIDTH + ATT_WIDTH + gi * POOL_GROUP
        ycat[:, c0:c0 + POOL_GROUP] = z.astype(_BF16)

    mixed = jnp.dot(ycat[...], wo_ref[...], preferred_element_type=_F32)
    out_ref[...] = x_ref[...] + mixed.reshape(nb, tt, D_MODEL)

    tail_a = ext_a[:, tt:tt + CONV_HALO, :]
    tail_u = ext_u[:, tt:tt + POOL_HALO, :]
    ext_a[:, 0:CONV_HALO, :] = tail_a
    ext_u[:, 0:POOL_HALO, :] = tail_u
    cs_out[...] = tail_a
    ps_out[...] = tail_u


def _mixer(proj, yb, x, conv_state, pool_state, conv_w, pool_w, pool_scale, w_out, layer, nb, tt, start, u_tile0):
    batch, seq, _ = x.shape
    proj3 = proj.reshape(batch, seq, proj.shape[-1])
    u_blk = u_tile0 * COL_TILE // POOL_WIDTH
    const2 = lambda b, t: (0, 0)
    return pl.pallas_call(
        functools.partial(_mixer_kernel, nb=nb, tt=tt, start=start),
        grid=(batch // nb, seq // tt),
        in_specs=[pl.BlockSpec((nb, tt, 3 * CONV_WIDTH), lambda b, t: (b, t, 0)),
                  pl.BlockSpec((nb, tt, POOL_WIDTH), lambda b, t: (b, t, u_blk)),
                  pl.BlockSpec((nb, tt, ATT_WIDTH), lambda b, t: (b, t, 0)),
                  pl.BlockSpec((nb, tt, D_MODEL), lambda b, t: (b, t, 0)),
                  pl.BlockSpec((nb, CONV_HALO, CONV_WIDTH), lambda b, t: (b, 0, 0)),
                  pl.BlockSpec((nb, POOL_HALO, POOL_WIDTH), lambda b, t: (b, 0, 0)),
                  pl.BlockSpec((3, CONV_WIDTH), const2),
                  pl.BlockSpec((None, len(POOL_WINDOWS), POOL_GROUP, POOL_GROUP), lambda b, t: (layer, 0, 0, 0)),
                  pl.BlockSpec((1, POOL_WIDTH), const2),
                  pl.BlockSpec((None, D_MODEL, D_MODEL), lambda b, t: (layer, 0, 0))],
        out_specs=[pl.BlockSpec((nb, tt, D_MODEL), lambda b, t: (b, t, 0)),
                   pl.BlockSpec((nb, CONV_HALO, CONV_WIDTH), lambda b, t: (b, 0, 0)),
                   pl.BlockSpec((nb, POOL_HALO, POOL_WIDTH), lambda b, t: (b, 0, 0))],
        out_shape=[jax.ShapeDtypeStruct((batch, seq, D_MODEL), _F32),
                   jax.ShapeDtypeStruct((batch, CONV_HALO, CONV_WIDTH), _F32),
                   jax.ShapeDtypeStruct((batch, POOL_HALO, POOL_WIDTH), _F32)],
        scratch_shapes=[pltpu.VMEM((nb, tt + CONV_HALO, CONV_WIDTH), _F32),
                        pltpu.VMEM((nb, tt + POOL_HALO, POOL_WIDTH), _F32),
                        pltpu.VMEM((nb * tt, D_MODEL), _BF16)],
        compiler_params=pltpu.CompilerParams(dimension_semantics=("parallel", "arbitrary"),
                                             vmem_limit_bytes=VMEM_LIMIT),
        name="mixer",
    )(proj3, proj3, yb, x, conv_state, pool_state, conv_w, pool_w, pool_scale.reshape(1, POOL_WIDTH), w_out)


def _ffn_sample_kernel(x_ref, n2_ref, wg_ref, wu_ref, cw_ref, wd_ref, st0_ref, fn_ref,
                       out_ref, st_out, wgt_ref, wut_ref, wdt_ref, h_s, ext_s, *, nb, tt, final):
    f = pl.program_id(0)
    rows = nb * tt

    @pl.when(f == 0)
    def _():
        x = x_ref[...]
        h_s[...] = _rms_norm(x, n2_ref[...]).reshape(rows, D_MODEL).astype(_BF16)
        out_ref[...] = x

    def tile(width):
        cols = slice(0, width)
        wg, wu, wd = wg_ref[:, cols].astype(_BF16), wu_ref[:, cols].astype(_BF16), wd_ref[cols, :].astype(_BF16)
        wgt_ref[:, cols], wut_ref[:, cols], wdt_ref[cols, :] = wg, wu, wd
        if width < FF_TILE:
            wgt_ref[:, width:] = jnp.zeros((D_MODEL, FF_TILE - width), _BF16)
            wut_ref[:, width:] = jnp.zeros((D_MODEL, FF_TILE - width), _BF16)
            wdt_ref[width:, :] = jnp.zeros((FF_TILE - width, D_MODEL), _BF16)
        h = h_s[...]
        gate = jnp.dot(h, wg, preferred_element_type=_F32).reshape(nb, tt, width)
        up = jnp.dot(h, wu, preferred_element_type=_F32).reshape(nb, tt, width)
        ext_s[:, 0:CONV_HALO, cols] = st0_ref[:, :, cols]
        ext_s[:, CONV_HALO:CONV_HALO + tt, cols] = gate
        cw = cw_ref[:, cols]
        gc = (cw[0] * ext_s[:, CONV_HALO - 2:CONV_HALO - 2 + tt, cols]
              + cw[1] * ext_s[:, CONV_HALO - 1:CONV_HALO - 1 + tt, cols] + cw[2] * gate)
        st_out[:, :, cols] = ext_s[:, tt:tt + CONV_HALO, cols]
        act = (jax.nn.silu(gc) * up).reshape(rows, width).astype(_BF16)
        out_ref[...] += jnp.dot(act, wd, preferred_element_type=_F32).reshape(nb, tt, D_MODEL)

    @pl.when(f < FF_TILES - 1)
    def _():
        tile(FF_TILE)

    @pl.when(f == FF_TILES - 1)
    def _():
        tile(FF_LAST)
        if final:
            out_ref[...] = _rms_norm(out_ref[...], fn_ref[...])


def _ffn_prompt_kernel(x_ref, n2_ref, wg_ref, wu_ref, cw_ref, wd_ref, fn_ref,
                       out_ref, st_out, h_s, ext_s, carry_s, *, tt, final):
    ti = pl.program_id(1)
    f = pl.program_id(2)

    @pl.when(f == 0)
    def _():
        x = x_ref[...]
        h_s[...] = _rms_norm(x, n2_ref[...]).astype(_BF16)
        out_ref[...] = x

    @pl.when(ti == 0)
    def _():
        ext_s[0:CONV_HALO, :] = jnp.zeros((CONV_HALO, FF_TILE), _F32)

    @pl.when(ti > 0)
    def _():
        ext_s[0:CONV_HALO, :] = carry_s[f]

    wg, wu, wd, cw = wg_ref[...], wu_ref[...], wd_ref[...], cw_ref[...]
    chunks = [slice(r0, r0 + FFN_SUB_ROWS) for r0 in range(0, tt, FFN_SUB_ROWS)]
    for rs in chunks:
        gate = jnp.dot(h_s[rs, :], wg, preferred_element_type=_F32)
        ext_s[CONV_HALO + rs.start:CONV_HALO + rs.stop, :] = gate
    ups = [jnp.dot(h_s[rs, :], wu, preferred_element_type=_F32) for rs in chunks]
    for rs, up in zip(chunks, ups):
        gc = (cw[0] * ext_s[CONV_HALO - 2 + rs.start:CONV_HALO - 2 + rs.stop, :]
              + cw[1] * ext_s[CONV_HALO - 1 + rs.start:CONV_HALO - 1 + rs.stop, :]
              + cw[2] * ext_s[CONV_HALO + rs.start:CONV_HALO + rs.stop, :])
        act = (jax.nn.silu(gc) * up).astype(_BF16)
        out_ref[rs, :] += jnp.dot(act, wd, preferred_element_type=_F32)
    tail = ext_s[tt:tt + CONV_HALO, :]
    carry_s[f] = tail
    st_out[...] = tail

    if final:
        @pl.when(f == FF_TILES - 1)
        def _():
            out_ref[...] = _rms_norm(out_ref[...], fn_ref[...])


def _ffn_sample(x, norm2, w_gate, w_up, conv_w, w_down, state, final_norm, layer, final):
    nb, tt, _ = x.shape
    vec = lambda f: (0, 0)
    return pl.pallas_call(
        functools.partial(_ffn_sample_kernel, nb=nb, tt=tt, final=final),
        grid=(FF_TILES,),
        in_specs=[pl.BlockSpec((nb, tt, D_MODEL), lambda f: (0, 0, 0)),
                  pl.BlockSpec((1, D_MODEL), vec),
                  pl.BlockSpec((None, D_MODEL, FF_TILE), lambda f: (layer, 0, f)),
                  pl.BlockSpec((None, D_MODEL, FF_TILE), lambda f: (layer, 0, f)),
                  pl.BlockSpec((3, FF_TILE), lambda f: (0, f)),
                  pl.BlockSpec((None, FF_TILE, D_MODEL), lambda f: (layer, f, 0)),
                  pl.BlockSpec((nb, CONV_HALO, FF_TILE), lambda f: (0, 0, f)),
                  pl.BlockSpec((1, D_MODEL), vec)],
        out_specs=[pl.BlockSpec((nb, tt, D_MODEL), lambda f: (0, 0, 0)),
                   pl.BlockSpec((nb, CONV_HALO, FF_TILE), lambda f: (0, 0, f)),
                   pl.BlockSpec((None, D_MODEL, FF_TILE), lambda f: (f, 0, 0)),
                   pl.BlockSpec((None, D_MODEL, FF_TILE), lambda f: (f, 0, 0)),
                   pl.BlockSpec((None, FF_TILE, D_MODEL), lambda f: (f, 0, 0))],
        out_shape=[jax.ShapeDtypeStruct((nb, tt, D_MODEL), _F32),
                   jax.ShapeDtypeStruct((nb, CONV_HALO, D_FF), _F32),
                   jax.ShapeDtypeStruct((FF_TILES, D_MODEL, FF_TILE), _BF16),
                   jax.ShapeDtypeStruct((FF_TILES, D_MODEL, FF_TILE), _BF16),
                   jax.ShapeDtypeStruct((FF_TILES, FF_TILE, D_MODEL), _BF16)],
        scratch_shapes=[pltpu.VMEM((nb * tt, D_MODEL), _BF16),
                        pltpu.VMEM((nb, tt + CONV_HALO, FF_TILE), _F32)],
        compiler_params=pltpu.CompilerParams(dimension_semantics=("arbitrary",), vmem_limit_bytes=VMEM_LIMIT),
        name="ffn_sample",
    )(x, norm2.reshape(1, D_MODEL), w_gate, w_up, conv_w, w_down, state, final_norm.reshape(1, D_MODEL))


def _ffn_prompt(x, norm2, w_gate_t, w_up_t, conv_w, w_down_t, final_norm, tt, final):
    batch, seq, _ = x.shape
    vec = lambda b, t, f: (0, 0)
    return pl.pallas_call(
        functools.partial(_ffn_prompt_kernel, tt=tt, final=final),
        grid=(batch, seq // tt, FF_TILES),
        in_specs=[pl.BlockSpec((None, tt, D_MODEL), lambda b, t, f: (b, t, 0)),
                  pl.BlockSpec((1, D_MODEL), vec),
                  pl.BlockSpec((None, D_MODEL, FF_TILE), lambda b, t, f: (f, 0, 0)),
                  pl.BlockSpec((None, D_MODEL, FF_TILE), lambda b, t, f: (f, 0, 0)),
                  pl.BlockSpec((3, FF_TILE), lambda b, t, f: (0, f)),
                  pl.BlockSpec((None, FF_TILE, D_MODEL), lambda b, t, f: (f, 0, 0)),
                  pl.BlockSpec((1, D_MODEL), vec)],
        out_specs=[pl.BlockSpec((None, tt, D_MODEL), lambda b, t, f: (b, t, 0)),
                   pl.BlockSpec((None, None, CONV_HALO, FF_TILE), lambda b, t, f: (b, t, 0, f))],
        out_shape=[jax.ShapeDtypeStruct((batch, seq, D_MODEL), _F32),
                   jax.ShapeDtypeStruct((batch, seq // tt, CONV_HALO, FF_TILES * FF_TILE), _F32)],
        scratch_shapes=[pltpu.VMEM((tt, D_MODEL), _BF16),
                        pltpu.VMEM((tt + CONV_HALO, FF_TILE), _F32),
                        pltpu.VMEM((FF_TILES, CONV_HALO, FF_TILE), _F32)],
        compiler_params=pltpu.CompilerParams(dimension_semantics=("parallel", "arbitrary", "arbitrary"),
                                             vmem_limit_bytes=VMEM_LIMIT),
        name="ffn_prompt",
    )(x, norm2.reshape(1, D_MODEL), w_gate_t, w_up_t, conv_w, w_down_t, final_norm.reshape(1, D_MODEL))


def _pad_rows_front(a, rows):
    return jnp.pad(a, ((0, 0), (rows - a.shape[1], 0), (0, 0)))


def kernel(x_prompt, x_sample, cache_kv_w128, cache_kv_w512, cache_kv_w2048, state_conv_a, state_pool, state_ffn_conv, norm1, w_in, conv_a_w, w_out, pool_w, pool_scale, norm2, w_gate, w_up, ffn_conv_w, w_down, final_norm):
    bp, sp, _ = x_prompt.shape
    bs, ss, _ = x_sample.shape
    caches = tuple(c.reshape(DEPTH, bs, win * KV_ROWS, HEAD_DIM)
                   for c, (win, _) in zip((cache_kv_w128, cache_kv_w512, cache_kv_w2048), DIL_GROUPS))
    zeros_conv = jnp.zeros((bp, CONV_HALO, CONV_WIDTH), _F32)
    zeros_pool = jnp.zeros((bp, POOL_HALO, POOL_WIDTH), _F32)
    tm, tt = 1024, 512
    w_out_b, pool_w_b = w_out.astype(_BF16), pool_w.astype(_BF16)
    xp, xs = x_prompt, x_sample
    kv_p = kv_s = None
    small_p, small_s = [], []
    for l in range(DEPTH):
        final = l == DEPTH - 1

        proj, *kv_new, w_in_t = _inproj_sample(xs.reshape(bs * ss, D_MODEL), norm1[l], w_in, l)
        yb, *kv_s = _attn_sample(proj, kv_new, caches, bs, ss, l, kv_s)
        xs, conv_o, pool_o = _mixer(proj, yb, xs, _pad_rows_front(state_conv_a[l], CONV_HALO),
                                    _pad_rows_front(state_pool[l], POOL_HALO), conv_a_w[l], pool_w_b, pool_scale[l],
                                    w_out_b, l, bs, ss, PAST_LEN, U_TILE0)
        xs, ffn_o, w_gate_t, w_up_t, w_down_t = _ffn_sample(
            xs, norm2[l], w_gate, w_up, ffn_conv_w[l], w_down, _pad_rows_front(state_ffn_conv[l], CONV_HALO),
            final_norm, l, final)
        small_s.append((conv_o[:, CONV_HALO - 2:], pool_o[:, POOL_HALO - POOL_BUF:], ffn_o[:, CONV_HALO - 2:]))

        nat, qkv1, qkv2, *kv_p = _inproj_prompt(xp.reshape(bp * sp, D_MODEL), norm1[l], w_in_t, bp, sp, tm, l, kv_p)
        yb = _attn_prompt(nat, qkv1, qkv2, bp, sp)
        xp, conv_o, pool_o = _mixer(nat, yb, xp, zeros_conv, zeros_pool, conv_a_w[l], pool_w_b, pool_scale[l], w_out_b,
                                    l, 1, tt, 0, NAT_TILES - 2)
        conv_w_pad = jnp.pad(ffn_conv_w[l], ((0, 0), (0, FF_TILES * FF_TILE - D_FF)))
        xp, ffn_o = _ffn_prompt(xp, norm2[l], w_gate_t, w_up_t, conv_w_pad, w_down_t, final_norm, tt, final)
        small_p.append((conv_o[:, CONV_HALO - 2:], pool_o[:, POOL_HALO - POOL_BUF:],
                        ffn_o[:, -1, CONV_HALO - 2:, :D_FF]))

    out = [xp, xs]
    for g, (win, _) in enumerate(DIL_GROUPS):
        out += [kv_p[g].reshape(DEPTH, bp, win, 2, HEADS, HEAD_DIM), kv_s[g].reshape(DEPTH, bs, win, 2, HEADS, HEAD_DIM)]
    for i in range(3):
        out += [jnp.stack([s[i] for s in small_p]), jnp.stack([s[i] for s in small_s])]
    return tuple(out)
```

```python
import functools

import numpy as np
import jax
import jax.numpy as jnp
from jax import lax
from jax.experimental import pallas as pl
from jax.experimental.pallas import tpu as pltpu

D_MODEL = 2048
DEPTH = 2
PAST_LEN = 16384
CONV_WIDTH = 512
ATT_WIDTH = 512
HEADS = 4
HEAD_DIM = 128
DIL_GROUPS = ((128, 1), (512, 4), (2048, 16))
POOL_WIDTH = 1024
POOL_WINDOWS = (2, 4, 8, 16)
POOL_GROUP = 256
POOL_BUF = 15
IN_COLS = 7168
D_FF = 5504
RMS_EPS = 1e-6
ATT_SCALE = HEAD_DIM ** -0.5

COL_TILE = 512
IN_TILES = IN_COLS // COL_TILE
Q_TILE = (3, 6, 9)
U_TILE0 = 12
NAT_TILES = 8
KV_ROWS = 2 * HEADS
SUB_ROWS = 512
FF_TILE = 512
FF_TILES = -(-D_FF // FF_TILE)
FF_LAST = D_FF - (FF_TILES - 1) * FF_TILE
FFN_SUB_ROWS = 256
CONV_HALO = 8
POOL_HALO = 16
POOL_BLOCK = 128
VMEM_LIMIT = 56 * 1024 * 1024

_BF16 = jnp.bfloat16
_F32 = jnp.float32


def _alibi_slopes():
    h = np.arange(1, 3 * HEADS + 1, dtype=np.float32)
    return [float(v) for v in np.power(np.float32(2.0), -8.0 * h / (3 * HEADS))]


SLOPES = _alibi_slopes()


def _rms_norm(x, g):
    return (x * lax.rsqrt(jnp.mean(x * x, axis=-1, keepdims=True) + RMS_EPS)) * g


def _head_cols(h, base=0):
    return slice(base + h * HEAD_DIM, base + (h + 1) * HEAD_DIM)


def _inproj_sample_kernel(x_ref, g_ref, w_ref, proj_ref, kv0_ref, kv1_ref, kv2_ref, wt_ref, h_s):
    j = pl.program_id(1)

    @pl.when(j == 0)
    def _():
        h_s[...] = _rms_norm(x_ref[...], g_ref[...]).astype(_BF16)

    w = w_ref[...].astype(_BF16)
    wt_ref[...] = w
    acc = jnp.dot(h_s[...], w, preferred_element_type=_F32)
    proj_ref[...] = acc.astype(_BF16)
    for g, ref in enumerate((kv0_ref, kv1_ref, kv2_ref)):
        @pl.when((j == Q_TILE[g] + 1) | (j == Q_TILE[g] + 2))
        def _(ref=ref):
            ref[...] = acc


def _inproj_sample(x2d, g, w, layer):
    m = x2d.shape[0]

    def kv_spec(grp):
        return pl.BlockSpec((m, COL_TILE), lambda i, j: (i, jnp.clip(j - Q_TILE[grp] - 1, 0, 1)))

    return pl.pallas_call(
        _inproj_sample_kernel,
        grid=(1, IN_TILES),
        in_specs=[pl.BlockSpec((m, D_MODEL), lambda i, j: (i, 0)),
                  pl.BlockSpec((1, D_MODEL), lambda i, j: (0, 0)),
                  pl.BlockSpec((None, D_MODEL, COL_TILE), lambda i, j: (layer, 0, j))],
        out_specs=[pl.BlockSpec((m, COL_TILE), lambda i, j: (i, j)), kv_spec(0), kv_spec(1), kv_spec(2),
                   pl.BlockSpec((None, D_MODEL, COL_TILE), lambda i, j: (j, 0, 0))],
        out_shape=[jax.ShapeDtypeStruct((m, IN_COLS), _BF16)] +
                  [jax.ShapeDtypeStruct((m, 2 * ATT_WIDTH), _F32)] * 3 +
                  [jax.ShapeDtypeStruct((IN_TILES, D_MODEL, COL_TILE), _BF16)],
        scratch_shapes=[pltpu.VMEM((m, D_MODEL), _BF16)],
        compiler_params=pltpu.CompilerParams(dimension_semantics=("parallel", "arbitrary"),
                                             vmem_limit_bytes=VMEM_LIMIT),
        name="inproj_sample",
    )(x2d, g.reshape(1, D_MODEL), w)


def _inproj_prompt_kernel(*refs, tm, aliased):
    n_in = 6 if aliased else 3
    x_ref, g_ref, w_ref = refs[:3]
    nat_ref, g1_ref, g2_ref, kv0_ref, kv1_ref, kv2_ref, h_s, de_s = refs[n_in:]
    kv_refs = (kv0_ref, kv1_ref, kv2_ref)
    res_refs = (None, g1_ref, g2_ref)
    j = pl.program_id(1)

    @pl.when(j == 0)
    def _():
        h_s[...] = _rms_norm(x_ref[...], g_ref[...]).astype(_BF16)

    def column_tile(grp, part):
        w = w_ref[...]
        for c in range(tm // SUB_ROWS):
            r0 = c * SUB_ROWS
            acc = jnp.dot(h_s[r0:r0 + SUB_ROWS, :], w, preferred_element_type=_F32)
            dil = 1 if grp is None else DIL_GROUPS[grp][1]
            if dil == 1:
                nat_ref[r0:r0 + SUB_ROWS, :] = acc.astype(_BF16)
            else:
                for h in range(HEADS):
                    de_s[c, h] = acc[:, _head_cols(h)]
                n = SUB_ROWS // dil
                for res in range(dil):
                    for h in range(HEADS):
                        res_refs[grp][res, c * n:(c + 1) * n, _head_cols(h)] = (
                            de_s[c, h, pl.ds(res, n, stride=dil), :].astype(_BF16))
            if part is not None:
                keep = min(DIL_GROUPS[grp][0], tm)
                lo = max(r0, tm - keep)
                n = r0 + SUB_ROWS - lo
                if n > 0:
                    for h in range(HEADS):
                        dst = pl.ds((lo - (tm - keep)) * KV_ROWS + part * HEADS + h, n, stride=KV_ROWS)
                        kv_refs[grp][dst, :] = acc[lo - r0:lo - r0 + n, _head_cols(h)]

    @pl.when((j <= Q_TILE[0]) | (j >= U_TILE0))
    def _():
        column_tile(None, None)

    for grp in range(len(DIL_GROUPS)):
        for t, part in enumerate((None, 0, 1)):
            if grp == 0 and part is None:
                continue

            @pl.when(j == Q_TILE[grp] + t)
            def _(grp=grp, part=part):
                column_tile(grp, part)


def _inproj_prompt(x2d, g, w, batch, seq, tm, layer, prev_kv):
    m = x2d.shape[0]
    tps = seq // tm
    aliased = prev_kv is not None
    (w0, _), (w1, d1), (w2, d2) = DIL_GROUPS
    assert w2 == seq and w0 <= tm and w1 <= tm and tm % SUB_ROWS == 0

    in_specs = [pl.BlockSpec((tm, D_MODEL), lambda i, j: (i, 0)),
                pl.BlockSpec((1, D_MODEL), lambda i, j: (0, 0)),
                pl.BlockSpec((None, D_MODEL, COL_TILE), lambda i, j: (j, 0, 0))]
    args = [x2d, g.reshape(1, D_MODEL), w]
    if aliased:
        in_specs += [pl.BlockSpec(memory_space=pl.ANY)] * 3
        args += list(prev_kv)

    def res_spec(grp, dil):
        return pl.BlockSpec((None, dil, tm // dil, COL_TILE),
                            lambda i, j: (i // tps, 0, i % tps, jnp.clip(j - Q_TILE[grp], 0, 2)))

    out_specs = [
        pl.BlockSpec((tm, COL_TILE), lambda i, j: (i, jnp.clip(j, 0, Q_TILE[1] - 1) + jnp.clip(j - U_TILE0 + 1, 0, 2))),
        res_spec(1, d1), res_spec(2, d2),
        pl.BlockSpec((None, None, w0 * KV_ROWS, HEAD_DIM), lambda i, j: (layer, i // tps, 0, 0)),
        pl.BlockSpec((None, None, w1 * KV_ROWS, HEAD_DIM), lambda i, j: (layer, i // tps, 0, 0)),
        pl.BlockSpec((None, None, tm * KV_ROWS, HEAD_DIM), lambda i, j: (layer, i // tps, i % tps, 0)),
    ]
    out_shape = [
        jax.ShapeDtypeStruct((m, NAT_TILES * COL_TILE), _BF16),
        jax.ShapeDtypeStruct((batch, d1, seq // d1, 3 * ATT_WIDTH), _BF16),
        jax.ShapeDtypeStruct((batch, d2, seq // d2, 3 * ATT_WIDTH), _BF16),
    ] + [jax.ShapeDtypeStruct((DEPTH, batch, win * KV_ROWS, HEAD_DIM), _F32) for win, _ in DIL_GROUPS]
    return pl.pallas_call(
        functools.partial(_inproj_prompt_kernel, tm=tm, aliased=aliased),
        grid=(m // tm, IN_TILES),
        in_specs=in_specs, out_specs=out_specs, out_shape=out_shape,
        scratch_shapes=[pltpu.VMEM((tm, D_MODEL), _BF16),
                        pltpu.VMEM((tm // SUB_ROWS, HEADS, SUB_ROWS, HEAD_DIM), _F32)],
        input_output_aliases={3: 3, 4: 4, 5: 5} if aliased else {},
        compiler_params=pltpu.CompilerParams(dimension_semantics=("arbitrary", "arbitrary"),
                                             vmem_limit_bytes=VMEM_LIMIT),
        name="inproj_prompt",
    )(*args)


QB = 128
G0_UNROLL = 3
G2_RES_PER_STEP = 4


def _attn_prompt_kernel(q0, k0, v0, q1, k1, v1, q2, k2, v2, o_ref, acc_s, m_s, l_s, shuf_s, *, seq):
    step = pl.program_id(1)
    n1 = DIL_GROUPS[1][1]
    n2_steps = DIL_GROUPS[2][1] // G2_RES_PER_STEP

    row0 = lax.broadcasted_iota(jnp.int32, (QB, QB), 0)
    col0 = lax.broadcasted_iota(jnp.int32, (QB, QB), 1)
    dist_first = jnp.where(col0 <= row0, (row0 - col0).astype(_F32), jnp.inf)
    row1 = lax.broadcasted_iota(jnp.int32, (QB, 2 * QB), 0)
    col1 = lax.broadcasted_iota(jnp.int32, (QB, 2 * QB), 1)
    d1 = QB + row1 - col1
    dist_next = jnp.where((d1 >= 0) & (d1 <= QB), d1.astype(_F32), jnp.inf)
    ones = jnp.ones((2 * QB, HEAD_DIM), _BF16)

    seg = seq // n1
    chunk = QB // n1
    assert n1 == 4 and DIL_GROUPS[2][1] == 16

    def blocks(specs):
        tiles = []
        for slot, (qr, kr, vr, grp, res, i, is_first_block) in enumerate(specs):
            dil = DIL_GROUPS[grp][1]
            for h in range(HEADS):
                hs = _head_cols(h)
                slope_step = SLOPES[grp * HEADS + h] * dil
                if is_first_block:
                    q, k, v = qr[0:QB, hs], kr[0:QB, hs], vr[0:QB, hs]
                    bias, one = dist_first * slope_step, ones[0:QB]
                else:
                    qs = pl.ds(pl.multiple_of(i * QB, QB), QB)
                    ks = pl.ds(pl.multiple_of(i * QB - QB, QB), 2 * QB)
                    q, k, v = qr[qs, hs], kr[ks, hs], vr[ks, hs]
                    bias, one = dist_next * slope_step, ones
                tiles.append((q, k, jnp.concatenate([v, one], axis=1), bias, grp, res, i, is_first_block, slot, h))
        scores = [lax.dot_general(t[0], t[1], (((1,), (1,)), ((), ())), preferred_element_type=_F32) for t in tiles]
        probs, maxes = [], []
        for t, s in zip(tiles, scores):
            s = s * ATT_SCALE - t[3]
            m = jnp.max(s, axis=1, keepdims=True)
            probs.append(jnp.exp(s - m).astype(_BF16))
            maxes.append(jnp.broadcast_to(m, (QB, HEAD_DIM)))
        outs = [jnp.dot(p, t[2], preferred_element_type=_F32) for t, p in zip(tiles, probs)]
        for (_, _, _, _, grp, res, i, is_first_block, slot, h), m, acc_l in zip(tiles, maxes, outs):
            acc, l = acc_l[:, :HEAD_DIM], acc_l[:, HEAD_DIM:]
            if grp == 0:
                for a, (val, ref) in enumerate(((acc, acc_s), (m, m_s), (l, l_s))):
                    tmp = shuf_s.at[(slot * HEADS + h) * 3 + a]
                    tmp[...] = val
                    for r4 in range(n1):
                        dst = r4 * seg + i * chunk
                        dst = pl.ds(dst if is_first_block else pl.multiple_of(dst, chunk), chunk)
                        ref[h, dst, :] = tmp[pl.ds(r4, chunk, stride=n1), :]
            else:
                if grp == 1:
                    rows = pl.ds(pl.multiple_of(res * seg + i * QB, QB), QB)
                else:
                    rows = pl.ds((res & (n1 - 1)) * seg + (res >> 2), QB, stride=n1)
                m_old, l_old, a_old = m_s[h, rows, :], l_s[h, rows, :], acc_s[h, rows, :]
                m_new = jnp.maximum(m_old, m)
                w_old = jnp.exp(m_old - m_new)
                w_new = jnp.exp(m - m_new)
                acc_s[h, rows, :] = a_old * w_old + acc * w_new
                l_s[h, rows, :] = l_old * w_old + l * w_new
                m_s[h, rows, :] = m_new

    @pl.when(step == 0)
    def _():
        n_blk = seq // QB
        assert (n_blk - 1) % G0_UNROLL == 0
        blocks([(q0, k0, v0, 0, 0, 0, True)])

        def body(it, carry):
            blocks([(q0, k0, v0, 0, 0, 1 + it * G0_UNROLL + u, False) for u in range(G0_UNROLL)])
            return carry
        lax.fori_loop(0, (n_blk - 1) // G0_UNROLL, body, 0)

    @pl.when((step >= 1) & (step <= n1))
    def _():
        blocks([(q1, k1, v1, 1, step - 1, i, i == 0) for i in range(seg // QB)])

    @pl.when(step > n1)
    def _():
        assert seq // DIL_GROUPS[2][1] == QB
        blocks([(q2.at[r], k2.at[r], v2.at[r], 2, (step - 1 - n1) * G2_RES_PER_STEP + r, 0, True)
                for r in range(G2_RES_PER_STEP)])

    @pl.when(step == n1 + n2_steps)
    def _():
        def body(c, carry):
            for h in range(HEADS):
                tmp = shuf_s.at[h]
                for r4 in range(n1):
                    src = pl.ds(pl.multiple_of(r4 * seg + c * chunk, chunk), chunk)
                    tmp[pl.ds(r4, chunk, stride=n1), :] = acc_s[h, src, :] / l_s[h, src, :]
                o_ref[pl.ds(pl.multiple_of(c * QB, QB), QB), _head_cols(h)] = tmp[...].astype(_BF16)
            return carry
        lax.fori_loop(0, seq // QB, body, 0)


def _attn_prompt(nat, qkv1, qkv2, batch, seq):
    n1, n2 = DIL_GROUPS[1][1], DIL_GROUPS[2][1]
    n2_steps = n2 // G2_RES_PER_STEP
    nat3 = nat.reshape(batch, seq, NAT_TILES * COL_TILE)
    in_specs = [pl.BlockSpec((None, seq, COL_TILE), lambda b, s, t=t: (b, 0, Q_TILE[0] + t)) for t in range(3)]
    in_specs += [pl.BlockSpec((None, None, seq // n1, COL_TILE),
                              lambda b, s, t=t: (b, jnp.clip(s - 1, 0, n1 - 1), 0, t)) for t in range(3)]
    in_specs += [pl.BlockSpec((None, G2_RES_PER_STEP, seq // n2, COL_TILE),
                              lambda b, s, t=t: (b, jnp.clip(s - 1 - n1, 0, n2_steps - 1), 0, t)) for t in range(3)]
    args = [nat3] * 3 + [qkv1] * 3 + [qkv2] * 3
    return pl.pallas_call(
        functools.partial(_attn_prompt_kernel, seq=seq),
        grid=(batch, 1 + n1 + n2_steps),
        in_specs=in_specs,
        out_specs=pl.BlockSpec((None, seq, ATT_WIDTH), lambda b, s: (b, 0, 0)),
        out_shape=jax.ShapeDtypeStruct((batch, seq, ATT_WIDTH), _BF16),
        scratch_shapes=[pltpu.VMEM((HEADS, seq, HEAD_DIM), _F32)] * 3
                       + [pltpu.VMEM((G0_UNROLL * HEADS * 3, QB, HEAD_DIM), _F32)],
        compiler_params=pltpu.CompilerParams(dimension_semantics=("parallel", "arbitrary"),
                                             vmem_limit_bytes=VMEM_LIMIT),
        name="attn_prompt",
    )(*args)


def _attn_sample_kernel(*refs, tdec, aliased):
    n_in = 10 if aliased else 7
    proj_ref, n0, n1, n2, c0, c1, c2 = refs[:7]
    yb_ref, o0, o1, o2 = refs[n_in:]
    news, caches, outs = (n0, n1, n2), (c0, c1, c2), (o0, o1, o2)
    for h in range(HEADS):
        pieces = []
        for grp, (win, dil) in enumerate(DIL_GROUPS):
            slope = SLOPES[grp * HEADS + h]
            q = proj_ref[:, _head_cols(h, Q_TILE[grp] * COL_TILE)]
            k_new = news[grp][:, _head_cols(h)]
            v_new = news[grp][:, _head_cols(h, ATT_WIDTH)]
            k_old = caches[grp][pl.ds(h, win, stride=KV_ROWS), :]
            v_old = caches[grp][pl.ds(HEADS + h, win, stride=KV_ROWS), :]
            for k, v, n_keys, base in ((k_old, v_old, win, win), (k_new, v_new, tdec, 0)):
                s = lax.dot_general(q, k.astype(_BF16), (((1,), (1,)), ((), ())), preferred_element_type=_F32)
                t = lax.broadcasted_iota(jnp.int32, (tdec, n_keys), 0)
                c = lax.broadcasted_iota(jnp.int32, (tdec, n_keys), 1)
                dist = base + t - c
                ok = (dist >= 0) & (dist <= win) & ((dist & (dil - 1)) == 0)
                s = jnp.where(ok, s * ATT_SCALE - slope * dist.astype(_F32), -jnp.inf)
                pieces.append((s, v.astype(_BF16)))
        m = functools.reduce(jnp.maximum, [jnp.max(s, axis=1, keepdims=True) for s, _ in pieces])
        l = jnp.zeros((tdec, 1), _F32)
        acc = jnp.zeros((tdec, HEAD_DIM), _F32)
        for s, v in pieces:
            p = jnp.exp(s - m)
            l = l + jnp.sum(p, axis=1, keepdims=True)
            acc = acc + jnp.dot(p.astype(_BF16), v, preferred_element_type=_F32)
        yb_ref[:, _head_cols(h)] = (acc / l).astype(_BF16)
    for grp, (win, _) in enumerate(DIL_GROUPS):
        kept = (win - tdec) * KV_ROWS
        outs[grp][0:kept, :] = caches[grp][tdec * KV_ROWS:win * KV_ROWS, :]
        for part in range(2):
            for h in range(HEADS):
                outs[grp][pl.ds(kept + part * HEADS + h, tdec, stride=KV_ROWS), :] = (
                    news[grp][:, _head_cols(h, part * ATT_WIDTH)])


def _attn_sample(proj, kv_new, caches, batch, tdec, layer, prev_out):
    aliased = prev_out is not None
    in_specs = [pl.BlockSpec((None, tdec, IN_COLS), lambda b: (b, 0, 0))]
    in_specs += [pl.BlockSpec((None, tdec, 2 * ATT_WIDTH), lambda b: (b, 0, 0))] * 3
    cache_specs = [pl.BlockSpec((None, None, win * KV_ROWS, HEAD_DIM), lambda b: (layer, b, 0, 0)) for win, _ in DIL_GROUPS]
    in_specs += cache_specs
    args = [proj.reshape(batch, tdec, IN_COLS)] + [k.reshape(batch, tdec, 2 * ATT_WIDTH) for k in kv_new] + list(caches)
    if aliased:
        in_specs += [pl.BlockSpec(memory_space=pl.ANY)] * 3
        args += list(prev_out)
    out_specs = [pl.BlockSpec((None, tdec, ATT_WIDTH), lambda b: (b, 0, 0))] + cache_specs
    out_shape = [jax.ShapeDtypeStruct((batch, tdec, ATT_WIDTH), _BF16)]
    out_shape += [jax.ShapeDtypeStruct((DEPTH, batch, win * KV_ROWS, HEAD_DIM), _F32) for win, _ in DIL_GROUPS]
    return pl.pallas_call(
        functools.partial(_attn_sample_kernel, tdec=tdec, aliased=aliased),
        grid=(batch,),
        in_specs=in_specs, out_specs=out_specs, out_shape=out_shape,
        input_output_aliases={7: 1, 8: 2, 9: 3} if aliased else {},
        compiler_params=pltpu.CompilerParams(dimension_semantics=("arbitrary",), vmem_limit_bytes=VMEM_LIMIT),
        name="attn_sample",
    )(*args)


def _mixer_kernel(*refs, nb, tt, start, from_zero):
    if from_zero:
        pa_ref, pu_ref, yb_ref, x_ref, cw_ref, pw_ref, psc_ref, wo_ref = refs[:8]
    else:
        pa_ref, pu_ref, yb_ref, x_ref, cs0_ref, ps0_ref, cw_ref, pw_ref, psc_ref, wo_ref = refs[:10]
    out_ref, cs_out, ps_out, ext_a, ext_u, yc = refs[-6:]
    ti = pl.program_id(1)
    rows = nb * tt
    pool_halo = POOL_BLOCK if from_zero else POOL_HALO

    @pl.when(ti == 0)
    def _():
        if from_zero:
            ext_a[:, 0:CONV_HALO, :] = jnp.zeros((nb, CONV_HALO, CONV_WIDTH), _F32)
            ext_u[:, 0:pool_halo, :] = jnp.zeros((nb, pool_halo, POOL_WIDTH), _BF16)
        else:
            ext_a[:, 0:CONV_HALO, :] = cs0_ref[...]
            ext_u[:, 0:pool_halo, :] = ps0_ref[...]

    yb = yb_ref[...].astype(_F32).reshape(rows, ATT_WIDTH).astype(_BF16)
    mixed = jnp.dot(yb, wo_ref[CONV_WIDTH:CONV_WIDTH + ATT_WIDTH, :], preferred_element_type=_F32)

    pa = pa_ref[...].astype(_F32)
    xa, gate_b, gate_c = (pa[:, :, k * CONV_WIDTH:(k + 1) * CONV_WIDTH] for k in range(3))
    prod = gate_c * xa
    ext_a[:, CONV_HALO:CONV_HALO + tt, :] = prod
    cw = cw_ref[...]
    cu = (cw[0] * ext_a[:, CONV_HALO - 2:CONV_HALO - 2 + tt, :]
          + cw[1] * ext_a[:, CONV_HALO - 1:CONV_HALO - 1 + tt, :] + cw[2] * prod)
    ya = (gate_b * cu).reshape(rows, CONV_WIDTH).astype(_BF16)
    mixed = mixed + jnp.dot(ya, wo_ref[0:CONV_WIDTH, :], preferred_element_type=_F32)

    ext_u[:, pool_halo:pool_halo + tt, :] = pu_ref[...].astype(ext_u.dtype)
    if from_zero:
        pb = POOL_BLOCK
        row = lax.broadcasted_iota(jnp.int32, (pb, 2 * pb), 0)
        col = lax.broadcasted_iota(jnp.int32, (pb, 2 * pb), 1)
        lag = pb + row - col
        for gi, w in enumerate(POOL_WINDOWS):
            cols = slice(gi * POOL_GROUP, (gi + 1) * POOL_GROUP)
            band = ((lag >= 0) & (lag < w)).astype(_F32).astype(_BF16)
            for r in range(tt // pb):
                tot = jnp.dot(band, ext_u[0, r * pb:(r + 2) * pb, cols], preferred_element_type=_F32)
                pos = start + ti * tt + r * pb + lax.broadcasted_iota(jnp.int32, (pb, 1), 0)
                cnt = jnp.minimum(w, pos + 1).astype(_F32)
                ug = pu_ref[0, r * pb:(r + 1) * pb, cols].astype(_F32)
                diff = (tot / cnt - ug).astype(_BF16)
                z = jnp.dot(diff, pw_ref[gi], preferred_element_type=_F32) * psc_ref[:, cols]
                yc[r * pb:(r + 1) * pb, cols] = z.astype(_BF16)
    else:
        u = pu_ref[...].astype(_F32)
        pos = start + ti * tt + lax.broadcasted_iota(jnp.int32, (1, tt, 1), 1)
        for gi, w in enumerate(POOL_WINDOWS):
            cols = slice(gi * POOL_GROUP, (gi + 1) * POOL_GROUP)
            ug = u[:, :, cols]
            tot = ug
            for k in range(1, w):
                tot = tot + ext_u[:, pool_halo - k:pool_halo - k + tt, cols]
            cnt = jnp.minimum(w, pos + 1).astype(_F32)
            diff = (tot / cnt - ug).reshape(rows, POOL_GROUP).astype(_BF16)
            z = jnp.dot(diff, pw_ref[gi], preferred_element_type=_F32) * psc_ref[:, cols]
            yc[:, cols] = z.astype(_BF16)

    mixed = mixed + jnp.dot(yc[...], wo_ref[CONV_WIDTH + ATT_WIDTH:, :], preferred_element_type=_F32)
    out_ref[...] = x_ref[...] + mixed.reshape(nb, tt, D_MODEL)

    tail_a = ext_a[:, tt:tt + CONV_HALO, :]
    ext_a[:, 0:CONV_HALO, :] = tail_a
    cs_out[...] = tail_a
    ps_out[...] = ext_u[:, pool_halo + tt - POOL_HALO:pool_halo + tt, :].astype(_F32)
    ext_u[:, 0:pool_halo, :] = ext_u[:, tt:tt + pool_halo, :]


def _mixer(proj, yb, x, conv_state, pool_state, conv_w, pool_w, pool_scale, w_out, layer, nb, tt, start, u_tile0):
    batch, seq, _ = x.shape
    proj3 = proj.reshape(batch, seq, proj.shape[-1])
    u_blk = u_tile0 * COL_TILE // POOL_WIDTH
    const2 = lambda b, t: (0, 0)
    from_zero = conv_state is None and pool_state is None
    assert from_zero or (conv_state is not None and pool_state is not None)
    assert not from_zero or (nb == 1 and start == 0 and tt % POOL_BLOCK == 0)
    state_specs = [] if from_zero else [pl.BlockSpec((nb, CONV_HALO, CONV_WIDTH), lambda b, t: (b, 0, 0)),
                                        pl.BlockSpec((nb, POOL_HALO, POOL_WIDTH), lambda b, t: (b, 0, 0))]
    state_args = [] if from_zero else [conv_state, pool_state]
    ext_u = (pltpu.VMEM((nb, tt + POOL_BLOCK, POOL_WIDTH), _BF16) if from_zero
             else pltpu.VMEM((nb, tt + POOL_HALO, POOL_WIDTH), _F32))
    return pl.pallas_call(
        functools.partial(_mixer_kernel, nb=nb, tt=tt, start=start, from_zero=from_zero),
        grid=(batch // nb, seq // tt),
        in_specs=[pl.BlockSpec((nb, tt, 3 * CONV_WIDTH), lambda b, t: (b, t, 0)),
                  pl.BlockSpec((nb, tt, POOL_WIDTH), lambda b, t: (b, t, u_blk)),
                  pl.BlockSpec((nb, tt, ATT_WIDTH), lambda b, t: (b, t, 0)),
                  pl.BlockSpec((nb, tt, D_MODEL), lambda b, t: (b, t, 0))] + state_specs + [
                  pl.BlockSpec((3, CONV_WIDTH), const2),
                  pl.BlockSpec((None, len(POOL_WINDOWS), POOL_GROUP, POOL_GROUP), lambda b, t: (layer, 0, 0, 0)),
                  pl.BlockSpec((1, POOL_WIDTH), const2),
                  pl.BlockSpec((None, D_MODEL, D_MODEL), lambda b, t: (layer, 0, 0))],
        out_specs=[pl.BlockSpec((nb, tt, D_MODEL), lambda b, t: (b, t, 0)),
                   pl.BlockSpec((nb, CONV_HALO, CONV_WIDTH), lambda b, t: (b, 0, 0)),
                   pl.BlockSpec((nb, POOL_HALO, POOL_WIDTH), lambda b, t: (b, 0, 0))],
        out_shape=[jax.ShapeDtypeStruct((batch, seq, D_MODEL), _F32),
                   jax.ShapeDtypeStruct((batch, CONV_HALO, CONV_WIDTH), _F32),
                   jax.ShapeDtypeStruct((batch, POOL_HALO, POOL_WIDTH), _F32)],
        scratch_shapes=[pltpu.VMEM((nb, tt + CONV_HALO, CONV_WIDTH), _F32), ext_u,
                        pltpu.VMEM((nb * tt, POOL_WIDTH), _BF16)],
        compiler_params=pltpu.CompilerParams(dimension_semantics=("parallel", "arbitrary"),
                                             vmem_limit_bytes=VMEM_LIMIT),
        name="mixer",
    )(proj3, proj3, yb, x, *state_args, conv_w, pool_w, pool_scale.reshape(1, POOL_WIDTH), w_out)


def _ffn_sample_kernel(x_ref, n2_ref, wg_ref, wu_ref, cw_ref, wd_ref, st0_ref, fn_ref,
                       out_ref, st_out, wgt_ref, wut_ref, wdt_ref, h_s, ext_s, *, nb, tt, final):
    f = pl.program_id(0)
    rows = nb * tt

    @pl.when(f == 0)
    def _():
        x = x_ref[...]
        h_s[...] = _rms_norm(x, n2_ref[...]).reshape(rows, D_MODEL).astype(_BF16)
        out_ref[...] = x

    def tile(width):
        cols = slice(0, width)
        wg, wu, wd = wg_ref[:, cols].astype(_BF16), wu_ref[:, cols].astype(_BF16), wd_ref[cols, :].astype(_BF16)
        wgt_ref[:, cols], wut_ref[:, cols], wdt_ref[cols, :] = wg, wu, wd
        if width < FF_TILE:
            wgt_ref[:, width:] = jnp.zeros((D_MODEL, FF_TILE - width), _BF16)
            wut_ref[:, width:] = jnp.zeros((D_MODEL, FF_TILE - width), _BF16)
            wdt_ref[width:, :] = jnp.zeros((FF_TILE - width, D_MODEL), _BF16)
        h = h_s[...]
        gate = jnp.dot(h, wg, preferred_element_type=_F32).reshape(nb, tt, width)
        up = jnp.dot(h, wu, preferred_element_type=_F32).reshape(nb, tt, width)
        ext_s[:, 0:CONV_HALO, cols] = st0_ref[:, :, cols]
        ext_s[:, CONV_HALO:CONV_HALO + tt, cols] = gate
        cw = cw_ref[:, cols]
        gc = (cw[0] * ext_s[:, CONV_HALO - 2:CONV_HALO - 2 + tt, cols]
              + cw[1] * ext_s[:, CONV_HALO - 1:CONV_HALO - 1 + tt, cols] + cw[2] * gate)
        st_out[:, :, cols] = ext_s[:, tt:tt + CONV_HALO, cols]
        act = (jax.nn.silu(gc) * up).reshape(rows, width).astype(_BF16)
        out_ref[...] += jnp.dot(act, wd, preferred_element_type=_F32).reshape(nb, tt, D_MODEL)

    @pl.when(f < FF_TILES - 1)
    def _():
        tile(FF_TILE)

    @pl.when(f == FF_TILES - 1)
    def _():
        tile(FF_LAST)
        if final:
            out_ref[...] = _rms_norm(out_ref[...], fn_ref[...])


def _ffn_prompt_kernel(x_ref, n2_ref, wg_ref, wu_ref, cw_ref, wd_ref, fn_ref,
                       out_ref, st_out, h_s, ext_s, carry_s, *, tt, final):
    ti = pl.program_id(1)
    f = pl.program_id(2)

    @pl.when(f == 0)
    def _():
        x = x_ref[...]
        h_s[...] = _rms_norm(x, n2_ref[...]).astype(_BF16)
        out_ref[...] = x

    @pl.when(ti == 0)
    def _():
        ext_s[0:CONV_HALO, :] = jnp.zeros((CONV_HALO, FF_TILE), _F32)

    @pl.when(ti > 0)
    def _():
        ext_s[0:CONV_HALO, :] = carry_s[f]

    wg, wu, wd, cw = wg_ref[...], wu_ref[...], wd_ref[...], cw_ref[...]
    chunks = [slice(r0, r0 + FFN_SUB_ROWS) for r0 in range(0, tt, FFN_SUB_ROWS)]
    for rs in chunks:
        gate = jnp.dot(h_s[rs, :], wg, preferred_element_type=_F32)
        ext_s[CONV_HALO + rs.start:CONV_HALO + rs.stop, :] = gate
    ups = [jnp.dot(h_s[rs, :], wu, preferred_element_type=_F32) for rs in chunks]
    for rs, up in zip(chunks, ups):
        gc = (cw[0] * ext_s[CONV_HALO - 2 + rs.start:CONV_HALO - 2 + rs.stop, :]
              + cw[1] * ext_s[CONV_HALO - 1 + rs.start:CONV_HALO - 1 + rs.stop, :]
              + cw[2] * ext_s[CONV_HALO + rs.start:CONV_HALO + rs.stop, :])
        act = (jax.nn.silu(gc) * up).astype(_BF16)
        out_ref[rs, :] += jnp.dot(act, wd, preferred_element_type=_F32)
    tail = ext_s[tt:tt + CONV_HALO, :]
    carry_s[f] = tail
    st_out[...] = tail

    if final:
        @pl.when(f == FF_TILES - 1)
        def _():
            out_ref[...] = _rms_norm(out_ref[...], fn_ref[...])


def _ffn_sample(x, norm2, w_gate, w_up, conv_w, w_down, state, final_norm, layer, final):
    nb, tt, _ = x.shape
    vec = lambda f: (0, 0)
    return pl.pallas_call(
        functools.partial(_ffn_sample_kernel, nb=nb, tt=tt, final=final),
        grid=(FF_TILES,),
        in_specs=[pl.BlockSpec((nb, tt, D_MODEL), lambda f: (0, 0, 0)),
                  pl.BlockSpec((1, D_MODEL), vec),
                  pl.BlockSpec((None, D_MODEL, FF_TILE), lambda f: (layer, 0, f)),
                  pl.BlockSpec((None, D_MODEL, FF_TILE), lambda f: (layer, 0, f)),
                  pl.BlockSpec((3, FF_TILE), lambda f: (0, f)),
                  pl.BlockSpec((None, FF_TILE, D_MODEL), lambda f: (layer, f, 0)),
                  pl.BlockSpec((nb, CONV_HALO, FF_TILE), lambda f: (0, 0, f)),
                  pl.BlockSpec((1, D_MODEL), vec)],
        out_specs=[pl.BlockSpec((nb, tt, D_MODEL), lambda f: (0, 0, 0)),
                   pl.BlockSpec((nb, CONV_HALO, FF_TILE), lambda f: (0, 0, f)),
                   pl.BlockSpec((None, D_MODEL, FF_TILE), lambda f: (f, 0, 0)),
                   pl.BlockSpec((None, D_MODEL, FF_TILE), lambda f: (f, 0, 0)),
                   pl.BlockSpec((None, FF_TILE, D_MODEL), lambda f: (f, 0, 0))],
        out_shape=[jax.ShapeDtypeStruct((nb, tt, D_MODEL), _F32),
                   jax.ShapeDtypeStruct((nb, CONV_HALO, D_FF), _F32),
                   jax.ShapeDtypeStruct((FF_TILES, D_MODEL, FF_TILE), _BF16),
                   jax.ShapeDtypeStruct((FF_TILES, D_MODEL, FF_TILE), _BF16),
                   jax.ShapeDtypeStruct((FF_TILES, FF_TILE, D_MODEL), _BF16)],
        scratch_shapes=[pltpu.VMEM((nb * tt, D_MODEL), _BF16),
                        pltpu.VMEM((nb, tt + CONV_HALO, FF_TILE), _F32)],
        compiler_params=pltpu.CompilerParams(dimension_semantics=("arbitrary",), vmem_limit_bytes=VMEM_LIMIT),
        name="ffn_sample",
    )(x, norm2.reshape(1, D_MODEL), w_gate, w_up, conv_w, w_down, state, final_norm.reshape(1, D_MODEL))


def _ffn_prompt(x, norm2, w_gate_t, w_up_t, conv_w, w_down_t, final_norm, tt, final):
    batch, seq, _ = x.shape
    vec = lambda b, t, f: (0, 0)
    return pl.pallas_call(
        functools.partial(_ffn_prompt_kernel, tt=tt, final=final),
        grid=(batch, seq // tt, FF_TILES),
        in_specs=[pl.BlockSpec((None, tt, D_MODEL), lambda b, t, f: (b, t, 0)),
                  pl.BlockSpec((1, D_MODEL), vec),
                  pl.BlockSpec((None, D_MODEL, FF_TILE), lambda b, t, f: (f, 0, 0)),
                  pl.BlockSpec((None, D_MODEL, FF_TILE), lambda b, t, f: (f, 0, 0)),
                  pl.BlockSpec((3, FF_TILE), lambda b, t, f: (0, f)),
                  pl.BlockSpec((None, FF_TILE, D_MODEL), lambda b, t, f: (f, 0, 0)),
                  pl.BlockSpec((1, D_MODEL), vec)],
        out_specs=[pl.BlockSpec((None, tt, D_MODEL), lambda b, t, f: (b, t, 0)),
                   pl.BlockSpec((None, None, CONV_HALO, FF_TILE), lambda b, t, f: (b, t, 0, f))],
        out_shape=[jax.ShapeDtypeStruct((batch, seq, D_MODEL), _F32),
                   jax.ShapeDtypeStruct((batch, seq // tt, CONV_HALO, FF_TILES * FF_TILE), _F32)],
        scratch_shapes=[pltpu.VMEM((tt, D_MODEL), _BF16),
                        pltpu.VMEM((tt + CONV_HALO, FF_TILE), _F32),
                        pltpu.VMEM((FF_TILES, CONV_HALO, FF_TILE), _F32)],
        compiler_params=pltpu.CompilerParams(dimension_semantics=("parallel", "arbitrary", "arbitrary"),
                                             vmem_limit_bytes=VMEM_LIMIT),
        name="ffn_prompt",
    )(x, norm2.reshape(1, D_MODEL), w_gate_t, w_up_t, conv_w, w_down_t, final_norm.reshape(1, D_MODEL))


def _pad_rows_front(a, rows):
    return jnp.pad(a, ((0, 0), (rows - a.shape[1], 0), (0, 0)))


def kernel(x_prompt, x_sample, cache_kv_w128, cache_kv_w512, cache_kv_w2048, state_conv_a, state_pool, state_ffn_conv, norm1, w_in, conv_a_w, w_out, pool_w, pool_scale, norm2, w_gate, w_up, ffn_conv_w, w_down, final_norm):
    bp, sp, _ = x_prompt.shape
    bs, ss, _ = x_sample.shape
    caches = tuple(c.reshape(DEPTH, bs, win * KV_ROWS, HEAD_DIM)
                   for c, (win, _) in zip((cache_kv_w128, cache_kv_w512, cache_kv_w2048), DIL_GROUPS))
    tm, tt = 1024, 512
    w_out_b, pool_w_b = w_out.astype(_BF16), pool_w.astype(_BF16)
    xp, xs = x_prompt, x_sample
    kv_p = kv_s = None
    small_p, small_s = [], []
    for l in range(DEPTH):
        final = l == DEPTH - 1

        proj, *kv_new, w_in_t = _inproj_sample(xs.reshape(bs * ss, D_MODEL), norm1[l], w_in, l)
        yb, *kv_s = _attn_sample(proj, kv_new, caches, bs, ss, l, kv_s)
        xs, conv_o, pool_o = _mixer(proj, yb, xs, _pad_rows_front(state_conv_a[l], CONV_HALO),
                                    _pad_rows_front(state_pool[l], POOL_HALO), conv_a_w[l], pool_w_b, pool_scale[l],
                                    w_out_b, l, bs, ss, PAST_LEN, U_TILE0)
        xs, ffn_o, w_gate_t, w_up_t, w_down_t = _ffn_sample(
            xs, norm2[l], w_gate, w_up, ffn_conv_w[l], w_down, _pad_rows_front(state_ffn_conv[l], CONV_HALO),
            final_norm, l, final)
        small_s.append((conv_o[:, CONV_HALO - 2:], pool_o[:, POOL_HALO - POOL_BUF:], ffn_o[:, CONV_HALO - 2:]))

        nat, qkv1, qkv2, *kv_p = _inproj_prompt(xp.reshape(bp * sp, D_MODEL), norm1[l], w_in_t, bp, sp, tm, l, kv_p)
        yb = _attn_prompt(nat, qkv1, qkv2, bp, sp)
        xp, conv_o, pool_o = _mixer(nat, yb, xp, None, None, conv_a_w[l], pool_w_b, pool_scale[l], w_out_b,
                                    l, 1, tt, 0, NAT_TILES - 2)
        conv_w_pad = jnp.pad(ffn_conv_w[l], ((0, 0), (0, FF_TILES * FF_TILE - D_FF)))
        xp, ffn_o = _ffn_prompt(xp, norm2[l], w_gate_t, w_up_t, conv_w_pad, w_down_t, final_norm, tm, final)
        small_p.append((conv_o[:, CONV_HALO - 2:], pool_o[:, POOL_HALO - POOL_BUF:],
                        ffn_o[:, -1, CONV_HALO - 2:, :D_FF]))

    out = [xp, xs]
    for g, (win, _) in enumerate(DIL_GROUPS):
        out += [kv_p[g].reshape(DEPTH, bp, win, 2, HEADS, HEAD_DIM), kv_s[g].reshape(DEPTH, bs, win, 2, HEADS, HEAD_DIM)]
    for i in range(3):
        out += [jnp.stack([s[i] for s in small_p]), jnp.stack([s[i] for s in small_s])]
    return tuple(out)
```

```python
import functools

import numpy as np
import jax
import jax.numpy as jnp
from jax import lax
from jax.experimental import pallas as pl
from jax.experimental.pallas import tpu as pltpu

D_MODEL = 2048
DEPTH = 2
PAST_LEN = 16384
CONV_WIDTH = 512
ATT_WIDTH = 512
HEADS = 4
HEAD_DIM = 128
DIL_GROUPS = ((128, 1), (512, 4), (2048, 16))
POOL_WIDTH = 1024
POOL_WINDOWS = (2, 4, 8, 16)
POOL_GROUP = 256
POOL_BUF = 15
IN_COLS = 7168
D_FF = 5504
RMS_EPS = 1e-6
ATT_SCALE = HEAD_DIM ** -0.5

COL_TILE = 512
IN_TILES = IN_COLS // COL_TILE
Q_TILE = (3, 6, 9)
U_TILE0 = 12
NAT_TILES = 8
KV_ROWS = 2 * HEADS
SUB_ROWS = 512
IN_STEP_TILES = 2
FF_TILE = 512
FF_TILES = -(-D_FF // FF_TILE)
FF_LAST = D_FF - (FF_TILES - 1) * FF_TILE
FFN_SUB_ROWS = 256
CONV_HALO = 8
POOL_HALO = 16
POOL_BLOCK = 128
VMEM_LIMIT = 56 * 1024 * 1024

_BF16 = jnp.bfloat16
_F32 = jnp.float32


def _alibi_slopes():
    h = np.arange(1, 3 * HEADS + 1, dtype=np.float32)
    return [float(v) for v in np.power(np.float32(2.0), -8.0 * h / (3 * HEADS))]


SLOPES = _alibi_slopes()


def _rms_norm(x, g):
    return (x * lax.rsqrt(jnp.mean(x * x, axis=-1, keepdims=True) + RMS_EPS)) * g


def _head_cols(h, base=0):
    return slice(base + h * HEAD_DIM, base + (h + 1) * HEAD_DIM)


def _inproj_sample_kernel(x_ref, g_ref, w_ref, proj_ref, kv0_ref, kv1_ref, kv2_ref, wt_ref, h_s):
    j = pl.program_id(1)

    @pl.when(j == 0)
    def _():
        h_s[...] = _rms_norm(x_ref[...], g_ref[...]).astype(_BF16)

    w = w_ref[...].astype(_BF16)
    wt_ref[...] = w
    acc = jnp.dot(h_s[...], w, preferred_element_type=_F32)
    proj_ref[...] = acc.astype(_BF16)
    for g, ref in enumerate((kv0_ref, kv1_ref, kv2_ref)):
        @pl.when((j == Q_TILE[g] + 1) | (j == Q_TILE[g] + 2))
        def _(ref=ref):
            ref[...] = acc


def _inproj_sample(x2d, g, w, layer):
    m = x2d.shape[0]

    def kv_spec(grp):
        return pl.BlockSpec((m, COL_TILE), lambda i, j: (i, jnp.clip(j - Q_TILE[grp] - 1, 0, 1)))

    return pl.pallas_call(
        _inproj_sample_kernel,
        grid=(1, IN_TILES),
        in_specs=[pl.BlockSpec((m, D_MODEL), lambda i, j: (i, 0)),
                  pl.BlockSpec((1, D_MODEL), lambda i, j: (0, 0)),
                  pl.BlockSpec((None, D_MODEL, COL_TILE), lambda i, j: (layer, 0, j))],
        out_specs=[pl.BlockSpec((m, COL_TILE), lambda i, j: (i, j)), kv_spec(0), kv_spec(1), kv_spec(2),
                   pl.BlockSpec((None, D_MODEL, COL_TILE), lambda i, j: (j, 0, 0))],
        out_shape=[jax.ShapeDtypeStruct((m, IN_COLS), _BF16)] +
                  [jax.ShapeDtypeStruct((m, 2 * ATT_WIDTH), _F32)] * 3 +
                  [jax.ShapeDtypeStruct((IN_TILES, D_MODEL, COL_TILE), _BF16)],
        scratch_shapes=[pltpu.VMEM((m, D_MODEL), _BF16)],
        compiler_params=pltpu.CompilerParams(dimension_semantics=("parallel", "arbitrary"),
                                             vmem_limit_bytes=VMEM_LIMIT),
        name="inproj_sample",
    )(x2d, g.reshape(1, D_MODEL), w)


def _inproj_prompt_kernel(*refs, tm, aliased):
    n_in = 6 if aliased else 3
    x_ref, g_ref, w_ref = refs[:3]
    nat_ref, g1_ref, g2_ref, kv0_ref, kv1_ref, kv2_ref, h_s, de_s = refs[n_in:]
    kv_refs = (kv0_ref, kv1_ref, kv2_ref)
    res_refs = (None, g1_ref, g2_ref)
    j = pl.program_id(1)

    @pl.when(j == 0)
    def _():
        h_s[...] = _rms_norm(x_ref[...], g_ref[...]).astype(_BF16)

    def column_tiles(tiles):
        for c in range(tm // SUB_ROWS):
            r0 = c * SUB_ROWS
            for half, tile in enumerate(tiles):
                acc = jnp.dot(h_s[r0:r0 + SUB_ROWS, :], w_ref[half], preferred_element_type=_F32)
                grp = (tile - Q_TILE[0]) // 3 if Q_TILE[0] <= tile < U_TILE0 else None
                part = None if grp is None or tile == Q_TILE[grp] else tile - Q_TILE[grp] - 1
                dil = 1 if grp is None else DIL_GROUPS[grp][1]
                if dil == 1:
                    nat_ref[r0:r0 + SUB_ROWS, half * COL_TILE:(half + 1) * COL_TILE] = acc.astype(_BF16)
                else:
                    for h in range(HEADS):
                        de_s[c, half, h] = acc[:, _head_cols(h)]
                    n = SUB_ROWS // dil
                    for res in range(dil):
                        for h in range(HEADS):
                            res_refs[grp][res, c * n:(c + 1) * n, _head_cols(h, (tile - Q_TILE[grp]) * COL_TILE)] = (
                                de_s[c, half, h, pl.ds(res, n, stride=dil), :].astype(_BF16))
                if part is not None:
                    keep = min(DIL_GROUPS[grp][0], tm)
                    lo = max(r0, tm - keep)
                    n = r0 + SUB_ROWS - lo
                    if n > 0:
                        for h in range(HEADS):
                            dst = pl.ds((lo - (tm - keep)) * KV_ROWS + part * HEADS + h, n, stride=KV_ROWS)
                            kv_refs[grp][dst, :] = acc[lo - r0:lo - r0 + n, _head_cols(h)]

    for step in range(IN_TILES // IN_STEP_TILES):
        @pl.when(j == step)
        def _(step=step):
            column_tiles(tuple(range(step * IN_STEP_TILES, (step + 1) * IN_STEP_TILES)))


def _inproj_prompt(x2d, g, w, batch, seq, tm, layer, prev_kv):
    m = x2d.shape[0]
    tps = seq // tm
    aliased = prev_kv is not None
    (w0, _), (w1, d1), (w2, d2) = DIL_GROUPS
    assert w2 == seq and w0 <= tm and w1 <= tm and tm % SUB_ROWS == 0

    in_specs = [pl.BlockSpec((tm, D_MODEL), lambda i, j: (i, 0)),
                pl.BlockSpec((1, D_MODEL), lambda i, j: (0, 0)),
                pl.BlockSpec((IN_STEP_TILES, D_MODEL, COL_TILE), lambda i, j: (j, 0, 0))]
    args = [x2d, g.reshape(1, D_MODEL), w]
    if aliased:
        in_specs += [pl.BlockSpec(memory_space=pl.ANY)] * 3
        args += list(prev_kv)

    def res_spec(dil):
        return pl.BlockSpec((None, dil, tm // dil, 3 * ATT_WIDTH), lambda i, j: (i // tps, 0, i % tps, 0),
                            pipeline_mode=pl.Buffered(1))

    assert Q_TILE[1] % IN_STEP_TILES == 0 and U_TILE0 % IN_STEP_TILES == 0 and IN_TILES - U_TILE0 == IN_STEP_TILES
    nat_lo, u_step = Q_TILE[1] // IN_STEP_TILES, U_TILE0 // IN_STEP_TILES
    out_specs = [
        pl.BlockSpec((tm, IN_STEP_TILES * COL_TILE),
                     lambda i, j: (i, jnp.clip(j, 0, nat_lo - 1) + jnp.clip(j - u_step + 1, 0, 1))),
        res_spec(d1), res_spec(d2),
        pl.BlockSpec((None, None, w0 * KV_ROWS, HEAD_DIM), lambda i, j: (layer, i // tps, 0, 0),
                     pipeline_mode=pl.Buffered(1)),
        pl.BlockSpec((None, None, w1 * KV_ROWS, HEAD_DIM), lambda i, j: (layer, i // tps, 0, 0),
                     pipeline_mode=pl.Buffered(1)),
        pl.BlockSpec((None, None, tm * KV_ROWS, HEAD_DIM), lambda i, j: (layer, i // tps, i % tps, 0),
                     pipeline_mode=pl.Buffered(1)),
    ]
    out_shape = [
        jax.ShapeDtypeStruct((m, NAT_TILES * COL_TILE), _BF16),
        jax.ShapeDtypeStruct((batch, d1, seq // d1, 3 * ATT_WIDTH), _BF16),
        jax.ShapeDtypeStruct((batch, d2, seq // d2, 3 * ATT_WIDTH), _BF16),
    ] + [jax.ShapeDtypeStruct((DEPTH, batch, win * KV_ROWS, HEAD_DIM), _F32) for win, _ in DIL_GROUPS]
    return pl.pallas_call(
        functools.partial(_inproj_prompt_kernel, tm=tm, aliased=aliased),
        grid=(m // tm, IN_TILES // IN_STEP_TILES),
        in_specs=in_specs, out_specs=out_specs, out_shape=out_shape,
        scratch_shapes=[pltpu.VMEM((tm, D_MODEL), _BF16),
                        pltpu.VMEM((tm // SUB_ROWS, IN_STEP_TILES, HEADS, SUB_ROWS, HEAD_DIM), _F32)],
        input_output_aliases={3: 3, 4: 4, 5: 5} if aliased else {},
        compiler_params=pltpu.CompilerParams(dimension_semantics=("arbitrary", "arbitrary"),
                                             vmem_limit_bytes=VMEM_LIMIT),
        name="inproj_prompt",
    )(*args)


QB = 128
G0_UNROLL = 3
G2_RES_PER_STEP = 4


def _attn_prompt_kernel(q0, k0, v0, q1, k1, v1, q2, k2, v2, o_ref, acc_s, m_s, l_s, shuf_s, *, seq):
    step = pl.program_id(1)
    n1 = DIL_GROUPS[1][1]
    n2_steps = DIL_GROUPS[2][1] // G2_RES_PER_STEP

    row0 = lax.broadcasted_iota(jnp.int32, (QB, QB), 0)
    col0 = lax.broadcasted_iota(jnp.int32, (QB, QB), 1)
    dist_first = jnp.where(col0 <= row0, (row0 - col0).astype(_F32), jnp.inf)
    row1 = lax.broadcasted_iota(jnp.int32, (QB, 2 * QB), 0)
    col1 = lax.broadcasted_iota(jnp.int32, (QB, 2 * QB), 1)
    d1 = QB + row1 - col1
    dist_next = jnp.where((d1 >= 0) & (d1 <= QB), d1.astype(_F32), jnp.inf)
    ones = jnp.ones((2 * QB, HEAD_DIM), _BF16)

    seg = seq // n1
    chunk = QB // n1
    assert n1 == 4 and DIL_GROUPS[2][1] == 16

    def blocks(specs):
        tiles = []
        for slot, (qr, kr, vr, grp, res, i, is_first_block) in enumerate(specs):
            dil = DIL_GROUPS[grp][1]
            for h in range(HEADS):
                hs = _head_cols(h)
                slope_step = SLOPES[grp * HEADS + h] * dil
                if is_first_block:
                    q, k, v = qr[0:QB, hs], kr[0:QB, hs], vr[0:QB, hs]
                    bias, one = dist_first * slope_step, ones[0:QB]
                else:
                    qs = pl.ds(pl.multiple_of(i * QB, QB), QB)
                    ks = pl.ds(pl.multiple_of(i * QB - QB, QB), 2 * QB)
                    q, k, v = qr[qs, hs], kr[ks, hs], vr[ks, hs]
                    bias, one = dist_next * slope_step, ones
                tiles.append((q, k, jnp.concatenate([v, one], axis=1), bias, grp, res, i, is_first_block, slot, h))
        scores = [lax.dot_general(t[0], t[1], (((1,), (1,)), ((), ())), preferred_element_type=_F32) for t in tiles]
        probs, maxes = [], []
        for t, s in zip(tiles, scores):
            s = s * ATT_SCALE - t[3]
            m = jnp.max(s, axis=1, keepdims=True)
            probs.append(jnp.exp(s - m).astype(_BF16))
            maxes.append(jnp.broadcast_to(m, (QB, HEAD_DIM)))
        outs = [jnp.dot(p, t[2], preferred_element_type=_F32) for t, p in zip(tiles, probs)]
        for (_, _, _, _, grp, res, i, is_first_block, slot, h), m, acc_l in zip(tiles, maxes, outs):
            acc, l = acc_l[:, :HEAD_DIM], acc_l[:, HEAD_DIM:]
            if grp == 0:
                for a, (val, ref) in enumerate(((acc, acc_s), (m, m_s), (l, l_s))):
                    tmp = shuf_s.at[(slot * HEADS + h) * 3 + a]
                    tmp[...] = val
                    for r4 in range(n1):
                        dst = r4 * seg + i * chunk
                        dst = pl.ds(dst if is_first_block else pl.multiple_of(dst, chunk), chunk)
                        ref[h, dst, :] = tmp[pl.ds(r4, chunk, stride=n1), :]
            else:
                if grp == 1:
                    rows = pl.ds(pl.multiple_of(res * seg + i * QB, QB), QB)
                else:
                    rows = pl.ds((res & (n1 - 1)) * seg + (res >> 2), QB, stride=n1)
                m_old, l_old, a_old = m_s[h, rows, :], l_s[h, rows, :], acc_s[h, rows, :]
                m_new = jnp.maximum(m_old, m)
                w_old = jnp.exp(m_old - m_new)
                w_new = jnp.exp(m - m_new)
                acc_s[h, rows, :] = a_old * w_old + acc * w_new
                l_s[h, rows, :] = l_old * w_old + l * w_new
                m_s[h, rows, :] = m_new

    @pl.when(step == 0)
    def _():
        n_blk = seq // QB
        assert (n_blk - 1) % G0_UNROLL == 0
        blocks([(q0, k0, v0, 0, 0, 0, True)])

        def body(it, carry):
            blocks([(q0, k0, v0, 0, 0, 1 + it * G0_UNROLL + u, False) for u in range(G0_UNROLL)])
            return carry
        lax.fori_loop(0, (n_blk - 1) // G0_UNROLL, body, 0)

    @pl.when((step >= 1) & (step <= n1))
    def _():
        blocks([(q1, k1, v1, 1, step - 1, i, i == 0) for i in range(seg // QB)])

    @pl.when(step > n1)
    def _():
        assert seq // DIL_GROUPS[2][1] == QB
        blocks([(q2.at[r], k2.at[r], v2.at[r], 2, (step - 1 - n1) * G2_RES_PER_STEP + r, 0, True)
                for r in range(G2_RES_PER_STEP)])

    @pl.when(step == n1 + n2_steps)
    def _():
        def body(c, carry):
            for h in range(HEADS):
                tmp = shuf_s.at[h]
                for r4 in range(n1):
                    src = pl.ds(pl.multiple_of(r4 * seg + c * chunk, chunk), chunk)
                    tmp[pl.ds(r4, chunk, stride=n1), :] = acc_s[h, src, :] / l_s[h, src, :]
                o_ref[pl.ds(pl.multiple_of(c * QB, QB), QB), _head_cols(h)] = tmp[...].astype(_BF16)
            return carry
        lax.fori_loop(0, seq // QB, body, 0)


def _attn_prompt(nat, qkv1, qkv2, batch, seq):
    n1, n2 = DIL_GROUPS[1][1], DIL_GROUPS[2][1]
    n2_steps = n2 // G2_RES_PER_STEP
    nat3 = nat.reshape(batch, seq, NAT_TILES * COL_TILE)
    in_specs = [pl.BlockSpec((None, seq, COL_TILE), lambda b, s, t=t: (b, 0, Q_TILE[0] + t)) for t in range(3)]
    in_specs += [pl.BlockSpec((None, None, seq // n1, COL_TILE),
                              lambda b, s, t=t: (b, jnp.clip(s - 1, 0, n1 - 1), 0, t)) for t in range(3)]
    in_specs += [pl.BlockSpec((None, G2_RES_PER_STEP, seq // n2, COL_TILE),
                              lambda b, s, t=t: (b, jnp.clip(s - 1 - n1, 0, n2_steps - 1), 0, t)) for t in range(3)]
    args = [nat3] * 3 + [qkv1] * 3 + [qkv2] * 3
    return pl.pallas_call(
        functools.partial(_attn_prompt_kernel, seq=seq),
        grid=(batch, 1 + n1 + n2_steps),
        in_specs=in_specs,
        out_specs=pl.BlockSpec((None, seq, ATT_WIDTH), lambda b, s: (b, 0, 0)),
        out_shape=jax.ShapeDtypeStruct((batch, seq, ATT_WIDTH), _BF16),
        scratch_shapes=[pltpu.VMEM((HEADS, seq, HEAD_DIM), _F32)] * 3
                       + [pltpu.VMEM((G0_UNROLL * HEADS * 3, QB, HEAD_DIM), _F32)],
        compiler_params=pltpu.CompilerParams(dimension_semantics=("parallel", "arbitrary"),
                                             vmem_limit_bytes=VMEM_LIMIT),
        name="attn_prompt",
    )(*args)


def _attn_sample_kernel(*refs, tdec, aliased):
    n_in = 10 if aliased else 7
    proj_ref, n0, n1, n2, c0, c1, c2 = refs[:7]
    yb_ref, o0, o1, o2 = refs[n_in:]
    news, caches, outs = (n0, n1, n2), (c0, c1, c2), (o0, o1, o2)
    for h in range(HEADS):
        pieces = []
        for grp, (win, dil) in enumerate(DIL_GROUPS):
            slope = SLOPES[grp * HEADS + h]
            q = proj_ref[:, _head_cols(h, Q_TILE[grp] * COL_TILE)]
            k_new = news[grp][:, _head_cols(h)]
            v_new = news[grp][:, _head_cols(h, ATT_WIDTH)]
            k_old = caches[grp][pl.ds(h, win, stride=KV_ROWS), :]
            v_old = caches[grp][pl.ds(HEADS + h, win, stride=KV_ROWS), :]
            for k, v, n_keys, base in ((k_old, v_old, win, win), (k_new, v_new, tdec, 0)):
                s = lax.dot_general(q, k.astype(_BF16), (((1,), (1,)), ((), ())), preferred_element_type=_F32)
                t = lax.broadcasted_iota(jnp.int32, (tdec, n_keys), 0)
                c = lax.broadcasted_iota(jnp.int32, (tdec, n_keys), 1)
                dist = base + t - c
                ok = (dist >= 0) & (dist <= win) & ((dist & (dil - 1)) == 0)
                s = jnp.where(ok, s * ATT_SCALE - slope * dist.astype(_F32), -jnp.inf)
                pieces.append((s, v.astype(_BF16)))
        m = functools.reduce(jnp.maximum, [jnp.max(s, axis=1, keepdims=True) for s, _ in pieces])
        l = jnp.zeros((tdec, 1), _F32)
        acc = jnp.zeros((tdec, HEAD_DIM), _F32)
        for s, v in pieces:
            p = jnp.exp(s - m)
            l = l + jnp.sum(p, axis=1, keepdims=True)
            acc = acc + jnp.dot(p.astype(_BF16), v, preferred_element_type=_F32)
        yb_ref[:, _head_cols(h)] = (acc / l).astype(_BF16)
    for grp, (win, _) in enumerate(DIL_GROUPS):
        kept = (win - tdec) * KV_ROWS
        outs[grp][0:kept, :] = caches[grp][tdec * KV_ROWS:win * KV_ROWS, :]
        for part in range(2):
            for h in range(HEADS):
                outs[grp][pl.ds(kept + part * HEADS + h, tdec, stride=KV_ROWS), :] = (
                    news[grp][:, _head_cols(h, part * ATT_WIDTH)])


def _attn_sample(proj, kv_new, caches, batch, tdec, layer, prev_out):
    aliased = prev_out is not None
    in_specs = [pl.BlockSpec((None, tdec, IN_COLS), lambda b: (b, 0, 0))]
    in_specs += [pl.BlockSpec((None, tdec, 2 * ATT_WIDTH), lambda b: (b, 0, 0))] * 3
    cache_specs = [pl.BlockSpec((None, None, win * KV_ROWS, HEAD_DIM), lambda b: (layer, b, 0, 0)) for win, _ in DIL_GROUPS]
    in_specs += cache_specs
    args = [proj.reshape(batch, tdec, IN_COLS)] + [k.reshape(batch, tdec, 2 * ATT_WIDTH) for k in kv_new] + list(caches)
    if aliased:
        in_specs += [pl.BlockSpec(memory_space=pl.ANY)] * 3
        args += list(prev_out)
    out_specs = [pl.BlockSpec((None, tdec, ATT_WIDTH), lambda b: (b, 0, 0))] + cache_specs
    out_shape = [jax.ShapeDtypeStruct((batch, tdec, ATT_WIDTH), _BF16)]
    out_shape += [jax.ShapeDtypeStruct((DEPTH, batch, win * KV_ROWS, HEAD_DIM), _F32) for win, _ in DIL_GROUPS]
    return pl.pallas_call(
        functools.partial(_attn_sample_kernel, tdec=tdec, aliased=aliased),
        grid=(batch,),
        in_specs=in_specs, out_specs=out_specs, out_shape=out_shape,
        input_output_aliases={7: 1, 8: 2, 9: 3} if aliased else {},
        compiler_params=pltpu.CompilerParams(dimension_semantics=("arbitrary",), vmem_limit_bytes=VMEM_LIMIT),
        name="attn_sample",
    )(*args)


def _mixer_kernel(*refs, nb, tt, start, from_zero):
    if from_zero:
        pa_ref, pu_ref, yb_ref, x_ref, cw_ref, pw_ref, psc_ref, wo_ref = refs[:8]
    else:
        pa_ref, pu_ref, yb_ref, x_ref, cs0_ref, ps0_ref, cw_ref, pw_ref, psc_ref, wo_ref = refs[:10]
    out_ref, cs_out, ps_out, ext_a, ext_u, yc = refs[-6:]
    ti = pl.program_id(1)
    rows = nb * tt
    pool_halo = POOL_BLOCK if from_zero else POOL_HALO

    @pl.when(ti == 0)
    def _():
        if from_zero:
            ext_a[:, 0:CONV_HALO, :] = jnp.zeros((nb, CONV_HALO, CONV_WIDTH), _F32)
            ext_u[:, 0:pool_halo, :] = jnp.zeros((nb, pool_halo, POOL_WIDTH), _BF16)
        else:
            ext_a[:, 0:CONV_HALO, :] = cs0_ref[...]
            ext_u[:, 0:pool_halo, :] = ps0_ref[...]

    yb = yb_ref[...].astype(_F32).reshape(rows, ATT_WIDTH).astype(_BF16)
    mixed = jnp.dot(yb, wo_ref[CONV_WIDTH:CONV_WIDTH + ATT_WIDTH, :], preferred_element_type=_F32)

    pa = pa_ref[...].astype(_F32)
    xa, gate_b, gate_c = (pa[:, :, k * CONV_WIDTH:(k + 1) * CONV_WIDTH] for k in range(3))
    prod = gate_c * xa
    ext_a[:, CONV_HALO:CONV_HALO + tt, :] = prod
    cw = cw_ref[...]
    cu = (cw[0] * ext_a[:, CONV_HALO - 2:CONV_HALO - 2 + tt, :]
          + cw[1] * ext_a[:, CONV_HALO - 1:CONV_HALO - 1 + tt, :] + cw[2] * prod)
    ya = (gate_b * cu).reshape(rows, CONV_WIDTH).astype(_BF16)
    mixed = mixed + jnp.dot(ya, wo_ref[0:CONV_WIDTH, :], preferred_element_type=_F32)

    ext_u[:, pool_halo:pool_halo + tt, :] = pu_ref[...].astype(ext_u.dtype)
    if from_zero:
        pb = POOL_BLOCK
        row = lax.broadcasted_iota(jnp.int32, (pb, 2 * pb), 0)
        col = lax.broadcasted_iota(jnp.int32, (pb, 2 * pb), 1)
        lag = pb + row - col
        for gi, w in enumerate(POOL_WINDOWS):
            cols = slice(gi * POOL_GROUP, (gi + 1) * POOL_GROUP)
            band = ((lag >= 0) & (lag < w)).astype(_F32).astype(_BF16)
            for r in range(tt // pb):
                tot = jnp.dot(band, ext_u[0, r * pb:(r + 2) * pb, cols], preferred_element_type=_F32)
                pos = start + ti * tt + r * pb + lax.broadcasted_iota(jnp.int32, (pb, 1), 0)
                cnt = jnp.minimum(w, pos + 1).astype(_F32)
                ug = pu_ref[0, r * pb:(r + 1) * pb, cols].astype(_F32)
                diff = (tot / cnt - ug).astype(_BF16)
                z = jnp.dot(diff, pw_ref[gi], preferred_element_type=_F32) * psc_ref[:, cols]
                yc[r * pb:(r + 1) * pb, cols] = z.astype(_BF16)
    else:
        u = pu_ref[...].astype(_F32)
        pos = start + ti * tt + lax.broadcasted_iota(jnp.int32, (1, tt, 1), 1)
        for gi, w in enumerate(POOL_WINDOWS):
            cols = slice(gi * POOL_GROUP, (gi + 1) * POOL_GROUP)
            ug = u[:, :, cols]
            tot = ug
            for k in range(1, w):
                tot = tot + ext_u[:, pool_halo - k:pool_halo - k + tt, cols]
            cnt = jnp.minimum(w, pos + 1).astype(_F32)
            diff = (tot / cnt - ug).reshape(rows, POOL_GROUP).astype(_BF16)
            z = jnp.dot(diff, pw_ref[gi], preferred_element_type=_F32) * psc_ref[:, cols]
            yc[:, cols] = z.astype(_BF16)

    mixed = mixed + jnp.dot(yc[...], wo_ref[CONV_WIDTH + ATT_WIDTH:, :], preferred_element_type=_F32)
    out_ref[...] = x_ref[...] + mixed.reshape(nb, tt, D_MODEL)

    tail_a = ext_a[:, tt:tt + CONV_HALO, :]
    ext_a[:, 0:CONV_HALO, :] = tail_a
    cs_out[...] = tail_a
    ps_out[...] = ext_u[:, pool_halo + tt - POOL_HALO:pool_halo + tt, :].astype(_F32)
    ext_u[:, 0:pool_halo, :] = ext_u[:, tt:tt + pool_halo, :]


def _mixer(proj, yb, x, conv_state, pool_state, conv_w, pool_w, pool_scale, w_out, layer, nb, tt, start, u_tile0):
    batch, seq, _ = x.shape
    proj3 = proj.reshape(batch, seq, proj.shape[-1])
    u_blk = u_tile0 * COL_TILE // POOL_WIDTH
    const2 = lambda b, t: (0, 0)
    from_zero = conv_state is None and pool_state is None
    assert from_zero or (conv_state is not None and pool_state is not None)
    assert not from_zero or (nb == 1 and start == 0 and tt % POOL_BLOCK == 0)
    state_specs = [] if from_zero else [pl.BlockSpec((nb, CONV_HALO, CONV_WIDTH), lambda b, t: (b, 0, 0)),
                                        pl.BlockSpec((nb, POOL_HALO, POOL_WIDTH), lambda b, t: (b, 0, 0))]
    state_args = [] if from_zero else [conv_state, pool_state]
    ext_u = (pltpu.VMEM((nb, tt + POOL_BLOCK, POOL_WIDTH), _BF16) if from_zero
             else pltpu.VMEM((nb, tt + POOL_HALO, POOL_WIDTH), _F32))
    return pl.pallas_call(
        functools.partial(_mixer_kernel, nb=nb, tt=tt, start=start, from_zero=from_zero),
        grid=(batch // nb, seq // tt),
        in_specs=[pl.BlockSpec((nb, tt, 3 * CONV_WIDTH), lambda b, t: (b, t, 0)),
                  pl.BlockSpec((nb, tt, POOL_WIDTH), lambda b, t: (b, t, u_blk)),
                  pl.BlockSpec((nb, tt, ATT_WIDTH), lambda b, t: (b, t, 0)),
                  pl.BlockSpec((nb, tt, D_MODEL), lambda b, t: (b, t, 0))] + state_specs + [
                  pl.BlockSpec((3, CONV_WIDTH), const2),
                  pl.BlockSpec((None, len(POOL_WINDOWS), POOL_GROUP, POOL_GROUP), lambda b, t: (layer, 0, 0, 0)),
                  pl.BlockSpec((1, POOL_WIDTH), const2),
                  pl.BlockSpec((None, D_MODEL, D_MODEL), lambda b, t: (layer, 0, 0))],
        out_specs=[pl.BlockSpec((nb, tt, D_MODEL), lambda b, t: (b, t, 0)),
                   pl.BlockSpec((nb, CONV_HALO, CONV_WIDTH), lambda b, t: (b, 0, 0)),
                   pl.BlockSpec((nb, POOL_HALO, POOL_WIDTH), lambda b, t: (b, 0, 0))],
        out_shape=[jax.ShapeDtypeStruct((batch, seq, D_MODEL), _F32),
                   jax.ShapeDtypeStruct((batch, CONV_HALO, CONV_WIDTH), _F32),
                   jax.ShapeDtypeStruct((batch, POOL_HALO, POOL_WIDTH), _F32)],
        scratch_shapes=[pltpu.VMEM((nb, tt + CONV_HALO, CONV_WIDTH), _F32), ext_u,
                        pltpu.VMEM((nb * tt, POOL_WIDTH), _BF16)],
        compiler_params=pltpu.CompilerParams(dimension_semantics=("parallel", "arbitrary"),
                                             vmem_limit_bytes=VMEM_LIMIT),
        name="mixer",
    )(proj3, proj3, yb, x, *state_args, conv_w, pool_w, pool_scale.reshape(1, POOL_WIDTH), w_out)


def _ffn_sample_kernel(x_ref, n2_ref, wg_ref, wu_ref, cw_ref, wd_ref, st0_ref, fn_ref,
                       out_ref, st_out, wgt_ref, wut_ref, wdt_ref, h_s, ext_s, *, nb, tt, final):
    f = pl.program_id(0)
    rows = nb * tt

    @pl.when(f == 0)
    def _():
        x = x_ref[...]
        h_s[...] = _rms_norm(x, n2_ref[...]).reshape(rows, D_MODEL).astype(_BF16)
        out_ref[...] = x

    def tile(width):
        cols = slice(0, width)
        wg, wu, wd = wg_ref[:, cols].astype(_BF16), wu_ref[:, cols].astype(_BF16), wd_ref[cols, :].astype(_BF16)
        wgt_ref[:, cols], wut_ref[:, cols], wdt_ref[cols, :] = wg, wu, wd
        if width < FF_TILE:
            wgt_ref[:, width:] = jnp.zeros((D_MODEL, FF_TILE - width), _BF16)
            wut_ref[:, width:] = jnp.zeros((D_MODEL, FF_TILE - width), _BF16)
            wdt_ref[width:, :] = jnp.zeros((FF_TILE - width, D_MODEL), _BF16)
        h = h_s[...]
        gate = jnp.dot(h, wg, preferred_element_type=_F32).reshape(nb, tt, width)
        up = jnp.dot(h, wu, preferred_element_type=_F32).reshape(nb, tt, width)
        ext_s[:, 0:CONV_HALO, cols] = st0_ref[:, :, cols]
        ext_s[:, CONV_HALO:CONV_HALO + tt, cols] = gate
        cw = cw_ref[:, cols]
        gc = (cw[0] * ext_s[:, CONV_HALO - 2:CONV_HALO - 2 + tt, cols]
              + cw[1] * ext_s[:, CONV_HALO - 1:CONV_HALO - 1 + tt, cols] + cw[2] * gate)
        st_out[:, :, cols] = ext_s[:, tt:tt + CONV_HALO, cols]
        act = (jax.nn.silu(gc) * up).reshape(rows, width).astype(_BF16)
        out_ref[...] += jnp.dot(act, wd, preferred_element_type=_F32).reshape(nb, tt, D_MODEL)

    @pl.when(f < FF_TILES - 1)
    def _():
        tile(FF_TILE)

    @pl.when(f == FF_TILES - 1)
    def _():
        tile(FF_LAST)
        if final:
            out_ref[...] = _rms_norm(out_ref[...], fn_ref[...])


def _ffn_prompt_kernel(x_ref, n2_ref, wg_ref, wu_ref, cw_ref, wd_ref, fn_ref,
                       out_ref, st_out, h_s, ext_s, carry_s, *, tt, final):
    ti = pl.program_id(1)
    f = pl.program_id(2)

    @pl.when(f == 0)
    def _():
        x = x_ref[...]
        h_s[...] = _rms_norm(x, n2_ref[...]).astype(_BF16)
        out_ref[...] = x

    @pl.when(ti == 0)
    def _():
        ext_s[0:CONV_HALO, :] = jnp.zeros((CONV_HALO, FF_TILE), _F32)

    @pl.when(ti > 0)
    def _():
        ext_s[0:CONV_HALO, :] = carry_s[f]

    wg, wu, wd, cw = wg_ref[...], wu_ref[...], wd_ref[...], cw_ref[...]
    chunks = [slice(r0, r0 + FFN_SUB_ROWS) for r0 in range(0, tt, FFN_SUB_ROWS)]
    for rs in chunks:
        gate = jnp.dot(h_s[rs, :], wg, preferred_element_type=_F32)
        ext_s[CONV_HALO + rs.start:CONV_HALO + rs.stop, :] = gate
    ups = [jnp.dot(h_s[rs, :], wu, preferred_element_type=_F32) for rs in chunks]
    for rs, up in zip(chunks, ups):
        gc = (cw[0] * ext_s[CONV_HALO - 2 + rs.start:CONV_HALO - 2 + rs.stop, :]
              + cw[1] * ext_s[CONV_HALO - 1 + rs.start:CONV_HALO - 1 + rs.stop, :]
              + cw[2] * ext_s[CONV_HALO + rs.start:CONV_HALO + rs.stop, :])
        act = (jax.nn.silu(gc) * up).astype(_BF16)
        out_ref[rs, :] += jnp.dot(act, wd, preferred_element_type=_F32)
    tail = ext_s[tt:tt + CONV_HALO, :]
    carry_s[f] = tail
    st_out[...] = tail

    if final:
        @pl.when(f == FF_TILES - 1)
        def _():
            out_ref[...] = _rms_norm(out_ref[...], fn_ref[...])


def _ffn_sample(x, norm2, w_gate, w_up, conv_w, w_down, state, final_norm, layer, final):
    nb, tt, _ = x.shape
    vec = lambda f: (0, 0)
    return pl.pallas_call(
        functools.partial(_ffn_sample_kernel, nb=nb, tt=tt, final=final),
        grid=(FF_TILES,),
        in_specs=[pl.BlockSpec((nb, tt, D_MODEL), lambda f: (0, 0, 0)),
                  pl.BlockSpec((1, D_MODEL), vec),
                  pl.BlockSpec((None, D_MODEL, FF_TILE), lambda f: (layer, 0, f)),
                  pl.BlockSpec((None, D_MODEL, FF_TILE), lambda f: (layer, 0, f)),
                  pl.BlockSpec((3, FF_TILE), lambda f: (0, f)),
                  pl.BlockSpec((None, FF_TILE, D_MODEL), lambda f: (layer, f, 0)),
                  pl.BlockSpec((nb, CONV_HALO, FF_TILE), lambda f: (0, 0, f)),
                  pl.BlockSpec((1, D_MODEL), vec)],
        out_specs=[pl.BlockSpec((nb, tt, D_MODEL), lambda f: (0, 0, 0)),
                   pl.BlockSpec((nb, CONV_HALO, FF_TILE), lambda f: (0, 0, f)),
                   pl.BlockSpec((None, D_MODEL, FF_TILE), lambda f: (f, 0, 0)),
                   pl.BlockSpec((None, D_MODEL, FF_TILE), lambda f: (f, 0, 0)),
                   pl.BlockSpec((None, FF_TILE, D_MODEL), lambda f: (f, 0, 0))],
        out_shape=[jax.ShapeDtypeStruct((nb, tt, D_MODEL), _F32),
                   jax.ShapeDtypeStruct((nb, CONV_HALO, D_FF), _F32),
                   jax.ShapeDtypeStruct((FF_TILES, D_MODEL, FF_TILE), _BF16),
                   jax.ShapeDtypeStruct((FF_TILES, D_MODEL, FF_TILE), _BF16),
                   jax.ShapeDtypeStruct((FF_TILES, FF_TILE, D_MODEL), _BF16)],
        scratch_shapes=[pltpu.VMEM((nb * tt, D_MODEL), _BF16),
                        pltpu.VMEM((nb, tt + CONV_HALO, FF_TILE), _F32)],
        compiler_params=pltpu.CompilerParams(dimension_semantics=("arbitrary",), vmem_limit_bytes=VMEM_LIMIT),
        name="ffn_sample",
    )(x, norm2.reshape(1, D_MODEL), w_gate, w_up, conv_w, w_down, state, final_norm.reshape(1, D_MODEL))


def _ffn_prompt(x, norm2, w_gate_t, w_up_t, conv_w, w_down_t, final_norm, tt, final):
    batch, seq, _ = x.shape
    vec = lambda b, t, f: (0, 0)
    return pl.pallas_call(
        functools.partial(_ffn_prompt_kernel, tt=tt, final=final),
        grid=(batch, seq // tt, FF_TILES),
        in_specs=[pl.BlockSpec((None, tt, D_MODEL), lambda b, t, f: (b, t, 0)),
                  pl.BlockSpec((1, D_MODEL), vec),
                  pl.BlockSpec((None, D_MODEL, FF_TILE), lambda b, t, f: (f, 0, 0)),
                  pl.BlockSpec((None, D_MODEL, FF_TILE), lambda b, t, f: (f, 0, 0)),
                  pl.BlockSpec((3, FF_TILE), lambda b, t, f: (0, f)),
                  pl.BlockSpec((None, FF_TILE, D_MODEL), lambda b, t, f: (f, 0, 0)),
                  pl.BlockSpec((1, D_MODEL), vec)],
        out_specs=[pl.BlockSpec((None, tt, D_MODEL), lambda b, t, f: (b, t, 0)),
                   pl.BlockSpec((None, None, CONV_HALO, FF_TILE), lambda b, t, f: (b, t, 0, f))],
        out_shape=[jax.ShapeDtypeStruct((batch, seq, D_MODEL), _F32),
                   jax.ShapeDtypeStruct((batch, seq // tt, CONV_HALO, FF_TILES * FF_TILE), _F32)],
        scratch_shapes=[pltpu.VMEM((tt, D_MODEL), _BF16),
                        pltpu.VMEM((tt + CONV_HALO, FF_TILE), _F32),
                        pltpu.VMEM((FF_TILES, CONV_HALO, FF_TILE), _F32)],
        compiler_params=pltpu.CompilerParams(dimension_semantics=("parallel", "arbitrary", "arbitrary"),
                                             vmem_limit_bytes=VMEM_LIMIT),
        name="ffn_prompt",
    )(x, norm2.reshape(1, D_MODEL), w_gate_t, w_up_t, conv_w, w_down_t, final_norm.reshape(1, D_MODEL))


def _pad_rows_front(a, rows):
    return jnp.pad(a, ((0, 0), (rows - a.shape[1], 0), (0, 0)))


def kernel(x_prompt, x_sample, cache_kv_w128, cache_kv_w512, cache_kv_w2048, state_conv_a, state_pool, state_ffn_conv, norm1, w_in, conv_a_w, w_out, pool_w, pool_scale, norm2, w_gate, w_up, ffn_conv_w, w_down, final_norm):
    bp, sp, _ = x_prompt.shape
    bs, ss, _ = x_sample.shape
    caches = tuple(c.reshape(DEPTH, bs, win * KV_ROWS, HEAD_DIM)
                   for c, (win, _) in zip((cache_kv_w128, cache_kv_w512, cache_kv_w2048), DIL_GROUPS))
    tm, tt = 1024, 512
    w_out_b, pool_w_b = w_out.astype(_BF16), pool_w.astype(_BF16)
    xp, xs = x_prompt, x_sample
    kv_p = kv_s = None
    small_p, small_s = [], []
    for l in range(DEPTH):
        final = l == DEPTH - 1

        proj, *kv_new, w_in_t = _inproj_sample(xs.reshape(bs * ss, D_MODEL), norm1[l], w_in, l)
        yb, *kv_s = _attn_sample(proj, kv_new, caches, bs, ss, l, kv_s)
        xs, conv_o, pool_o = _mixer(proj, yb, xs, _pad_rows_front(state_conv_a[l], CONV_HALO),
                                    _pad_rows_front(state_pool[l], POOL_HALO), conv_a_w[l], pool_w_b, pool_scale[l],
                                    w_out_b, l, bs, ss, PAST_LEN, U_TILE0)
        xs, ffn_o, w_gate_t, w_up_t, w_down_t = _ffn_sample(
            xs, norm2[l], w_gate, w_up, ffn_conv_w[l], w_down, _pad_rows_front(state_ffn_conv[l], CONV_HALO),
            final_norm, l, final)
        small_s.append((conv_o[:, CONV_HALO - 2:], pool_o[:, POOL_HALO - POOL_BUF:], ffn_o[:, CONV_HALO - 2:]))

        nat, qkv1, qkv2, *kv_p = _inproj_prompt(xp.reshape(bp * sp, D_MODEL), norm1[l], w_in_t, bp, sp, tm, l, kv_p)
        yb = _attn_prompt(nat, qkv1, qkv2, bp, sp)
        xp, conv_o, pool_o = _mixer(nat, yb, xp, None, None, conv_a_w[l], pool_w_b, pool_scale[l], w_out_b,
                                    l, 1, tt, 0, NAT_TILES - 2)
        conv_w_pad = jnp.pad(ffn_conv_w[l], ((0, 0), (0, FF_TILES * FF_TILE - D_FF)))
        xp, ffn_o = _ffn_prompt(xp, norm2[l], w_gate_t, w_up_t, conv_w_pad, w_down_t, final_norm, tm, final)
        small_p.append((conv_o[:, CONV_HALO - 2:], pool_o[:, POOL_HALO - POOL_BUF:],
                        ffn_o[:, -1, CONV_HALO - 2:, :D_FF]))

    out = [xp, xs]
    for g, (win, _) in enumerate(DIL_GROUPS):
        out += [kv_p[g].reshape(DEPTH, bp, win, 2, HEADS, HEAD_DIM), kv_s[g].reshape(DEPTH, bs, win, 2, HEADS, HEAD_DIM)]
    for i in range(3):
        out += [jnp.stack([s[i] for s in small_p]), jnp.stack([s[i] for s in small_s])]
    return tuple(out)
```

```python
import functools

import numpy as np
import jax
import jax.numpy as jnp
from jax import lax
from jax.experimental import pallas as pl
from jax.experimental.pallas import tpu as pltpu

D_MODEL = 2048
DEPTH = 2
PAST_LEN = 16384
CONV_WIDTH = 512
ATT_WIDTH = 512
HEADS = 4
HEAD_DIM = 128
DIL_GROUPS = ((128, 1), (512, 4), (2048, 16))
POOL_WIDTH = 1024
POOL_WINDOWS = (2, 4, 8, 16)
POOL_GROUP = 256
POOL_BUF = 15
IN_COLS = 7168
D_FF = 5504
RMS_EPS = 1e-6
ATT_SCALE = HEAD_DIM ** -0.5

COL_TILE = 512
IN_TILES = IN_COLS // COL_TILE
Q_TILE = (3, 6, 9)
U_TILE0 = 12
NAT_TILES = 8
KV_ROWS = 2 * HEADS
SUB_ROWS = 512
IN_STEP_TILES = 2
FF_TILE = 512
FF_TILES = -(-D_FF // FF_TILE)
FF_LAST = D_FF - (FF_TILES - 1) * FF_TILE
FFN_SUB_ROWS = 256
CONV_HALO = 8
POOL_HALO = 16
POOL_BLOCK = 128
VMEM_LIMIT = 56 * 1024 * 1024

_BF16 = jnp.bfloat16
_F32 = jnp.float32


def _alibi_slopes():
    h = np.arange(1, 3 * HEADS + 1, dtype=np.float32)
    return [float(v) for v in np.power(np.float32(2.0), -8.0 * h / (3 * HEADS))]


SLOPES = _alibi_slopes()


def _rms_norm(x, g):
    return (x * lax.rsqrt(jnp.mean(x * x, axis=-1, keepdims=True) + RMS_EPS)) * g


def _head_cols(h, base=0):
    return slice(base + h * HEAD_DIM, base + (h + 1) * HEAD_DIM)


def _inproj_sample_kernel(x_ref, g_ref, w_ref, proj_ref, kv0_ref, kv1_ref, kv2_ref, wt_ref, h_s):
    j = pl.program_id(1)

    @pl.when(j == 0)
    def _():
        h_s[...] = _rms_norm(x_ref[...], g_ref[...]).astype(_BF16)

    w = w_ref[...].astype(_BF16)
    wt_ref[...] = w
    acc = jnp.dot(h_s[...], w, preferred_element_type=_F32)
    proj_ref[...] = acc.astype(_BF16)
    for g, ref in enumerate((kv0_ref, kv1_ref, kv2_ref)):
        @pl.when((j == Q_TILE[g] + 1) | (j == Q_TILE[g] + 2))
        def _(ref=ref):
            ref[...] = acc


def _inproj_sample(x2d, g, w, layer):
    m = x2d.shape[0]

    def kv_spec(grp):
        return pl.BlockSpec((m, COL_TILE), lambda i, j: (i, jnp.clip(j - Q_TILE[grp] - 1, 0, 1)))

    return pl.pallas_call(
        _inproj_sample_kernel,
        grid=(1, IN_TILES),
        in_specs=[pl.BlockSpec((m, D_MODEL), lambda i, j: (i, 0)),
                  pl.BlockSpec((1, D_MODEL), lambda i, j: (0, 0)),
                  pl.BlockSpec((None, D_MODEL, COL_TILE), lambda i, j: (layer, 0, j))],
        out_specs=[pl.BlockSpec((m, COL_TILE), lambda i, j: (i, j)), kv_spec(0), kv_spec(1), kv_spec(2),
                   pl.BlockSpec((None, D_MODEL, COL_TILE), lambda i, j: (j, 0, 0))],
        out_shape=[jax.ShapeDtypeStruct((m, IN_COLS), _BF16)] +
                  [jax.ShapeDtypeStruct((m, 2 * ATT_WIDTH), _F32)] * 3 +
                  [jax.ShapeDtypeStruct((IN_TILES, D_MODEL, COL_TILE), _BF16)],
        scratch_shapes=[pltpu.VMEM((m, D_MODEL), _BF16)],
        compiler_params=pltpu.CompilerParams(dimension_semantics=("parallel", "arbitrary"),
                                             vmem_limit_bytes=VMEM_LIMIT),
        name="inproj_sample",
    )(x2d, g.reshape(1, D_MODEL), w)


def _inproj_prompt_kernel(*refs, tm, aliased):
    n_in = 6 if aliased else 3
    x_ref, g_ref, w_ref = refs[:3]
    nat_ref, g1_ref, g2_ref, kv0_ref, kv1_ref, kv2_ref, h_s, de_s = refs[n_in:]
    kv_refs = (kv0_ref, kv1_ref, kv2_ref)
    res_refs = (None, g1_ref, g2_ref)
    i = pl.program_id(0)
    j = pl.program_id(1)
    n_steps = IN_TILES // IN_STEP_TILES
    slot = i % 2

    @pl.when((i == 0) & (j == 0))
    def _():
        h_s[0] = _rms_norm(x_ref[...], g_ref[...]).astype(_BF16)

    def column_tiles(tiles, prepare_next):
        if prepare_next:
            h_s[1 - slot] = _rms_norm(x_ref[...], g_ref[...]).astype(_BF16)
        for c in range(tm // SUB_ROWS):
            r0 = c * SUB_ROWS
            for half, tile in enumerate(tiles):
                acc = jnp.dot(h_s[slot, r0:r0 + SUB_ROWS, :], w_ref[half], preferred_element_type=_F32)
                grp = (tile - Q_TILE[0]) // 3 if Q_TILE[0] <= tile < U_TILE0 else None
                part = None if grp is None or tile == Q_TILE[grp] else tile - Q_TILE[grp] - 1
                dil = 1 if grp is None else DIL_GROUPS[grp][1]
                if dil == 1:
                    nat_ref[r0:r0 + SUB_ROWS, half * COL_TILE:(half + 1) * COL_TILE] = acc.astype(_BF16)
                else:
                    for h in range(HEADS):
                        de_s[c, half, h] = acc[:, _head_cols(h)]
                    n = SUB_ROWS // dil
                    for res in range(dil):
                        for h in range(HEADS):
                            res_refs[grp][res, c * n:(c + 1) * n, _head_cols(h, (tile - Q_TILE[grp]) * COL_TILE)] = (
                                de_s[c, half, h, pl.ds(res, n, stride=dil), :].astype(_BF16))
                if part is not None:
                    keep = min(DIL_GROUPS[grp][0], tm)
                    lo = max(r0, tm - keep)
                    n = r0 + SUB_ROWS - lo
                    if n > 0:
                        for h in range(HEADS):
                            dst = pl.ds((lo - (tm - keep)) * KV_ROWS + part * HEADS + h, n, stride=KV_ROWS)
                            kv_refs[grp][dst, :] = acc[lo - r0:lo - r0 + n, _head_cols(h)]

    for step in range(n_steps):
        @pl.when(j == step)
        def _(step=step):
            column_tiles(tuple(range(step * IN_STEP_TILES, (step + 1) * IN_STEP_TILES)), step == n_steps - 1)


def _inproj_prompt(x2d, g, w, batch, seq, tm, layer, prev_kv):
    m = x2d.shape[0]
    tps = seq // tm
    aliased = prev_kv is not None
    (w0, _), (w1, d1), (w2, d2) = DIL_GROUPS
    assert w2 == seq and w0 <= tm and w1 <= tm and tm % SUB_ROWS == 0

    n_steps = IN_TILES // IN_STEP_TILES
    in_specs = [pl.BlockSpec((tm, D_MODEL),
                             lambda i, j: (jnp.minimum(i + (j == n_steps - 1).astype(jnp.int32), m // tm - 1), 0)),
                pl.BlockSpec((1, D_MODEL), lambda i, j: (0, 0)),
                pl.BlockSpec((IN_STEP_TILES, D_MODEL, COL_TILE), lambda i, j: (j, 0, 0))]
    args = [x2d, g.reshape(1, D_MODEL), w]
    if aliased:
        in_specs += [pl.BlockSpec(memory_space=pl.ANY)] * 3
        args += list(prev_kv)

    def row_tile(i, j, grp):
        last_step = (Q_TILE[grp] + 2) // IN_STEP_TILES
        return jnp.minimum(i + (j > last_step).astype(jnp.int32), m // tm - 1)

    def res_spec(grp, dil):
        def imap(i, j):
            it = row_tile(i, j, grp)
            return (it // tps, 0, it % tps, 0)
        return pl.BlockSpec((None, dil, tm // dil, 3 * ATT_WIDTH), imap, pipeline_mode=pl.Buffered(1))

    def kv_all_spec(grp):
        def imap(i, j):
            it = row_tile(i, j, grp)
            return (layer, it // tps, it % tps, 0)
        return pl.BlockSpec((None, None, tm * KV_ROWS, HEAD_DIM), imap, pipeline_mode=pl.Buffered(1))

    assert Q_TILE[1] % IN_STEP_TILES == 0 and U_TILE0 % IN_STEP_TILES == 0 and IN_TILES - U_TILE0 == IN_STEP_TILES
    nat_lo, u_step = Q_TILE[1] // IN_STEP_TILES, U_TILE0 // IN_STEP_TILES
    out_specs = [
        pl.BlockSpec((tm, IN_STEP_TILES * COL_TILE),
                     lambda i, j: (i, jnp.clip(j, 0, nat_lo - 1) + jnp.clip(j - u_step + 1, 0, 1))),
        res_spec(1, d1), res_spec(2, d2),
        pl.BlockSpec((None, None, w0 * KV_ROWS, HEAD_DIM), lambda i, j: (layer, i // tps, 0, 0),
                     pipeline_mode=pl.Buffered(1)),
        pl.BlockSpec((None, None, w1 * KV_ROWS, HEAD_DIM), lambda i, j: (layer, i // tps, 0, 0),
                     pipeline_mode=pl.Buffered(1)),
        kv_all_spec(2),
    ]
    out_shape = [
        jax.ShapeDtypeStruct((m, NAT_TILES * COL_TILE), _BF16),
        jax.ShapeDtypeStruct((batch, d1, seq // d1, 3 * ATT_WIDTH), _BF16),
        jax.ShapeDtypeStruct((batch, d2, seq // d2, 3 * ATT_WIDTH), _BF16),
    ] + [jax.ShapeDtypeStruct((DEPTH, batch, win * KV_ROWS, HEAD_DIM), _F32) for win, _ in DIL_GROUPS]
    return pl.pallas_call(
        functools.partial(_inproj_prompt_kernel, tm=tm, aliased=aliased),
        grid=(m // tm, n_steps),
        in_specs=in_specs, out_specs=out_specs, out_shape=out_shape,
        scratch_shapes=[pltpu.VMEM((2, tm, D_MODEL), _BF16),
                        pltpu.VMEM((tm // SUB_ROWS, IN_STEP_TILES, HEADS, SUB_ROWS, HEAD_DIM), _F32)],
        input_output_aliases={3: 3, 4: 4, 5: 5} if aliased else {},
        compiler_params=pltpu.CompilerParams(dimension_semantics=("arbitrary", "arbitrary"),
                                             vmem_limit_bytes=VMEM_LIMIT),
        name="inproj_prompt",
    )(*args)


QB = 128
G0_UNROLL = 3
G2_RES_PER_STEP = 4


def _attn_prompt_kernel(q0, k0, v0, q1, k1, v1, q2, k2, v2, o_ref, acc_s, m_s, l_s, shuf_s, *, seq):
    step = pl.program_id(1)
    n1 = DIL_GROUPS[1][1]
    n2_steps = DIL_GROUPS[2][1] // G2_RES_PER_STEP

    row0 = lax.broadcasted_iota(jnp.int32, (QB, QB), 0)
    col0 = lax.broadcasted_iota(jnp.int32, (QB, QB), 1)
    dist_first = jnp.where(col0 <= row0, (row0 - col0).astype(_F32), jnp.inf)
    row1 = lax.broadcasted_iota(jnp.int32, (QB, 2 * QB), 0)
    col1 = lax.broadcasted_iota(jnp.int32, (QB, 2 * QB), 1)
    d1 = QB + row1 - col1
    dist_next = jnp.where((d1 >= 0) & (d1 <= QB), d1.astype(_F32), jnp.inf)
    ones = jnp.ones((2 * QB, HEAD_DIM), _BF16)

    seg = seq // n1
    chunk = QB // n1
    assert n1 == 4 and DIL_GROUPS[2][1] == 16

    def blocks(specs):
        tiles = []
        for slot, (qr, kr, vr, grp, res, i, is_first_block) in enumerate(specs):
            dil = DIL_GROUPS[grp][1]
            for h in range(HEADS):
                hs = _head_cols(h)
                slope_step = SLOPES[grp * HEADS + h] * dil
                if is_first_block:
                    q, k, v = qr[0:QB, hs], kr[0:QB, hs], vr[0:QB, hs]
                    bias, one = dist_first * slope_step, ones[0:QB]
                else:
                    qs = pl.ds(pl.multiple_of(i * QB, QB), QB)
                    ks = pl.ds(pl.multiple_of(i * QB - QB, QB), 2 * QB)
                    q, k, v = qr[qs, hs], kr[ks, hs], vr[ks, hs]
                    bias, one = dist_next * slope_step, ones
                tiles.append((q, k, jnp.concatenate([v, one], axis=1), bias, grp, res, i, is_first_block, slot, h))
        scores = [lax.dot_general(t[0], t[1], (((1,), (1,)), ((), ())), preferred_element_type=_F32) for t in tiles]
        probs, maxes = [], []
        for t, s in zip(tiles, scores):
            s = s * ATT_SCALE - t[3]
            m = jnp.max(s, axis=1, keepdims=True)
            probs.append(jnp.exp(s - m).astype(_BF16))
            maxes.append(jnp.broadcast_to(m, (QB, HEAD_DIM)))
        outs = [jnp.dot(p, t[2], preferred_element_type=_F32) for t, p in zip(tiles, probs)]
        for (_, _, _, _, grp, res, i, is_first_block, slot, h), m, acc_l in zip(tiles, maxes, outs):
            acc, l = acc_l[:, :HEAD_DIM], acc_l[:, HEAD_DIM:]
            if grp == 0:
                for a, (val, ref) in enumerate(((acc, acc_s), (m, m_s), (l, l_s))):
                    tmp = shuf_s.at[(slot * HEADS + h) * 3 + a]
                    tmp[...] = val
                    for r4 in range(n1):
                        dst = r4 * seg + i * chunk
                        dst = pl.ds(dst if is_first_block else pl.multiple_of(dst, chunk), chunk)
                        ref[h, dst, :] = tmp[pl.ds(r4, chunk, stride=n1), :]
            else:
                if grp == 1:
                    rows = pl.ds(pl.multiple_of(res * seg + i * QB, QB), QB)
                else:
                    rows = pl.ds((res & (n1 - 1)) * seg + (res >> 2), QB, stride=n1)
                m_old, l_old, a_old = m_s[h, rows, :], l_s[h, rows, :], acc_s[h, rows, :]
                m_new = jnp.maximum(m_old, m)
                w_old = jnp.exp(m_old - m_new)
                w_new = jnp.exp(m - m_new)
                acc_s[h, rows, :] = a_old * w_old + acc * w_new
                l_s[h, rows, :] = l_old * w_old + l * w_new
                m_s[h, rows, :] = m_new

    @pl.when(step == 0)
    def _():
        n_blk = seq // QB
        assert (n_blk - 1) % G0_UNROLL == 0
        blocks([(q0, k0, v0, 0, 0, 0, True)])

        def body(it, carry):
            blocks([(q0, k0, v0, 0, 0, 1 + it * G0_UNROLL + u, False) for u in range(G0_UNROLL)])
            return carry
        lax.fori_loop(0, (n_blk - 1) // G0_UNROLL, body, 0)

    @pl.when((step >= 1) & (step <= n1))
    def _():
        blocks([(q1, k1, v1, 1, step - 1, i, i == 0) for i in range(seg // QB)])

    @pl.when(step > n1)
    def _():
        assert seq // DIL_GROUPS[2][1] == QB
        blocks([(q2.at[r], k2.at[r], v2.at[r], 2, (step - 1 - n1) * G2_RES_PER_STEP + r, 0, True)
                for r in range(G2_RES_PER_STEP)])

    @pl.when(step == n1 + n2_steps)
    def _():
        def body(c, carry):
            for h in range(HEADS):
                tmp = shuf_s.at[h]
                for r4 in range(n1):
                    src = pl.ds(pl.multiple_of(r4 * seg + c * chunk, chunk), chunk)
                    tmp[pl.ds(r4, chunk, stride=n1), :] = acc_s[h, src, :] / l_s[h, src, :]
                o_ref[pl.ds(pl.multiple_of(c * QB, QB), QB), _head_cols(h)] = tmp[...].astype(_BF16)
            return carry
        lax.fori_loop(0, seq // QB, body, 0)


def _attn_prompt(nat, qkv1, qkv2, batch, seq):
    n1, n2 = DIL_GROUPS[1][1], DIL_GROUPS[2][1]
    n2_steps = n2 // G2_RES_PER_STEP
    nat3 = nat.reshape(batch, seq, NAT_TILES * COL_TILE)
    in_specs = [pl.BlockSpec((None, seq, COL_TILE), lambda b, s, t=t: (b, 0, Q_TILE[0] + t)) for t in range(3)]
    in_specs += [pl.BlockSpec((None, None, seq // n1, COL_TILE),
                              lambda b, s, t=t: (b, jnp.clip(s - 1, 0, n1 - 1), 0, t)) for t in range(3)]
    in_specs += [pl.BlockSpec((None, G2_RES_PER_STEP, seq // n2, COL_TILE),
                              lambda b, s, t=t: (b, jnp.clip(s - 1 - n1, 0, n2_steps - 1), 0, t)) for t in range(3)]
    args = [nat3] * 3 + [qkv1] * 3 + [qkv2] * 3
    return pl.pallas_call(
        functools.partial(_attn_prompt_kernel, seq=seq),
        grid=(batch, 1 + n1 + n2_steps),
        in_specs=in_specs,
        out_specs=pl.BlockSpec((None, seq, ATT_WIDTH), lambda b, s: (b, 0, 0)),
        out_shape=jax.ShapeDtypeStruct((batch, seq, ATT_WIDTH), _BF16),
        scratch_shapes=[pltpu.VMEM((HEADS, seq, HEAD_DIM), _F32)] * 3
                       + [pltpu.VMEM((G0_UNROLL * HEADS * 3, QB, HEAD_DIM), _F32)],
        compiler_params=pltpu.CompilerParams(dimension_semantics=("parallel", "arbitrary"),
                                             vmem_limit_bytes=VMEM_LIMIT),
        name="attn_prompt",
    )(*args)


def _attn_sample_kernel(*refs, tdec, aliased):
    n_in = 10 if aliased else 7
    proj_ref, n0, n1, n2, c0, c1, c2 = refs[:7]
    yb_ref, o0, o1, o2 = refs[n_in:]
    news, caches, outs = (n0, n1, n2), (c0, c1, c2), (o0, o1, o2)
    for h in range(HEADS):
        pieces = []
        for grp, (win, dil) in enumerate(DIL_GROUPS):
            slope = SLOPES[grp * HEADS + h]
            q = proj_ref[:, _head_cols(h, Q_TILE[grp] * COL_TILE)]
            k_new = news[grp][:, _head_cols(h)]
            v_new = news[grp][:, _head_cols(h, ATT_WIDTH)]
            k_old = caches[grp][pl.ds(h, win, stride=KV_ROWS), :]
            v_old = caches[grp][pl.ds(HEADS + h, win, stride=KV_ROWS), :]
            for k, v, n_keys, base in ((k_old, v_old, win, win), (k_new, v_new, tdec, 0)):
                s = lax.dot_general(q, k.astype(_BF16), (((1,), (1,)), ((), ())), preferred_element_type=_F32)
                t = lax.broadcasted_iota(jnp.int32, (tdec, n_keys), 0)
                c = lax.broadcasted_iota(jnp.int32, (tdec, n_keys), 1)
                dist = base + t - c
                ok = (dist >= 0) & (dist <= win) & ((dist & (dil - 1)) == 0)
                s = jnp.where(ok, s * ATT_SCALE - slope * dist.astype(_F32), -jnp.inf)
                pieces.append((s, v.astype(_BF16)))
        m = functools.reduce(jnp.maximum, [jnp.max(s, axis=1, keepdims=True) for s, _ in pieces])
        l = jnp.zeros((tdec, 1), _F32)
        acc = jnp.zeros((tdec, HEAD_DIM), _F32)
        for s, v in pieces:
            p = jnp.exp(s - m)
            l = l + jnp.sum(p, axis=1, keepdims=True)
            acc = acc + jnp.dot(p.astype(_BF16), v, preferred_element_type=_F32)
        yb_ref[:, _head_cols(h)] = (acc / l).astype(_BF16)
    for grp, (win, _) in enumerate(DIL_GROUPS):
        kept = (win - tdec) * KV_ROWS
        outs[grp][0:kept, :] = caches[grp][tdec * KV_ROWS:win * KV_ROWS, :]
        for part in range(2):
            for h in range(HEADS):
                outs[grp][pl.ds(kept + part * HEADS + h, tdec, stride=KV_ROWS), :] = (
                    news[grp][:, _head_cols(h, part * ATT_WIDTH)])


def _attn_sample(proj, kv_new, caches, batch, tdec, layer, prev_out):
    aliased = prev_out is not None
    in_specs = [pl.BlockSpec((None, tdec, IN_COLS), lambda b: (b, 0, 0))]
    in_specs += [pl.BlockSpec((None, tdec, 2 * ATT_WIDTH), lambda b: (b, 0, 0))] * 3
    cache_specs = [pl.BlockSpec((None, None, win * KV_ROWS, HEAD_DIM), lambda b: (layer, b, 0, 0)) for win, _ in DIL_GROUPS]
    in_specs += cache_specs
    args = [proj.reshape(batch, tdec, IN_COLS)] + [k.reshape(batch, tdec, 2 * ATT_WIDTH) for k in kv_new] + list(caches)
    if aliased:
        in_specs += [pl.BlockSpec(memory_space=pl.ANY)] * 3
        args += list(prev_out)
    out_specs = [pl.BlockSpec((None, tdec, ATT_WIDTH), lambda b: (b, 0, 0))] + cache_specs
    out_shape = [jax.ShapeDtypeStruct((batch, tdec, ATT_WIDTH), _BF16)]
    out_shape += [jax.ShapeDtypeStruct((DEPTH, batch, win * KV_ROWS, HEAD_DIM), _F32) for win, _ in DIL_GROUPS]
    return pl.pallas_call(
        functools.partial(_attn_sample_kernel, tdec=tdec, aliased=aliased),
        grid=(batch,),
        in_specs=in_specs, out_specs=out_specs, out_shape=out_shape,
        input_output_aliases={7: 1, 8: 2, 9: 3} if aliased else {},
        compiler_params=pltpu.CompilerParams(dimension_semantics=("arbitrary",), vmem_limit_bytes=VMEM_LIMIT),
        name="attn_sample",
    )(*args)


def _mixer_kernel(*refs, nb, tt, start, from_zero):
    if from_zero:
        pa_ref, pu_ref, yb_ref, x_ref, cw_ref, pw_ref, psc_ref, wo_ref = refs[:8]
    else:
        pa_ref, pu_ref, yb_ref, x_ref, cs0_ref, ps0_ref, cw_ref, pw_ref, psc_ref, wo_ref = refs[:10]
    out_ref, cs_out, ps_out, ext_a, ext_u, yc = refs[-6:]
    ti = pl.program_id(1)
    rows = nb * tt
    pool_halo = POOL_BLOCK if from_zero else POOL_HALO

    @pl.when(ti == 0)
    def _():
        if from_zero:
            ext_a[:, 0:CONV_HALO, :] = jnp.zeros((nb, CONV_HALO, CONV_WIDTH), _F32)
            ext_u[:, 0:pool_halo, :] = jnp.zeros((nb, pool_halo, POOL_WIDTH), _BF16)
        else:
            ext_a[:, 0:CONV_HALO, :] = cs0_ref[...]
            ext_u[:, 0:pool_halo, :] = ps0_ref[...]

    yb = yb_ref[...].astype(_F32).reshape(rows, ATT_WIDTH).astype(_BF16)
    mixed = jnp.dot(yb, wo_ref[CONV_WIDTH:CONV_WIDTH + ATT_WIDTH, :], preferred_element_type=_F32)

    pa = pa_ref[...].astype(_F32)
    xa, gate_b, gate_c = (pa[:, :, k * CONV_WIDTH:(k + 1) * CONV_WIDTH] for k in range(3))
    prod = gate_c * xa
    ext_a[:, CONV_HALO:CONV_HALO + tt, :] = prod
    cw = cw_ref[...]
    cu = (cw[0] * ext_a[:, CONV_HALO - 2:CONV_HALO - 2 + tt, :]
          + cw[1] * ext_a[:, CONV_HALO - 1:CONV_HALO - 1 + tt, :] + cw[2] * prod)
    ya = (gate_b * cu).reshape(rows, CONV_WIDTH).astype(_BF16)
    mixed = mixed + jnp.dot(ya, wo_ref[0:CONV_WIDTH, :], preferred_element_type=_F32)

    ext_u[:, pool_halo:pool_halo + tt, :] = pu_ref[...].astype(ext_u.dtype)
    if from_zero:
        pb = POOL_BLOCK
        row = lax.broadcasted_iota(jnp.int32, (pb, 2 * pb), 0)
        col = lax.broadcasted_iota(jnp.int32, (pb, 2 * pb), 1)
        lag = pb + row - col
        for gi, w in enumerate(POOL_WINDOWS):
            cols = slice(gi * POOL_GROUP, (gi + 1) * POOL_GROUP)
            band = ((lag >= 0) & (lag < w)).astype(_F32).astype(_BF16)
            for r in range(tt // pb):
                tot = jnp.dot(band, ext_u[0, r * pb:(r + 2) * pb, cols], preferred_element_type=_F32)
                pos = start + ti * tt + r * pb + lax.broadcasted_iota(jnp.int32, (pb, 1), 0)
                cnt = jnp.minimum(w, pos + 1).astype(_F32)
                ug = pu_ref[0, r * pb:(r + 1) * pb, cols].astype(_F32)
                diff = (tot / cnt - ug).astype(_BF16)
                z = jnp.dot(diff, pw_ref[gi], preferred_element_type=_F32) * psc_ref[:, cols]
                yc[r * pb:(r + 1) * pb, cols] = z.astype(_BF16)
    else:
        u = pu_ref[...].astype(_F32)
        pos = start + ti * tt + lax.broadcasted_iota(jnp.int32, (1, tt, 1), 1)
        for gi, w in enumerate(POOL_WINDOWS):
            cols = slice(gi * POOL_GROUP, (gi + 1) * POOL_GROUP)
            ug = u[:, :, cols]
            tot = ug
            for k in range(1, w):
                tot = tot + ext_u[:, pool_halo - k:pool_halo - k + tt, cols]
            cnt = jnp.minimum(w, pos + 1).astype(_F32)
            diff = (tot / cnt - ug).reshape(rows, POOL_GROUP).astype(_BF16)
            z = jnp.dot(diff, pw_ref[gi], preferred_element_type=_F32) * psc_ref[:, cols]
            yc[:, cols] = z.astype(_BF16)

    mixed = mixed + jnp.dot(yc[...], wo_ref[CONV_WIDTH + ATT_WIDTH:, :], preferred_element_type=_F32)
    out_ref[...] = x_ref[...] + mixed.reshape(nb, tt, D_MODEL)

    tail_a = ext_a[:, tt:tt + CONV_HALO, :]
    ext_a[:, 0:CONV_HALO, :] = tail_a
    cs_out[...] = tail_a
    ps_out[...] = ext_u[:, pool_halo + tt - POOL_HALO:pool_halo + tt, :].astype(_F32)
    ext_u[:, 0:pool_halo, :] = ext_u[:, tt:tt + pool_halo, :]


def _mixer(proj, yb, x, conv_state, pool_state, conv_w, pool_w, pool_scale, w_out, layer, nb, tt, start, u_tile0):
    batch, seq, _ = x.shape
    proj3 = proj.reshape(batch, seq, proj.shape[-1])
    u_blk = u_tile0 * COL_TILE // POOL_WIDTH
    const2 = lambda b, t: (0, 0)
    from_zero = conv_state is None and pool_state is None
    assert from_zero or (conv_state is not None and pool_state is not None)
    assert not from_zero or (nb == 1 and start == 0 and tt % POOL_BLOCK == 0)
    state_specs = [] if from_zero else [pl.BlockSpec((nb, CONV_HALO, CONV_WIDTH), lambda b, t: (b, 0, 0)),
                                        pl.BlockSpec((nb, POOL_HALO, POOL_WIDTH), lambda b, t: (b, 0, 0))]
    state_args = [] if from_zero else [conv_state, pool_state]
    ext_u = (pltpu.VMEM((nb, tt + POOL_BLOCK, POOL_WIDTH), _BF16) if from_zero
             else pltpu.VMEM((nb, tt + POOL_HALO, POOL_WIDTH), _F32))
    return pl.pallas_call(
        functools.partial(_mixer_kernel, nb=nb, tt=tt, start=start, from_zero=from_zero),
        grid=(batch // nb, seq // tt),
        in_specs=[pl.BlockSpec((nb, tt, 3 * CONV_WIDTH), lambda b, t: (b, t, 0)),
                  pl.BlockSpec((nb, tt, POOL_WIDTH), lambda b, t: (b, t, u_blk)),
                  pl.BlockSpec((nb, tt, ATT_WIDTH), lambda b, t: (b, t, 0)),
                  pl.BlockSpec((nb, tt, D_MODEL), lambda b, t: (b, t, 0))] + state_specs + [
                  pl.BlockSpec((3, CONV_WIDTH), const2),
                  pl.BlockSpec((None, len(POOL_WINDOWS), POOL_GROUP, POOL_GROUP), lambda b, t: (layer, 0, 0, 0)),
                  pl.BlockSpec((1, POOL_WIDTH), const2),
                  pl.BlockSpec((None, D_MODEL, D_MODEL), lambda b, t: (layer, 0, 0))],
        out_specs=[pl.BlockSpec((nb, tt, D_MODEL), lambda b, t: (b, t, 0)),
                   pl.BlockSpec((nb, CONV_HALO, CONV_WIDTH), lambda b, t: (b, 0, 0)),
                   pl.BlockSpec((nb, POOL_HALO, POOL_WIDTH), lambda b, t: (b, 0, 0))],
        out_shape=[jax.ShapeDtypeStruct((batch, seq, D_MODEL), _F32),
                   jax.ShapeDtypeStruct((batch, CONV_HALO, CONV_WIDTH), _F32),
                   jax.ShapeDtypeStruct((batch, POOL_HALO, POOL_WIDTH), _F32)],
        scratch_shapes=[pltpu.VMEM((nb, tt + CONV_HALO, CONV_WIDTH), _F32), ext_u,
                        pltpu.VMEM((nb * tt, POOL_WIDTH), _BF16)],
        compiler_params=pltpu.CompilerParams(dimension_semantics=("parallel", "arbitrary"),
                                             vmem_limit_bytes=VMEM_LIMIT),
        name="mixer",
    )(proj3, proj3, yb, x, *state_args, conv_w, pool_w, pool_scale.reshape(1, POOL_WIDTH), w_out)


def _ffn_sample_kernel(x_ref, n2_ref, wg_ref, wu_ref, cw_ref, wd_ref, st0_ref, fn_ref,
                       out_ref, st_out, wgt_ref, wut_ref, wdt_ref, h_s, ext_s, *, nb, tt, final):
    f = pl.program_id(0)
    rows = nb * tt

    @pl.when(f == 0)
    def _():
        x = x_ref[...]
        h_s[...] = _rms_norm(x, n2_ref[...]).reshape(rows, D_MODEL).astype(_BF16)
        out_ref[...] = x

    def tile(width):
        cols = slice(0, width)
        wg, wu, wd = wg_ref[:, cols].astype(_BF16), wu_ref[:, cols].astype(_BF16), wd_ref[cols, :].astype(_BF16)
        wgt_ref[:, cols], wut_ref[:, cols], wdt_ref[cols, :] = wg, wu, wd
        if width < FF_TILE:
            wgt_ref[:, width:] = jnp.zeros((D_MODEL, FF_TILE - width), _BF16)
            wut_ref[:, width:] = jnp.zeros((D_MODEL, FF_TILE - width), _BF16)
            wdt_ref[width:, :] = jnp.zeros((FF_TILE - width, D_MODEL), _BF16)
        h = h_s[...]
        gate = jnp.dot(h, wg, preferred_element_type=_F32).reshape(nb, tt, width)
        up = jnp.dot(h, wu, preferred_element_type=_F32).reshape(nb, tt, width)
        ext_s[:, 0:CONV_HALO, cols] = st0_ref[:, :, cols]
        ext_s[:, CONV_HALO:CONV_HALO + tt, cols] = gate
        cw = cw_ref[:, cols]
        gc = (cw[0] * ext_s[:, CONV_HALO - 2:CONV_HALO - 2 + tt, cols]
              + cw[1] * ext_s[:, CONV_HALO - 1:CONV_HALO - 1 + tt, cols] + cw[2] * gate)
        st_out[:, :, cols] = ext_s[:, tt:tt + CONV_HALO, cols]
        act = (jax.nn.silu(gc) * up).reshape(rows, width).astype(_BF16)
        out_ref[...] += jnp.dot(act, wd, preferred_element_type=_F32).reshape(nb, tt, D_MODEL)

    @pl.when(f < FF_TILES - 1)
    def _():
        tile(FF_TILE)

    @pl.when(f == FF_TILES - 1)
    def _():
        tile(FF_LAST)
        if final:
            out_ref[...] = _rms_norm(out_ref[...], fn_ref[...])


def _ffn_prompt_kernel(x_ref, n2_ref, wg_ref, wu_ref, cw_ref, wd_ref, fn_ref,
                       out_ref, st_out, h_s, ext_s, carry_s, *, tt, final):
    ti = pl.program_id(1)
    f = pl.program_id(2)

    @pl.when(f == 0)
    def _():
        x = x_ref[...]
        h_s[...] = _rms_norm(x, n2_ref[...]).astype(_BF16)
        out_ref[...] = x

    @pl.when(ti == 0)
    def _():
        ext_s[0:CONV_HALO, :] = jnp.zeros((CONV_HALO, FF_TILE), _F32)

    @pl.when(ti > 0)
    def _():
        ext_s[0:CONV_HALO, :] = carry_s[f]

    wg, wu, wd, cw = wg_ref[...], wu_ref[...], wd_ref[...], cw_ref[...]
    chunks = [slice(r0, r0 + FFN_SUB_ROWS) for r0 in range(0, tt, FFN_SUB_ROWS)]
    for rs in chunks:
        gate = jnp.dot(h_s[rs, :], wg, preferred_element_type=_F32)
        ext_s[CONV_HALO + rs.start:CONV_HALO + rs.stop, :] = gate
    ups = [jnp.dot(h_s[rs, :], wu, preferred_element_type=_F32) for rs in chunks]
    for rs, up in zip(chunks, ups):
        gc = (cw[0] * ext_s[CONV_HALO - 2 + rs.start:CONV_HALO - 2 + rs.stop, :]
              + cw[1] * ext_s[CONV_HALO - 1 + rs.start:CONV_HALO - 1 + rs.stop, :]
              + cw[2] * ext_s[CONV_HALO + rs.start:CONV_HALO + rs.stop, :])
        act = (jax.nn.silu(gc) * up).astype(_BF16)
        out_ref[rs, :] += jnp.dot(act, wd, preferred_element_type=_F32)
    tail = ext_s[tt:tt + CONV_HALO, :]
    carry_s[f] = tail
    st_out[...] = tail

    if final:
        @pl.when(f == FF_TILES - 1)
        def _():
            out_ref[...] = _rms_norm(out_ref[...], fn_ref[...])


def _ffn_sample(x, norm2, w_gate, w_up, conv_w, w_down, state, final_norm, layer, final):
    nb, tt, _ = x.shape
    vec = lambda f: (0, 0)
    return pl.pallas_call(
        functools.partial(_ffn_sample_kernel, nb=nb, tt=tt, final=final),
        grid=(FF_TILES,),
        in_specs=[pl.BlockSpec((nb, tt, D_MODEL), lambda f: (0, 0, 0)),
                  pl.BlockSpec((1, D_MODEL), vec),
                  pl.BlockSpec((None, D_MODEL, FF_TILE), lambda f: (layer, 0, f)),
                  pl.BlockSpec((None, D_MODEL, FF_TILE), lambda f: (layer, 0, f)),
                  pl.BlockSpec((3, FF_TILE), lambda f: (0, f)),
                  pl.BlockSpec((None, FF_TILE, D_MODEL), lambda f: (layer, f, 0)),
                  pl.BlockSpec((nb, CONV_HALO, FF_TILE), lambda f: (0, 0, f)),
                  pl.BlockSpec((1, D_MODEL), vec)],
        out_specs=[pl.BlockSpec((nb, tt, D_MODEL), lambda f: (0, 0, 0)),
                   pl.BlockSpec((nb, CONV_HALO, FF_TILE), lambda f: (0, 0, f)),
                   pl.BlockSpec((None, D_MODEL, FF_TILE), lambda f: (f, 0, 0)),
                   pl.BlockSpec((None, D_MODEL, FF_TILE), lambda f: (f, 0, 0)),
                   pl.BlockSpec((None, FF_TILE, D_MODEL), lambda f: (f, 0, 0))],
        out_shape=[jax.ShapeDtypeStruct((nb, tt, D_MODEL), _F32),
                   jax.ShapeDtypeStruct((nb, CONV_HALO, D_FF), _F32),
                   jax.ShapeDtypeStruct((FF_TILES, D_MODEL, FF_TILE), _BF16),
                   jax.ShapeDtypeStruct((FF_TILES, D_MODEL, FF_TILE), _BF16),
                   jax.ShapeDtypeStruct((FF_TILES, FF_TILE, D_MODEL), _BF16)],
        scratch_shapes=[pltpu.VMEM((nb * tt, D_MODEL), _BF16),
                        pltpu.VMEM((nb, tt + CONV_HALO, FF_TILE), _F32)],
        compiler_params=pltpu.CompilerParams(dimension_semantics=("arbitrary",), vmem_limit_bytes=VMEM_LIMIT),
        name="ffn_sample",
    )(x, norm2.reshape(1, D_MODEL), w_gate, w_up, conv_w, w_down, state, final_norm.reshape(1, D_MODEL))


def _ffn_prompt(x, norm2, w_gate_t, w_up_t, conv_w, w_down_t, final_norm, tt, final):
    batch, seq, _ = x.shape
    vec = lambda b, t, f: (0, 0)
    return pl.pallas_call(
        functools.partial(_ffn_prompt_kernel, tt=tt, final=final),
        grid=(batch, seq // tt, FF_TILES),
        in_specs=[pl.BlockSpec((None, tt, D_MODEL), lambda b, t, f: (b, t, 0)),
                  pl.BlockSpec((1, D_MODEL), vec),
                  pl.BlockSpec((None, D_MODEL, FF_TILE), lambda b, t, f: (f, 0, 0)),
                  pl.BlockSpec((None, D_MODEL, FF_TILE), lambda b, t, f: (f, 0, 0)),
                  pl.BlockSpec((3, FF_TILE), lambda b, t, f: (0, f)),
                  pl.BlockSpec((None, FF_TILE, D_MODEL), lambda b, t, f: (f, 0, 0)),
                  pl.BlockSpec((1, D_MODEL), vec)],
        out_specs=[pl.BlockSpec((None, tt, D_MODEL), lambda b, t, f: (b, t, 0)),
                   pl.BlockSpec((None, None, CONV_HALO, FF_TILE), lambda b, t, f: (b, t, 0, f))],
        out_shape=[jax.ShapeDtypeStruct((batch, seq, D_MODEL), _F32),
                   jax.ShapeDtypeStruct((batch, seq // tt, CONV_HALO, FF_TILES * FF_TILE), _F32)],
        scratch_shapes=[pltpu.VMEM((tt, D_MODEL), _BF16),
                        pltpu.VMEM((tt + CONV_HALO, FF_TILE), _F32),
                        pltpu.VMEM((FF_TILES, CONV_HALO, FF_TILE), _F32)],
        compiler_params=pltpu.CompilerParams(dimension_semantics=("parallel", "arbitrary", "arbitrary"),
                                             vmem_limit_bytes=VMEM_LIMIT),
        name="ffn_prompt",
    )(x, norm2.reshape(1, D_MODEL), w_gate_t, w_up_t, conv_w, w_down_t, final_norm.reshape(1, D_MODEL))


def _pad_rows_front(a, rows):
    return jnp.pad(a, ((0, 0), (rows - a.shape[1], 0), (0, 0)))


def kernel(x_prompt, x_sample, cache_kv_w128, cache_kv_w512, cache_kv_w2048, state_conv_a, state_pool, state_ffn_conv, norm1, w_in, conv_a_w, w_out, pool_w, pool_scale, norm2, w_gate, w_up, ffn_conv_w, w_down, final_norm):
    bp, sp, _ = x_prompt.shape
    bs, ss, _ = x_sample.shape
    caches = tuple(c.reshape(DEPTH, bs, win * KV_ROWS, HEAD_DIM)
                   for c, (win, _) in zip((cache_kv_w128, cache_kv_w512, cache_kv_w2048), DIL_GROUPS))
    tm, tt = 1024, 512
    w_out_b, pool_w_b = w_out.astype(_BF16), pool_w.astype(_BF16)
    xp, xs = x_prompt, x_sample
    kv_p = kv_s = None
    small_p, small_s = [], []
    for l in range(DEPTH):
        final = l == DEPTH - 1

        proj, *kv_new, w_in_t = _inproj_sample(xs.reshape(bs * ss, D_MODEL), norm1[l], w_in, l)
        yb, *kv_s = _attn_sample(proj, kv_new, caches, bs, ss, l, kv_s)
        xs, conv_o, pool_o = _mixer(proj, yb, xs, _pad_rows_front(state_conv_a[l], CONV_HALO),
                                    _pad_rows_front(state_pool[l], POOL_HALO), conv_a_w[l], pool_w_b, pool_scale[l],
                                    w_out_b, l, bs, ss, PAST_LEN, U_TILE0)
        xs, ffn_o, w_gate_t, w_up_t, w_down_t = _ffn_sample(
            xs, norm2[l], w_gate, w_up, ffn_conv_w[l], w_down, _pad_rows_front(state_ffn_conv[l], CONV_HALO),
            final_norm, l, final)
        small_s.append((conv_o[:, CONV_HALO - 2:], pool_o[:, POOL_HALO - POOL_BUF:], ffn_o[:, CONV_HALO - 2:]))

        nat, qkv1, qkv2, *kv_p = _inproj_prompt(xp.reshape(bp * sp, D_MODEL), norm1[l], w_in_t, bp, sp, tm, l, kv_p)
        yb = _attn_prompt(nat, qkv1, qkv2, bp, sp)
        xp, conv_o, pool_o = _mixer(nat, yb, xp, None, None, conv_a_w[l], pool_w_b, pool_scale[l], w_out_b,
                                    l, 1, tt, 0, NAT_TILES - 2)
        conv_w_pad = jnp.pad(ffn_conv_w[l], ((0, 0), (0, FF_TILES * FF_TILE - D_FF)))
        xp, ffn_o = _ffn_prompt(xp, norm2[l], w_gate_t, w_up_t, conv_w_pad, w_down_t, final_norm, tm, final)
        small_p.append((conv_o[:, CONV_HALO - 2:], pool_o[:, POOL_HALO - POOL_BUF:],
                        ffn_o[:, -1, CONV_HALO - 2:, :D_FF]))

    out = [xp, xs]
    for g, (win, _) in enumerate(DIL_GROUPS):
        out += [kv_p[g].reshape(DEPTH, bp, win, 2, HEADS, HEAD_DIM), kv_s[g].reshape(DEPTH, bs, win, 2, HEADS, HEAD_DIM)]
    for i in range(3):
        out += [jnp.stack([s[i] for s in small_p]), jnp.stack([s[i] for s in small_s])]
    return tuple(out)
```

```python
import functools

import numpy as np
import jax
import jax.numpy as jnp
from jax import lax
from jax.experimental import pallas as pl
from jax.experimental.pallas import tpu as pltpu

D_MODEL = 2048
DEPTH = 2
PAST_LEN = 16384
CONV_WIDTH = 512
ATT_WIDTH = 512
HEADS = 4
HEAD_DIM = 128
DIL_GROUPS = ((128, 1), (512, 4), (2048, 16))
POOL_WIDTH = 1024
POOL_WINDOWS = (2, 4, 8, 16)
POOL_GROUP = 256
POOL_BUF = 15
IN_COLS = 7168
D_FF = 5504
RMS_EPS = 1e-6
ATT_SCALE = HEAD_DIM ** -0.5

COL_TILE = 512
IN_TILES = IN_COLS // COL_TILE
Q_TILE = (3, 6, 9)
U_TILE0 = 12
NAT_TILES = 8
KV_ROWS = 2 * HEADS
SUB_ROWS = 512
IN_STEP_TILES = 2
FF_TILE = 512
FF_TILES = -(-D_FF // FF_TILE)
FF_LAST = D_FF - (FF_TILES - 1) * FF_TILE
FFN_SUB_ROWS = 256
CONV_HALO = 8
POOL_HALO = 16
POOL_BLOCK = 128
VMEM_LIMIT = 56 * 1024 * 1024
VMEM_LIMIT_HIGH = 61 * 1024 * 1024

_BF16 = jnp.bfloat16
_F32 = jnp.float32


def _alibi_slopes():
    h = np.arange(1, 3 * HEADS + 1, dtype=np.float32)
    return [float(v) for v in np.power(np.float32(2.0), -8.0 * h / (3 * HEADS))]


SLOPES = _alibi_slopes()


def _rms_norm(x, g):
    return (x * lax.rsqrt(jnp.mean(x * x, axis=-1, keepdims=True) + RMS_EPS)) * g


def _head_cols(h, base=0):
    return slice(base + h * HEAD_DIM, base + (h + 1) * HEAD_DIM)


def _inproj_sample_kernel(x_ref, g_ref, w_ref, proj_ref, kv0_ref, kv1_ref, kv2_ref, wt_ref, h_s):
    j = pl.program_id(1)

    @pl.when(j == 0)
    def _():
        h_s[...] = _rms_norm(x_ref[...], g_ref[...]).astype(_BF16)

    w = w_ref[...].astype(_BF16)
    wt_ref[...] = w
    acc = jnp.dot(h_s[...], w, preferred_element_type=_F32)
    proj_ref[...] = acc.astype(_BF16)
    for g, ref in enumerate((kv0_ref, kv1_ref, kv2_ref)):
        @pl.when((j == Q_TILE[g] + 1) | (j == Q_TILE[g] + 2))
        def _(ref=ref):
            ref[...] = acc


def _inproj_sample(x2d, g, w, layer):
    m = x2d.shape[0]

    def kv_spec(grp):
        return pl.BlockSpec((m, COL_TILE), lambda i, j: (i, jnp.clip(j - Q_TILE[grp] - 1, 0, 1)))

    return pl.pallas_call(
        _inproj_sample_kernel,
        grid=(1, IN_TILES),
        in_specs=[pl.BlockSpec((m, D_MODEL), lambda i, j: (i, 0)),
                  pl.BlockSpec((1, D_MODEL), lambda i, j: (0, 0)),
                  pl.BlockSpec((None, D_MODEL, COL_TILE), lambda i, j: (layer, 0, j))],
        out_specs=[pl.BlockSpec((m, COL_TILE), lambda i, j: (i, j)), kv_spec(0), kv_spec(1), kv_spec(2),
                   pl.BlockSpec((None, D_MODEL, COL_TILE), lambda i, j: (j, 0, 0))],
        out_shape=[jax.ShapeDtypeStruct((m, IN_COLS), _BF16)] +
                  [jax.ShapeDtypeStruct((m, 2 * ATT_WIDTH), _F32)] * 3 +
                  [jax.ShapeDtypeStruct((IN_TILES, D_MODEL, COL_TILE), _BF16)],
        scratch_shapes=[pltpu.VMEM((m, D_MODEL), _BF16)],
        compiler_params=pltpu.CompilerParams(dimension_semantics=("parallel", "arbitrary"),
                                             vmem_limit_bytes=VMEM_LIMIT),
        name="inproj_sample",
    )(x2d, g.reshape(1, D_MODEL), w)


def _inproj_prompt_kernel(*refs, tm, aliased):
    n_in = 6 if aliased else 3
    x_ref, g_ref, w_ref = refs[:3]
    nat_ref, g1_ref, g2_ref, kv0_ref, kv1_ref, kv2_ref, h_s, de_s = refs[n_in:]
    kv_refs = (kv0_ref, kv1_ref, kv2_ref)
    res_refs = (None, g1_ref, g2_ref)
    j = pl.program_id(1)

    @pl.when(j == 0)
    def _():
        h_s[...] = _rms_norm(x_ref[...], g_ref[...]).astype(_BF16)

    def column_tiles(tiles):
        for c in range(tm // SUB_ROWS):
            r0 = c * SUB_ROWS
            for half, tile in enumerate(tiles):
                acc = jnp.dot(h_s[r0:r0 + SUB_ROWS, :], w_ref[half], preferred_element_type=_F32)
                grp = (tile - Q_TILE[0]) // 3 if Q_TILE[0] <= tile < U_TILE0 else None
                part = None if grp is None or tile == Q_TILE[grp] else tile - Q_TILE[grp] - 1
                dil = 1 if grp is None else DIL_GROUPS[grp][1]
                if dil == 1:
                    nat_ref[r0:r0 + SUB_ROWS, half * COL_TILE:(half + 1) * COL_TILE] = acc.astype(_BF16)
                else:
                    for h in range(HEADS):
                        de_s[c, half, h] = acc[:, _head_cols(h)]
                    n = SUB_ROWS // dil
                    for res in range(dil):
                        for h in range(HEADS):
                            res_refs[grp][res, c * n:(c + 1) * n, _head_cols(h, (tile - Q_TILE[grp]) * COL_TILE)] = (
                                de_s[c, half, h, pl.ds(res, n, stride=dil), :].astype(_BF16))
                if part is not None:
                    keep = min(DIL_GROUPS[grp][0], tm)
                    lo = max(r0, tm - keep)
                    n = r0 + SUB_ROWS - lo
                    if n > 0:
                        for h in range(HEADS):
                            dst = pl.ds((lo - (tm - keep)) * KV_ROWS + part * HEADS + h, n, stride=KV_ROWS)
                            kv_refs[grp][dst, :] = acc[lo - r0:lo - r0 + n, _head_cols(h)]

    for step in range(IN_TILES // IN_STEP_TILES):
        @pl.when(j == step)
        def _(step=step):
            column_tiles(tuple(range(step * IN_STEP_TILES, (step + 1) * IN_STEP_TILES)))


def _inproj_prompt(x2d, g, w, batch, seq, tm, layer, prev_kv):
    m = x2d.shape[0]
    tps = seq // tm
    aliased = prev_kv is not None
    (w0, _), (w1, d1), (w2, d2) = DIL_GROUPS
    assert w2 == seq and w0 <= tm and w1 <= tm and tm % SUB_ROWS == 0

    in_specs = [pl.BlockSpec((tm, D_MODEL), lambda i, j: (i, 0)),
                pl.BlockSpec((1, D_MODEL), lambda i, j: (0, 0)),
                pl.BlockSpec((IN_STEP_TILES, D_MODEL, COL_TILE), lambda i, j: (j, 0, 0))]
    args = [x2d, g.reshape(1, D_MODEL), w]
    if aliased:
        in_specs += [pl.BlockSpec(memory_space=pl.ANY)] * 3
        args += list(prev_kv)

    def res_spec(dil):
        return pl.BlockSpec((None, dil, tm // dil, 3 * ATT_WIDTH), lambda i, j: (i // tps, 0, i % tps, 0))

    assert Q_TILE[1] % IN_STEP_TILES == 0 and U_TILE0 % IN_STEP_TILES == 0 and IN_TILES - U_TILE0 == IN_STEP_TILES
    nat_lo, u_step = Q_TILE[1] // IN_STEP_TILES, U_TILE0 // IN_STEP_TILES
    out_specs = [
        pl.BlockSpec((tm, IN_STEP_TILES * COL_TILE),
                     lambda i, j: (i, jnp.clip(j, 0, nat_lo - 1) + jnp.clip(j - u_step + 1, 0, 1))),
        res_spec(d1), res_spec(d2),
        pl.BlockSpec((None, None, w0 * KV_ROWS, HEAD_DIM), lambda i, j: (layer, i // tps, 0, 0),
                     pipeline_mode=pl.Buffered(1)),
        pl.BlockSpec((None, None, w1 * KV_ROWS, HEAD_DIM), lambda i, j: (layer, i // tps, 0, 0),
                     pipeline_mode=pl.Buffered(1)),
        pl.BlockSpec((None, None, tm * KV_ROWS, HEAD_DIM), lambda i, j: (layer, i // tps, i % tps, 0)),
    ]
    out_shape = [
        jax.ShapeDtypeStruct((m, NAT_TILES * COL_TILE), _BF16),
        jax.ShapeDtypeStruct((batch, d1, seq // d1, 3 * ATT_WIDTH), _BF16),
        jax.ShapeDtypeStruct((batch, d2, seq // d2, 3 * ATT_WIDTH), _BF16),
    ] + [jax.ShapeDtypeStruct((DEPTH, batch, win * KV_ROWS, HEAD_DIM), _F32) for win, _ in DIL_GROUPS]
    return pl.pallas_call(
        functools.partial(_inproj_prompt_kernel, tm=tm, aliased=aliased),
        grid=(m // tm, IN_TILES // IN_STEP_TILES),
        in_specs=in_specs, out_specs=out_specs, out_shape=out_shape,
        scratch_shapes=[pltpu.VMEM((tm, D_MODEL), _BF16),
                        pltpu.VMEM((tm // SUB_ROWS, IN_STEP_TILES, HEADS, SUB_ROWS, HEAD_DIM), _F32)],
        input_output_aliases={3: 3, 4: 4, 5: 5} if aliased else {},
        compiler_params=pltpu.CompilerParams(dimension_semantics=("arbitrary", "arbitrary"),
                                             vmem_limit_bytes=VMEM_LIMIT_HIGH),
        name="inproj_prompt",
    )(*args)


QB = 128
G0_UNROLL = 3
G2_RES_PER_STEP = 4


def _attn_prompt_kernel(q0, k0, v0, q1, k1, v1, q2, k2, v2, o_ref, acc_s, m_s, l_s, shuf_s, *, seq):
    step = pl.program_id(1)
    n1 = DIL_GROUPS[1][1]
    n2_steps = DIL_GROUPS[2][1] // G2_RES_PER_STEP

    row0 = lax.broadcasted_iota(jnp.int32, (QB, QB), 0)
    col0 = lax.broadcasted_iota(jnp.int32, (QB, QB), 1)
    dist_first = jnp.where(col0 <= row0, (row0 - col0).astype(_F32), jnp.inf)
    row1 = lax.broadcasted_iota(jnp.int32, (QB, 2 * QB), 0)
    col1 = lax.broadcasted_iota(jnp.int32, (QB, 2 * QB), 1)
    d1 = QB + row1 - col1
    dist_next = jnp.where((d1 >= 0) & (d1 <= QB), d1.astype(_F32), jnp.inf)
    ones = jnp.ones((2 * QB, HEAD_DIM), _BF16)

    seg = seq // n1
    chunk = QB // n1
    assert n1 == 4 and DIL_GROUPS[2][1] == 16

    def blocks(specs):
        tiles = []
        for slot, (qr, kr, vr, grp, res, i, is_first_block) in enumerate(specs):
            dil = DIL_GROUPS[grp][1]
            for h in range(HEADS):
                hs = _head_cols(h)
                slope_step = SLOPES[grp * HEADS + h] * dil
                if is_first_block:
                    q, k, v = qr[0:QB, hs], kr[0:QB, hs], vr[0:QB, hs]
                    bias, one = dist_first * slope_step, ones[0:QB]
                else:
                    qs = pl.ds(pl.multiple_of(i * QB, QB), QB)
                    ks = pl.ds(pl.multiple_of(i * QB - QB, QB), 2 * QB)
                    q, k, v = qr[qs, hs], kr[ks, hs], vr[ks, hs]
                    bias, one = dist_next * slope_step, ones
                tiles.append((q, k, jnp.concatenate([v, one], axis=1), bias, grp, res, i, is_first_block, slot, h))
        scores = [lax.dot_general(t[0], t[1], (((1,), (1,)), ((), ())), preferred_element_type=_F32) for t in tiles]
        probs, maxes = [], []
        for t, s in zip(tiles, scores):
            s = s * ATT_SCALE - t[3]
            m = jnp.max(s, axis=1, keepdims=True)
            probs.append(jnp.exp(s - m).astype(_BF16))
            maxes.append(jnp.broadcast_to(m, (QB, HEAD_DIM)))
        outs = [jnp.dot(p, t[2], preferred_element_type=_F32) for t, p in zip(tiles, probs)]
        for (_, _, _, _, grp, res, i, is_first_block, slot, h), m, acc_l in zip(tiles, maxes, outs):
            acc, l = acc_l[:, :HEAD_DIM], acc_l[:, HEAD_DIM:]
            if grp == 0:
                for a, (val, ref) in enumerate(((acc, acc_s), (m, m_s), (l, l_s))):
                    tmp = shuf_s.at[(slot * HEADS + h) * 3 + a]
                    tmp[...] = val
                    for r4 in range(n1):
                        dst = r4 * seg + i * chunk
                        dst = pl.ds(dst if is_first_block else pl.multiple_of(dst, chunk), chunk)
                        ref[h, dst, :] = tmp[pl.ds(r4, chunk, stride=n1), :]
            else:
                if grp == 1:
                    rows = pl.ds(pl.multiple_of(res * seg + i * QB, QB), QB)
                else:
                    rows = pl.ds((res & (n1 - 1)) * seg + (res >> 2), QB, stride=n1)
                m_old, l_old, a_old = m_s[h, rows, :], l_s[h, rows, :], acc_s[h, rows, :]
                m_new = jnp.maximum(m_old, m)
                w_old = jnp.exp(m_old - m_new)
                w_new = jnp.exp(m - m_new)
                acc_s[h, rows, :] = a_old * w_old + acc * w_new
                l_s[h, rows, :] = l_old * w_old + l * w_new
                m_s[h, rows, :] = m_new

    @pl.when(step == 0)
    def _():
        n_blk = seq // QB
        assert (n_blk - 1) % G0_UNROLL == 0
        blocks([(q0, k0, v0, 0, 0, 0, True)])

        def body(it, carry):
            blocks([(q0, k0, v0, 0, 0, 1 + it * G0_UNROLL + u, False) for u in range(G0_UNROLL)])
            return carry
        lax.fori_loop(0, (n_blk - 1) // G0_UNROLL, body, 0)

    @pl.when((step >= 1) & (step <= n1))
    def _():
        blocks([(q1, k1, v1, 1, step - 1, i, i == 0) for i in range(seg // QB)])

    @pl.when(step > n1)
    def _():
        assert seq // DIL_GROUPS[2][1] == QB
        blocks([(q2.at[r], k2.at[r], v2.at[r], 2, (step - 1 - n1) * G2_RES_PER_STEP + r, 0, True)
                for r in range(G2_RES_PER_STEP)])

    @pl.when(step == n1 + n2_steps)
    def _():
        def body(c, carry):
            for h in range(HEADS):
                tmp = shuf_s.at[h]
                for r4 in range(n1):
                    src = pl.ds(pl.multiple_of(r4 * seg + c * chunk, chunk), chunk)
                    tmp[pl.ds(r4, chunk, stride=n1), :] = acc_s[h, src, :] / l_s[h, src, :]
                o_ref[pl.ds(pl.multiple_of(c * QB, QB), QB), _head_cols(h)] = tmp[...].astype(_BF16)
            return carry
        lax.fori_loop(0, seq // QB, body, 0)


def _attn_prompt(nat, qkv1, qkv2, batch, seq):
    n1, n2 = DIL_GROUPS[1][1], DIL_GROUPS[2][1]
    n2_steps = n2 // G2_RES_PER_STEP
    nat3 = nat.reshape(batch, seq, NAT_TILES * COL_TILE)
    in_specs = [pl.BlockSpec((None, seq, COL_TILE), lambda b, s, t=t: (b, 0, Q_TILE[0] + t)) for t in range(3)]
    in_specs += [pl.BlockSpec((None, None, seq // n1, COL_TILE),
                              lambda b, s, t=t: (b, jnp.clip(s - 1, 0, n1 - 1), 0, t)) for t in range(3)]
    in_specs += [pl.BlockSpec((None, G2_RES_PER_STEP, seq // n2, COL_TILE),
                              lambda b, s, t=t: (b, jnp.clip(s - 1 - n1, 0, n2_steps - 1), 0, t)) for t in range(3)]
    args = [nat3] * 3 + [qkv1] * 3 + [qkv2] * 3
    return pl.pallas_call(
        functools.partial(_attn_prompt_kernel, seq=seq),
        grid=(batch, 1 + n1 + n2_steps),
        in_specs=in_specs,
        out_specs=pl.BlockSpec((None, seq, ATT_WIDTH), lambda b, s: (b, 0, 0)),
        out_shape=jax.ShapeDtypeStruct((batch, seq, ATT_WIDTH), _BF16),
        scratch_shapes=[pltpu.VMEM((HEADS, seq, HEAD_DIM), _F32)] * 3
                       + [pltpu.VMEM((G0_UNROLL * HEADS * 3, QB, HEAD_DIM), _F32)],
        compiler_params=pltpu.CompilerParams(dimension_semantics=("parallel", "arbitrary"),
                                             vmem_limit_bytes=VMEM_LIMIT),
        name="attn_prompt",
    )(*args)


def _attn_sample_kernel(*refs, tdec, aliased):
    n_in = 10 if aliased else 7
    proj_ref, n0, n1, n2, c0, c1, c2 = refs[:7]
    yb_ref, o0, o1, o2 = refs[n_in:]
    news, caches, outs = (n0, n1, n2), (c0, c1, c2), (o0, o1, o2)
    for h in range(HEADS):
        pieces = []
        for grp, (win, dil) in enumerate(DIL_GROUPS):
            slope = SLOPES[grp * HEADS + h]
            q = proj_ref[:, _head_cols(h, Q_TILE[grp] * COL_TILE)]
            k_new = news[grp][:, _head_cols(h)]
            v_new = news[grp][:, _head_cols(h, ATT_WIDTH)]
            k_old = caches[grp][pl.ds(h, win, stride=KV_ROWS), :]
            v_old = caches[grp][pl.ds(HEADS + h, win, stride=KV_ROWS), :]
            for k, v, n_keys, base in ((k_old, v_old, win, win), (k_new, v_new, tdec, 0)):
                s = lax.dot_general(q, k.astype(_BF16), (((1,), (1,)), ((), ())), preferred_element_type=_F32)
                t = lax.broadcasted_iota(jnp.int32, (tdec, n_keys), 0)
                c = lax.broadcasted_iota(jnp.int32, (tdec, n_keys), 1)
                dist = base + t - c
                ok = (dist >= 0) & (dist <= win) & ((dist & (dil - 1)) == 0)
                s = jnp.where(ok, s * ATT_SCALE - slope * dist.astype(_F32), -jnp.inf)
                pieces.append((s, v.astype(_BF16)))
        m = functools.reduce(jnp.maximum, [jnp.max(s, axis=1, keepdims=True) for s, _ in pieces])
        l = jnp.zeros((tdec, 1), _F32)
        acc = jnp.zeros((tdec, HEAD_DIM), _F32)
        for s, v in pieces:
            p = jnp.exp(s - m)
            l = l + jnp.sum(p, axis=1, keepdims=True)
            acc = acc + jnp.dot(p.astype(_BF16), v, preferred_element_type=_F32)
        yb_ref[:, _head_cols(h)] = (acc / l).astype(_BF16)
    for grp, (win, _) in enumerate(DIL_GROUPS):
        kept = (win - tdec) * KV_ROWS
        outs[grp][0:kept, :] = caches[grp][tdec * KV_ROWS:win * KV_ROWS, :]
        for part in range(2):
            for h in range(HEADS):
                outs[grp][pl.ds(kept + part * HEADS + h, tdec, stride=KV_ROWS), :] = (
                    news[grp][:, _head_cols(h, part * ATT_WIDTH)])


def _attn_sample(proj, kv_new, caches, batch, tdec, layer, prev_out):
    aliased = prev_out is not None
    in_specs = [pl.BlockSpec((None, tdec, IN_COLS), lambda b: (b, 0, 0))]
    in_specs += [pl.BlockSpec((None, tdec, 2 * ATT_WIDTH), lambda b: (b, 0, 0))] * 3
    cache_specs = [pl.BlockSpec((None, None, win * KV_ROWS, HEAD_DIM), lambda b: (layer, b, 0, 0)) for win, _ in DIL_GROUPS]
    in_specs += cache_specs
    args = [proj.reshape(batch, tdec, IN_COLS)] + [k.reshape(batch, tdec, 2 * ATT_WIDTH) for k in kv_new] + list(caches)
    if aliased:
        in_specs += [pl.BlockSpec(memory_space=pl.ANY)] * 3
        args += list(prev_out)
    out_specs = [pl.BlockSpec((None, tdec, ATT_WIDTH), lambda b: (b, 0, 0))] + cache_specs
    out_shape = [jax.ShapeDtypeStruct((batch, tdec, ATT_WIDTH), _BF16)]
    out_shape += [jax.ShapeDtypeStruct((DEPTH, batch, win * KV_ROWS, HEAD_DIM), _F32) for win, _ in DIL_GROUPS]
    return pl.pallas_call(
        functools.partial(_attn_sample_kernel, tdec=tdec, aliased=aliased),
        grid=(batch,),
        in_specs=in_specs, out_specs=out_specs, out_shape=out_shape,
        input_output_aliases={7: 1, 8: 2, 9: 3} if aliased else {},
        compiler_params=pltpu.CompilerParams(dimension_semantics=("arbitrary",), vmem_limit_bytes=VMEM_LIMIT),
        name="attn_sample",
    )(*args)


def _mixer_kernel(*refs, nb, tt, start, from_zero):
    if from_zero:
        pa_ref, pu_ref, yb_ref, x_ref, cw_ref, pw_ref, psc_ref, wo_ref = refs[:8]
    else:
        pa_ref, pu_ref, yb_ref, x_ref, cs0_ref, ps0_ref, cw_ref, pw_ref, psc_ref, wo_ref = refs[:10]
    out_ref, cs_out, ps_out, ext_a, ext_u, yc = refs[-6:]
    ti = pl.program_id(1)
    rows = nb * tt
    pool_halo = POOL_BLOCK if from_zero else POOL_HALO

    @pl.when(ti == 0)
    def _():
        if from_zero:
            ext_a[:, 0:CONV_HALO, :] = jnp.zeros((nb, CONV_HALO, CONV_WIDTH), _F32)
            ext_u[:, 0:pool_halo, :] = jnp.zeros((nb, pool_halo, POOL_WIDTH), _BF16)
        else:
            ext_a[:, 0:CONV_HALO, :] = cs0_ref[...]
            ext_u[:, 0:pool_halo, :] = ps0_ref[...]

    yb = yb_ref[...].astype(_F32).reshape(rows, ATT_WIDTH).astype(_BF16)
    mixed = jnp.dot(yb, wo_ref[CONV_WIDTH:CONV_WIDTH + ATT_WIDTH, :], preferred_element_type=_F32)

    pa = pa_ref[...].astype(_F32)
    xa, gate_b, gate_c = (pa[:, :, k * CONV_WIDTH:(k + 1) * CONV_WIDTH] for k in range(3))
    prod = gate_c * xa
    ext_a[:, CONV_HALO:CONV_HALO + tt, :] = prod
    cw = cw_ref[...]
    cu = (cw[0] * ext_a[:, CONV_HALO - 2:CONV_HALO - 2 + tt, :]
          + cw[1] * ext_a[:, CONV_HALO - 1:CONV_HALO - 1 + tt, :] + cw[2] * prod)
    ya = (gate_b * cu).reshape(rows, CONV_WIDTH).astype(_BF16)
    mixed = mixed + jnp.dot(ya, wo_ref[0:CONV_WIDTH, :], preferred_element_type=_F32)

    ext_u[:, pool_halo:pool_halo + tt, :] = pu_ref[...].astype(ext_u.dtype)
    if from_zero:
        pb = POOL_BLOCK
        row = lax.broadcasted_iota(jnp.int32, (pb, 2 * pb), 0)
        col = lax.broadcasted_iota(jnp.int32, (pb, 2 * pb), 1)
        lag = pb + row - col
        for gi, w in enumerate(POOL_WINDOWS):
            cols = slice(gi * POOL_GROUP, (gi + 1) * POOL_GROUP)
            band = ((lag >= 0) & (lag < w)).astype(_F32).astype(_BF16)
            for r in range(tt // pb):
                tot = jnp.dot(band, ext_u[0, r * pb:(r + 2) * pb, cols], preferred_element_type=_F32)
                pos = start + ti * tt + r * pb + lax.broadcasted_iota(jnp.int32, (pb, 1), 0)
                cnt = jnp.minimum(w, pos + 1).astype(_F32)
                ug = pu_ref[0, r * pb:(r + 1) * pb, cols].astype(_F32)
                diff = (tot / cnt - ug).astype(_BF16)
                z = jnp.dot(diff, pw_ref[gi], preferred_element_type=_F32) * psc_ref[:, cols]
                yc[r * pb:(r + 1) * pb, cols] = z.astype(_BF16)
    else:
        u = pu_ref[...].astype(_F32)
        pos = start + ti * tt + lax.broadcasted_iota(jnp.int32, (1, tt, 1), 1)
        for gi, w in enumerate(POOL_WINDOWS):
            cols = slice(gi * POOL_GROUP, (gi + 1) * POOL_GROUP)
            ug = u[:, :, cols]
            tot = ug
            for k in range(1, w):
                tot = tot + ext_u[:, pool_halo - k:pool_halo - k + tt, cols]
            cnt = jnp.minimum(w, pos + 1).astype(_F32)
            diff = (tot / cnt - ug).reshape(rows, POOL_GROUP).astype(_BF16)
            z = jnp.dot(diff, pw_ref[gi], preferred_element_type=_F32) * psc_ref[:, cols]
            yc[:, cols] = z.astype(_BF16)

    mixed = mixed + jnp.dot(yc[...], wo_ref[CONV_WIDTH + ATT_WIDTH:, :], preferred_element_type=_F32)
    out_ref[...] = x_ref[...] + mixed.reshape(nb, tt, D_MODEL)

    tail_a = ext_a[:, tt:tt + CONV_HALO, :]
    ext_a[:, 0:CONV_HALO, :] = tail_a
    cs_out[...] = tail_a
    ps_out[...] = ext_u[:, pool_halo + tt - POOL_HALO:pool_halo + tt, :].astype(_F32)
    ext_u[:, 0:pool_halo, :] = ext_u[:, tt:tt + pool_halo, :]


def _mixer(proj, yb, x, conv_state, pool_state, conv_w, pool_w, pool_scale, w_out, layer, nb, tt, start, u_tile0):
    batch, seq, _ = x.shape
    proj3 = proj.reshape(batch, seq, proj.shape[-1])
    u_blk = u_tile0 * COL_TILE // POOL_WIDTH
    const2 = lambda b, t: (0, 0)
    from_zero = conv_state is None and pool_state is None
    assert from_zero or (conv_state is not None and pool_state is not None)
    assert not from_zero or (nb == 1 and start == 0 and tt % POOL_BLOCK == 0)
    state_specs = [] if from_zero else [pl.BlockSpec((nb, CONV_HALO, CONV_WIDTH), lambda b, t: (b, 0, 0)),
                                        pl.BlockSpec((nb, POOL_HALO, POOL_WIDTH), lambda b, t: (b, 0, 0))]
    state_args = [] if from_zero else [conv_state, pool_state]
    ext_u = (pltpu.VMEM((nb, tt + POOL_BLOCK, POOL_WIDTH), _BF16) if from_zero
             else pltpu.VMEM((nb, tt + POOL_HALO, POOL_WIDTH), _F32))
    return pl.pallas_call(
        functools.partial(_mixer_kernel, nb=nb, tt=tt, start=start, from_zero=from_zero),
        grid=(batch // nb, seq // tt),
        in_specs=[pl.BlockSpec((nb, tt, 3 * CONV_WIDTH), lambda b, t: (b, t, 0)),
                  pl.BlockSpec((nb, tt, POOL_WIDTH), lambda b, t: (b, t, u_blk)),
                  pl.BlockSpec((nb, tt, ATT_WIDTH), lambda b, t: (b, t, 0)),
                  pl.BlockSpec((nb, tt, D_MODEL), lambda b, t: (b, t, 0))] + state_specs + [
                  pl.BlockSpec((3, CONV_WIDTH), const2),
                  pl.BlockSpec((None, len(POOL_WINDOWS), POOL_GROUP, POOL_GROUP), lambda b, t: (layer, 0, 0, 0)),
                  pl.BlockSpec((1, POOL_WIDTH), const2),
                  pl.BlockSpec((None, D_MODEL, D_MODEL), lambda b, t: (layer, 0, 0))],
        out_specs=[pl.BlockSpec((nb, tt, D_MODEL), lambda b, t: (b, t, 0)),
                   pl.BlockSpec((nb, CONV_HALO, CONV_WIDTH), lambda b, t: (b, 0, 0)),
                   pl.BlockSpec((nb, POOL_HALO, POOL_WIDTH), lambda b, t: (b, 0, 0))],
        out_shape=[jax.ShapeDtypeStruct((batch, seq, D_MODEL), _F32),
                   jax.ShapeDtypeStruct((batch, CONV_HALO, CONV_WIDTH), _F32),
                   jax.ShapeDtypeStruct((batch, POOL_HALO, POOL_WIDTH), _F32)],
        scratch_shapes=[pltpu.VMEM((nb, tt + CONV_HALO, CONV_WIDTH), _F32), ext_u,
                        pltpu.VMEM((nb * tt, POOL_WIDTH), _BF16)],
        compiler_params=pltpu.CompilerParams(dimension_semantics=("parallel", "arbitrary"),
                                             vmem_limit_bytes=VMEM_LIMIT),
        name="mixer",
    )(proj3, proj3, yb, x, *state_args, conv_w, pool_w, pool_scale.reshape(1, POOL_WIDTH), w_out)


def _ffn_sample_kernel(x_ref, n2_ref, wg_ref, wu_ref, cw_ref, wd_ref, st0_ref, fn_ref,
                       out_ref, st_out, wgt_ref, wut_ref, wdt_ref, h_s, ext_s, *, nb, tt, final):
    f = pl.program_id(0)
    rows = nb * tt

    @pl.when(f == 0)
    def _():
        x = x_ref[...]
        h_s[...] = _rms_norm(x, n2_ref[...]).reshape(rows, D_MODEL).astype(_BF16)
        out_ref[...] = x

    def tile(width):
        cols = slice(0, width)
        wg, wu, wd = wg_ref[:, cols].astype(_BF16), wu_ref[:, cols].astype(_BF16), wd_ref[cols, :].astype(_BF16)
        wgt_ref[:, cols], wut_ref[:, cols], wdt_ref[cols, :] = wg, wu, wd
        if width < FF_TILE:
            wgt_ref[:, width:] = jnp.zeros((D_MODEL, FF_TILE - width), _BF16)
            wut_ref[:, width:] = jnp.zeros((D_MODEL, FF_TILE - width), _BF16)
            wdt_ref[width:, :] = jnp.zeros((FF_TILE - width, D_MODEL), _BF16)
        h = h_s[...]
        gate = jnp.dot(h, wg, preferred_element_type=_F32).reshape(nb, tt, width)
        up = jnp.dot(h, wu, preferred_element_type=_F32).reshape(nb, tt, width)
        ext_s[:, 0:CONV_HALO, cols] = st0_ref[:, :, cols]
        ext_s[:, CONV_HALO:CONV_HALO + tt, cols] = gate
        cw = cw_ref[:, cols]
        gc = (cw[0] * ext_s[:, CONV_HALO - 2:CONV_HALO - 2 + tt, cols]
              + cw[1] * ext_s[:, CONV_HALO - 1:CONV_HALO - 1 + tt, cols] + cw[2] * gate)
        st_out[:, :, cols] = ext_s[:, tt:tt + CONV_HALO, cols]
        act = (jax.nn.silu(gc) * up).reshape(rows, width).astype(_BF16)
        out_ref[...] += jnp.dot(act, wd, preferred_element_type=_F32).reshape(nb, tt, D_MODEL)

    @pl.when(f < FF_TILES - 1)
    def _():
        tile(FF_TILE)

    @pl.when(f == FF_TILES - 1)
    def _():
        tile(FF_LAST)
        if final:
            out_ref[...] = _rms_norm(out_ref[...], fn_ref[...])


def _ffn_prompt_kernel(x_ref, n2_ref, wg_ref, wu_ref, cw_ref, wd_ref, fn_ref,
                       out_ref, st_out, h_s, ext_s, carry_s, *, tt, final):
    ti = pl.program_id(1)
    f = pl.program_id(2)

    @pl.when(f == 0)
    def _():
        x = x_ref[...]
        h_s[...] = _rms_norm(x, n2_ref[...]).astype(_BF16)
        out_ref[...] = x

    @pl.when(ti == 0)
    def _():
        ext_s[0:CONV_HALO, :] = jnp.zeros((CONV_HALO, FF_TILE), _F32)

    @pl.when(ti > 0)
    def _():
        ext_s[0:CONV_HALO, :] = carry_s[f]

    wg, wu, wd, cw = wg_ref[...], wu_ref[...], wd_ref[...], cw_ref[...]
    chunks = [slice(r0, r0 + FFN_SUB_ROWS) for r0 in range(0, tt, FFN_SUB_ROWS)]
    for rs in chunks:
        gate = jnp.dot(h_s[rs, :], wg, preferred_element_type=_F32)
        ext_s[CONV_HALO + rs.start:CONV_HALO + rs.stop, :] = gate
    ups = [jnp.dot(h_s[rs, :], wu, preferred_element_type=_F32) for rs in chunks]
    for rs, up in zip(chunks, ups):
        gc = (cw[0] * ext_s[CONV_HALO - 2 + rs.start:CONV_HALO - 2 + rs.stop, :]
              + cw[1] * ext_s[CONV_HALO - 1 + rs.start:CONV_HALO - 1 + rs.stop, :]
              + cw[2] * ext_s[CONV_HALO + rs.start:CONV_HALO + rs.stop, :])
        act = (jax.nn.silu(gc) * up).astype(_BF16)
        out_ref[rs, :] += jnp.dot(act, wd, preferred_element_type=_F32)
    tail = ext_s[tt:tt + CONV_HALO, :]
    carry_s[f] = tail
    st_out[...] = tail

    if final:
        @pl.when(f == FF_TILES - 1)
        def _():
            out_ref[...] = _rms_norm(out_ref[...], fn_ref[...])


def _ffn_sample(x, norm2, w_gate, w_up, conv_w, w_down, state, final_norm, layer, final):
    nb, tt, _ = x.shape
    vec = lambda f: (0, 0)
    return pl.pallas_call(
        functools.partial(_ffn_sample_kernel, nb=nb, tt=tt, final=final),
        grid=(FF_TILES,),
        in_specs=[pl.BlockSpec((nb, tt, D_MODEL), lambda f: (0, 0, 0)),
                  pl.BlockSpec((1, D_MODEL), vec),
                  pl.BlockSpec((None, D_MODEL, FF_TILE), lambda f: (layer, 0, f)),
                  pl.BlockSpec((None, D_MODEL, FF_TILE), lambda f: (layer, 0, f)),
                  pl.BlockSpec((3, FF_TILE), lambda f: (0, f)),
                  pl.BlockSpec((None, FF_TILE, D_MODEL), lambda f: (layer, f, 0)),
                  pl.BlockSpec((nb, CONV_HALO, FF_TILE), lambda f: (0, 0, f)),
                  pl.BlockSpec((1, D_MODEL), vec)],
        out_specs=[pl.BlockSpec((nb, tt, D_MODEL), lambda f: (0, 0, 0)),
                   pl.BlockSpec((nb, CONV_HALO, FF_TILE), lambda f: (0, 0, f)),
                   pl.BlockSpec((None, D_MODEL, FF_TILE), lambda f: (f, 0, 0)),
                   pl.BlockSpec((None, D_MODEL, FF_TILE), lambda f: (f, 0, 0)),
                   pl.BlockSpec((None, FF_TILE, D_MODEL), lambda f: (f, 0, 0))],
        out_shape=[jax.ShapeDtypeStruct((nb, tt, D_MODEL), _F32),
                   jax.ShapeDtypeStruct((nb, CONV_HALO, D_FF), _F32),
                   jax.ShapeDtypeStruct((FF_TILES, D_MODEL, FF_TILE), _BF16),
                   jax.ShapeDtypeStruct((FF_TILES, D_MODEL, FF_TILE), _BF16),
                   jax.ShapeDtypeStruct((FF_TILES, FF_TILE, D_MODEL), _BF16)],
        scratch_shapes=[pltpu.VMEM((nb * tt, D_MODEL), _BF16),
                        pltpu.VMEM((nb, tt + CONV_HALO, FF_TILE), _F32)],
        compiler_params=pltpu.CompilerParams(dimension_semantics=("arbitrary",), vmem_limit_bytes=VMEM_LIMIT),
        name="ffn_sample",
    )(x, norm2.reshape(1, D_MODEL), w_gate, w_up, conv_w, w_down, state, final_norm.reshape(1, D_MODEL))


def _ffn_prompt(x, norm2, w_gate_t, w_up_t, conv_w, w_down_t, final_norm, tt, final):
    batch, seq, _ = x.shape
    vec = lambda b, t, f: (0, 0)
    return pl.pallas_call(
        functools.partial(_ffn_prompt_kernel, tt=tt, final=final),
        grid=(batch, seq // tt, FF_TILES),
        in_specs=[pl.BlockSpec((None, tt, D_MODEL), lambda b, t, f: (b, t, 0)),
                  pl.BlockSpec((1, D_MODEL), vec),
                  pl.BlockSpec((None, D_MODEL, FF_TILE), lambda b, t, f: (f, 0, 0)),
                  pl.BlockSpec((None, D_MODEL, FF_TILE), lambda b, t, f: (f, 0, 0)),
                  pl.BlockSpec((3, FF_TILE), lambda b, t, f: (0, f)),
                  pl.BlockSpec((None, FF_TILE, D_MODEL), lambda b, t, f: (f, 0, 0)),
                  pl.BlockSpec((1, D_MODEL), vec)],
        out_specs=[pl.BlockSpec((None, tt, D_MODEL), lambda b, t, f: (b, t, 0)),
                   pl.BlockSpec((None, None, CONV_HALO, FF_TILE), lambda b, t, f: (b, t, 0, f))],
        out_shape=[jax.ShapeDtypeStruct((batch, seq, D_MODEL), _F32),
                   jax.ShapeDtypeStruct((batch, seq // tt, CONV_HALO, FF_TILES * FF_TILE), _F32)],
        scratch_shapes=[pltpu.VMEM((tt, D_MODEL), _BF16),
                        pltpu.VMEM((tt + CONV_HALO, FF_TILE), _F32),
                        pltpu.VMEM((FF_TILES, CONV_HALO, FF_TILE), _F32)],
        compiler_params=pltpu.CompilerParams(dimension_semantics=("parallel", "arbitrary", "arbitrary"),
                                             vmem_limit_bytes=VMEM_LIMIT),
        name="ffn_prompt",
    )(x, norm2.reshape(1, D_MODEL), w_gate_t, w_up_t, conv_w, w_down_t, final_norm.reshape(1, D_MODEL))


def _pad_rows_front(a, rows):
    return jnp.pad(a, ((0, 0), (rows - a.shape[1], 0), (0, 0)))


def kernel(x_prompt, x_sample, cache_kv_w128, cache_kv_w512, cache_kv_w2048, state_conv_a, state_pool, state_ffn_conv, norm1, w_in, conv_a_w, w_out, pool_w, pool_scale, norm2, w_gate, w_up, ffn_conv_w, w_down, final_norm):
    bp, sp, _ = x_prompt.shape
    bs, ss, _ = x_sample.shape
    caches = tuple(c.reshape(DEPTH, bs, win * KV_ROWS, HEAD_DIM)
                   for c, (win, _) in zip((cache_kv_w128, cache_kv_w512, cache_kv_w2048), DIL_GROUPS))
    tm, tt = 1024, 512
    w_out_b, pool_w_b = w_out.astype(_BF16), pool_w.astype(_BF16)
    xp, xs = x_prompt, x_sample
    kv_p = kv_s = None
    small_p, small_s = [], []
    for l in range(DEPTH):
        final = l == DEPTH - 1

        proj, *kv_new, w_in_t = _inproj_sample(xs.reshape(bs * ss, D_MODEL), norm1[l], w_in, l)
        yb, *kv_s = _attn_sample(proj, kv_new, caches, bs, ss, l, kv_s)
        xs, conv_o, pool_o = _mixer(proj, yb, xs, _pad_rows_front(state_conv_a[l], CONV_HALO),
                                    _pad_rows_front(state_pool[l], POOL_HALO), conv_a_w[l], pool_w_b, pool_scale[l],
                                    w_out_b, l, bs, ss, PAST_LEN, U_TILE0)
        xs, ffn_o, w_gate_t, w_up_t, w_down_t = _ffn_sample(
            xs, norm2[l], w_gate, w_up, ffn_conv_w[l], w_down, _pad_rows_front(state_ffn_conv[l], CONV_HALO),
            final_norm, l, final)
        small_s.append((conv_o[:, CONV_HALO - 2:], pool_o[:, POOL_HALO - POOL_BUF:], ffn_o[:, CONV_HALO - 2:]))

        nat, qkv1, qkv2, *kv_p = _inproj_prompt(xp.reshape(bp * sp, D_MODEL), norm1[l], w_in_t, bp, sp, tm, l, kv_p)
        yb = _attn_prompt(nat, qkv1, qkv2, bp, sp)
        xp, conv_o, pool_o = _mixer(nat, yb, xp, None, None, conv_a_w[l], pool_w_b, pool_scale[l], w_out_b,
                                    l, 1, tt, 0, NAT_TILES - 2)
        conv_w_pad = jnp.pad(ffn_conv_w[l], ((0, 0), (0, FF_TILES * FF_TILE - D_FF)))
        xp, ffn_o = _ffn_prompt(xp, norm2[l], w_gate_t, w_up_t, conv_w_pad, w_down_t, final_norm, tm, final)
        small_p.append((conv_o[:, CONV_HALO - 2:], pool_o[:, POOL_HALO - POOL_BUF:],
                        ffn_o[:, -1, CONV_HALO - 2:, :D_FF]))

    out = [xp, xs]
    for g, (win, _) in enumerate(DIL_GROUPS):
        out += [kv_p[g].reshape(DEPTH, bp, win, 2, HEADS, HEAD_DIM), kv_s[g].reshape(DEPTH, bs, win, 2, HEADS, HEAD_DIM)]
    for i in range(3):
        out += [jnp.stack([s[i] for s in small_p]), jnp.stack([s[i] for s in small_s])]
    return tuple(out)
```

```python
import functools

import numpy as np
import jax
import jax.numpy as jnp
from jax import lax
from jax.experimental import pallas as pl
from jax.experimental.pallas import tpu as pltpu

D_MODEL = 2048
DEPTH = 2
PAST_LEN = 16384
CONV_WIDTH = 512
ATT_WIDTH = 512
HEADS = 4
HEAD_DIM = 128
DIL_GROUPS = ((128, 1), (512, 4), (2048, 16))
POOL_WIDTH = 1024
POOL_WINDOWS = (2, 4, 8, 16)
POOL_GROUP = 256
POOL_BUF = 15
IN_COLS = 7168
D_FF = 5504
RMS_EPS = 1e-6
ATT_SCALE = HEAD_DIM ** -0.5

COL_TILE = 512
IN_TILES = IN_COLS // COL_TILE
Q_TILE = (3, 6, 9)
U_TILE0 = 12
NAT_TILES = 8
KV_ROWS = 2 * HEADS
SUB_ROWS = 512
IN_STEP_TILES = 2
FF_TILE = 512
FF_TILES = -(-D_FF // FF_TILE)
FF_LAST = D_FF - (FF_TILES - 1) * FF_TILE
FFN_SUB_ROWS = 256
CONV_HALO = 8
POOL_HALO = 16
POOL_BLOCK = 128
VMEM_LIMIT = 56 * 1024 * 1024
VMEM_LIMIT_HIGH = 61 * 1024 * 1024

_BF16 = jnp.bfloat16
_F32 = jnp.float32


def _alibi_slopes():
    h = np.arange(1, 3 * HEADS + 1, dtype=np.float32)
    return [float(v) for v in np.power(np.float32(2.0), -8.0 * h / (3 * HEADS))]


SLOPES = _alibi_slopes()


def _rms_norm(x, g):
    return (x * lax.rsqrt(jnp.mean(x * x, axis=-1, keepdims=True) + RMS_EPS)) * g


def _head_cols(h, base=0):
    return slice(base + h * HEAD_DIM, base + (h + 1) * HEAD_DIM)


def _inproj_sample_kernel(x_ref, g_ref, w_ref, proj_ref, kv0_ref, kv1_ref, kv2_ref, wt_ref, h_s):
    j = pl.program_id(1)

    @pl.when(j == 0)
    def _():
        h_s[...] = _rms_norm(x_ref[...], g_ref[...]).astype(_BF16)

    w = w_ref[...].astype(_BF16)
    wt_ref[...] = w
    acc = jnp.dot(h_s[...], w, preferred_element_type=_F32)
    proj_ref[...] = acc.astype(_BF16)
    for g, ref in enumerate((kv0_ref, kv1_ref, kv2_ref)):
        @pl.when((j == Q_TILE[g] + 1) | (j == Q_TILE[g] + 2))
        def _(ref=ref):
            ref[...] = acc


def _inproj_sample(x2d, g, w, layer):
    m = x2d.shape[0]

    def kv_spec(grp):
        return pl.BlockSpec((m, COL_TILE), lambda i, j: (i, jnp.clip(j - Q_TILE[grp] - 1, 0, 1)))

    return pl.pallas_call(
        _inproj_sample_kernel,
        grid=(1, IN_TILES),
        in_specs=[pl.BlockSpec((m, D_MODEL), lambda i, j: (i, 0)),
                  pl.BlockSpec((1, D_MODEL), lambda i, j: (0, 0)),
                  pl.BlockSpec((None, D_MODEL, COL_TILE), lambda i, j: (layer, 0, j))],
        out_specs=[pl.BlockSpec((m, COL_TILE), lambda i, j: (i, j)), kv_spec(0), kv_spec(1), kv_spec(2),
                   pl.BlockSpec((None, D_MODEL, COL_TILE), lambda i, j: (j, 0, 0))],
        out_shape=[jax.ShapeDtypeStruct((m, IN_COLS), _BF16)] +
                  [jax.ShapeDtypeStruct((m, 2 * ATT_WIDTH), _F32)] * 3 +
                  [jax.ShapeDtypeStruct((IN_TILES, D_MODEL, COL_TILE), _BF16)],
        scratch_shapes=[pltpu.VMEM((m, D_MODEL), _BF16)],
        compiler_params=pltpu.CompilerParams(dimension_semantics=("parallel", "arbitrary"),
                                             vmem_limit_bytes=VMEM_LIMIT),
        name="inproj_sample",
    )(x2d, g.reshape(1, D_MODEL), w)


def _inproj_prompt_kernel(*refs, tm, aliased):
    n_in = 6 if aliased else 3
    x_ref, g_ref, w_ref = refs[:3]
    nat_ref, g1_ref, g2_ref, kv0_ref, kv1_ref, kv2_ref, h_s, de_s = refs[n_in:]
    kv_refs = (kv0_ref, kv1_ref, kv2_ref)
    res_refs = (None, g1_ref, g2_ref)
    j = pl.program_id(1)

    @pl.when(j == 0)
    def _():
        h_s[...] = _rms_norm(x_ref[...], g_ref[...]).astype(_BF16)

    def column_tiles(tiles):
        for c in range(tm // SUB_ROWS):
            r0 = c * SUB_ROWS
            for half, tile in enumerate(tiles):
                acc = jnp.dot(h_s[r0:r0 + SUB_ROWS, :], w_ref[half], preferred_element_type=_F32)
                grp = (tile - Q_TILE[0]) // 3 if Q_TILE[0] <= tile < U_TILE0 else None
                part = None if grp is None or tile == Q_TILE[grp] else tile - Q_TILE[grp] - 1
                dil = 1 if grp is None else DIL_GROUPS[grp][1]
                if dil == 1:
                    nat_ref[r0:r0 + SUB_ROWS, half * COL_TILE:(half + 1) * COL_TILE] = acc.astype(_BF16)
                else:
                    for h in range(HEADS):
                        de_s[c, half, h] = acc[:, _head_cols(h)]
                    n = SUB_ROWS // dil
                    for res in range(dil):
                        for h in range(HEADS):
                            res_refs[grp][res, c * n:(c + 1) * n, _head_cols(h, (tile - Q_TILE[grp]) * COL_TILE)] = (
                                de_s[c, half, h, pl.ds(res, n, stride=dil), :].astype(_BF16))
                if part is not None:
                    keep = min(DIL_GROUPS[grp][0], tm)
                    lo = max(r0, tm - keep)
                    n = r0 + SUB_ROWS - lo
                    if n > 0:
                        for h in range(HEADS):
                            dst = pl.ds((lo - (tm - keep)) * KV_ROWS + part * HEADS + h, n, stride=KV_ROWS)
                            kv_refs[grp][dst, :] = acc[lo - r0:lo - r0 + n, _head_cols(h)]

    for step in range(IN_TILES // IN_STEP_TILES):
        @pl.when(j == step)
        def _(step=step):
            column_tiles(tuple(range(step * IN_STEP_TILES, (step + 1) * IN_STEP_TILES)))


def _inproj_prompt(x2d, g, w, batch, seq, tm, layer, prev_kv):
    m = x2d.shape[0]
    tps = seq // tm
    aliased = prev_kv is not None
    (w0, _), (w1, d1), (w2, d2) = DIL_GROUPS
    assert w2 == seq and w0 <= tm and w1 <= tm and tm % SUB_ROWS == 0

    in_specs = [pl.BlockSpec((tm, D_MODEL), lambda i, j: (i, 0)),
                pl.BlockSpec((1, D_MODEL), lambda i, j: (0, 0)),
                pl.BlockSpec((IN_STEP_TILES, D_MODEL, COL_TILE), lambda i, j: (j, 0, 0))]
    args = [x2d, g.reshape(1, D_MODEL), w]
    if aliased:
        in_specs += [pl.BlockSpec(memory_space=pl.ANY)] * 3
        args += list(prev_kv)

    def res_spec(dil):
        return pl.BlockSpec((None, dil, tm // dil, 3 * ATT_WIDTH), lambda i, j: (i // tps, 0, i % tps, 0))

    assert Q_TILE[1] % IN_STEP_TILES == 0 and U_TILE0 % IN_STEP_TILES == 0 and IN_TILES - U_TILE0 == IN_STEP_TILES
    nat_lo, u_step = Q_TILE[1] // IN_STEP_TILES, U_TILE0 // IN_STEP_TILES
    out_specs = [
        pl.BlockSpec((tm, IN_STEP_TILES * COL_TILE),
                     lambda i, j: (i, jnp.clip(j, 0, nat_lo - 1) + jnp.clip(j - u_step + 1, 0, 1))),
        res_spec(d1), res_spec(d2),
        pl.BlockSpec((None, None, w0 * KV_ROWS, HEAD_DIM), lambda i, j: (layer, i // tps, 0, 0),
                     pipeline_mode=pl.Buffered(1)),
        pl.BlockSpec((None, None, w1 * KV_ROWS, HEAD_DIM), lambda i, j: (layer, i // tps, 0, 0),
                     pipeline_mode=pl.Buffered(1)),
        pl.BlockSpec((None, None, tm * KV_ROWS, HEAD_DIM), lambda i, j: (layer, i // tps, i % tps, 0)),
    ]
    out_shape = [
        jax.ShapeDtypeStruct((m, NAT_TILES * COL_TILE), _BF16),
        jax.ShapeDtypeStruct((batch, d1, seq // d1, 3 * ATT_WIDTH), _BF16),
        jax.ShapeDtypeStruct((batch, d2, seq // d2, 3 * ATT_WIDTH), _BF16),
    ] + [jax.ShapeDtypeStruct((DEPTH, batch, win * KV_ROWS, HEAD_DIM), _F32) for win, _ in DIL_GROUPS]
    return pl.pallas_call(
        functools.partial(_inproj_prompt_kernel, tm=tm, aliased=aliased),
        grid=(m // tm, IN_TILES // IN_STEP_TILES),
        in_specs=in_specs, out_specs=out_specs, out_shape=out_shape,
        scratch_shapes=[pltpu.VMEM((tm, D_MODEL), _BF16),
                        pltpu.VMEM((tm // SUB_ROWS, IN_STEP_TILES, HEADS, SUB_ROWS, HEAD_DIM), _F32)],
        input_output_aliases={3: 3, 4: 4, 5: 5} if aliased else {},
        compiler_params=pltpu.CompilerParams(dimension_semantics=("arbitrary", "arbitrary"),
                                             vmem_limit_bytes=VMEM_LIMIT_HIGH),
        name="inproj_prompt",
    )(*args)


QB = 128
G0_UNROLL = 3
G2_RES_PER_STEP = 4


def _attn_prompt_kernel(q0, k0, v0, q1, k1, v1, q2, k2, v2, o_ref, acc_s, m_s, l_s, shuf_s, *, seq):
    step = pl.program_id(1)
    n1 = DIL_GROUPS[1][1]
    n2_steps = DIL_GROUPS[2][1] // G2_RES_PER_STEP

    row0 = lax.broadcasted_iota(jnp.int32, (QB, QB), 0)
    col0 = lax.broadcasted_iota(jnp.int32, (QB, QB), 1)
    dist_first = jnp.where(col0 <= row0, (row0 - col0).astype(_F32), jnp.inf)
    row1 = lax.broadcasted_iota(jnp.int32, (QB, 2 * QB), 0)
    col1 = lax.broadcasted_iota(jnp.int32, (QB, 2 * QB), 1)
    d1 = QB + row1 - col1
    dist_next = jnp.where((d1 >= 0) & (d1 <= QB), d1.astype(_F32), jnp.inf)
    ones = jnp.ones((2 * QB, HEAD_DIM), _BF16)

    seg = seq // n1
    chunk = QB // n1
    assert n1 == 4 and DIL_GROUPS[2][1] == 16

    def blocks(specs):
        tiles = []
        for slot, (qr, kr, vr, grp, res, i, is_first_block) in enumerate(specs):
            dil = DIL_GROUPS[grp][1]
            for h in range(HEADS):
                hs = _head_cols(h)
                slope_step = SLOPES[grp * HEADS + h] * dil
                if is_first_block:
                    q, k, v = qr[0:QB, hs], kr[0:QB, hs], vr[0:QB, hs]
                    bias, one = dist_first * slope_step, ones[0:QB]
                else:
                    qs = pl.ds(pl.multiple_of(i * QB, QB), QB)
                    ks = pl.ds(pl.multiple_of(i * QB - QB, QB), 2 * QB)
                    q, k, v = qr[qs, hs], kr[ks, hs], vr[ks, hs]
                    bias, one = dist_next * slope_step, ones
                tiles.append((q, k, jnp.concatenate([v, one], axis=1), bias, grp, res, i, is_first_block, slot, h))
        scores = [lax.dot_general(t[0], t[1], (((1,), (1,)), ((), ())), preferred_element_type=_F32) for t in tiles]
        probs, maxes = [], []
        for t, s in zip(tiles, scores):
            s = s * ATT_SCALE - t[3]
            m = jnp.max(s, axis=1, keepdims=True)
            probs.append(jnp.exp(s - m).astype(_BF16))
            maxes.append(jnp.broadcast_to(m, (QB, HEAD_DIM)))
        outs = [jnp.dot(p, t[2], preferred_element_type=_F32) for t, p in zip(tiles, probs)]
        def merge(h, rows, acc, m, l):
            m_old, l_old, a_old = m_s[h, rows, :], l_s[h, rows, :], acc_s[h, rows, :]
            m_new = jnp.maximum(m_old, m)
            w_old = jnp.exp(m_old - m_new)
            w_new = jnp.exp(m - m_new)
            acc_s[h, rows, :] = a_old * w_old + acc * w_new
            l_s[h, rows, :] = l_old * w_old + l * w_new
            m_s[h, rows, :] = m_new

        for (_, _, _, _, grp, res, i, is_first_block, slot, h), m, acc_l in zip(tiles, maxes, outs):
            acc, l = acc_l[:, :HEAD_DIM], acc_l[:, HEAD_DIM:]
            if grp == 1:
                rows = pl.ds(pl.multiple_of(res * seg + i * QB, QB), QB)
                acc_s[h, rows, :], m_s[h, rows, :], l_s[h, rows, :] = acc, m, l
            elif grp == 2:
                merge(h, pl.ds((res & (n1 - 1)) * seg + (res >> 2), QB, stride=n1), acc, m, l)
            else:
                tmps = [shuf_s.at[(slot * HEADS + h) * 3 + a] for a in range(3)]
                for tmp, val in zip(tmps, (acc, m, l)):
                    tmp[...] = val
                for r4 in range(n1):
                    dst = r4 * seg + i * chunk
                    dst = pl.ds(dst if is_first_block else pl.multiple_of(dst, chunk), chunk)
                    merge(h, dst, *(tmp[pl.ds(r4, chunk, stride=n1), :] for tmp in tmps))

    @pl.when(step < n1)
    def _():
        blocks([(q1, k1, v1, 1, step, i, i == 0) for i in range(seg // QB)])

    @pl.when((step >= n1) & (step < n1 + n2_steps))
    def _():
        assert seq // DIL_GROUPS[2][1] == QB
        blocks([(q2.at[r], k2.at[r], v2.at[r], 2, (step - n1) * G2_RES_PER_STEP + r, 0, True)
                for r in range(G2_RES_PER_STEP)])

    @pl.when(step == n1 + n2_steps)
    def _():
        n_blk = seq // QB
        assert (n_blk - 1) % G0_UNROLL == 0
        blocks([(q0, k0, v0, 0, 0, 0, True)])

        def g0_body(it, carry):
            blocks([(q0, k0, v0, 0, 0, 1 + it * G0_UNROLL + u, False) for u in range(G0_UNROLL)])
            return carry
        lax.fori_loop(0, (n_blk - 1) // G0_UNROLL, g0_body, 0)

        def body(c, carry):
            for h in range(HEADS):
                tmp = shuf_s.at[h]
                for r4 in range(n1):
                    src = pl.ds(pl.multiple_of(r4 * seg + c * chunk, chunk), chunk)
                    tmp[pl.ds(r4, chunk, stride=n1), :] = acc_s[h, src, :] / l_s[h, src, :]
                o_ref[pl.ds(pl.multiple_of(c * QB, QB), QB), _head_cols(h)] = tmp[...].astype(_BF16)
            return carry
        lax.fori_loop(0, seq // QB, body, 0)


def _attn_prompt(nat, qkv1, qkv2, batch, seq):
    n1, n2 = DIL_GROUPS[1][1], DIL_GROUPS[2][1]
    n2_steps = n2 // G2_RES_PER_STEP
    nat3 = nat.reshape(batch, seq, NAT_TILES * COL_TILE)
    in_specs = [pl.BlockSpec((None, seq, COL_TILE), lambda b, s, t=t: (b, 0, Q_TILE[0] + t)) for t in range(3)]
    in_specs += [pl.BlockSpec((None, None, seq // n1, COL_TILE),
                              lambda b, s, t=t: (b, jnp.clip(s, 0, n1 - 1), 0, t)) for t in range(3)]
    in_specs += [pl.BlockSpec((None, G2_RES_PER_STEP, seq // n2, COL_TILE),
                              lambda b, s, t=t: (b, jnp.clip(s - n1, 0, n2_steps - 1), 0, t)) for t in range(3)]
    args = [nat3] * 3 + [qkv1] * 3 + [qkv2] * 3
    return pl.pallas_call(
        functools.partial(_attn_prompt_kernel, seq=seq),
        grid=(batch, 1 + n1 + n2_steps),
        in_specs=in_specs,
        out_specs=pl.BlockSpec((None, seq, ATT_WIDTH), lambda b, s: (b, 0, 0)),
        out_shape=jax.ShapeDtypeStruct((batch, seq, ATT_WIDTH), _BF16),
        scratch_shapes=[pltpu.VMEM((HEADS, seq, HEAD_DIM), _F32)] * 3
                       + [pltpu.VMEM((G0_UNROLL * HEADS * 3, QB, HEAD_DIM), _F32)],
        compiler_params=pltpu.CompilerParams(dimension_semantics=("parallel", "arbitrary"),
                                             vmem_limit_bytes=VMEM_LIMIT),
        name="attn_prompt",
    )(*args)


def _attn_sample_kernel(*refs, tdec, aliased):
    n_in = 10 if aliased else 7
    proj_ref, n0, n1, n2, c0, c1, c2 = refs[:7]
    yb_ref, o0, o1, o2 = refs[n_in:]
    news, caches, outs = (n0, n1, n2), (c0, c1, c2), (o0, o1, o2)
    for h in range(HEADS):
        pieces = []
        for grp, (win, dil) in enumerate(DIL_GROUPS):
            slope = SLOPES[grp * HEADS + h]
            q = proj_ref[:, _head_cols(h, Q_TILE[grp] * COL_TILE)]
            k_new = news[grp][:, _head_cols(h)]
            v_new = news[grp][:, _head_cols(h, ATT_WIDTH)]
            k_old = caches[grp][pl.ds(h, win, stride=KV_ROWS), :]
            v_old = caches[grp][pl.ds(HEADS + h, win, stride=KV_ROWS), :]
            for k, v, n_keys, base in ((k_old, v_old, win, win), (k_new, v_new, tdec, 0)):
                s = lax.dot_general(q, k.astype(_BF16), (((1,), (1,)), ((), ())), preferred_element_type=_F32)
                t = lax.broadcasted_iota(jnp.int32, (tdec, n_keys), 0)
                c = lax.broadcasted_iota(jnp.int32, (tdec, n_keys), 1)
                dist = base + t - c
                ok = (dist >= 0) & (dist <= win) & ((dist & (dil - 1)) == 0)
                s = jnp.where(ok, s * ATT_SCALE - slope * dist.astype(_F32), -jnp.inf)
                pieces.append((s, v.astype(_BF16)))
        m = functools.reduce(jnp.maximum, [jnp.max(s, axis=1, keepdims=True) for s, _ in pieces])
        l = jnp.zeros((tdec, 1), _F32)
        acc = jnp.zeros((tdec, HEAD_DIM), _F32)
        for s, v in pieces:
            p = jnp.exp(s - m)
            l = l + jnp.sum(p, axis=1, keepdims=True)
            acc = acc + jnp.dot(p.astype(_BF16), v, preferred_element_type=_F32)
        yb_ref[:, _head_cols(h)] = (acc / l).astype(_BF16)
    for grp, (win, _) in enumerate(DIL_GROUPS):
        kept = (win - tdec) * KV_ROWS
        outs[grp][0:kept, :] = caches[grp][tdec * KV_ROWS:win * KV_ROWS, :]
        for part in range(2):
            for h in range(HEADS):
                outs[grp][pl.ds(kept + part * HEADS + h, tdec, stride=KV_ROWS), :] = (
                    news[grp][:, _head_cols(h, part * ATT_WIDTH)])


def _attn_sample(proj, kv_new, caches, batch, tdec, layer, prev_out):
    aliased = prev_out is not None
    in_specs = [pl.BlockSpec((None, tdec, IN_COLS), lambda b: (b, 0, 0))]
    in_specs += [pl.BlockSpec((None, tdec, 2 * ATT_WIDTH), lambda b: (b, 0, 0))] * 3
    cache_specs = [pl.BlockSpec((None, None, win * KV_ROWS, HEAD_DIM), lambda b: (layer, b, 0, 0)) for win, _ in DIL_GROUPS]
    in_specs += cache_specs
    args = [proj.reshape(batch, tdec, IN_COLS)] + [k.reshape(batch, tdec, 2 * ATT_WIDTH) for k in kv_new] + list(caches)
    if aliased:
        in_specs += [pl.BlockSpec(memory_space=pl.ANY)] * 3
        args += list(prev_out)
    out_specs = [pl.BlockSpec((None, tdec, ATT_WIDTH), lambda b: (b, 0, 0))] + cache_specs
    out_shape = [jax.ShapeDtypeStruct((batch, tdec, ATT_WIDTH), _BF16)]
    out_shape += [jax.ShapeDtypeStruct((DEPTH, batch, win * KV_ROWS, HEAD_DIM), _F32) for win, _ in DIL_GROUPS]
    return pl.pallas_call(
        functools.partial(_attn_sample_kernel, tdec=tdec, aliased=aliased),
        grid=(batch,),
        in_specs=in_specs, out_specs=out_specs, out_shape=out_shape,
        input_output_aliases={7: 1, 8: 2, 9: 3} if aliased else {},
        compiler_params=pltpu.CompilerParams(dimension_semantics=("arbitrary",), vmem_limit_bytes=VMEM_LIMIT),
        name="attn_sample",
    )(*args)


def _mixer_kernel(*refs, nb, tt, start, from_zero):
    if from_zero:
        pa_ref, pu_ref, yb_ref, x_ref, cw_ref, pw_ref, psc_ref, wo_ref = refs[:8]
    else:
        pa_ref, pu_ref, yb_ref, x_ref, cs0_ref, ps0_ref, cw_ref, pw_ref, psc_ref, wo_ref = refs[:10]
    out_ref, cs_out, ps_out, ext_a, ext_u, yc = refs[-6:]
    ti = pl.program_id(1)
    rows = nb * tt
    pool_halo = POOL_BLOCK if from_zero else POOL_HALO

    @pl.when(ti == 0)
    def _():
        if from_zero:
            ext_a[:, 0:CONV_HALO, :] = jnp.zeros((nb, CONV_HALO, CONV_WIDTH), _F32)
            ext_u[:, 0:pool_halo, :] = jnp.zeros((nb, pool_halo, POOL_WIDTH), _BF16)
        else:
            ext_a[:, 0:CONV_HALO, :] = cs0_ref[...]
            ext_u[:, 0:pool_halo, :] = ps0_ref[...]

    yb = yb_ref[...].astype(_F32).reshape(rows, ATT_WIDTH).astype(_BF16)
    mixed = jnp.dot(yb, wo_ref[CONV_WIDTH:CONV_WIDTH + ATT_WIDTH, :], preferred_element_type=_F32)

    pa = pa_ref[...].astype(_F32)
    xa, gate_b, gate_c = (pa[:, :, k * CONV_WIDTH:(k + 1) * CONV_WIDTH] for k in range(3))
    prod = gate_c * xa
    ext_a[:, CONV_HALO:CONV_HALO + tt, :] = prod
    cw = cw_ref[...]
    cu = (cw[0] * ext_a[:, CONV_HALO - 2:CONV_HALO - 2 + tt, :]
          + cw[1] * ext_a[:, CONV_HALO - 1:CONV_HALO - 1 + tt, :] + cw[2] * prod)
    ya = (gate_b * cu).reshape(rows, CONV_WIDTH).astype(_BF16)
    mixed = mixed + jnp.dot(ya, wo_ref[0:CONV_WIDTH, :], preferred_element_type=_F32)

    ext_u[:, pool_halo:pool_halo + tt, :] = pu_ref[...].astype(ext_u.dtype)
    if from_zero:
        pb = POOL_BLOCK
        row = lax.broadcasted_iota(jnp.int32, (pb, 2 * pb), 0)
        col = lax.broadcasted_iota(jnp.int32, (pb, 2 * pb), 1)
        lag = pb + row - col
        for gi, w in enumerate(POOL_WINDOWS):
            cols = slice(gi * POOL_GROUP, (gi + 1) * POOL_GROUP)
            band = ((lag >= 0) & (lag < w)).astype(_F32).astype(_BF16)
            for r in range(tt // pb):
                tot = jnp.dot(band, ext_u[0, r * pb:(r + 2) * pb, cols], preferred_element_type=_F32)
                pos = start + ti * tt + r * pb + lax.broadcasted_iota(jnp.int32, (pb, 1), 0)
                cnt = jnp.minimum(w, pos + 1).astype(_F32)
                ug = pu_ref[0, r * pb:(r + 1) * pb, cols].astype(_F32)
                diff = (tot / cnt - ug).astype(_BF16)
                z = jnp.dot(diff, pw_ref[gi], preferred_element_type=_F32) * psc_ref[:, cols]
                yc[r * pb:(r + 1) * pb, cols] = z.astype(_BF16)
    else:
        u = pu_ref[...].astype(_F32)
        pos = start + ti * tt + lax.broadcasted_iota(jnp.int32, (1, tt, 1), 1)
        for gi, w in enumerate(POOL_WINDOWS):
            cols = slice(gi * POOL_GROUP, (gi + 1) * POOL_GROUP)
            ug = u[:, :, cols]
            tot = ug
            for k in range(1, w):
                tot = tot + ext_u[:, pool_halo - k:pool_halo - k + tt, cols]
            cnt = jnp.minimum(w, pos + 1).astype(_F32)
            diff = (tot / cnt - ug).reshape(rows, POOL_GROUP).astype(_BF16)
            z = jnp.dot(diff, pw_ref[gi], preferred_element_type=_F32) * psc_ref[:, cols]
            yc[:, cols] = z.astype(_BF16)

    mixed = mixed + jnp.dot(yc[...], wo_ref[CONV_WIDTH + ATT_WIDTH:, :], preferred_element_type=_F32)
    out_ref[...] = x_ref[...] + mixed.reshape(nb, tt, D_MODEL)

    tail_a = ext_a[:, tt:tt + CONV_HALO, :]
    ext_a[:, 0:CONV_HALO, :] = tail_a
    cs_out[...] = tail_a
    ps_out[...] = ext_u[:, pool_halo + tt - POOL_HALO:pool_halo + tt, :].astype(_F32)
    ext_u[:, 0:pool_halo, :] = ext_u[:, tt:tt + pool_halo, :]


def _mixer(proj, yb, x, conv_state, pool_state, conv_w, pool_w, pool_scale, w_out, layer, nb, tt, start, u_tile0):
    batch, seq, _ = x.shape
    proj3 = proj.reshape(batch, seq, proj.shape[-1])
    u_blk = u_tile0 * COL_TILE // POOL_WIDTH
    const2 = lambda b, t: (0, 0)
    from_zero = conv_state is None and pool_state is None
    assert from_zero or (conv_state is not None and pool_state is not None)
    assert not from_zero or (nb == 1 and start == 0 and tt % POOL_BLOCK == 0)
    state_specs = [] if from_zero else [pl.BlockSpec((nb, CONV_HALO, CONV_WIDTH), lambda b, t: (b, 0, 0)),
                                        pl.BlockSpec((nb, POOL_HALO, POOL_WIDTH), lambda b, t: (b, 0, 0))]
    state_args = [] if from_zero else [conv_state, pool_state]
    ext_u = (pltpu.VMEM((nb, tt + POOL_BLOCK, POOL_WIDTH), _BF16) if from_zero
             else pltpu.VMEM((nb, tt + POOL_HALO, POOL_WIDTH), _F32))
    return pl.pallas_call(
        functools.partial(_mixer_kernel, nb=nb, tt=tt, start=start, from_zero=from_zero),
        grid=(batch // nb, seq // tt),
        in_specs=[pl.BlockSpec((nb, tt, 3 * CONV_WIDTH), lambda b, t: (b, t, 0)),
                  pl.BlockSpec((nb, tt, POOL_WIDTH), lambda b, t: (b, t, u_blk)),
                  pl.BlockSpec((nb, tt, ATT_WIDTH), lambda b, t: (b, t, 0)),
                  pl.BlockSpec((nb, tt, D_MODEL), lambda b, t: (b, t, 0))] + state_specs + [
                  pl.BlockSpec((3, CONV_WIDTH), const2),
                  pl.BlockSpec((None, len(POOL_WINDOWS), POOL_GROUP, POOL_GROUP), lambda b, t: (layer, 0, 0, 0)),
                  pl.BlockSpec((1, POOL_WIDTH), const2),
                  pl.BlockSpec((None, D_MODEL, D_MODEL), lambda b, t: (layer, 0, 0))],
        out_specs=[pl.BlockSpec((nb, tt, D_MODEL), lambda b, t: (b, t, 0)),
                   pl.BlockSpec((nb, CONV_HALO, CONV_WIDTH), lambda b, t: (b, 0, 0)),
                   pl.BlockSpec((nb, POOL_HALO, POOL_WIDTH), lambda b, t: (b, 0, 0))],
        out_shape=[jax.ShapeDtypeStruct((batch, seq, D_MODEL), _F32),
                   jax.ShapeDtypeStruct((batch, CONV_HALO, CONV_WIDTH), _F32),
                   jax.ShapeDtypeStruct((batch, POOL_HALO, POOL_WIDTH), _F32)],
        scratch_shapes=[pltpu.VMEM((nb, tt + CONV_HALO, CONV_WIDTH), _F32), ext_u,
                        pltpu.VMEM((nb * tt, POOL_WIDTH), _BF16)],
        compiler_params=pltpu.CompilerParams(dimension_semantics=("parallel", "arbitrary"),
                                             vmem_limit_bytes=VMEM_LIMIT),
        name="mixer",
    )(proj3, proj3, yb, x, *state_args, conv_w, pool_w, pool_scale.reshape(1, POOL_WIDTH), w_out)


def _ffn_sample_kernel(x_ref, n2_ref, wg_ref, wu_ref, cw_ref, wd_ref, st0_ref, fn_ref,
                       out_ref, st_out, wgt_ref, wut_ref, wdt_ref, h_s, ext_s, *, nb, tt, final):
    f = pl.program_id(0)
    rows = nb * tt

    @pl.when(f == 0)
    def _():
        x = x_ref[...]
        h_s[...] = _rms_norm(x, n2_ref[...]).reshape(rows, D_MODEL).astype(_BF16)
        out_ref[...] = x

    def tile(width):
        cols = slice(0, width)
        wg, wu, wd = wg_ref[:, cols].astype(_BF16), wu_ref[:, cols].astype(_BF16), wd_ref[cols, :].astype(_BF16)
        wgt_ref[:, cols], wut_ref[:, cols], wdt_ref[cols, :] = wg, wu, wd
        if width < FF_TILE:
            wgt_ref[:, width:] = jnp.zeros((D_MODEL, FF_TILE - width), _BF16)
            wut_ref[:, width:] = jnp.zeros((D_MODEL, FF_TILE - width), _BF16)
            wdt_ref[width:, :] = jnp.zeros((FF_TILE - width, D_MODEL), _BF16)
        h = h_s[...]
        gate = jnp.dot(h, wg, preferred_element_type=_F32).reshape(nb, tt, width)
        up = jnp.dot(h, wu, preferred_element_type=_F32).reshape(nb, tt, width)
        ext_s[:, 0:CONV_HALO, cols] = st0_ref[:, :, cols]
        ext_s[:, CONV_HALO:CONV_HALO + tt, cols] = gate
        cw = cw_ref[:, cols]
        gc = (cw[0] * ext_s[:, CONV_HALO - 2:CONV_HALO - 2 + tt, cols]
              + cw[1] * ext_s[:, CONV_HALO - 1:CONV_HALO - 1 + tt, cols] + cw[2] * gate)
        st_out[:, :, cols] = ext_s[:, tt:tt + CONV_HALO, cols]
        act = (jax.nn.silu(gc) * up).reshape(rows, width).astype(_BF16)
        out_ref[...] += jnp.dot(act, wd, preferred_element_type=_F32).reshape(nb, tt, D_MODEL)

    @pl.when(f < FF_TILES - 1)
    def _():
        tile(FF_TILE)

    @pl.when(f == FF_TILES - 1)
    def _():
        tile(FF_LAST)
        if final:
            out_ref[...] = _rms_norm(out_ref[...], fn_ref[...])


def _ffn_prompt_kernel(x_ref, n2_ref, wg_ref, wu_ref, cw_ref, wd_ref, fn_ref,
                       out_ref, st_out, h_s, ext_s, carry_s, *, tt, final):
    ti = pl.program_id(1)
    f = pl.program_id(2)

    @pl.when(f == 0)
    def _():
        x = x_ref[...]
        h_s[...] = _rms_norm(x, n2_ref[...]).astype(_BF16)
        out_ref[...] = x

    @pl.when(ti == 0)
    def _():
        ext_s[0:CONV_HALO, :] = jnp.zeros((CONV_HALO, FF_TILE), _F32)

    @pl.when(ti > 0)
    def _():
        ext_s[0:CONV_HALO, :] = carry_s[f]

    wg, wu, wd, cw = wg_ref[...], wu_ref[...], wd_ref[...], cw_ref[f]
    chunks = [slice(r0, r0 + FFN_SUB_ROWS) for r0 in range(0, tt, FFN_SUB_ROWS)]
    for rs in chunks:
        gate = jnp.dot(h_s[rs, :], wg, preferred_element_type=_F32)
        ext_s[CONV_HALO + rs.start:CONV_HALO + rs.stop, :] = gate
    ups = [jnp.dot(h_s[rs, :], wu, preferred_element_type=_F32) for rs in chunks]
    for rs, up in zip(chunks, ups):
        gc = (cw[0] * ext_s[CONV_HALO - 2 + rs.start:CONV_HALO - 2 + rs.stop, :]
              + cw[1] * ext_s[CONV_HALO - 1 + rs.start:CONV_HALO - 1 + rs.stop, :]
              + cw[2] * ext_s[CONV_HALO + rs.start:CONV_HALO + rs.stop, :])
        act = (jax.nn.silu(gc) * up).astype(_BF16)
        out_ref[rs, :] += jnp.dot(act, wd, preferred_element_type=_F32)
    tail = ext_s[tt:tt + CONV_HALO, :]
    carry_s[f] = tail
    st_out[f] = tail

    if final:
        @pl.when(f == FF_TILES - 1)
        def _():
            out_ref[...] = _rms_norm(out_ref[...], fn_ref[...])


def _ffn_sample(x, norm2, w_gate, w_up, conv_w, w_down, state, final_norm, layer, final):
    nb, tt, _ = x.shape
    vec = lambda f: (0, 0)
    return pl.pallas_call(
        functools.partial(_ffn_sample_kernel, nb=nb, tt=tt, final=final),
        grid=(FF_TILES,),
        in_specs=[pl.BlockSpec((nb, tt, D_MODEL), lambda f: (0, 0, 0)),
                  pl.BlockSpec((1, D_MODEL), vec),
                  pl.BlockSpec((None, D_MODEL, FF_TILE), lambda f: (layer, 0, f)),
                  pl.BlockSpec((None, D_MODEL, FF_TILE), lambda f: (layer, 0, f)),
                  pl.BlockSpec((3, FF_TILE), lambda f: (0, f)),
                  pl.BlockSpec((None, FF_TILE, D_MODEL), lambda f: (layer, f, 0)),
                  pl.BlockSpec((nb, CONV_HALO, FF_TILE), lambda f: (0, 0, f)),
                  pl.BlockSpec((1, D_MODEL), vec)],
        out_specs=[pl.BlockSpec((nb, tt, D_MODEL), lambda f: (0, 0, 0)),
                   pl.BlockSpec((nb, CONV_HALO, FF_TILE), lambda f: (0, 0, f)),
                   pl.BlockSpec((None, D_MODEL, FF_TILE), lambda f: (f, 0, 0)),
                   pl.BlockSpec((None, D_MODEL, FF_TILE), lambda f: (f, 0, 0)),
                   pl.BlockSpec((None, FF_TILE, D_MODEL), lambda f: (f, 0, 0))],
        out_shape=[jax.ShapeDtypeStruct((nb, tt, D_MODEL), _F32),
                   jax.ShapeDtypeStruct((nb, CONV_HALO, D_FF), _F32),
                   jax.ShapeDtypeStruct((FF_TILES, D_MODEL, FF_TILE), _BF16),
                   jax.ShapeDtypeStruct((FF_TILES, D_MODEL, FF_TILE), _BF16),
                   jax.ShapeDtypeStruct((FF_TILES, FF_TILE, D_MODEL), _BF16)],
        scratch_shapes=[pltpu.VMEM((nb * tt, D_MODEL), _BF16),
                        pltpu.VMEM((nb, tt + CONV_HALO, FF_TILE), _F32)],
        compiler_params=pltpu.CompilerParams(dimension_semantics=("arbitrary",), vmem_limit_bytes=VMEM_LIMIT),
        name="ffn_sample",
    )(x, norm2.reshape(1, D_MODEL), w_gate, w_up, conv_w, w_down, state, final_norm.reshape(1, D_MODEL))


def _ffn_prompt(x, norm2, w_gate_t, w_up_t, conv_w, w_down_t, final_norm, tt, final):
    batch, seq, _ = x.shape
    vec = lambda b, t, f: (0, 0)
    return pl.pallas_call(
        functools.partial(_ffn_prompt_kernel, tt=tt, final=final),
        grid=(batch, seq // tt, FF_TILES),
        in_specs=[pl.BlockSpec((None, tt, D_MODEL), lambda b, t, f: (b, t, 0)),
                  pl.BlockSpec((1, D_MODEL), vec),
                  pl.BlockSpec((None, D_MODEL, FF_TILE), lambda b, t, f: (f, 0, 0)),
                  pl.BlockSpec((None, D_MODEL, FF_TILE), lambda b, t, f: (f, 0, 0)),
                  pl.BlockSpec((FF_TILES, 3, FF_TILE), lambda b, t, f: (0, 0, 0)),
                  pl.BlockSpec((None, FF_TILE, D_MODEL), lambda b, t, f: (f, 0, 0)),
                  pl.BlockSpec((1, D_MODEL), vec)],
        out_specs=[pl.BlockSpec((None, tt, D_MODEL), lambda b, t, f: (b, t, 0)),
                   pl.BlockSpec((None, None, FF_TILES, CONV_HALO, FF_TILE), lambda b, t, f: (b, t, 0, 0, 0))],
        out_shape=[jax.ShapeDtypeStruct((batch, seq, D_MODEL), _F32),
                   jax.ShapeDtypeStruct((batch, seq // tt, FF_TILES, CONV_HALO, FF_TILE), _F32)],
        scratch_shapes=[pltpu.VMEM((tt, D_MODEL), _BF16),
                        pltpu.VMEM((tt + CONV_HALO, FF_TILE), _F32),
                        pltpu.VMEM((FF_TILES, CONV_HALO, FF_TILE), _F32)],
        compiler_params=pltpu.CompilerParams(dimension_semantics=("parallel", "arbitrary", "arbitrary"),
                                             vmem_limit_bytes=VMEM_LIMIT),
        name="ffn_prompt",
    )(x, norm2.reshape(1, D_MODEL), w_gate_t, w_up_t, conv_w, w_down_t, final_norm.reshape(1, D_MODEL))


def _pad_rows_front(a, rows):
    return jnp.pad(a, ((0, 0), (rows - a.shape[1], 0), (0, 0)))


def kernel(x_prompt, x_sample, cache_kv_w128, cache_kv_w512, cache_kv_w2048, state_conv_a, state_pool, state_ffn_conv, norm1, w_in, conv_a_w, w_out, pool_w, pool_scale, norm2, w_gate, w_up, ffn_conv_w, w_down, final_norm):
    bp, sp, _ = x_prompt.shape
    bs, ss, _ = x_sample.shape
    caches = tuple(c.reshape(DEPTH, bs, win * KV_ROWS, HEAD_DIM)
                   for c, (win, _) in zip((cache_kv_w128, cache_kv_w512, cache_kv_w2048), DIL_GROUPS))
    tm, tt = 1024, 512
    w_out_b, pool_w_b = w_out.astype(_BF16), pool_w.astype(_BF16)
    xp, xs = x_prompt, x_sample
    kv_p = kv_s = None
    small_p, small_s = [], []
    for l in range(DEPTH):
        final = l == DEPTH - 1

        proj, *kv_new, w_in_t = _inproj_sample(xs.reshape(bs * ss, D_MODEL), norm1[l], w_in, l)
        yb, *kv_s = _attn_sample(proj, kv_new, caches, bs, ss, l, kv_s)
        xs, conv_o, pool_o = _mixer(proj, yb, xs, _pad_rows_front(state_conv_a[l], CONV_HALO),
                                    _pad_rows_front(state_pool[l], POOL_HALO), conv_a_w[l], pool_w_b, pool_scale[l],
                                    w_out_b, l, bs, ss, PAST_LEN, U_TILE0)
        xs, ffn_o, w_gate_t, w_up_t, w_down_t = _ffn_sample(
            xs, norm2[l], w_gate, w_up, ffn_conv_w[l], w_down, _pad_rows_front(state_ffn_conv[l], CONV_HALO),
            final_norm, l, final)
        small_s.append((conv_o[:, CONV_HALO - 2:], pool_o[:, POOL_HALO - POOL_BUF:], ffn_o[:, CONV_HALO - 2:]))

        nat, qkv1, qkv2, *kv_p = _inproj_prompt(xp.reshape(bp * sp, D_MODEL), norm1[l], w_in_t, bp, sp, tm, l, kv_p)
        yb = _attn_prompt(nat, qkv1, qkv2, bp, sp)
        xp, conv_o, pool_o = _mixer(nat, yb, xp, None, None, conv_a_w[l], pool_w_b, pool_scale[l], w_out_b,
                                    l, 1, tt, 0, NAT_TILES - 2)
        conv_w_t = jnp.pad(ffn_conv_w[l], ((0, 0), (0, FF_TILES * FF_TILE - D_FF)))
        conv_w_t = conv_w_t.reshape(3, FF_TILES, FF_TILE).transpose(1, 0, 2)
        xp, ffn_o = _ffn_prompt(xp, norm2[l], w_gate_t, w_up_t, conv_w_t, w_down_t, final_norm, tm, final)
        ffn_tail = ffn_o[:, -1, :, CONV_HALO - 2:].transpose(0, 2, 1, 3).reshape(bp, 2, FF_TILES * FF_TILE)
        small_p.append((conv_o[:, CONV_HALO - 2:], pool_o[:, POOL_HALO - POOL_BUF:], ffn_tail[:, :, :D_FF]))

    out = [xp, xs]
    for g, (win, _) in enumerate(DIL_GROUPS):
        out += [kv_p[g].reshape(DEPTH, bp, win, 2, HEADS, HEAD_DIM), kv_s[g].reshape(DEPTH, bs, win, 2, HEADS, HEAD_DIM)]
    for i in range(3):
        out += [jnp.stack([s[i] for s in small_p]), jnp.stack([s[i] for s in small_s])]
    return tuple(out)
```

```python
import functools

import numpy as np
import jax
import jax.numpy as jnp
from jax import lax
from jax.experimental import pallas as pl
from jax.experimental.pallas import tpu as pltpu

D_MODEL = 2048
DEPTH = 2
PAST_LEN = 16384
CONV_WIDTH = 512
ATT_WIDTH = 512
HEADS = 4
HEAD_DIM = 128
DIL_GROUPS = ((128, 1), (512, 4), (2048, 16))
POOL_WIDTH = 1024
POOL_WINDOWS = (2, 4, 8, 16)
POOL_GROUP = 256
POOL_BUF = 15
IN_COLS = 7168
D_FF = 5504
RMS_EPS = 1e-6
ATT_SCALE = HEAD_DIM ** -0.5

COL_TILE = 512
IN_TILES = IN_COLS // COL_TILE
Q_TILE = (3, 6, 9)
U_TILE0 = 12
NAT_TILES = 8
KV_ROWS = 2 * HEADS
SUB_ROWS = 512
IN_STEP_TILES = 2
FF_TILE = 512
FF_TILES = -(-D_FF // FF_TILE)
FF_LAST = D_FF - (FF_TILES - 1) * FF_TILE
FFN_SUB_ROWS = 256
CONV_HALO = 8
POOL_HALO = 16
POOL_BLOCK = 128
VMEM_LIMIT = 56 * 1024 * 1024
VMEM_LIMIT_HIGH = 61 * 1024 * 1024

_BF16 = jnp.bfloat16
_F32 = jnp.float32


def _alibi_slopes():
    h = np.arange(1, 3 * HEADS + 1, dtype=np.float32)
    return [float(v) for v in np.power(np.float32(2.0), -8.0 * h / (3 * HEADS))]


SLOPES = _alibi_slopes()


def _rms_norm(x, g):
    return (x * lax.rsqrt(jnp.mean(x * x, axis=-1, keepdims=True) + RMS_EPS)) * g


def _head_cols(h, base=0):
    return slice(base + h * HEAD_DIM, base + (h + 1) * HEAD_DIM)


def _inproj_sample_kernel(x_ref, g_ref, w_ref, proj_ref, kv0_ref, kv1_ref, kv2_ref, wt_ref, h_s):
    j = pl.program_id(1)

    @pl.when(j == 0)
    def _():
        h_s[...] = _rms_norm(x_ref[...], g_ref[...]).astype(_BF16)

    w = w_ref[...].astype(_BF16)
    wt_ref[...] = w
    acc = jnp.dot(h_s[...], w, preferred_element_type=_F32)
    proj_ref[...] = acc.astype(_BF16)
    for g, ref in enumerate((kv0_ref, kv1_ref, kv2_ref)):
        @pl.when((j == Q_TILE[g] + 1) | (j == Q_TILE[g] + 2))
        def _(ref=ref):
            ref[...] = acc


def _inproj_sample(x2d, g, w, layer):
    m = x2d.shape[0]

    def kv_spec(grp):
        return pl.BlockSpec((m, COL_TILE), lambda i, j: (i, jnp.clip(j - Q_TILE[grp] - 1, 0, 1)))

    return pl.pallas_call(
        _inproj_sample_kernel,
        grid=(1, IN_TILES),
        in_specs=[pl.BlockSpec((m, D_MODEL), lambda i, j: (i, 0)),
                  pl.BlockSpec((1, D_MODEL), lambda i, j: (0, 0)),
                  pl.BlockSpec((None, D_MODEL, COL_TILE), lambda i, j: (layer, 0, j))],
        out_specs=[pl.BlockSpec((m, COL_TILE), lambda i, j: (i, j)), kv_spec(0), kv_spec(1), kv_spec(2),
                   pl.BlockSpec((None, D_MODEL, COL_TILE), lambda i, j: (j, 0, 0))],
        out_shape=[jax.ShapeDtypeStruct((m, IN_COLS), _BF16)] +
                  [jax.ShapeDtypeStruct((m, 2 * ATT_WIDTH), _F32)] * 3 +
                  [jax.ShapeDtypeStruct((IN_TILES, D_MODEL, COL_TILE), _BF16)],
        scratch_shapes=[pltpu.VMEM((m, D_MODEL), _BF16)],
        compiler_params=pltpu.CompilerParams(dimension_semantics=("parallel", "arbitrary"),
                                             vmem_limit_bytes=VMEM_LIMIT),
        name="inproj_sample",
    )(x2d, g.reshape(1, D_MODEL), w)


def _inproj_prompt_kernel(*refs, tm, aliased):
    n_in = 6 if aliased else 3
    x_ref, g_ref, w_ref = refs[:3]
    nat_ref, g1_ref, g2_ref, kv0_ref, kv1_ref, kv2_ref, h_s, de_s = refs[n_in:]
    kv_refs = (kv0_ref, kv1_ref, kv2_ref)
    res_refs = (None, g1_ref, g2_ref)
    j = pl.program_id(1)

    @pl.when(j == 0)
    def _():
        h_s[...] = _rms_norm(x_ref[...], g_ref[...]).astype(_BF16)

    def column_tiles(tiles):
        for c in range(tm // SUB_ROWS):
            r0 = c * SUB_ROWS
            for half, tile in enumerate(tiles):
                acc = jnp.dot(h_s[r0:r0 + SUB_ROWS, :], w_ref[half], preferred_element_type=_F32)
                grp = (tile - Q_TILE[0]) // 3 if Q_TILE[0] <= tile < U_TILE0 else None
                part = None if grp is None or tile == Q_TILE[grp] else tile - Q_TILE[grp] - 1
                dil = 1 if grp is None else DIL_GROUPS[grp][1]
                if dil == 1:
                    nat_ref[r0:r0 + SUB_ROWS, half * COL_TILE:(half + 1) * COL_TILE] = acc.astype(_BF16)
                else:
                    for h in range(HEADS):
                        de_s[c, half, h] = acc[:, _head_cols(h)]
                    n = SUB_ROWS // dil
                    for res in range(dil):
                        for h in range(HEADS):
                            res_refs[grp][res, c * n:(c + 1) * n, _head_cols(h, (tile - Q_TILE[grp]) * COL_TILE)] = (
                                de_s[c, half, h, pl.ds(res, n, stride=dil), :].astype(_BF16))
                if part is not None:
                    keep = min(DIL_GROUPS[grp][0], tm)
                    lo = max(r0, tm - keep)
                    n = r0 + SUB_ROWS - lo
                    if n > 0:
                        for h in range(HEADS):
                            dst = pl.ds((lo - (tm - keep)) * KV_ROWS + part * HEADS + h, n, stride=KV_ROWS)
                            kv_refs[grp][dst, :] = acc[lo - r0:lo - r0 + n, _head_cols(h)]

    for step in range(IN_TILES // IN_STEP_TILES):
        @pl.when(j == step)
        def _(step=step):
            column_tiles(tuple(range(step * IN_STEP_TILES, (step + 1) * IN_STEP_TILES)))


def _inproj_prompt(x2d, g, w, batch, seq, tm, layer, prev_kv):
    m = x2d.shape[0]
    tps = seq // tm
    aliased = prev_kv is not None
    (w0, _), (w1, d1), (w2, d2) = DIL_GROUPS
    assert w2 == seq and w0 <= tm and w1 <= tm and tm % SUB_ROWS == 0

    in_specs = [pl.BlockSpec((tm, D_MODEL), lambda i, j: (i, 0)),
                pl.BlockSpec((1, D_MODEL), lambda i, j: (0, 0)),
                pl.BlockSpec((IN_STEP_TILES, D_MODEL, COL_TILE), lambda i, j: (j, 0, 0))]
    args = [x2d, g.reshape(1, D_MODEL), w]
    if aliased:
        in_specs += [pl.BlockSpec(memory_space=pl.ANY)] * 3
        args += list(prev_kv)

    def res_spec(dil):
        return pl.BlockSpec((None, dil, tm // dil, 3 * ATT_WIDTH), lambda i, j: (i // tps, 0, i % tps, 0))

    assert Q_TILE[1] % IN_STEP_TILES == 0 and U_TILE0 % IN_STEP_TILES == 0 and IN_TILES - U_TILE0 == IN_STEP_TILES
    nat_lo, u_step = Q_TILE[1] // IN_STEP_TILES, U_TILE0 // IN_STEP_TILES
    out_specs = [
        pl.BlockSpec((tm, IN_STEP_TILES * COL_TILE),
                     lambda i, j: (i, jnp.clip(j, 0, nat_lo - 1) + jnp.clip(j - u_step + 1, 0, 1))),
        res_spec(d1), res_spec(d2),
        pl.BlockSpec((None, None, w0 * KV_ROWS, HEAD_DIM), lambda i, j: (layer, i // tps, 0, 0),
                     pipeline_mode=pl.Buffered(1)),
        pl.BlockSpec((None, None, w1 * KV_ROWS, HEAD_DIM), lambda i, j: (layer, i // tps, 0, 0),
                     pipeline_mode=pl.Buffered(1)),
        pl.BlockSpec((None, None, tm * KV_ROWS, HEAD_DIM), lambda i, j: (layer, i // tps, i % tps, 0)),
    ]
    out_shape = [
        jax.ShapeDtypeStruct((m, NAT_TILES * COL_TILE), _BF16),
        jax.ShapeDtypeStruct((batch, d1, seq // d1, 3 * ATT_WIDTH), _BF16),
        jax.ShapeDtypeStruct((batch, d2, seq // d2, 3 * ATT_WIDTH), _BF16),
    ] + [jax.ShapeDtypeStruct((DEPTH, batch, win * KV_ROWS, HEAD_DIM), _F32) for win, _ in DIL_GROUPS]
    return pl.pallas_call(
        functools.partial(_inproj_prompt_kernel, tm=tm, aliased=aliased),
        grid=(m // tm, IN_TILES // IN_STEP_TILES),
        in_specs=in_specs, out_specs=out_specs, out_shape=out_shape,
        scratch_shapes=[pltpu.VMEM((tm, D_MODEL), _BF16),
                        pltpu.VMEM((tm // SUB_ROWS, IN_STEP_TILES, HEADS, SUB_ROWS, HEAD_DIM), _F32)],
        input_output_aliases={3: 3, 4: 4, 5: 5} if aliased else {},
        compiler_params=pltpu.CompilerParams(dimension_semantics=("arbitrary", "arbitrary"),
                                             vmem_limit_bytes=VMEM_LIMIT_HIGH),
        name="inproj_prompt",
    )(*args)


QB = 128
G0_UNROLL = 3
G2_RES_PER_STEP = 4


def _attn_prompt_kernel(q0, k0, v0, q1, k1, v1, q2, k2, v2, o_ref, acc_s, m_s, l_s, shuf_s, *, seq):
    step = pl.program_id(1)
    n1 = DIL_GROUPS[1][1]
    n2_steps = DIL_GROUPS[2][1] // G2_RES_PER_STEP

    row0 = lax.broadcasted_iota(jnp.int32, (QB, QB), 0)
    col0 = lax.broadcasted_iota(jnp.int32, (QB, QB), 1)
    dist_first = jnp.where(col0 <= row0, (row0 - col0).astype(_F32), jnp.inf)
    row1 = lax.broadcasted_iota(jnp.int32, (QB, 2 * QB), 0)
    col1 = lax.broadcasted_iota(jnp.int32, (QB, 2 * QB), 1)
    d1 = QB + row1 - col1
    dist_next = jnp.where((d1 >= 0) & (d1 <= QB), d1.astype(_F32), jnp.inf)
    ones = jnp.ones((2 * QB, HEAD_DIM), _BF16)

    seg = seq // n1
    chunk = QB // n1
    assert n1 == 4 and DIL_GROUPS[2][1] == 16

    def blocks(specs):
        tiles = []
        for slot, (qr, kr, vr, grp, res, i, is_first_block) in enumerate(specs):
            dil = DIL_GROUPS[grp][1]
            for h in range(HEADS):
                hs = _head_cols(h)
                slope_step = SLOPES[grp * HEADS + h] * dil
                if is_first_block:
                    q, k, v = qr[0:QB, hs], kr[0:QB, hs], vr[0:QB, hs]
                    bias, one = dist_first * slope_step, ones[0:QB]
                else:
                    qs = pl.ds(pl.multiple_of(i * QB, QB), QB)
                    ks = pl.ds(pl.multiple_of(i * QB - QB, QB), 2 * QB)
                    q, k, v = qr[qs, hs], kr[ks, hs], vr[ks, hs]
                    bias, one = dist_next * slope_step, ones
                tiles.append((q, k, jnp.concatenate([v, one], axis=1), bias, grp, res, i, is_first_block, slot, h))
        scores = [lax.dot_general(t[0], t[1], (((1,), (1,)), ((), ())), preferred_element_type=_F32) for t in tiles]
        probs, maxes = [], []
        for t, s in zip(tiles, scores):
            s = s * ATT_SCALE - t[3]
            m = jnp.max(s, axis=1, keepdims=True)
            probs.append(jnp.exp(s - m).astype(_BF16))
            maxes.append(jnp.broadcast_to(m, (QB, HEAD_DIM)))
        outs = [jnp.dot(p, t[2], preferred_element_type=_F32) for t, p in zip(tiles, probs)]
        def merge(h, rows, acc, m, l):
            m_old, l_old, a_old = m_s[h, rows, :], l_s[h, rows, :], acc_s[h, rows, :]
            m_new = jnp.maximum(m_old, m)
            w_old = jnp.exp(m_old - m_new)
            w_new = jnp.exp(m - m_new)
            acc_s[h, rows, :] = a_old * w_old + acc * w_new
            l_s[h, rows, :] = l_old * w_old + l * w_new
            m_s[h, rows, :] = m_new

        for (_, _, _, _, grp, res, i, is_first_block, slot, h), m, acc_l in zip(tiles, maxes, outs):
            acc, l = acc_l[:, :HEAD_DIM], acc_l[:, HEAD_DIM:]
            if grp == 1:
                rows = pl.ds(pl.multiple_of(res * seg + i * QB, QB), QB)
                acc_s[h, rows, :], m_s[h, rows, :], l_s[h, rows, :] = acc, m, l
            elif grp == 2:
                merge(h, pl.ds((res & (n1 - 1)) * seg + (res >> 2), QB, stride=n1), acc, m, l)
            else:
                tmps = [shuf_s.at[(slot * HEADS + h) * 3 + a] for a in range(3)]
                for tmp, val in zip(tmps, (acc, m, l)):
                    tmp[...] = val
                for r4 in range(n1):
                    dst = r4 * seg + i * chunk
                    dst = pl.ds(dst if is_first_block else pl.multiple_of(dst, chunk), chunk)
                    merge(h, dst, *(tmp[pl.ds(r4, chunk, stride=n1), :] for tmp in tmps))

    @pl.when(step < n1)
    def _():
        blocks([(q1, k1, v1, 1, step, i, i == 0) for i in range(seg // QB)])

    @pl.when((step >= n1) & (step < n1 + n2_steps))
    def _():
        assert seq // DIL_GROUPS[2][1] == QB
        blocks([(q2.at[r], k2.at[r], v2.at[r], 2, (step - n1) * G2_RES_PER_STEP + r, 0, True)
                for r in range(G2_RES_PER_STEP)])

    @pl.when(step == n1 + n2_steps)
    def _():
        n_blk = seq // QB
        assert (n_blk - 1) % G0_UNROLL == 0
        blocks([(q0, k0, v0, 0, 0, 0, True)])

        def g0_body(it, carry):
            blocks([(q0, k0, v0, 0, 0, 1 + it * G0_UNROLL + u, False) for u in range(G0_UNROLL)])
            return carry
        lax.fori_loop(0, (n_blk - 1) // G0_UNROLL, g0_body, 0)

        def body(c, carry):
            for h in range(HEADS):
                tmp = shuf_s.at[h]
                for r4 in range(n1):
                    src = pl.ds(pl.multiple_of(r4 * seg + c * chunk, chunk), chunk)
                    tmp[pl.ds(r4, chunk, stride=n1), :] = acc_s[h, src, :] / l_s[h, src, :]
                o_ref[pl.ds(pl.multiple_of(c * QB, QB), QB), _head_cols(h)] = tmp[...].astype(_BF16)
            return carry
        lax.fori_loop(0, seq // QB, body, 0)


def _attn_prompt(nat, qkv1, qkv2, batch, seq):
    n1, n2 = DIL_GROUPS[1][1], DIL_GROUPS[2][1]
    n2_steps = n2 // G2_RES_PER_STEP
    nat3 = nat.reshape(batch, seq, NAT_TILES * COL_TILE)
    in_specs = [pl.BlockSpec((None, seq, COL_TILE), lambda b, s, t=t: (b, 0, Q_TILE[0] + t)) for t in range(3)]
    in_specs += [pl.BlockSpec((None, None, seq // n1, COL_TILE),
                              lambda b, s, t=t: (b, jnp.clip(s, 0, n1 - 1), 0, t)) for t in range(3)]
    in_specs += [pl.BlockSpec((None, G2_RES_PER_STEP, seq // n2, COL_TILE),
                              lambda b, s, t=t: (b, jnp.clip(s - n1, 0, n2_steps - 1), 0, t)) for t in range(3)]
    args = [nat3] * 3 + [qkv1] * 3 + [qkv2] * 3
    return pl.pallas_call(
        functools.partial(_attn_prompt_kernel, seq=seq),
        grid=(batch, 1 + n1 + n2_steps),
        in_specs=in_specs,
        out_specs=pl.BlockSpec((None, seq, ATT_WIDTH), lambda b, s: (b, 0, 0)),
        out_shape=jax.ShapeDtypeStruct((batch, seq, ATT_WIDTH), _BF16),
        scratch_shapes=[pltpu.VMEM((HEADS, seq, HEAD_DIM), _F32)] * 3
                       + [pltpu.VMEM((G0_UNROLL * HEADS * 3, QB, HEAD_DIM), _F32)],
        compiler_params=pltpu.CompilerParams(dimension_semantics=("parallel", "arbitrary"),
                                             vmem_limit_bytes=VMEM_LIMIT),
        name="attn_prompt",
    )(*args)


def _attn_sample_kernel(*refs, tdec, aliased):
    n_in = 10 if aliased else 7
    proj_ref, n0, n1, n2, c0, c1, c2 = refs[:7]
    yb_ref, o0, o1, o2 = refs[n_in:]
    news, caches, outs = (n0, n1, n2), (c0, c1, c2), (o0, o1, o2)
    for h in range(HEADS):
        pieces = []
        for grp, (win, dil) in enumerate(DIL_GROUPS):
            slope = SLOPES[grp * HEADS + h]
            q = proj_ref[:, _head_cols(h, Q_TILE[grp] * COL_TILE)]
            k_new = news[grp][:, _head_cols(h)]
            v_new = news[grp][:, _head_cols(h, ATT_WIDTH)]
            k_old = caches[grp][pl.ds(h, win, stride=KV_ROWS), :]
            v_old = caches[grp][pl.ds(HEADS + h, win, stride=KV_ROWS), :]
            for k, v, n_keys, base in ((k_old, v_old, win, win), (k_new, v_new, tdec, 0)):
                s = lax.dot_general(q, k.astype(_BF16), (((1,), (1,)), ((), ())), preferred_element_type=_F32)
                t = lax.broadcasted_iota(jnp.int32, (tdec, n_keys), 0)
                c = lax.broadcasted_iota(jnp.int32, (tdec, n_keys), 1)
                dist = base + t - c
                ok = (dist >= 0) & (dist <= win) & ((dist & (dil - 1)) == 0)
                s = jnp.where(ok, s * ATT_SCALE - slope * dist.astype(_F32), -jnp.inf)
                pieces.append((s, v.astype(_BF16)))
        m = functools.reduce(jnp.maximum, [jnp.max(s, axis=1, keepdims=True) for s, _ in pieces])
        l = jnp.zeros((tdec, 1), _F32)
        acc = jnp.zeros((tdec, HEAD_DIM), _F32)
        for s, v in pieces:
            p = jnp.exp(s - m)
            l = l + jnp.sum(p, axis=1, keepdims=True)
            acc = acc + jnp.dot(p.astype(_BF16), v, preferred_element_type=_F32)
        yb_ref[:, _head_cols(h)] = (acc / l).astype(_BF16)
    for grp, (win, _) in enumerate(DIL_GROUPS):
        kept = (win - tdec) * KV_ROWS
        outs[grp][0:kept, :] = caches[grp][tdec * KV_ROWS:win * KV_ROWS, :]
        for part in range(2):
            for h in range(HEADS):
                outs[grp][pl.ds(kept + part * HEADS + h, tdec, stride=KV_ROWS), :] = (
                    news[grp][:, _head_cols(h, part * ATT_WIDTH)])


def _attn_sample(proj, kv_new, caches, batch, tdec, layer, prev_out):
    aliased = prev_out is not None
    in_specs = [pl.BlockSpec((None, tdec, IN_COLS), lambda b: (b, 0, 0))]
    in_specs += [pl.BlockSpec((None, tdec, 2 * ATT_WIDTH), lambda b: (b, 0, 0))] * 3
    cache_specs = [pl.BlockSpec((None, None, win * KV_ROWS, HEAD_DIM), lambda b: (layer, b, 0, 0)) for win, _ in DIL_GROUPS]
    in_specs += cache_specs
    args = [proj.reshape(batch, tdec, IN_COLS)] + [k.reshape(batch, tdec, 2 * ATT_WIDTH) for k in kv_new] + list(caches)
    if aliased:
        in_specs += [pl.BlockSpec(memory_space=pl.ANY)] * 3
        args += list(prev_out)
    out_specs = [pl.BlockSpec((None, tdec, ATT_WIDTH), lambda b: (b, 0, 0))] + cache_specs
    out_shape = [jax.ShapeDtypeStruct((batch, tdec, ATT_WIDTH), _BF16)]
    out_shape += [jax.ShapeDtypeStruct((DEPTH, batch, win * KV_ROWS, HEAD_DIM), _F32) for win, _ in DIL_GROUPS]
    return pl.pallas_call(
        functools.partial(_attn_sample_kernel, tdec=tdec, aliased=aliased),
        grid=(batch,),
        in_specs=in_specs, out_specs=out_specs, out_shape=out_shape,
        input_output_aliases={7: 1, 8: 2, 9: 3} if aliased else {},
        compiler_params=pltpu.CompilerParams(dimension_semantics=("arbitrary",), vmem_limit_bytes=VMEM_LIMIT),
        name="attn_sample",
    )(*args)


def _mixer_kernel(*refs, nb, tt, start, from_zero):
    if from_zero:
        pa_ref, pu_ref, yb_ref, x_ref, cw_ref, pw_ref, psc_ref, wo_ref = refs[:8]
    else:
        pa_ref, pu_ref, yb_ref, x_ref, cs0_ref, ps0_ref, cw_ref, pw_ref, psc_ref, wo_ref = refs[:10]
    out_ref, cs_out, ps_out, ext_a, ext_u, yc = refs[-6:]
    ti = pl.program_id(1)
    rows = nb * tt
    pool_halo = POOL_BLOCK if from_zero else POOL_HALO

    @pl.when(ti == 0)
    def _():
        if from_zero:
            ext_a[:, 0:CONV_HALO, :] = jnp.zeros((nb, CONV_HALO, CONV_WIDTH), _F32)
            ext_u[:, 0:pool_halo, :] = jnp.zeros((nb, pool_halo, POOL_WIDTH), _BF16)
        else:
            ext_a[:, 0:CONV_HALO, :] = cs0_ref[...]
            ext_u[:, 0:pool_halo, :] = ps0_ref[...]

    yb = yb_ref[...].astype(_F32).reshape(rows, ATT_WIDTH).astype(_BF16)
    mixed = jnp.dot(yb, wo_ref[CONV_WIDTH:CONV_WIDTH + ATT_WIDTH, :], preferred_element_type=_F32)

    pa = pa_ref[...].astype(_F32)
    xa, gate_b, gate_c = (pa[:, :, k * CONV_WIDTH:(k + 1) * CONV_WIDTH] for k in range(3))
    prod = gate_c * xa
    ext_a[:, CONV_HALO:CONV_HALO + tt, :] = prod
    cw = cw_ref[...]
    cu = (cw[0] * ext_a[:, CONV_HALO - 2:CONV_HALO - 2 + tt, :]
          + cw[1] * ext_a[:, CONV_HALO - 1:CONV_HALO - 1 + tt, :] + cw[2] * prod)
    ya = (gate_b * cu).reshape(rows, CONV_WIDTH).astype(_BF16)
    mixed = mixed + jnp.dot(ya, wo_ref[0:CONV_WIDTH, :], preferred_element_type=_F32)

    ext_u[:, pool_halo:pool_halo + tt, :] = pu_ref[...].astype(ext_u.dtype)
    if from_zero:
        pb = POOL_BLOCK
        row = lax.broadcasted_iota(jnp.int32, (pb, 2 * pb), 0)
        col = lax.broadcasted_iota(jnp.int32, (pb, 2 * pb), 1)
        lag = pb + row - col
        for gi, w in enumerate(POOL_WINDOWS):
            cols = slice(gi * POOL_GROUP, (gi + 1) * POOL_GROUP)
            band = ((lag >= 0) & (lag < w)).astype(_F32).astype(_BF16)
            for r in range(tt // pb):
                tot = jnp.dot(band, ext_u[0, r * pb:(r + 2) * pb, cols], preferred_element_type=_F32)
                pos = start + ti * tt + r * pb + lax.broadcasted_iota(jnp.int32, (pb, 1), 0)
                cnt = jnp.minimum(w, pos + 1).astype(_F32)
                ug = pu_ref[0, r * pb:(r + 1) * pb, cols].astype(_F32)
                diff = (tot / cnt - ug).astype(_BF16)
                z = jnp.dot(diff, pw_ref[gi], preferred_element_type=_F32) * psc_ref[:, cols]
                yc[r * pb:(r + 1) * pb, cols] = z.astype(_BF16)
    else:
        u = pu_ref[...].astype(_F32)
        pos = start + ti * tt + lax.broadcasted_iota(jnp.int32, (1, tt, 1), 1)
        for gi, w in enumerate(POOL_WINDOWS):
            cols = slice(gi * POOL_GROUP, (gi + 1) * POOL_GROUP)
            ug = u[:, :, cols]
            tot = ug
            for k in range(1, w):
                tot = tot + ext_u[:, pool_halo - k:pool_halo - k + tt, cols]
            cnt = jnp.minimum(w, pos + 1).astype(_F32)
            diff = (tot / cnt - ug).reshape(rows, POOL_GROUP).astype(_BF16)
            z = jnp.dot(diff, pw_ref[gi], preferred_element_type=_F32) * psc_ref[:, cols]
            yc[:, cols] = z.astype(_BF16)

    mixed = mixed + jnp.dot(yc[...], wo_ref[CONV_WIDTH + ATT_WIDTH:, :], preferred_element_type=_F32)
    out_ref[...] = x_ref[...] + mixed.reshape(nb, tt, D_MODEL)

    tail_a = ext_a[:, tt:tt + CONV_HALO, :]
    ext_a[:, 0:CONV_HALO, :] = tail_a
    cs_out[...] = tail_a
    ps_out[...] = ext_u[:, pool_halo + tt - POOL_HALO:pool_halo + tt, :].astype(_F32)
    ext_u[:, 0:pool_halo, :] = ext_u[:, tt:tt + pool_halo, :]


def _mixer(proj, yb, x, conv_state, pool_state, conv_w, pool_w, pool_scale, w_out, layer, nb, tt, start, u_tile0):
    batch, seq, _ = x.shape
    proj3 = proj.reshape(batch, seq, proj.shape[-1])
    u_blk = u_tile0 * COL_TILE // POOL_WIDTH
    const2 = lambda b, t: (0, 0)
    from_zero = conv_state is None and pool_state is None
    assert from_zero or (conv_state is not None and pool_state is not None)
    assert not from_zero or (nb == 1 and start == 0 and tt % POOL_BLOCK == 0)
    state_specs = [] if from_zero else [pl.BlockSpec((nb, CONV_HALO, CONV_WIDTH), lambda b, t: (b, 0, 0)),
                                        pl.BlockSpec((nb, POOL_HALO, POOL_WIDTH), lambda b, t: (b, 0, 0))]
    state_args = [] if from_zero else [conv_state, pool_state]
    ext_u = (pltpu.VMEM((nb, tt + POOL_BLOCK, POOL_WIDTH), _BF16) if from_zero
             else pltpu.VMEM((nb, tt + POOL_HALO, POOL_WIDTH), _F32))
    return pl.pallas_call(
        functools.partial(_mixer_kernel, nb=nb, tt=tt, start=start, from_zero=from_zero),
        grid=(batch // nb, seq // tt),
        in_specs=[pl.BlockSpec((nb, tt, 3 * CONV_WIDTH), lambda b, t: (b, t, 0)),
                  pl.BlockSpec((nb, tt, POOL_WIDTH), lambda b, t: (b, t, u_blk)),
                  pl.BlockSpec((nb, tt, ATT_WIDTH), lambda b, t: (b, t, 0)),
                  pl.BlockSpec((nb, tt, D_MODEL), lambda b, t: (b, t, 0))] + state_specs + [
                  pl.BlockSpec((3, CONV_WIDTH), const2),
                  pl.BlockSpec((None, len(POOL_WINDOWS), POOL_GROUP, POOL_GROUP), lambda b, t: (layer, 0, 0, 0)),
                  pl.BlockSpec((1, POOL_WIDTH), const2),
                  pl.BlockSpec((None, D_MODEL, D_MODEL), lambda b, t: (layer, 0, 0))],
        out_specs=[pl.BlockSpec((nb, tt, D_MODEL), lambda b, t: (b, t, 0)),
                   pl.BlockSpec((nb, CONV_HALO, CONV_WIDTH), lambda b, t: (b, 0, 0)),
                   pl.BlockSpec((nb, POOL_HALO, POOL_WIDTH), lambda b, t: (b, 0, 0))],
        out_shape=[jax.ShapeDtypeStruct((batch, seq, D_MODEL), _F32),
                   jax.ShapeDtypeStruct((batch, CONV_HALO, CONV_WIDTH), _F32),
                   jax.ShapeDtypeStruct((batch, POOL_HALO, POOL_WIDTH), _F32)],
        scratch_shapes=[pltpu.VMEM((nb, tt + CONV_HALO, CONV_WIDTH), _F32), ext_u,
                        pltpu.VMEM((nb * tt, POOL_WIDTH), _BF16)],
        compiler_params=pltpu.CompilerParams(dimension_semantics=("parallel", "arbitrary"),
                                             vmem_limit_bytes=VMEM_LIMIT),
        name="mixer",
    )(proj3, proj3, yb, x, *state_args, conv_w, pool_w, pool_scale.reshape(1, POOL_WIDTH), w_out)


def _ffn_sample_kernel(x_ref, n2_ref, wg_ref, wu_ref, cw_ref, wd_ref, st0_ref, fn_ref,
                       out_ref, st_out, wgt_ref, wut_ref, wdt_ref, h_s, ext_s, *, nb, tt, final):
    f = pl.program_id(0)
    rows = nb * tt

    @pl.when(f == 0)
    def _():
        x = x_ref[...]
        h_s[...] = _rms_norm(x, n2_ref[...]).reshape(rows, D_MODEL).astype(_BF16)
        out_ref[...] = x

    def tile(width):
        cols = slice(0, width)
        wg, wu, wd = wg_ref[:, cols].astype(_BF16), wu_ref[:, cols].astype(_BF16), wd_ref[cols, :].astype(_BF16)
        wgt_ref[:, cols], wut_ref[:, cols], wdt_ref[cols, :] = wg, wu, wd
        if width < FF_TILE:
            wgt_ref[:, width:] = jnp.zeros((D_MODEL, FF_TILE - width), _BF16)
            wut_ref[:, width:] = jnp.zeros((D_MODEL, FF_TILE - width), _BF16)
            wdt_ref[width:, :] = jnp.zeros((FF_TILE - width, D_MODEL), _BF16)
        h = h_s[...]
        gate = jnp.dot(h, wg, preferred_element_type=_F32).reshape(nb, tt, width)
        up = jnp.dot(h, wu, preferred_element_type=_F32).reshape(nb, tt, width)
        ext_s[:, 0:CONV_HALO, cols] = st0_ref[:, :, cols]
        ext_s[:, CONV_HALO:CONV_HALO + tt, cols] = gate
        cw = cw_ref[:, cols]
        gc = (cw[0] * ext_s[:, CONV_HALO - 2:CONV_HALO - 2 + tt, cols]
              + cw[1] * ext_s[:, CONV_HALO - 1:CONV_HALO - 1 + tt, cols] + cw[2] * gate)
        st_out[:, :, cols] = ext_s[:, tt:tt + CONV_HALO, cols]
        act = (jax.nn.silu(gc) * up).reshape(rows, width).astype(_BF16)
        out_ref[...] += jnp.dot(act, wd, preferred_element_type=_F32).reshape(nb, tt, D_MODEL)

    @pl.when(f < FF_TILES - 1)
    def _():
        tile(FF_TILE)

    @pl.when(f == FF_TILES - 1)
    def _():
        tile(FF_LAST)
        if final:
            out_ref[...] = _rms_norm(out_ref[...], fn_ref[...])


def _ffn_prompt_kernel(x_ref, n2_ref, wg_ref, wu_ref, cw_ref, wd_ref, fn_ref,
                       out_ref, st_out, h_s, ext_s, carry_s, *, tt, final):
    ti = pl.program_id(1)
    f = pl.program_id(2)

    def tile(first, last):
        if first:
            @pl.when(ti == 0)
            def _():
                carry_s[...] = jnp.zeros(carry_s.shape, _F32)
            x = x_ref[...]
            h_s[...] = _rms_norm(x, n2_ref[...]).astype(_BF16)
            out_ref[...] = x
        ext_s[0:CONV_HALO, :] = carry_s[f]

        wg, wu, wd, cw = wg_ref[...], wu_ref[...], wd_ref[...], cw_ref[f]
        chunks = [slice(r0, r0 + FFN_SUB_ROWS) for r0 in range(0, tt, FFN_SUB_ROWS)]
        for rs in chunks:
            gate = jnp.dot(h_s[rs, :], wg, preferred_element_type=_F32)
            ext_s[CONV_HALO + rs.start:CONV_HALO + rs.stop, :] = gate
        ups = [jnp.dot(h_s[rs, :], wu, preferred_element_type=_F32) for rs in chunks]
        for rs, up in zip(chunks, ups):
            gc = (cw[0] * ext_s[CONV_HALO - 2 + rs.start:CONV_HALO - 2 + rs.stop, :]
                  + cw[1] * ext_s[CONV_HALO - 1 + rs.start:CONV_HALO - 1 + rs.stop, :]
                  + cw[2] * ext_s[CONV_HALO + rs.start:CONV_HALO + rs.stop, :])
            act = (jax.nn.silu(gc) * up).astype(_BF16)
            out_ref[rs, :] += jnp.dot(act, wd, preferred_element_type=_F32)
        tail = ext_s[tt:tt + CONV_HALO, :]
        carry_s[f] = tail
        st_out[f] = tail
        if last:
            out_ref[...] = _rms_norm(out_ref[...], fn_ref[...])

    @pl.when(f == 0)
    def _():
        tile(True, False)

    if final:
        @pl.when((f > 0) & (f < FF_TILES - 1))
        def _():
            tile(False, False)

        @pl.when(f == FF_TILES - 1)
        def _():
            tile(False, True)
    else:
        @pl.when(f > 0)
        def _():
            tile(False, False)


def _ffn_sample(x, norm2, w_gate, w_up, conv_w, w_down, state, final_norm, layer, final):
    nb, tt, _ = x.shape
    vec = lambda f: (0, 0)
    return pl.pallas_call(
        functools.partial(_ffn_sample_kernel, nb=nb, tt=tt, final=final),
        grid=(FF_TILES,),
        in_specs=[pl.BlockSpec((nb, tt, D_MODEL), lambda f: (0, 0, 0)),
                  pl.BlockSpec((1, D_MODEL), vec),
                  pl.BlockSpec((None, D_MODEL, FF_TILE), lambda f: (layer, 0, f)),
                  pl.BlockSpec((None, D_MODEL, FF_TILE), lambda f: (layer, 0, f)),
                  pl.BlockSpec((3, FF_TILE), lambda f: (0, f)),
                  pl.BlockSpec((None, FF_TILE, D_MODEL), lambda f: (layer, f, 0)),
                  pl.BlockSpec((nb, CONV_HALO, FF_TILE), lambda f: (0, 0, f)),
                  pl.BlockSpec((1, D_MODEL), vec)],
        out_specs=[pl.BlockSpec((nb, tt, D_MODEL), lambda f: (0, 0, 0)),
                   pl.BlockSpec((nb, CONV_HALO, FF_TILE), lambda f: (0, 0, f)),
                   pl.BlockSpec((None, D_MODEL, FF_TILE), lambda f: (f, 0, 0)),
                   pl.BlockSpec((None, D_MODEL, FF_TILE), lambda f: (f, 0, 0)),
                   pl.BlockSpec((None, FF_TILE, D_MODEL), lambda f: (f, 0, 0))],
        out_shape=[jax.ShapeDtypeStruct((nb, tt, D_MODEL), _F32),
                   jax.ShapeDtypeStruct((nb, CONV_HALO, D_FF), _F32),
                   jax.ShapeDtypeStruct((FF_TILES, D_MODEL, FF_TILE), _BF16),
                   jax.ShapeDtypeStruct((FF_TILES, D_MODEL, FF_TILE), _BF16),
                   jax.ShapeDtypeStruct((FF_TILES, FF_TILE, D_MODEL), _BF16)],
        scratch_shapes=[pltpu.VMEM((nb * tt, D_MODEL), _BF16),
                        pltpu.VMEM((nb, tt + CONV_HALO, FF_TILE), _F32)],
        compiler_params=pltpu.CompilerParams(dimension_semantics=("arbitrary",), vmem_limit_bytes=VMEM_LIMIT),
        name="ffn_sample",
    )(x, norm2.reshape(1, D_MODEL), w_gate, w_up, conv_w, w_down, state, final_norm.reshape(1, D_MODEL))


def _ffn_prompt(x, norm2, w_gate_t, w_up_t, conv_w, w_down_t, final_norm, tt, final):
    batch, seq, _ = x.shape
    vec = lambda b, t, f: (0, 0)
    return pl.pallas_call(
        functools.partial(_ffn_prompt_kernel, tt=tt, final=final),
        grid=(batch, seq // tt, FF_TILES),
        in_specs=[pl.BlockSpec((None, tt, D_MODEL), lambda b, t, f: (b, t, 0)),
                  pl.BlockSpec((1, D_MODEL), vec),
                  pl.BlockSpec((None, D_MODEL, FF_TILE), lambda b, t, f: (f, 0, 0)),
                  pl.BlockSpec((None, D_MODEL, FF_TILE), lambda b, t, f: (f, 0, 0)),
                  pl.BlockSpec((FF_TILES, 3, FF_TILE), lambda b, t, f: (0, 0, 0)),
                  pl.BlockSpec((None, FF_TILE, D_MODEL), lambda b, t, f: (f, 0, 0)),
                  pl.BlockSpec((1, D_MODEL), vec)],
        out_specs=[pl.BlockSpec((None, tt, D_MODEL), lambda b, t, f: (b, t, 0)),
                   pl.BlockSpec((None, None, FF_TILES, CONV_HALO, FF_TILE), lambda b, t, f: (b, t, 0, 0, 0))],
        out_shape=[jax.ShapeDtypeStruct((batch, seq, D_MODEL), _F32),
                   jax.ShapeDtypeStruct((batch, seq // tt, FF_TILES, CONV_HALO, FF_TILE), _F32)],
        scratch_shapes=[pltpu.VMEM((tt, D_MODEL), _BF16),
                        pltpu.VMEM((tt + CONV_HALO, FF_TILE), _F32),
                        pltpu.VMEM((FF_TILES, CONV_HALO, FF_TILE), _F32)],
        compiler_params=pltpu.CompilerParams(dimension_semantics=("parallel", "arbitrary", "arbitrary"),
                                             vmem_limit_bytes=VMEM_LIMIT),
        name="ffn_prompt",
    )(x, norm2.reshape(1, D_MODEL), w_gate_t, w_up_t, conv_w, w_down_t, final_norm.reshape(1, D_MODEL))


def _pad_rows_front(a, rows):
    return jnp.pad(a, ((0, 0), (rows - a.shape[1], 0), (0, 0)))


def kernel(x_prompt, x_sample, cache_kv_w128, cache_kv_w512, cache_kv_w2048, state_conv_a, state_pool, state_ffn_conv, norm1, w_in, conv_a_w, w_out, pool_w, pool_scale, norm2, w_gate, w_up, ffn_conv_w, w_down, final_norm):
    bp, sp, _ = x_prompt.shape
    bs, ss, _ = x_sample.shape
    caches = tuple(c.reshape(DEPTH, bs, win * KV_ROWS, HEAD_DIM)
                   for c, (win, _) in zip((cache_kv_w128, cache_kv_w512, cache_kv_w2048), DIL_GROUPS))
    tm, tt = 1024, 512
    w_out_b, pool_w_b = w_out.astype(_BF16), pool_w.astype(_BF16)
    xp, xs = x_prompt, x_sample
    kv_p = kv_s = None
    small_p, small_s = [], []
    for l in range(DEPTH):
        final = l == DEPTH - 1

        proj, *kv_new, w_in_t = _inproj_sample(xs.reshape(bs * ss, D_MODEL), norm1[l], w_in, l)
        yb, *kv_s = _attn_sample(proj, kv_new, caches, bs, ss, l, kv_s)
        xs, conv_o, pool_o = _mixer(proj, yb, xs, _pad_rows_front(state_conv_a[l], CONV_HALO),
                                    _pad_rows_front(state_pool[l], POOL_HALO), conv_a_w[l], pool_w_b, pool_scale[l],
                                    w_out_b, l, bs, ss, PAST_LEN, U_TILE0)
        xs, ffn_o, w_gate_t, w_up_t, w_down_t = _ffn_sample(
            xs, norm2[l], w_gate, w_up, ffn_conv_w[l], w_down, _pad_rows_front(state_ffn_conv[l], CONV_HALO),
            final_norm, l, final)
        small_s.append((conv_o[:, CONV_HALO - 2:], pool_o[:, POOL_HALO - POOL_BUF:], ffn_o[:, CONV_HALO - 2:]))

        nat, qkv1, qkv2, *kv_p = _inproj_prompt(xp.reshape(bp * sp, D_MODEL), norm1[l], w_in_t, bp, sp, tm, l, kv_p)
        yb = _attn_prompt(nat, qkv1, qkv2, bp, sp)
        xp, conv_o, pool_o = _mixer(nat, yb, xp, None, None, conv_a_w[l], pool_w_b, pool_scale[l], w_out_b,
                                    l, 1, tt, 0, NAT_TILES - 2)
        conv_w_t = jnp.pad(ffn_conv_w[l], ((0, 0), (0, FF_TILES * FF_TILE - D_FF)))
        conv_w_t = conv_w_t.reshape(3, FF_TILES, FF_TILE).transpose(1, 0, 2)
        xp, ffn_o = _ffn_prompt(xp, norm2[l], w_gate_t, w_up_t, conv_w_t, w_down_t, final_norm, tm, final)
        ffn_tail = ffn_o[:, -1, :, CONV_HALO - 2:].transpose(0, 2, 1, 3).reshape(bp, 2, FF_TILES * FF_TILE)
        small_p.append((conv_o[:, CONV_HALO - 2:], pool_o[:, POOL_HALO - POOL_BUF:], ffn_tail[:, :, :D_FF]))

    out = [xp, xs]
    for g, (win, _) in enumerate(DIL_GROUPS):
        out += [kv_p[g].reshape(DEPTH, bp, win, 2, HEADS, HEAD_DIM), kv_s[g].reshape(DEPTH, bs, win, 2, HEADS, HEAD_DIM)]
    for i in range(3):
        out += [jnp.stack([s[i] for s in small_p]), jnp.stack([s[i] for s in small_s])]
    return tuple(out)
```

```python
import functools

import numpy as np
import jax
import jax.numpy as jnp
from jax import lax
from jax.experimental import pallas as pl
from jax.experimental.pallas import tpu as pltpu

D_MODEL = 2048
DEPTH = 2
PAST_LEN = 16384
CONV_WIDTH = 512
ATT_WIDTH = 512
HEADS = 4
HEAD_DIM = 128
DIL_GROUPS = ((128, 1), (512, 4), (2048, 16))
POOL_WIDTH = 1024
POOL_WINDOWS = (2, 4, 8, 16)
POOL_GROUP = 256
POOL_BUF = 15
IN_COLS = 7168
D_FF = 5504
RMS_EPS = 1e-6
ATT_SCALE = HEAD_DIM ** -0.5

COL_TILE = 512
IN_TILES = IN_COLS // COL_TILE
Q_TILE = (3, 6, 9)
U_TILE0 = 12
NAT_TILES = 8
KV_ROWS = 2 * HEADS
SUB_ROWS = 512
IN_STEP_TILES = 2
REGROUP_STRIDE = 4
FF_TILE = 512
FF_TILES = -(-D_FF // FF_TILE)
FF_LAST = D_FF - (FF_TILES - 1) * FF_TILE
FFN_SUB_ROWS = 256
CONV_HALO = 8
POOL_HALO = 16
POOL_BLOCK = 128
VMEM_LIMIT = 56 * 1024 * 1024
VMEM_LIMIT_HIGH = 61 * 1024 * 1024

_BF16 = jnp.bfloat16
_F32 = jnp.float32


def _alibi_slopes():
    h = np.arange(1, 3 * HEADS + 1, dtype=np.float32)
    return [float(v) for v in np.power(np.float32(2.0), -8.0 * h / (3 * HEADS))]


SLOPES = _alibi_slopes()


def _rms_norm(x, g):
    return (x * lax.rsqrt(jnp.mean(x * x, axis=-1, keepdims=True) + RMS_EPS)) * g


def _head_cols(h, base=0):
    return slice(base + h * HEAD_DIM, base + (h + 1) * HEAD_DIM)


def _inproj_sample_kernel(x_ref, g_ref, w_ref, proj_ref, kv0_ref, kv1_ref, kv2_ref, wt_ref, h_s):
    j = pl.program_id(1)

    @pl.when(j == 0)
    def _():
        h_s[...] = _rms_norm(x_ref[...], g_ref[...]).astype(_BF16)

    w = w_ref[...].astype(_BF16)
    wt_ref[...] = w
    acc = jnp.dot(h_s[...], w, preferred_element_type=_F32)
    proj_ref[...] = acc.astype(_BF16)
    for g, ref in enumerate((kv0_ref, kv1_ref, kv2_ref)):
        @pl.when((j == Q_TILE[g] + 1) | (j == Q_TILE[g] + 2))
        def _(ref=ref):
            ref[...] = acc


def _inproj_sample(x2d, g, w, layer):
    m = x2d.shape[0]

    def kv_spec(grp):
        return pl.BlockSpec((m, COL_TILE), lambda i, j: (i, jnp.clip(j - Q_TILE[grp] - 1, 0, 1)))

    return pl.pallas_call(
        _inproj_sample_kernel,
        grid=(1, IN_TILES),
        in_specs=[pl.BlockSpec((m, D_MODEL), lambda i, j: (i, 0)),
                  pl.BlockSpec((1, D_MODEL), lambda i, j: (0, 0)),
                  pl.BlockSpec((None, D_MODEL, COL_TILE), lambda i, j: (layer, 0, j))],
        out_specs=[pl.BlockSpec((m, COL_TILE), lambda i, j: (i, j)), kv_spec(0), kv_spec(1), kv_spec(2),
                   pl.BlockSpec((None, D_MODEL, COL_TILE), lambda i, j: (j, 0, 0))],
        out_shape=[jax.ShapeDtypeStruct((m, IN_COLS), _BF16)] +
                  [jax.ShapeDtypeStruct((m, 2 * ATT_WIDTH), _F32)] * 3 +
                  [jax.ShapeDtypeStruct((IN_TILES, D_MODEL, COL_TILE), _BF16)],
        scratch_shapes=[pltpu.VMEM((m, D_MODEL), _BF16)],
        compiler_params=pltpu.CompilerParams(dimension_semantics=("parallel", "arbitrary"),
                                             vmem_limit_bytes=VMEM_LIMIT),
        name="inproj_sample",
    )(x2d, g.reshape(1, D_MODEL), w)


def _inproj_prompt_kernel(*refs, tm, aliased):
    n_in = 6 if aliased else 3
    x_ref, g_ref, w_ref = refs[:3]
    nat_ref, g1_ref, g2_ref, kv0_ref, kv1_ref, kv2_ref, h_s, de_s, de2_s = refs[n_in:]
    kv_refs = (kv0_ref, kv1_ref, kv2_ref)
    res_refs = (None, g1_ref, g2_ref)
    j = pl.program_id(1)

    @pl.when(j == 0)
    def _():
        h_s[...] = _rms_norm(x_ref[...], g_ref[...]).astype(_BF16)

    def column_tiles(tiles):
        for c in range(tm // SUB_ROWS):
            r0 = c * SUB_ROWS
            for half, tile in enumerate(tiles):
                acc = jnp.dot(h_s[r0:r0 + SUB_ROWS, :], w_ref[half], preferred_element_type=_F32)
                grp = (tile - Q_TILE[0]) // 3 if Q_TILE[0] <= tile < U_TILE0 else None
                part = None if grp is None or tile == Q_TILE[grp] else tile - Q_TILE[grp] - 1
                dil = 1 if grp is None else DIL_GROUPS[grp][1]
                if dil == 1:
                    nat_ref[r0:r0 + SUB_ROWS, half * COL_TILE:(half + 1) * COL_TILE] = acc.astype(_BF16)
                else:
                    for h in range(HEADS):
                        de_s[c, half, h] = acc[:, _head_cols(h)]
                    n = SUB_ROWS // dil
                    col0 = (tile - Q_TILE[grp]) * COL_TILE
                    if dil == REGROUP_STRIDE:
                        for res in range(dil):
                            for h in range(HEADS):
                                res_refs[grp][res, c * n:(c + 1) * n, _head_cols(h, col0)] = (
                                    de_s[c, half, h, pl.ds(res, n, stride=dil), :].astype(_BF16))
                    else:
                        assert dil == REGROUP_STRIDE ** 2
                        q = SUB_ROWS // REGROUP_STRIDE
                        for h in range(HEADS):
                            for g in range(REGROUP_STRIDE):
                                de2_s[h, g * q:(g + 1) * q, :] = de_s[c, half, h, pl.ds(g, q, stride=REGROUP_STRIDE), :]
                        for g in range(REGROUP_STRIDE):
                            for r in range(REGROUP_STRIDE):
                                for h in range(HEADS):
                                    res_refs[grp][g + REGROUP_STRIDE * r, c * n:(c + 1) * n, _head_cols(h, col0)] = (
                                        de2_s[h, pl.ds(g * q + r, n, stride=REGROUP_STRIDE), :].astype(_BF16))
                if part is not None:
                    keep = min(DIL_GROUPS[grp][0], tm)
                    lo = max(r0, tm - keep)
                    n = r0 + SUB_ROWS - lo
                    if n > 0:
                        for h in range(HEADS):
                            dst = pl.ds((lo - (tm - keep)) * KV_ROWS + part * HEADS + h, n, stride=KV_ROWS)
                            kv_refs[grp][dst, :] = acc[lo - r0:lo - r0 + n, _head_cols(h)]

    for step in range(IN_TILES // IN_STEP_TILES):
        @pl.when(j == step)
        def _(step=step):
            column_tiles(tuple(range(step * IN_STEP_TILES, (step + 1) * IN_STEP_TILES)))


def _inproj_prompt(x2d, g, w, batch, seq, tm, layer, prev_kv):
    m = x2d.shape[0]
    tps = seq // tm
    aliased = prev_kv is not None
    (w0, _), (w1, d1), (w2, d2) = DIL_GROUPS
    assert w2 == seq and w0 <= tm and w1 <= tm and tm % SUB_ROWS == 0

    in_specs = [pl.BlockSpec((tm, D_MODEL), lambda i, j: (i, 0)),
                pl.BlockSpec((1, D_MODEL), lambda i, j: (0, 0)),
                pl.BlockSpec((IN_STEP_TILES, D_MODEL, COL_TILE), lambda i, j: (j, 0, 0))]
    args = [x2d, g.reshape(1, D_MODEL), w]
    if aliased:
        in_specs += [pl.BlockSpec(memory_space=pl.ANY)] * 3
        args += list(prev_kv)

    def res_spec(dil):
        return pl.BlockSpec((None, dil, tm // dil, 3 * ATT_WIDTH), lambda i, j: (i // tps, 0, i % tps, 0))

    assert Q_TILE[1] % IN_STEP_TILES == 0 and U_TILE0 % IN_STEP_TILES == 0 and IN_TILES - U_TILE0 == IN_STEP_TILES
    nat_lo, u_step = Q_TILE[1] // IN_STEP_TILES, U_TILE0 // IN_STEP_TILES
    out_specs = [
        pl.BlockSpec((tm, IN_STEP_TILES * COL_TILE),
                     lambda i, j: (i, jnp.clip(j, 0, nat_lo - 1) + jnp.clip(j - u_step + 1, 0, 1))),
        res_spec(d1), res_spec(d2),
        pl.BlockSpec((None, None, w0 * KV_ROWS, HEAD_DIM), lambda i, j: (layer, i // tps, 0, 0),
                     pipeline_mode=pl.Buffered(1)),
        pl.BlockSpec((None, None, w1 * KV_ROWS, HEAD_DIM), lambda i, j: (layer, i // tps, 0, 0),
                     pipeline_mode=pl.Buffered(1)),
        pl.BlockSpec((None, None, tm * KV_ROWS, HEAD_DIM), lambda i, j: (layer, i // tps, i % tps, 0)),
    ]
    out_shape = [
        jax.ShapeDtypeStruct((m, NAT_TILES * COL_TILE), _BF16),
        jax.ShapeDtypeStruct((batch, d1, seq // d1, 3 * ATT_WIDTH), _BF16),
        jax.ShapeDtypeStruct((batch, d2, seq // d2, 3 * ATT_WIDTH), _BF16),
    ] + [jax.ShapeDtypeStruct((DEPTH, batch, win * KV_ROWS, HEAD_DIM), _F32) for win, _ in DIL_GROUPS]
    return pl.pallas_call(
        functools.partial(_inproj_prompt_kernel, tm=tm, aliased=aliased),
        grid=(m // tm, IN_TILES // IN_STEP_TILES),
        in_specs=in_specs, out_specs=out_specs, out_shape=out_shape,
        scratch_shapes=[pltpu.VMEM((tm, D_MODEL), _BF16),
                        pltpu.VMEM((tm // SUB_ROWS, IN_STEP_TILES, HEADS, SUB_ROWS, HEAD_DIM), _F32),
                        pltpu.VMEM((HEADS, SUB_ROWS, HEAD_DIM), _F32)],
        input_output_aliases={3: 3, 4: 4, 5: 5} if aliased else {},
        compiler_params=pltpu.CompilerParams(dimension_semantics=("arbitrary", "arbitrary"),
                                             vmem_limit_bytes=VMEM_LIMIT_HIGH),
        name="inproj_prompt",
    )(*args)


QB = 128
G0_UNROLL = 3
G2_RES_PER_STEP = 4


def _attn_prompt_kernel(q0, k0, v0, q1, k1, v1, q2, k2, v2, o_ref, acc_s, m_s, l_s, shuf_s, *, seq):
    step = pl.program_id(1)
    n1 = DIL_GROUPS[1][1]
    n2_steps = DIL_GROUPS[2][1] // G2_RES_PER_STEP

    row0 = lax.broadcasted_iota(jnp.int32, (QB, QB), 0)
    col0 = lax.broadcasted_iota(jnp.int32, (QB, QB), 1)
    dist_first = jnp.where(col0 <= row0, (row0 - col0).astype(_F32), jnp.inf)
    row1 = lax.broadcasted_iota(jnp.int32, (QB, 2 * QB), 0)
    col1 = lax.broadcasted_iota(jnp.int32, (QB, 2 * QB), 1)
    d1 = QB + row1 - col1
    dist_next = jnp.where((d1 >= 0) & (d1 <= QB), d1.astype(_F32), jnp.inf)
    ones = jnp.ones((2 * QB, HEAD_DIM), _BF16)

    seg = seq // n1
    chunk = QB // n1
    assert n1 == 4 and DIL_GROUPS[2][1] == 16

    def blocks(specs):
        tiles = []
        for slot, (qr, kr, vr, grp, res, i, is_first_block) in enumerate(specs):
            dil = DIL_GROUPS[grp][1]
            for h in range(HEADS):
                hs = _head_cols(h)
                slope_step = SLOPES[grp * HEADS + h] * dil
                if is_first_block:
                    q, k, v = qr[0:QB, hs], kr[0:QB, hs], vr[0:QB, hs]
                    bias, one = dist_first * slope_step, ones[0:QB]
                else:
                    qs = pl.ds(pl.multiple_of(i * QB, QB), QB)
                    ks = pl.ds(pl.multiple_of(i * QB - QB, QB), 2 * QB)
                    q, k, v = qr[qs, hs], kr[ks, hs], vr[ks, hs]
                    bias, one = dist_next * slope_step, ones
                tiles.append((q, k, jnp.concatenate([v, one], axis=1), bias, grp, res, i, is_first_block, slot, h))
        scores = [lax.dot_general(t[0], t[1], (((1,), (1,)), ((), ())), preferred_element_type=_F32) for t in tiles]
        probs, maxes = [], []
        for t, s in zip(tiles, scores):
            s = s * ATT_SCALE - t[3]
            m = jnp.max(s, axis=1, keepdims=True)
            probs.append(jnp.exp(s - m).astype(_BF16))
            maxes.append(jnp.broadcast_to(m, (QB, HEAD_DIM)))
        outs = [jnp.dot(p, t[2], preferred_element_type=_F32) for t, p in zip(tiles, probs)]
        def merge(h, rows, acc, m, l):
            m_old, l_old, a_old = m_s[h, rows, :], l_s[h, rows, :], acc_s[h, rows, :]
            m_new = jnp.maximum(m_old, m)
            w_old = jnp.exp(m_old - m_new)
            w_new = jnp.exp(m - m_new)
            acc_s[h, rows, :] = a_old * w_old + acc * w_new
            l_s[h, rows, :] = l_old * w_old + l * w_new
            m_s[h, rows, :] = m_new

        for (_, _, _, _, grp, res, i, is_first_block, slot, h), m, acc_l in zip(tiles, maxes, outs):
            acc, l = acc_l[:, :HEAD_DIM], acc_l[:, HEAD_DIM:]
            if grp == 1:
                rows = pl.ds(pl.multiple_of(res * seg + i * QB, QB), QB)
                acc_s[h, rows, :], m_s[h, rows, :], l_s[h, rows, :] = acc, m, l
            elif grp == 2:
                merge(h, pl.ds((res & (n1 - 1)) * seg + (res >> 2), QB, stride=n1), acc, m, l)
            else:
                tmps = [shuf_s.at[(slot * HEADS + h) * 3 + a] for a in range(3)]
                for tmp, val in zip(tmps, (acc, m, l)):
                    tmp[...] = val
                for r4 in range(n1):
                    dst = r4 * seg + i * chunk
                    dst = pl.ds(dst if is_first_block else pl.multiple_of(dst, chunk), chunk)
                    merge(h, dst, *(tmp[pl.ds(r4, chunk, stride=n1), :] for tmp in tmps))

    @pl.when(step < n1)
    def _():
        blocks([(q1, k1, v1, 1, step, i, i == 0) for i in range(seg // QB)])

    @pl.when((step >= n1) & (step < n1 + n2_steps))
    def _():
        assert seq // DIL_GROUPS[2][1] == QB
        blocks([(q2.at[r], k2.at[r], v2.at[r], 2, (step - n1) * G2_RES_PER_STEP + r, 0, True)
                for r in range(G2_RES_PER_STEP)])

    @pl.when(step == n1 + n2_steps)
    def _():
        n_blk = seq // QB
        assert (n_blk - 1) % G0_UNROLL == 0
        blocks([(q0, k0, v0, 0, 0, 0, True)])

        def g0_body(it, carry):
            blocks([(q0, k0, v0, 0, 0, 1 + it * G0_UNROLL + u, False) for u in range(G0_UNROLL)])
            return carry
        lax.fori_loop(0, (n_blk - 1) // G0_UNROLL, g0_body, 0)

        def body(c, carry):
            for h in range(HEADS):
                tmp = shuf_s.at[h]
                for r4 in range(n1):
                    src = pl.ds(pl.multiple_of(r4 * seg + c * chunk, chunk), chunk)
                    tmp[pl.ds(r4, chunk, stride=n1), :] = acc_s[h, src, :] / l_s[h, src, :]
                o_ref[pl.ds(pl.multiple_of(c * QB, QB), QB), _head_cols(h)] = tmp[...].astype(_BF16)
            return carry
        lax.fori_loop(0, seq // QB, body, 0)


def _attn_prompt(nat, qkv1, qkv2, batch, seq):
    n1, n2 = DIL_GROUPS[1][1], DIL_GROUPS[2][1]
    n2_steps = n2 // G2_RES_PER_STEP
    nat3 = nat.reshape(batch, seq, NAT_TILES * COL_TILE)
    in_specs = [pl.BlockSpec((None, seq, COL_TILE), lambda b, s, t=t: (b, 0, Q_TILE[0] + t)) for t in range(3)]
    in_specs += [pl.BlockSpec((None, None, seq // n1, COL_TILE),
                              lambda b, s, t=t: (b, jnp.clip(s, 0, n1 - 1), 0, t)) for t in range(3)]
    in_specs += [pl.BlockSpec((None, G2_RES_PER_STEP, seq // n2, COL_TILE),
                              lambda b, s, t=t: (b, jnp.clip(s - n1, 0, n2_steps - 1), 0, t)) for t in range(3)]
    args = [nat3] * 3 + [qkv1] * 3 + [qkv2] * 3
    return pl.pallas_call(
        functools.partial(_attn_prompt_kernel, seq=seq),
        grid=(batch, 1 + n1 + n2_steps),
        in_specs=in_specs,
        out_specs=pl.BlockSpec((None, seq, ATT_WIDTH), lambda b, s: (b, 0, 0)),
        out_shape=jax.ShapeDtypeStruct((batch, seq, ATT_WIDTH), _BF16),
        scratch_shapes=[pltpu.VMEM((HEADS, seq, HEAD_DIM), _F32)] * 3
                       + [pltpu.VMEM((G0_UNROLL * HEADS * 3, QB, HEAD_DIM), _F32)],
        compiler_params=pltpu.CompilerParams(dimension_semantics=("parallel", "arbitrary"),
                                             vmem_limit_bytes=VMEM_LIMIT),
        name="attn_prompt",
    )(*args)


def _attn_sample_kernel(*refs, tdec, aliased):
    n_in = 10 if aliased else 7
    proj_ref, n0, n1, n2, c0, c1, c2 = refs[:7]
    yb_ref, o0, o1, o2 = refs[n_in:]
    news, caches, outs = (n0, n1, n2), (c0, c1, c2), (o0, o1, o2)
    for h in range(HEADS):
        pieces = []
        for grp, (win, dil) in enumerate(DIL_GROUPS):
            slope = SLOPES[grp * HEADS + h]
            q = proj_ref[:, _head_cols(h, Q_TILE[grp] * COL_TILE)]
            k_new = news[grp][:, _head_cols(h)]
            v_new = news[grp][:, _head_cols(h, ATT_WIDTH)]
            k_old = caches[grp][pl.ds(h, win, stride=KV_ROWS), :]
            v_old = caches[grp][pl.ds(HEADS + h, win, stride=KV_ROWS), :]
            for k, v, n_keys, base in ((k_old, v_old, win, win), (k_new, v_new, tdec, 0)):
                s = lax.dot_general(q, k.astype(_BF16), (((1,), (1,)), ((), ())), preferred_element_type=_F32)
                t = lax.broadcasted_iota(jnp.int32, (tdec, n_keys), 0)
                c = lax.broadcasted_iota(jnp.int32, (tdec, n_keys), 1)
                dist = base + t - c
                ok = (dist >= 0) & (dist <= win) & ((dist & (dil - 1)) == 0)
                s = jnp.where(ok, s * ATT_SCALE - slope * dist.astype(_F32), -jnp.inf)
                pieces.append((s, v.astype(_BF16)))
        m = functools.reduce(jnp.maximum, [jnp.max(s, axis=1, keepdims=True) for s, _ in pieces])
        l = jnp.zeros((tdec, 1), _F32)
        acc = jnp.zeros((tdec, HEAD_DIM), _F32)
        for s, v in pieces:
            p = jnp.exp(s - m)
            l = l + jnp.sum(p, axis=1, keepdims=True)
            acc = acc + jnp.dot(p.astype(_BF16), v, preferred_element_type=_F32)
        yb_ref[:, _head_cols(h)] = (acc / l).astype(_BF16)
    for grp, (win, _) in enumerate(DIL_GROUPS):
        kept = (win - tdec) * KV_ROWS
        outs[grp][0:kept, :] = caches[grp][tdec * KV_ROWS:win * KV_ROWS, :]
        for part in range(2):
            for h in range(HEADS):
                outs[grp][pl.ds(kept + part * HEADS + h, tdec, stride=KV_ROWS), :] = (
                    news[grp][:, _head_cols(h, part * ATT_WIDTH)])


def _attn_sample(proj, kv_new, caches, batch, tdec, layer, prev_out):
    aliased = prev_out is not None
    in_specs = [pl.BlockSpec((None, tdec, IN_COLS), lambda b: (b, 0, 0))]
    in_specs += [pl.BlockSpec((None, tdec, 2 * ATT_WIDTH), lambda b: (b, 0, 0))] * 3
    cache_specs = [pl.BlockSpec((None, None, win * KV_ROWS, HEAD_DIM), lambda b: (layer, b, 0, 0)) for win, _ in DIL_GROUPS]
    in_specs += cache_specs
    args = [proj.reshape(batch, tdec, IN_COLS)] + [k.reshape(batch, tdec, 2 * ATT_WIDTH) for k in kv_new] + list(caches)
    if aliased:
        in_specs += [pl.BlockSpec(memory_space=pl.ANY)] * 3
        args += list(prev_out)
    out_specs = [pl.BlockSpec((None, tdec, ATT_WIDTH), lambda b: (b, 0, 0))] + cache_specs
    out_shape = [jax.ShapeDtypeStruct((batch, tdec, ATT_WIDTH), _BF16)]
    out_shape += [jax.ShapeDtypeStruct((DEPTH, batch, win * KV_ROWS, HEAD_DIM), _F32) for win, _ in DIL_GROUPS]
    return pl.pallas_call(
        functools.partial(_attn_sample_kernel, tdec=tdec, aliased=aliased),
        grid=(batch,),
        in_specs=in_specs, out_specs=out_specs, out_shape=out_shape,
        input_output_aliases={7: 1, 8: 2, 9: 3} if aliased else {},
        compiler_params=pltpu.CompilerParams(dimension_semantics=("arbitrary",), vmem_limit_bytes=VMEM_LIMIT),
        name="attn_sample",
    )(*args)


def _mixer_kernel(*refs, nb, tt, start, from_zero):
    if from_zero:
        pa_ref, pu_ref, yb_ref, x_ref, cw_ref, pw_ref, psc_ref, wo_ref = refs[:8]
    else:
        pa_ref, pu_ref, yb_ref, x_ref, cs0_ref, ps0_ref, cw_ref, pw_ref, psc_ref, wo_ref = refs[:10]
    out_ref, cs_out, ps_out, ext_a, ext_u, yc = refs[-6:]
    ti = pl.program_id(1)
    rows = nb * tt
    pool_halo = POOL_BLOCK if from_zero else POOL_HALO

    @pl.when(ti == 0)
    def _():
        if from_zero:
            ext_a[:, 0:CONV_HALO, :] = jnp.zeros((nb, CONV_HALO, CONV_WIDTH), _F32)
            ext_u[:, 0:pool_halo, :] = jnp.zeros((nb, pool_halo, POOL_WIDTH), _BF16)
        else:
            ext_a[:, 0:CONV_HALO, :] = cs0_ref[...]
            ext_u[:, 0:pool_halo, :] = ps0_ref[...]

    yb = yb_ref[...].astype(_F32).reshape(rows, ATT_WIDTH).astype(_BF16)
    mixed = jnp.dot(yb, wo_ref[CONV_WIDTH:CONV_WIDTH + ATT_WIDTH, :], preferred_element_type=_F32)

    pa = pa_ref[...].astype(_F32)
    xa, gate_b, gate_c = (pa[:, :, k * CONV_WIDTH:(k + 1) * CONV_WIDTH] for k in range(3))
    prod = gate_c * xa
    ext_a[:, CONV_HALO:CONV_HALO + tt, :] = prod
    cw = cw_ref[...]
    cu = (cw[0] * ext_a[:, CONV_HALO - 2:CONV_HALO - 2 + tt, :]
          + cw[1] * ext_a[:, CONV_HALO - 1:CONV_HALO - 1 + tt, :] + cw[2] * prod)
    ya = (gate_b * cu).reshape(rows, CONV_WIDTH).astype(_BF16)
    mixed = mixed + jnp.dot(ya, wo_ref[0:CONV_WIDTH, :], preferred_element_type=_F32)

    ext_u[:, pool_halo:pool_halo + tt, :] = pu_ref[...].astype(ext_u.dtype)
    if from_zero:
        pb = POOL_BLOCK
        row = lax.broadcasted_iota(jnp.int32, (pb, 2 * pb), 0)
        col = lax.broadcasted_iota(jnp.int32, (pb, 2 * pb), 1)
        lag = pb + row - col
        for gi, w in enumerate(POOL_WINDOWS):
            cols = slice(gi * POOL_GROUP, (gi + 1) * POOL_GROUP)
            band = ((lag >= 0) & (lag < w)).astype(_F32).astype(_BF16)
            for r in range(tt // pb):
                tot = jnp.dot(band, ext_u[0, r * pb:(r + 2) * pb, cols], preferred_element_type=_F32)
                pos = start + ti * tt + r * pb + lax.broadcasted_iota(jnp.int32, (pb, 1), 0)
                cnt = jnp.minimum(w, pos + 1).astype(_F32)
                ug = pu_ref[0, r * pb:(r + 1) * pb, cols].astype(_F32)
                diff = (tot / cnt - ug).astype(_BF16)
                z = jnp.dot(diff, pw_ref[gi], preferred_element_type=_F32) * psc_ref[:, cols]
                yc[r * pb:(r + 1) * pb, cols] = z.astype(_BF16)
    else:
        u = pu_ref[...].astype(_F32)
        pos = start + ti * tt + lax.broadcasted_iota(jnp.int32, (1, tt, 1), 1)
        for gi, w in enumerate(POOL_WINDOWS):
            cols = slice(gi * POOL_GROUP, (gi + 1) * POOL_GROUP)
            ug = u[:, :, cols]
            tot = ug
            for k in range(1, w):
                tot = tot + ext_u[:, pool_halo - k:pool_halo - k + tt, cols]
            cnt = jnp.minimum(w, pos + 1).astype(_F32)
            diff = (tot / cnt - ug).reshape(rows, POOL_GROUP).astype(_BF16)
            z = jnp.dot(diff, pw_ref[gi], preferred_element_type=_F32) * psc_ref[:, cols]
            yc[:, cols] = z.astype(_BF16)

    mixed = mixed + jnp.dot(yc[...], wo_ref[CONV_WIDTH + ATT_WIDTH:, :], preferred_element_type=_F32)
    out_ref[...] = x_ref[...] + mixed.reshape(nb, tt, D_MODEL)

    tail_a = ext_a[:, tt:tt + CONV_HALO, :]
    ext_a[:, 0:CONV_HALO, :] = tail_a
    cs_out[...] = tail_a
    ps_out[...] = ext_u[:, pool_halo + tt - POOL_HALO:pool_halo + tt, :].astype(_F32)
    ext_u[:, 0:pool_halo, :] = ext_u[:, tt:tt + pool_halo, :]


def _mixer(proj, yb, x, conv_state, pool_state, conv_w, pool_w, pool_scale, w_out, layer, nb, tt, start, u_tile0):
    batch, seq, _ = x.shape
    proj3 = proj.reshape(batch, seq, proj.shape[-1])
    u_blk = u_tile0 * COL_TILE // POOL_WIDTH
    const2 = lambda b, t: (0, 0)
    from_zero = conv_state is None and pool_state is None
    assert from_zero or (conv_state is not None and pool_state is not None)
    assert not from_zero or (nb == 1 and start == 0 and tt % POOL_BLOCK == 0)
    state_specs = [] if from_zero else [pl.BlockSpec((nb, CONV_HALO, CONV_WIDTH), lambda b, t: (b, 0, 0)),
                                        pl.BlockSpec((nb, POOL_HALO, POOL_WIDTH), lambda b, t: (b, 0, 0))]
    state_args = [] if from_zero else [conv_state, pool_state]
    ext_u = (pltpu.VMEM((nb, tt + POOL_BLOCK, POOL_WIDTH), _BF16) if from_zero
             else pltpu.VMEM((nb, tt + POOL_HALO, POOL_WIDTH), _F32))
    return pl.pallas_call(
        functools.partial(_mixer_kernel, nb=nb, tt=tt, start=start, from_zero=from_zero),
        grid=(batch // nb, seq // tt),
        in_specs=[pl.BlockSpec((nb, tt, 3 * CONV_WIDTH), lambda b, t: (b, t, 0)),
                  pl.BlockSpec((nb, tt, POOL_WIDTH), lambda b, t: (b, t, u_blk)),
                  pl.BlockSpec((nb, tt, ATT_WIDTH), lambda b, t: (b, t, 0)),
                  pl.BlockSpec((nb, tt, D_MODEL), lambda b, t: (b, t, 0))] + state_specs + [
                  pl.BlockSpec((3, CONV_WIDTH), const2),
                  pl.BlockSpec((None, len(POOL_WINDOWS), POOL_GROUP, POOL_GROUP), lambda b, t: (layer, 0, 0, 0)),
                  pl.BlockSpec((1, POOL_WIDTH), const2),
                  pl.BlockSpec((None, D_MODEL, D_MODEL), lambda b, t: (layer, 0, 0))],
        out_specs=[pl.BlockSpec((nb, tt, D_MODEL), lambda b, t: (b, t, 0)),
                   pl.BlockSpec((nb, CONV_HALO, CONV_WIDTH), lambda b, t: (b, 0, 0)),
                   pl.BlockSpec((nb, POOL_HALO, POOL_WIDTH), lambda b, t: (b, 0, 0))],
        out_shape=[jax.ShapeDtypeStruct((batch, seq, D_MODEL), _F32),
                   jax.ShapeDtypeStruct((batch, CONV_HALO, CONV_WIDTH), _F32),
                   jax.ShapeDtypeStruct((batch, POOL_HALO, POOL_WIDTH), _F32)],
        scratch_shapes=[pltpu.VMEM((nb, tt + CONV_HALO, CONV_WIDTH), _F32), ext_u,
                        pltpu.VMEM((nb * tt, POOL_WIDTH), _BF16)],
        compiler_params=pltpu.CompilerParams(dimension_semantics=("parallel", "arbitrary"),
                                             vmem_limit_bytes=VMEM_LIMIT),
        name="mixer",
    )(proj3, proj3, yb, x, *state_args, conv_w, pool_w, pool_scale.reshape(1, POOL_WIDTH), w_out)


def _ffn_sample_kernel(x_ref, n2_ref, wg_ref, wu_ref, cw_ref, wd_ref, st0_ref, fn_ref,
                       out_ref, st_out, wgt_ref, wut_ref, wdt_ref, h_s, ext_s, *, nb, tt, final):
    f = pl.program_id(0)
    rows = nb * tt

    @pl.when(f == 0)
    def _():
        x = x_ref[...]
        h_s[...] = _rms_norm(x, n2_ref[...]).reshape(rows, D_MODEL).astype(_BF16)
        out_ref[...] = x

    def tile(width):
        cols = slice(0, width)
        wg, wu, wd = wg_ref[:, cols].astype(_BF16), wu_ref[:, cols].astype(_BF16), wd_ref[cols, :].astype(_BF16)
        wgt_ref[:, cols], wut_ref[:, cols], wdt_ref[cols, :] = wg, wu, wd
        if width < FF_TILE:
            wgt_ref[:, width:] = jnp.zeros((D_MODEL, FF_TILE - width), _BF16)
            wut_ref[:, width:] = jnp.zeros((D_MODEL, FF_TILE - width), _BF16)
            wdt_ref[width:, :] = jnp.zeros((FF_TILE - width, D_MODEL), _BF16)
        h = h_s[...]
        gate = jnp.dot(h, wg, preferred_element_type=_F32).reshape(nb, tt, width)
        up = jnp.dot(h, wu, preferred_element_type=_F32).reshape(nb, tt, width)
        ext_s[:, 0:CONV_HALO, cols] = st0_ref[:, :, cols]
        ext_s[:, CONV_HALO:CONV_HALO + tt, cols] = gate
        cw = cw_ref[:, cols]
        gc = (cw[0] * ext_s[:, CONV_HALO - 2:CONV_HALO - 2 + tt, cols]
              + cw[1] * ext_s[:, CONV_HALO - 1:CONV_HALO - 1 + tt, cols] + cw[2] * gate)
        st_out[:, :, cols] = ext_s[:, tt:tt + CONV_HALO, cols]
        act = (jax.nn.silu(gc) * up).reshape(rows, width).astype(_BF16)
        out_ref[...] += jnp.dot(act, wd, preferred_element_type=_F32).reshape(nb, tt, D_MODEL)

    @pl.when(f < FF_TILES - 1)
    def _():
        tile(FF_TILE)

    @pl.when(f == FF_TILES - 1)
    def _():
        tile(FF_LAST)
        if final:
            out_ref[...] = _rms_norm(out_ref[...], fn_ref[...])


def _ffn_prompt_kernel(x_ref, n2_ref, wg_ref, wu_ref, cw_ref, wd_ref, fn_ref,
                       out_ref, st_out, h_s, ext_s, carry_s, *, tt, final):
    ti = pl.program_id(1)
    f = pl.program_id(2)

    def tile(first, last):
        if first:
            @pl.when(ti == 0)
            def _():
                carry_s[...] = jnp.zeros(carry_s.shape, _F32)
            x = x_ref[...]
            h_s[...] = _rms_norm(x, n2_ref[...]).astype(_BF16)
            out_ref[...] = x
        ext_s[0:CONV_HALO, :] = carry_s[f]

        wg, wu, wd, cw = wg_ref[...], wu_ref[...], wd_ref[...], cw_ref[f]
        chunks = [slice(r0, r0 + FFN_SUB_ROWS) for r0 in range(0, tt, FFN_SUB_ROWS)]
        for rs in chunks:
            gate = jnp.dot(h_s[rs, :], wg, preferred_element_type=_F32)
            ext_s[CONV_HALO + rs.start:CONV_HALO + rs.stop, :] = gate
        ups = [jnp.dot(h_s[rs, :], wu, preferred_element_type=_F32) for rs in chunks]
        for rs, up in zip(chunks, ups):
            gc = (cw[0] * ext_s[CONV_HALO - 2 + rs.start:CONV_HALO - 2 + rs.stop, :]
                  + cw[1] * ext_s[CONV_HALO - 1 + rs.start:CONV_HALO - 1 + rs.stop, :]
                  + cw[2] * ext_s[CONV_HALO + rs.start:CONV_HALO + rs.stop, :])
            act = (jax.nn.silu(gc) * up).astype(_BF16)
            out_ref[rs, :] += jnp.dot(act, wd, preferred_element_type=_F32)
        tail = ext_s[tt:tt + CONV_HALO, :]
        carry_s[f] = tail
        st_out[f] = tail
        if last:
            out_ref[...] = _rms_norm(out_ref[...], fn_ref[...])

    @pl.when(f == 0)
    def _():
        tile(True, False)

    if final:
        @pl.when((f > 0) & (f < FF_TILES - 1))
        def _():
            tile(False, False)

        @pl.when(f == FF_TILES - 1)
        def _():
            tile(False, True)
    else:
        @pl.when(f > 0)
        def _():
            tile(False, False)


def _ffn_sample(x, norm2, w_gate, w_up, conv_w, w_down, state, final_norm, layer, final):
    nb, tt, _ = x.shape
    vec = lambda f: (0, 0)
    return pl.pallas_call(
        functools.partial(_ffn_sample_kernel, nb=nb, tt=tt, final=final),
        grid=(FF_TILES,),
        in_specs=[pl.BlockSpec((nb, tt, D_MODEL), lambda f: (0, 0, 0)),
                  pl.BlockSpec((1, D_MODEL), vec),
                  pl.BlockSpec((None, D_MODEL, FF_TILE), lambda f: (layer, 0, f)),
                  pl.BlockSpec((None, D_MODEL, FF_TILE), lambda f: (layer, 0, f)),
                  pl.BlockSpec((3, FF_TILE), lambda f: (0, f)),
                  pl.BlockSpec((None, FF_TILE, D_MODEL), lambda f: (layer, f, 0)),
                  pl.BlockSpec((nb, CONV_HALO, FF_TILE), lambda f: (0, 0, f)),
                  pl.BlockSpec((1, D_MODEL), vec)],
        out_specs=[pl.BlockSpec((nb, tt, D_MODEL), lambda f: (0, 0, 0)),
                   pl.BlockSpec((nb, CONV_HALO, FF_TILE), lambda f: (0, 0, f)),
                   pl.BlockSpec((None, D_MODEL, FF_TILE), lambda f: (f, 0, 0)),
                   pl.BlockSpec((None, D_MODEL, FF_TILE), lambda f: (f, 0, 0)),
                   pl.BlockSpec((None, FF_TILE, D_MODEL), lambda f: (f, 0, 0))],
        out_shape=[jax.ShapeDtypeStruct((nb, tt, D_MODEL), _F32),
                   jax.ShapeDtypeStruct((nb, CONV_HALO, D_FF), _F32),
                   jax.ShapeDtypeStruct((FF_TILES, D_MODEL, FF_TILE), _BF16),
                   jax.ShapeDtypeStruct((FF_TILES, D_MODEL, FF_TILE), _BF16),
                   jax.ShapeDtypeStruct((FF_TILES, FF_TILE, D_MODEL), _BF16)],
        scratch_shapes=[pltpu.VMEM((nb * tt, D_MODEL), _BF16),
                        pltpu.VMEM((nb, tt + CONV_HALO, FF_TILE), _F32)],
        compiler_params=pltpu.CompilerParams(dimension_semantics=("arbitrary",), vmem_limit_bytes=VMEM_LIMIT),
        name="ffn_sample",
    )(x, norm2.reshape(1, D_MODEL), w_gate, w_up, conv_w, w_down, state, final_norm.reshape(1, D_MODEL))


def _ffn_prompt(x, norm2, w_gate_t, w_up_t, conv_w, w_down_t, final_norm, tt, final):
    batch, seq, _ = x.shape
    vec = lambda b, t, f: (0, 0)
    return pl.pallas_call(
        functools.partial(_ffn_prompt_kernel, tt=tt, final=final),
        grid=(batch, seq // tt, FF_TILES),
        in_specs=[pl.BlockSpec((None, tt, D_MODEL), lambda b, t, f: (b, t, 0)),
                  pl.BlockSpec((1, D_MODEL), vec),
                  pl.BlockSpec((None, D_MODEL, FF_TILE), lambda b, t, f: (f, 0, 0)),
                  pl.BlockSpec((None, D_MODEL, FF_TILE), lambda b, t, f: (f, 0, 0)),
                  pl.BlockSpec((FF_TILES, 3, FF_TILE), lambda b, t, f: (0, 0, 0)),
                  pl.BlockSpec((None, FF_TILE, D_MODEL), lambda b, t, f: (f, 0, 0)),
                  pl.BlockSpec((1, D_MODEL), vec)],
        out_specs=[pl.BlockSpec((None, tt, D_MODEL), lambda b, t, f: (b, t, 0)),
                   pl.BlockSpec((None, None, FF_TILES, CONV_HALO, FF_TILE), lambda b, t, f: (b, t, 0, 0, 0))],
        out_shape=[jax.ShapeDtypeStruct((batch, seq, D_MODEL), _F32),
                   jax.ShapeDtypeStruct((batch, seq // tt, FF_TILES, CONV_HALO, FF_TILE), _F32)],
        scratch_shapes=[pltpu.VMEM((tt, D_MODEL), _BF16),
                        pltpu.VMEM((tt + CONV_HALO, FF_TILE), _F32),
                        pltpu.VMEM((FF_TILES, CONV_HALO, FF_TILE), _F32)],
        compiler_params=pltpu.CompilerParams(dimension_semantics=("parallel", "arbitrary", "arbitrary"),
                                             vmem_limit_bytes=VMEM_LIMIT),
        name="ffn_prompt",
    )(x, norm2.reshape(1, D_MODEL), w_gate_t, w_up_t, conv_w, w_down_t, final_norm.reshape(1, D_MODEL))


def _pad_rows_front(a, rows):
    return jnp.pad(a, ((0, 0), (rows - a.shape[1], 0), (0, 0)))


def kernel(x_prompt, x_sample, cache_kv_w128, cache_kv_w512, cache_kv_w2048, state_conv_a, state_pool, state_ffn_conv, norm1, w_in, conv_a_w, w_out, pool_w, pool_scale, norm2, w_gate, w_up, ffn_conv_w, w_down, final_norm):
    bp, sp, _ = x_prompt.shape
    bs, ss, _ = x_sample.shape
    caches = tuple(c.reshape(DEPTH, bs, win * KV_ROWS, HEAD_DIM)
                   for c, (win, _) in zip((cache_kv_w128, cache_kv_w512, cache_kv_w2048), DIL_GROUPS))
    tm, tt = 1024, 512
    w_out_b, pool_w_b = w_out.astype(_BF16), pool_w.astype(_BF16)
    xp, xs = x_prompt, x_sample
    kv_p = kv_s = None
    small_p, small_s = [], []
    for l in range(DEPTH):
        final = l == DEPTH - 1

        proj, *kv_new, w_in_t = _inproj_sample(xs.reshape(bs * ss, D_MODEL), norm1[l], w_in, l)
        yb, *kv_s = _attn_sample(proj, kv_new, caches, bs, ss, l, kv_s)
        xs, conv_o, pool_o = _mixer(proj, yb, xs, _pad_rows_front(state_conv_a[l], CONV_HALO),
                                    _pad_rows_front(state_pool[l], POOL_HALO), conv_a_w[l], pool_w_b, pool_scale[l],
                                    w_out_b, l, bs, ss, PAST_LEN, U_TILE0)
        xs, ffn_o, w_gate_t, w_up_t, w_down_t = _ffn_sample(
            xs, norm2[l], w_gate, w_up, ffn_conv_w[l], w_down, _pad_rows_front(state_ffn_conv[l], CONV_HALO),
            final_norm, l, final)
        small_s.append((conv_o[:, CONV_HALO - 2:], pool_o[:, POOL_HALO - POOL_BUF:], ffn_o[:, CONV_HALO - 2:]))

        nat, qkv1, qkv2, *kv_p = _inproj_prompt(xp.reshape(bp * sp, D_MODEL), norm1[l], w_in_t, bp, sp, tm, l, kv_p)
        yb = _attn_prompt(nat, qkv1, qkv2, bp, sp)
        xp, conv_o, pool_o = _mixer(nat, yb, xp, None, None, conv_a_w[l], pool_w_b, pool_scale[l], w_out_b,
                                    l, 1, tt, 0, NAT_TILES - 2)
        conv_w_t = jnp.pad(ffn_conv_w[l], ((0, 0), (0, FF_TILES * FF_TILE - D_FF)))
        conv_w_t = conv_w_t.reshape(3, FF_TILES, FF_TILE).transpose(1, 0, 2)
        xp, ffn_o = _ffn_prompt(xp, norm2[l], w_gate_t, w_up_t, conv_w_t, w_down_t, final_norm, tm, final)
        ffn_tail = ffn_o[:, -1, :, CONV_HALO - 2:].transpose(0, 2, 1, 3).reshape(bp, 2, FF_TILES * FF_TILE)
        small_p.append((conv_o[:, CONV_HALO - 2:], pool_o[:, POOL_HALO - POOL_BUF:], ffn_tail[:, :, :D_FF]))

    out = [xp, xs]
    for g, (win, _) in enumerate(DIL_GROUPS):
        out += [kv_p[g].reshape(DEPTH, bp, win, 2, HEADS, HEAD_DIM), kv_s[g].reshape(DEPTH, bs, win, 2, HEADS, HEAD_DIM)]
    for i in range(3):
        out += [jnp.stack([s[i] for s in small_p]), jnp.stack([s[i] for s in small_s])]
    return tuple(out)
```

```python
import functools

import numpy as np
import jax
import jax.numpy as jnp
from jax import lax
from jax.experimental import pallas as pl
from jax.experimental.pallas import tpu as pltpu

D_MODEL = 2048
DEPTH = 2
PAST_LEN = 16384
CONV_WIDTH = 512
ATT_WIDTH = 512
HEADS = 4
HEAD_DIM = 128
DIL_GROUPS = ((128, 1), (512, 4), (2048, 16))
POOL_WIDTH = 1024
POOL_WINDOWS = (2, 4, 8, 16)
POOL_GROUP = 256
POOL_BUF = 15
IN_COLS = 7168
D_FF = 5504
RMS_EPS = 1e-6
ATT_SCALE = HEAD_DIM ** -0.5

COL_TILE = 512
IN_TILES = IN_COLS // COL_TILE
Q_TILE = (3, 6, 9)
U_TILE0 = 12
NAT_TILES = 8
KV_ROWS = 2 * HEADS
SUB_ROWS = 512
IN_STEP_TILES = 2
REGROUP_STRIDE = 4
FF_TILE = 512
FF_TILES = -(-D_FF // FF_TILE)
FF_LAST = D_FF - (FF_TILES - 1) * FF_TILE
FFN_SUB_ROWS = 256
CONV_HALO = 8
POOL_HALO = 16
POOL_BLOCK = 128
VMEM_LIMIT = 56 * 1024 * 1024
VMEM_LIMIT_HIGH = 61 * 1024 * 1024

_BF16 = jnp.bfloat16
_F32 = jnp.float32


def _alibi_slopes():
    h = np.arange(1, 3 * HEADS + 1, dtype=np.float32)
    return [float(v) for v in np.power(np.float32(2.0), -8.0 * h / (3 * HEADS))]


SLOPES = _alibi_slopes()


def _rms_norm(x, g):
    return (x * lax.rsqrt(jnp.mean(x * x, axis=-1, keepdims=True) + RMS_EPS)) * g


def _head_cols(h, base=0):
    return slice(base + h * HEAD_DIM, base + (h + 1) * HEAD_DIM)


def _inproj_sample_kernel(x_ref, g_ref, w_ref, proj_ref, kv0_ref, kv1_ref, kv2_ref, wt_ref, h_s):
    j = pl.program_id(1)

    @pl.when(j == 0)
    def _():
        h_s[...] = _rms_norm(x_ref[...], g_ref[...]).astype(_BF16)

    w = w_ref[...].astype(_BF16)
    wt_ref[...] = w
    acc = jnp.dot(h_s[...], w, preferred_element_type=_F32)
    proj_ref[...] = acc.astype(_BF16)
    for g, ref in enumerate((kv0_ref, kv1_ref, kv2_ref)):
        @pl.when((j == Q_TILE[g] + 1) | (j == Q_TILE[g] + 2))
        def _(ref=ref):
            ref[...] = acc


def _inproj_sample(x2d, g, w, layer):
    m = x2d.shape[0]

    def kv_spec(grp):
        return pl.BlockSpec((m, COL_TILE), lambda i, j: (i, jnp.clip(j - Q_TILE[grp] - 1, 0, 1)))

    return pl.pallas_call(
        _inproj_sample_kernel,
        grid=(1, IN_TILES),
        in_specs=[pl.BlockSpec((m, D_MODEL), lambda i, j: (i, 0)),
                  pl.BlockSpec((1, D_MODEL), lambda i, j: (0, 0)),
                  pl.BlockSpec((None, D_MODEL, COL_TILE), lambda i, j: (layer, 0, j))],
        out_specs=[pl.BlockSpec((m, COL_TILE), lambda i, j: (i, j)), kv_spec(0), kv_spec(1), kv_spec(2),
                   pl.BlockSpec((None, D_MODEL, COL_TILE), lambda i, j: (j, 0, 0))],
        out_shape=[jax.ShapeDtypeStruct((m, IN_COLS), _BF16)] +
                  [jax.ShapeDtypeStruct((m, 2 * ATT_WIDTH), _F32)] * 3 +
                  [jax.ShapeDtypeStruct((IN_TILES, D_MODEL, COL_TILE), _BF16)],
        scratch_shapes=[pltpu.VMEM((m, D_MODEL), _BF16)],
        compiler_params=pltpu.CompilerParams(dimension_semantics=("parallel", "arbitrary"),
                                             vmem_limit_bytes=VMEM_LIMIT),
        name="inproj_sample",
    )(x2d, g.reshape(1, D_MODEL), w)


def _inproj_prompt_kernel(*refs, tm, aliased):
    n_in = 6 if aliased else 3
    x_ref, g_ref, w_ref = refs[:3]
    nat_ref, g1_ref, g2_ref, kv0_ref, kv1_ref, kv2_ref, h_s, de_s, de2_s = refs[n_in:]
    kv_refs = (kv0_ref, kv1_ref, kv2_ref)
    res_refs = (None, g1_ref, g2_ref)
    j = pl.program_id(1)

    @pl.when(j == 0)
    def _():
        h_s[...] = _rms_norm(x_ref[...], g_ref[...]).astype(_BF16)

    def column_tiles(tiles):
        for c in range(tm // SUB_ROWS):
            r0 = c * SUB_ROWS
            for half, tile in enumerate(tiles):
                acc = jnp.dot(h_s[r0:r0 + SUB_ROWS, :], w_ref[half], preferred_element_type=_F32)
                grp = (tile - Q_TILE[0]) // 3 if Q_TILE[0] <= tile < U_TILE0 else None
                part = None if grp is None or tile == Q_TILE[grp] else tile - Q_TILE[grp] - 1
                dil = 1 if grp is None else DIL_GROUPS[grp][1]
                if dil == 1:
                    nat_ref[r0:r0 + SUB_ROWS, half * COL_TILE:(half + 1) * COL_TILE] = acc.astype(_BF16)
                else:
                    for h in range(HEADS):
                        de_s[c, half, h] = acc[:, _head_cols(h)]
                    n = SUB_ROWS // dil
                    col0 = (tile - Q_TILE[grp]) * COL_TILE
                    if dil == REGROUP_STRIDE:
                        for res in range(dil):
                            for h in range(HEADS):
                                res_refs[grp][res, c * n:(c + 1) * n, _head_cols(h, col0)] = (
                                    de_s[c, half, h, pl.ds(res, n, stride=dil), :].astype(_BF16))
                    else:
                        assert dil == REGROUP_STRIDE ** 2
                        q = SUB_ROWS // REGROUP_STRIDE
                        for h in range(HEADS):
                            for g in range(REGROUP_STRIDE):
                                de2_s[h, g * q:(g + 1) * q, :] = de_s[c, half, h, pl.ds(g, q, stride=REGROUP_STRIDE), :]
                        for g in range(REGROUP_STRIDE):
                            for r in range(REGROUP_STRIDE):
                                for h in range(HEADS):
                                    res_refs[grp][g + REGROUP_STRIDE * r, c * n:(c + 1) * n, _head_cols(h, col0)] = (
                                        de2_s[h, pl.ds(g * q + r, n, stride=REGROUP_STRIDE), :].astype(_BF16))
                if part is not None:
                    keep = min(DIL_GROUPS[grp][0], tm)
                    lo = max(r0, tm - keep)
                    n = r0 + SUB_ROWS - lo
                    if n > 0:
                        for h in range(HEADS):
                            dst = pl.ds((lo - (tm - keep)) * KV_ROWS + part * HEADS + h, n, stride=KV_ROWS)
                            kv_refs[grp][dst, :] = acc[lo - r0:lo - r0 + n, _head_cols(h)]

    for step in range(IN_TILES // IN_STEP_TILES):
        @pl.when(j == step)
        def _(step=step):
            column_tiles(tuple(range(step * IN_STEP_TILES, (step + 1) * IN_STEP_TILES)))


def _inproj_prompt(x2d, g, w, batch, seq, tm, layer, prev_kv):
    m = x2d.shape[0]
    tps = seq // tm
    aliased = prev_kv is not None
    (w0, _), (w1, d1), (w2, d2) = DIL_GROUPS
    assert w2 == seq and w0 <= tm and w1 <= tm and tm % SUB_ROWS == 0

    in_specs = [pl.BlockSpec((tm, D_MODEL), lambda i, j: (i, 0)),
                pl.BlockSpec((1, D_MODEL), lambda i, j: (0, 0)),
                pl.BlockSpec((IN_STEP_TILES, D_MODEL, COL_TILE), lambda i, j: (j, 0, 0))]
    args = [x2d, g.reshape(1, D_MODEL), w]
    if aliased:
        in_specs += [pl.BlockSpec(memory_space=pl.ANY)] * 3
        args += list(prev_kv)

    def res_spec(dil):
        return pl.BlockSpec((None, dil, tm // dil, 3 * ATT_WIDTH), lambda i, j: (i // tps, 0, i % tps, 0))

    assert Q_TILE[1] % IN_STEP_TILES == 0 and U_TILE0 % IN_STEP_TILES == 0 and IN_TILES - U_TILE0 == IN_STEP_TILES
    nat_lo, u_step = Q_TILE[1] // IN_STEP_TILES, U_TILE0 // IN_STEP_TILES
    out_specs = [
        pl.BlockSpec((tm, IN_STEP_TILES * COL_TILE),
                     lambda i, j: (i, jnp.clip(j, 0, nat_lo - 1) + jnp.clip(j - u_step + 1, 0, 1))),
        res_spec(d1), res_spec(d2),
        pl.BlockSpec((None, None, w0 * KV_ROWS, HEAD_DIM), lambda i, j: (layer, i // tps, 0, 0),
                     pipeline_mode=pl.Buffered(1)),
        pl.BlockSpec((None, None, w1 * KV_ROWS, HEAD_DIM), lambda i, j: (layer, i // tps, 0, 0),
                     pipeline_mode=pl.Buffered(1)),
        pl.BlockSpec((None, None, tm * KV_ROWS, HEAD_DIM), lambda i, j: (layer, i // tps, i % tps, 0)),
    ]
    out_shape = [
        jax.ShapeDtypeStruct((m, NAT_TILES * COL_TILE), _BF16),
        jax.ShapeDtypeStruct((batch, d1, seq // d1, 3 * ATT_WIDTH), _BF16),
        jax.ShapeDtypeStruct((batch, d2, seq // d2, 3 * ATT_WIDTH), _BF16),
    ] + [jax.ShapeDtypeStruct((DEPTH, batch, win * KV_ROWS, HEAD_DIM), _F32) for win, _ in DIL_GROUPS]
    return pl.pallas_call(
        functools.partial(_inproj_prompt_kernel, tm=tm, aliased=aliased),
        grid=(m // tm, IN_TILES // IN_STEP_TILES),
        in_specs=in_specs, out_specs=out_specs, out_shape=out_shape,
        scratch_shapes=[pltpu.VMEM((tm, D_MODEL), _BF16),
                        pltpu.VMEM((tm // SUB_ROWS, IN_STEP_TILES, HEADS, SUB_ROWS, HEAD_DIM), _F32),
                        pltpu.VMEM((HEADS, SUB_ROWS, HEAD_DIM), _F32)],
        input_output_aliases={3: 3, 4: 4, 5: 5} if aliased else {},
        compiler_params=pltpu.CompilerParams(dimension_semantics=("arbitrary", "arbitrary"),
                                             vmem_limit_bytes=VMEM_LIMIT_HIGH),
        name="inproj_prompt",
    )(*args)


QB = 128
G0_UNROLL = 3
G2_RES_PER_STEP = 4


def _attn_prompt_kernel(q0, k0, v0, q1, k1, v1, q2, k2, v2, o_ref, acc_s, m_s, l_s, shuf_s, *, seq):
    step = pl.program_id(1)
    n1 = DIL_GROUPS[1][1]
    n2_steps = DIL_GROUPS[2][1] // G2_RES_PER_STEP

    row0 = lax.broadcasted_iota(jnp.int32, (QB, QB), 0)
    col0 = lax.broadcasted_iota(jnp.int32, (QB, QB), 1)
    dist_first = jnp.where(col0 <= row0, (row0 - col0).astype(_F32), jnp.inf)
    row1 = lax.broadcasted_iota(jnp.int32, (QB, 2 * QB), 0)
    col1 = lax.broadcasted_iota(jnp.int32, (QB, 2 * QB), 1)
    d1 = QB + row1 - col1
    dist_next = jnp.where((d1 >= 0) & (d1 <= QB), d1.astype(_F32), jnp.inf)
    ones = jnp.ones((2 * QB, HEAD_DIM), _BF16)

    seg = seq // n1
    chunk = QB // n1
    assert n1 == 4 and DIL_GROUPS[2][1] == 16

    def blocks(specs):
        tiles = []
        for slot, (qr, kr, vr, grp, res, i, is_first_block) in enumerate(specs):
            dil = DIL_GROUPS[grp][1]
            for h in range(HEADS):
                hs = _head_cols(h)
                slope_step = SLOPES[grp * HEADS + h] * dil
                if is_first_block:
                    q, k, v = qr[0:QB, hs], kr[0:QB, hs], vr[0:QB, hs]
                    bias, one = dist_first * slope_step, ones[0:QB]
                else:
                    qs = pl.ds(pl.multiple_of(i * QB, QB), QB)
                    ks = pl.ds(pl.multiple_of(i * QB - QB, QB), 2 * QB)
                    q, k, v = qr[qs, hs], kr[ks, hs], vr[ks, hs]
                    bias, one = dist_next * slope_step, ones
                tiles.append((q, k, jnp.concatenate([v, one], axis=1), bias, grp, res, i, is_first_block, slot, h))
        scores = [lax.dot_general(t[0], t[1], (((1,), (1,)), ((), ())), preferred_element_type=_F32) for t in tiles]
        probs, maxes = [], []
        for t, s in zip(tiles, scores):
            s = s * ATT_SCALE - t[3]
            m = jnp.max(s, axis=1, keepdims=True)
            probs.append(jnp.exp(s - m).astype(_BF16))
            maxes.append(jnp.broadcast_to(m, (QB, HEAD_DIM)))
        outs = [jnp.dot(p, t[2], preferred_element_type=_F32) for t, p in zip(tiles, probs)]
        def merge(h, rows, acc, m, l):
            m_old, l_old, a_old = m_s[h, rows, :], l_s[h, rows, :], acc_s[h, rows, :]
            m_new = jnp.maximum(m_old, m)
            w_old = jnp.exp(m_old - m_new)
            w_new = jnp.exp(m - m_new)
            acc_s[h, rows, :] = a_old * w_old + acc * w_new
            l_s[h, rows, :] = l_old * w_old + l * w_new
            m_s[h, rows, :] = m_new

        for (_, _, _, _, grp, res, i, is_first_block, slot, h), m, acc_l in zip(tiles, maxes, outs):
            acc, l = acc_l[:, :HEAD_DIM], acc_l[:, HEAD_DIM:]
            if grp == 1:
                rows = pl.ds(pl.multiple_of(res * seg + i * QB, QB), QB)
                acc_s[h, rows, :], m_s[h, rows, :], l_s[h, rows, :] = acc, m, l
            elif grp == 2:
                merge(h, pl.ds((res & (n1 - 1)) * seg + (res >> 2), QB, stride=n1), acc, m, l)
            else:
                tmps = [shuf_s.at[(slot * HEADS + h) * 3 + a] for a in range(3)]
                for tmp, val in zip(tmps, (acc, m, l)):
                    tmp[...] = val
                for r4 in range(n1):
                    dst = r4 * seg + i * chunk
                    dst = pl.ds(dst if is_first_block else pl.multiple_of(dst, chunk), chunk)
                    merge(h, dst, *(tmp[pl.ds(r4, chunk, stride=n1), :] for tmp in tmps))

    @pl.when(step < n1)
    def _():
        blocks([(q1, k1, v1, 1, step, i, i == 0) for i in range(seg // QB)])

    @pl.when((step >= n1) & (step < n1 + n2_steps))
    def _():
        assert seq // DIL_GROUPS[2][1] == QB
        blocks([(q2.at[r], k2.at[r], v2.at[r], 2, (step - n1) * G2_RES_PER_STEP + r, 0, True)
                for r in range(G2_RES_PER_STEP)])

    @pl.when(step == n1 + n2_steps)
    def _():
        n_blk = seq // QB
        assert (n_blk - 1) % G0_UNROLL == 0
        blocks([(q0, k0, v0, 0, 0, 0, True)])

        def g0_body(it, carry):
            blocks([(q0, k0, v0, 0, 0, 1 + it * G0_UNROLL + u, False) for u in range(G0_UNROLL)])
            return carry
        lax.fori_loop(0, (n_blk - 1) // G0_UNROLL, g0_body, 0)

        def body(c, carry):
            for h in range(HEADS):
                tmp = shuf_s.at[h]
                for r4 in range(n1):
                    src = pl.ds(pl.multiple_of(r4 * seg + c * chunk, chunk), chunk)
                    tmp[pl.ds(r4, chunk, stride=n1), :] = acc_s[h, src, :] / l_s[h, src, :]
                o_ref[pl.ds(pl.multiple_of(c * QB, QB), QB), _head_cols(h)] = tmp[...].astype(_BF16)
            return carry
        lax.fori_loop(0, seq // QB, body, 0)


def _attn_prompt(nat, qkv1, qkv2, batch, seq):
    n1, n2 = DIL_GROUPS[1][1], DIL_GROUPS[2][1]
    n2_steps = n2 // G2_RES_PER_STEP
    nat3 = nat.reshape(batch, seq, NAT_TILES * COL_TILE)
    in_specs = [pl.BlockSpec((None, seq, COL_TILE), lambda b, s, t=t: (b, 0, Q_TILE[0] + t)) for t in range(3)]
    in_specs += [pl.BlockSpec((None, None, seq // n1, COL_TILE),
                              lambda b, s, t=t: (b, jnp.clip(s, 0, n1 - 1), 0, t)) for t in range(3)]
    in_specs += [pl.BlockSpec((None, G2_RES_PER_STEP, seq // n2, COL_TILE),
                              lambda b, s, t=t: (b, jnp.clip(s - n1, 0, n2_steps - 1), 0, t)) for t in range(3)]
    args = [nat3] * 3 + [qkv1] * 3 + [qkv2] * 3
    return pl.pallas_call(
        functools.partial(_attn_prompt_kernel, seq=seq),
        grid=(batch, 1 + n1 + n2_steps),
        in_specs=in_specs,
        out_specs=pl.BlockSpec((None, seq, ATT_WIDTH), lambda b, s: (b, 0, 0)),
        out_shape=jax.ShapeDtypeStruct((batch, seq, ATT_WIDTH), _BF16),
        scratch_shapes=[pltpu.VMEM((HEADS, seq, HEAD_DIM), _F32)] * 3
                       + [pltpu.VMEM((G0_UNROLL * HEADS * 3, QB, HEAD_DIM), _F32)],
        compiler_params=pltpu.CompilerParams(dimension_semantics=("parallel", "arbitrary"),
                                             vmem_limit_bytes=VMEM_LIMIT),
        name="attn_prompt",
    )(*args)


def _attn_sample_kernel(*refs, tdec, aliased):
    n_in = 10 if aliased else 7
    proj_ref, n0, n1, n2, c0, c1, c2 = refs[:7]
    yb_ref, o0, o1, o2 = refs[n_in:]
    news, caches, outs = (n0, n1, n2), (c0, c1, c2), (o0, o1, o2)
    for h in range(HEADS):
        pieces = []
        for grp, (win, dil) in enumerate(DIL_GROUPS):
            slope = SLOPES[grp * HEADS + h]
            q = proj_ref[:, _head_cols(h, Q_TILE[grp] * COL_TILE)]
            k_new = news[grp][:, _head_cols(h)]
            v_new = news[grp][:, _head_cols(h, ATT_WIDTH)]
            k_old = caches[grp][pl.ds(h, win, stride=KV_ROWS), :]
            v_old = caches[grp][pl.ds(HEADS + h, win, stride=KV_ROWS), :]
            for k, v, n_keys, base in ((k_old, v_old, win, win), (k_new, v_new, tdec, 0)):
                s = lax.dot_general(q, k.astype(_BF16), (((1,), (1,)), ((), ())), preferred_element_type=_F32)
                t = lax.broadcasted_iota(jnp.int32, (tdec, n_keys), 0)
                c = lax.broadcasted_iota(jnp.int32, (tdec, n_keys), 1)
                dist = base + t - c
                ok = (dist >= 0) & (dist <= win) & ((dist & (dil - 1)) == 0)
                s = jnp.where(ok, s * ATT_SCALE - slope * dist.astype(_F32), -jnp.inf)
                pieces.append((s, v.astype(_BF16)))
        m = functools.reduce(jnp.maximum, [jnp.max(s, axis=1, keepdims=True) for s, _ in pieces])
        l = jnp.zeros((tdec, 1), _F32)
        acc = jnp.zeros((tdec, HEAD_DIM), _F32)
        for s, v in pieces:
            p = jnp.exp(s - m)
            l = l + jnp.sum(p, axis=1, keepdims=True)
            acc = acc + jnp.dot(p.astype(_BF16), v, preferred_element_type=_F32)
        yb_ref[:, _head_cols(h)] = (acc / l).astype(_BF16)
    for grp in range(len(DIL_GROUPS)):
        for part in range(2):
            for h in range(HEADS):
                outs[grp][pl.ds(part * HEADS + h, tdec, stride=KV_ROWS), :] = (
                    news[grp][:, _head_cols(h, part * ATT_WIDTH)])


def _attn_sample(proj, kv_new, caches, batch, tdec, layer, prev_out):
    aliased = prev_out is not None
    in_specs = [pl.BlockSpec((None, tdec, IN_COLS), lambda b: (b, 0, 0))]
    in_specs += [pl.BlockSpec((None, tdec, 2 * ATT_WIDTH), lambda b: (b, 0, 0))] * 3
    cache_specs = [pl.BlockSpec((None, None, win * KV_ROWS, HEAD_DIM), lambda b: (layer, b, 0, 0)) for win, _ in DIL_GROUPS]
    in_specs += cache_specs
    args = [proj.reshape(batch, tdec, IN_COLS)] + [k.reshape(batch, tdec, 2 * ATT_WIDTH) for k in kv_new] + list(caches)
    if aliased:
        in_specs += [pl.BlockSpec(memory_space=pl.ANY)] * 3
        args += list(prev_out)
    out_specs = [pl.BlockSpec((None, tdec, ATT_WIDTH), lambda b: (b, 0, 0))]
    out_specs += [pl.BlockSpec((None, None, tdec * KV_ROWS, HEAD_DIM), lambda b, win=win: (layer, b, win // tdec - 1, 0))
                  for win, _ in DIL_GROUPS]
    out_shape = [jax.ShapeDtypeStruct((batch, tdec, ATT_WIDTH), _BF16)]
    out_shape += [jax.ShapeDtypeStruct((DEPTH, batch, win * KV_ROWS, HEAD_DIM), _F32) for win, _ in DIL_GROUPS]
    return pl.pallas_call(
        functools.partial(_attn_sample_kernel, tdec=tdec, aliased=aliased),
        grid=(batch,),
        in_specs=in_specs, out_specs=out_specs, out_shape=out_shape,
        input_output_aliases={7: 1, 8: 2, 9: 3} if aliased else {},
        compiler_params=pltpu.CompilerParams(dimension_semantics=("arbitrary",), vmem_limit_bytes=VMEM_LIMIT),
        name="attn_sample",
    )(*args)


def _mixer_kernel(*refs, nb, tt, start, from_zero):
    if from_zero:
        pa_ref, pu_ref, yb_ref, x_ref, cw_ref, pw_ref, psc_ref, wo_ref = refs[:8]
    else:
        pa_ref, pu_ref, yb_ref, x_ref, cs0_ref, ps0_ref, cw_ref, pw_ref, psc_ref, wo_ref = refs[:10]
    out_ref, cs_out, ps_out, ext_a, ext_u, yc = refs[-6:]
    ti = pl.program_id(1)
    rows = nb * tt
    pool_halo = POOL_BLOCK if from_zero else POOL_HALO

    @pl.when(ti == 0)
    def _():
        if from_zero:
            ext_a[:, 0:CONV_HALO, :] = jnp.zeros((nb, CONV_HALO, CONV_WIDTH), _F32)
            ext_u[:, 0:pool_halo, :] = jnp.zeros((nb, pool_halo, POOL_WIDTH), _BF16)
        else:
            ext_a[:, 0:CONV_HALO, :] = cs0_ref[...]
            ext_u[:, 0:pool_halo, :] = ps0_ref[...]

    yb = yb_ref[...].astype(_F32).reshape(rows, ATT_WIDTH).astype(_BF16)
    mixed = jnp.dot(yb, wo_ref[CONV_WIDTH:CONV_WIDTH + ATT_WIDTH, :], preferred_element_type=_F32)

    pa = pa_ref[...].astype(_F32)
    xa, gate_b, gate_c = (pa[:, :, k * CONV_WIDTH:(k + 1) * CONV_WIDTH] for k in range(3))
    prod = gate_c * xa
    ext_a[:, CONV_HALO:CONV_HALO + tt, :] = prod
    cw = cw_ref[...]
    cu = (cw[0] * ext_a[:, CONV_HALO - 2:CONV_HALO - 2 + tt, :]
          + cw[1] * ext_a[:, CONV_HALO - 1:CONV_HALO - 1 + tt, :] + cw[2] * prod)
    ya = (gate_b * cu).reshape(rows, CONV_WIDTH).astype(_BF16)
    mixed = mixed + jnp.dot(ya, wo_ref[0:CONV_WIDTH, :], preferred_element_type=_F32)

    ext_u[:, pool_halo:pool_halo + tt, :] = pu_ref[...].astype(ext_u.dtype)
    if from_zero:
        pb = POOL_BLOCK
        row = lax.broadcasted_iota(jnp.int32, (pb, 2 * pb), 0)
        col = lax.broadcasted_iota(jnp.int32, (pb, 2 * pb), 1)
        lag = pb + row - col
        for gi, w in enumerate(POOL_WINDOWS):
            cols = slice(gi * POOL_GROUP, (gi + 1) * POOL_GROUP)
            band = ((lag >= 0) & (lag < w)).astype(_F32).astype(_BF16)
            for r in range(tt // pb):
                tot = jnp.dot(band, ext_u[0, r * pb:(r + 2) * pb, cols], preferred_element_type=_F32)
                pos = start + ti * tt + r * pb + lax.broadcasted_iota(jnp.int32, (pb, 1), 0)
                cnt = jnp.minimum(w, pos + 1).astype(_F32)
                ug = pu_ref[0, r * pb:(r + 1) * pb, cols].astype(_F32)
                diff = (tot / cnt - ug).astype(_BF16)
                z = jnp.dot(diff, pw_ref[gi], preferred_element_type=_F32) * psc_ref[:, cols]
                yc[r * pb:(r + 1) * pb, cols] = z.astype(_BF16)
    else:
        u = pu_ref[...].astype(_F32)
        pos = start + ti * tt + lax.broadcasted_iota(jnp.int32, (1, tt, 1), 1)
        for gi, w in enumerate(POOL_WINDOWS):
            cols = slice(gi * POOL_GROUP, (gi + 1) * POOL_GROUP)
            ug = u[:, :, cols]
            tot = ug
            for k in range(1, w):
                tot = tot + ext_u[:, pool_halo - k:pool_halo - k + tt, cols]
            cnt = jnp.minimum(w, pos + 1).astype(_F32)
            diff = (tot / cnt - ug).reshape(rows, POOL_GROUP).astype(_BF16)
            z = jnp.dot(diff, pw_ref[gi], preferred_element_type=_F32) * psc_ref[:, cols]
            yc[:, cols] = z.astype(_BF16)

    mixed = mixed + jnp.dot(yc[...], wo_ref[CONV_WIDTH + ATT_WIDTH:, :], preferred_element_type=_F32)
    out_ref[...] = x_ref[...] + mixed.reshape(nb, tt, D_MODEL)

    tail_a = ext_a[:, tt:tt + CONV_HALO, :]
    ext_a[:, 0:CONV_HALO, :] = tail_a
    cs_out[...] = tail_a
    ps_out[...] = ext_u[:, pool_halo + tt - POOL_HALO:pool_halo + tt, :].astype(_F32)
    ext_u[:, 0:pool_halo, :] = ext_u[:, tt:tt + pool_halo, :]


def _mixer(proj, yb, x, conv_state, pool_state, conv_w, pool_w, pool_scale, w_out, layer, nb, tt, start, u_tile0):
    batch, seq, _ = x.shape
    proj3 = proj.reshape(batch, seq, proj.shape[-1])
    u_blk = u_tile0 * COL_TILE // POOL_WIDTH
    const2 = lambda b, t: (0, 0)
    from_zero = conv_state is None and pool_state is None
    assert from_zero or (conv_state is not None and pool_state is not None)
    assert not from_zero or (nb == 1 and start == 0 and tt % POOL_BLOCK == 0)
    state_specs = [] if from_zero else [pl.BlockSpec((nb, CONV_HALO, CONV_WIDTH), lambda b, t: (b, 0, 0)),
                                        pl.BlockSpec((nb, POOL_HALO, POOL_WIDTH), lambda b, t: (b, 0, 0))]
    state_args = [] if from_zero else [conv_state, pool_state]
    ext_u = (pltpu.VMEM((nb, tt + POOL_BLOCK, POOL_WIDTH), _BF16) if from_zero
             else pltpu.VMEM((nb, tt + POOL_HALO, POOL_WIDTH), _F32))
    return pl.pallas_call(
        functools.partial(_mixer_kernel, nb=nb, tt=tt, start=start, from_zero=from_zero),
        grid=(batch // nb, seq // tt),
        in_specs=[pl.BlockSpec((nb, tt, 3 * CONV_WIDTH), lambda b, t: (b, t, 0)),
                  pl.BlockSpec((nb, tt, POOL_WIDTH), lambda b, t: (b, t, u_blk)),
                  pl.BlockSpec((nb, tt, ATT_WIDTH), lambda b, t: (b, t, 0)),
                  pl.BlockSpec((nb, tt, D_MODEL), lambda b, t: (b, t, 0))] + state_specs + [
                  pl.BlockSpec((3, CONV_WIDTH), const2),
                  pl.BlockSpec((None, len(POOL_WINDOWS), POOL_GROUP, POOL_GROUP), lambda b, t: (layer, 0, 0, 0)),
                  pl.BlockSpec((1, POOL_WIDTH), const2),
                  pl.BlockSpec((None, D_MODEL, D_MODEL), lambda b, t: (layer, 0, 0))],
        out_specs=[pl.BlockSpec((nb, tt, D_MODEL), lambda b, t: (b, t, 0)),
                   pl.BlockSpec((nb, CONV_HALO, CONV_WIDTH), lambda b, t: (b, 0, 0)),
                   pl.BlockSpec((nb, POOL_HALO, POOL_WIDTH), lambda b, t: (b, 0, 0))],
        out_shape=[jax.ShapeDtypeStruct((batch, seq, D_MODEL), _F32),
                   jax.ShapeDtypeStruct((batch, CONV_HALO, CONV_WIDTH), _F32),
                   jax.ShapeDtypeStruct((batch, POOL_HALO, POOL_WIDTH), _F32)],
        scratch_shapes=[pltpu.VMEM((nb, tt + CONV_HALO, CONV_WIDTH), _F32), ext_u,
                        pltpu.VMEM((nb * tt, POOL_WIDTH), _BF16)],
        compiler_params=pltpu.CompilerParams(dimension_semantics=("parallel", "arbitrary"),
                                             vmem_limit_bytes=VMEM_LIMIT),
        name="mixer",
    )(proj3, proj3, yb, x, *state_args, conv_w, pool_w, pool_scale.reshape(1, POOL_WIDTH), w_out)


def _ffn_sample_kernel(x_ref, n2_ref, wg_ref, wu_ref, cw_ref, wd_ref, st0_ref, fn_ref,
                       out_ref, st_out, wgt_ref, wut_ref, wdt_ref, h_s, ext_s, *, nb, tt, final):
    f = pl.program_id(0)
    rows = nb * tt

    @pl.when(f == 0)
    def _():
        x = x_ref[...]
        h_s[...] = _rms_norm(x, n2_ref[...]).reshape(rows, D_MODEL).astype(_BF16)
        out_ref[...] = x

    def tile(width):
        cols = slice(0, width)
        wg, wu, wd = wg_ref[:, cols].astype(_BF16), wu_ref[:, cols].astype(_BF16), wd_ref[cols, :].astype(_BF16)
        wgt_ref[:, cols], wut_ref[:, cols], wdt_ref[cols, :] = wg, wu, wd
        if width < FF_TILE:
            wgt_ref[:, width:] = jnp.zeros((D_MODEL, FF_TILE - width), _BF16)
            wut_ref[:, width:] = jnp.zeros((D_MODEL, FF_TILE - width), _BF16)
            wdt_ref[width:, :] = jnp.zeros((FF_TILE - width, D_MODEL), _BF16)
        h = h_s[...]
        gate = jnp.dot(h, wg, preferred_element_type=_F32).reshape(nb, tt, width)
        up = jnp.dot(h, wu, preferred_element_type=_F32).reshape(nb, tt, width)
        ext_s[:, 0:CONV_HALO, cols] = st0_ref[:, :, cols]
        ext_s[:, CONV_HALO:CONV_HALO + tt, cols] = gate
        cw = cw_ref[:, cols]
        gc = (cw[0] * ext_s[:, CONV_HALO - 2:CONV_HALO - 2 + tt, cols]
              + cw[1] * ext_s[:, CONV_HALO - 1:CONV_HALO - 1 + tt, cols] + cw[2] * gate)
        st_out[:, :, cols] = ext_s[:, tt:tt + CONV_HALO, cols]
        act = (jax.nn.silu(gc) * up).reshape(rows, width).astype(_BF16)
        out_ref[...] += jnp.dot(act, wd, preferred_element_type=_F32).reshape(nb, tt, D_MODEL)

    @pl.when(f < FF_TILES - 1)
    def _():
        tile(FF_TILE)

    @pl.when(f == FF_TILES - 1)
    def _():
        tile(FF_LAST)
        if final:
            out_ref[...] = _rms_norm(out_ref[...], fn_ref[...])


def _ffn_prompt_kernel(x_ref, n2_ref, wg_ref, wu_ref, cw_ref, wd_ref, fn_ref, c0, c1, c2, _b0, _b1, _b2,
                       out_ref, st_out, nc0, nc1, nc2, h_s, ext_s, carry_s, shift_sem, *, tt, final, layer, tdec):
    ti = pl.program_id(1)
    f = pl.program_id(2)
    first_step = (pl.program_id(0) == 0) & (ti == 0) & (f == 0)
    last_step = ((pl.program_id(0) == pl.num_programs(0) - 1) & (ti == pl.num_programs(1) - 1)
                 & (f == FF_TILES - 1))

    def shift_copy(grp):
        n = (DIL_GROUPS[grp][0] - tdec) * KV_ROWS
        src = (c0, c1, c2)[grp].at[layer, :, pl.ds(tdec * KV_ROWS, n), :]
        dst = (nc0, nc1, nc2)[grp].at[layer, :, pl.ds(0, n), :]
        return pltpu.make_async_copy(src, dst, shift_sem.at[grp])

    @pl.when(first_step)
    def _():
        for grp in range(len(DIL_GROUPS)):
            shift_copy(grp).start()

    def tile(first, last):
        if first:
            @pl.when(ti == 0)
            def _():
                carry_s[...] = jnp.zeros(carry_s.shape, _F32)
            x = x_ref[...]
            h_s[...] = _rms_norm(x, n2_ref[...]).astype(_BF16)
            out_ref[...] = x
        ext_s[0:CONV_HALO, :] = carry_s[f]

        wg, wu, wd, cw = wg_ref[...], wu_ref[...], wd_ref[...], cw_ref[f]
        chunks = [slice(r0, r0 + FFN_SUB_ROWS) for r0 in range(0, tt, FFN_SUB_ROWS)]
        for rs in chunks:
            gate = jnp.dot(h_s[rs, :], wg, preferred_element_type=_F32)
            ext_s[CONV_HALO + rs.start:CONV_HALO + rs.stop, :] = gate
        ups = [jnp.dot(h_s[rs, :], wu, preferred_element_type=_F32) for rs in chunks]
        for rs, up in zip(chunks, ups):
            gc = (cw[0] * ext_s[CONV_HALO - 2 + rs.start:CONV_HALO - 2 + rs.stop, :]
                  + cw[1] * ext_s[CONV_HALO - 1 + rs.start:CONV_HALO - 1 + rs.stop, :]
                  + cw[2] * ext_s[CONV_HALO + rs.start:CONV_HALO + rs.stop, :])
            act = (jax.nn.silu(gc) * up).astype(_BF16)
            out_ref[rs, :] += jnp.dot(act, wd, preferred_element_type=_F32)
        tail = ext_s[tt:tt + CONV_HALO, :]
        carry_s[f] = tail
        st_out[f] = tail
        if last:
            out_ref[...] = _rms_norm(out_ref[...], fn_ref[...])

    @pl.when(f == 0)
    def _():
        tile(True, False)

    if final:
        @pl.when((f > 0) & (f < FF_TILES - 1))
        def _():
            tile(False, False)

        @pl.when(f == FF_TILES - 1)
        def _():
            tile(False, True)
    else:
        @pl.when(f > 0)
        def _():
            tile(False, False)

    @pl.when(last_step)
    def _():
        for grp in range(len(DIL_GROUPS)):
            shift_copy(grp).wait()


def _ffn_sample(x, norm2, w_gate, w_up, conv_w, w_down, state, final_norm, layer, final):
    nb, tt, _ = x.shape
    vec = lambda f: (0, 0)
    return pl.pallas_call(
        functools.partial(_ffn_sample_kernel, nb=nb, tt=tt, final=final),
        grid=(FF_TILES,),
        in_specs=[pl.BlockSpec((nb, tt, D_MODEL), lambda f: (0, 0, 0)),
                  pl.BlockSpec((1, D_MODEL), vec),
                  pl.BlockSpec((None, D_MODEL, FF_TILE), lambda f: (layer, 0, f)),
                  pl.BlockSpec((None, D_MODEL, FF_TILE), lambda f: (layer, 0, f)),
                  pl.BlockSpec((3, FF_TILE), lambda f: (0, f)),
                  pl.BlockSpec((None, FF_TILE, D_MODEL), lambda f: (layer, f, 0)),
                  pl.BlockSpec((nb, CONV_HALO, FF_TILE), lambda f: (0, 0, f)),
                  pl.BlockSpec((1, D_MODEL), vec)],
        out_specs=[pl.BlockSpec((nb, tt, D_MODEL), lambda f: (0, 0, 0)),
                   pl.BlockSpec((nb, CONV_HALO, FF_TILE), lambda f: (0, 0, f)),
                   pl.BlockSpec((None, D_MODEL, FF_TILE), lambda f: (f, 0, 0)),
                   pl.BlockSpec((None, D_MODEL, FF_TILE), lambda f: (f, 0, 0)),
                   pl.BlockSpec((None, FF_TILE, D_MODEL), lambda f: (f, 0, 0))],
        out_shape=[jax.ShapeDtypeStruct((nb, tt, D_MODEL), _F32),
                   jax.ShapeDtypeStruct((nb, CONV_HALO, D_FF), _F32),
                   jax.ShapeDtypeStruct((FF_TILES, D_MODEL, FF_TILE), _BF16),
                   jax.ShapeDtypeStruct((FF_TILES, D_MODEL, FF_TILE), _BF16),
                   jax.ShapeDtypeStruct((FF_TILES, FF_TILE, D_MODEL), _BF16)],
        scratch_shapes=[pltpu.VMEM((nb * tt, D_MODEL), _BF16),
                        pltpu.VMEM((nb, tt + CONV_HALO, FF_TILE), _F32)],
        compiler_params=pltpu.CompilerParams(dimension_semantics=("arbitrary",), vmem_limit_bytes=VMEM_LIMIT),
        name="ffn_sample",
    )(x, norm2.reshape(1, D_MODEL), w_gate, w_up, conv_w, w_down, state, final_norm.reshape(1, D_MODEL))


def _ffn_prompt(x, norm2, w_gate_t, w_up_t, conv_w, w_down_t, final_norm, caches, new_caches, layer, tdec, tt, final):
    batch, seq, _ = x.shape
    vec = lambda b, t, f: (0, 0)
    any_spec = pl.BlockSpec(memory_space=pl.ANY)
    return pl.pallas_call(
        functools.partial(_ffn_prompt_kernel, tt=tt, final=final, layer=layer, tdec=tdec),
        grid=(batch, seq // tt, FF_TILES),
        in_specs=[pl.BlockSpec((None, tt, D_MODEL), lambda b, t, f: (b, t, 0)),
                  pl.BlockSpec((1, D_MODEL), vec),
                  pl.BlockSpec((None, D_MODEL, FF_TILE), lambda b, t, f: (f, 0, 0)),
                  pl.BlockSpec((None, D_MODEL, FF_TILE), lambda b, t, f: (f, 0, 0)),
                  pl.BlockSpec((FF_TILES, 3, FF_TILE), lambda b, t, f: (0, 0, 0)),
                  pl.BlockSpec((None, FF_TILE, D_MODEL), lambda b, t, f: (f, 0, 0)),
                  pl.BlockSpec((1, D_MODEL), vec)] + [any_spec] * 6,
        out_specs=[pl.BlockSpec((None, tt, D_MODEL), lambda b, t, f: (b, t, 0)),
                   pl.BlockSpec((None, None, FF_TILES, CONV_HALO, FF_TILE), lambda b, t, f: (b, t, 0, 0, 0))]
                  + [any_spec] * 3,
        out_shape=[jax.ShapeDtypeStruct((batch, seq, D_MODEL), _F32),
                   jax.ShapeDtypeStruct((batch, seq // tt, FF_TILES, CONV_HALO, FF_TILE), _F32)]
                  + [jax.ShapeDtypeStruct(c.shape, c.dtype) for c in new_caches],
        scratch_shapes=[pltpu.VMEM((tt, D_MODEL), _BF16),
                        pltpu.VMEM((tt + CONV_HALO, FF_TILE), _F32),
                        pltpu.VMEM((FF_TILES, CONV_HALO, FF_TILE), _F32),
                        pltpu.SemaphoreType.DMA((len(DIL_GROUPS),))],
        input_output_aliases={10: 2, 11: 3, 12: 4},
        compiler_params=pltpu.CompilerParams(dimension_semantics=("arbitrary", "arbitrary", "arbitrary"),
                                             vmem_limit_bytes=VMEM_LIMIT),
        name="ffn_prompt",
    )(x, norm2.reshape(1, D_MODEL), w_gate_t, w_up_t, conv_w, w_down_t, final_norm.reshape(1, D_MODEL),
      *caches, *new_caches)


def _pad_rows_front(a, rows):
    return jnp.pad(a, ((0, 0), (rows - a.shape[1], 0), (0, 0)))


def kernel(x_prompt, x_sample, cache_kv_w128, cache_kv_w512, cache_kv_w2048, state_conv_a, state_pool, state_ffn_conv, norm1, w_in, conv_a_w, w_out, pool_w, pool_scale, norm2, w_gate, w_up, ffn_conv_w, w_down, final_norm):
    bp, sp, _ = x_prompt.shape
    bs, ss, _ = x_sample.shape
    caches = tuple(c.reshape(DEPTH, bs, win * KV_ROWS, HEAD_DIM)
                   for c, (win, _) in zip((cache_kv_w128, cache_kv_w512, cache_kv_w2048), DIL_GROUPS))
    tm, tt = 1024, 512
    w_out_b, pool_w_b = w_out.astype(_BF16), pool_w.astype(_BF16)
    xp, xs = x_prompt, x_sample
    kv_p = kv_s = None
    small_p, small_s = [], []
    for l in range(DEPTH):
        final = l == DEPTH - 1

        proj, *kv_new, w_in_t = _inproj_sample(xs.reshape(bs * ss, D_MODEL), norm1[l], w_in, l)
        yb, *kv_s = _attn_sample(proj, kv_new, caches, bs, ss, l, kv_s)
        xs, conv_o, pool_o = _mixer(proj, yb, xs, _pad_rows_front(state_conv_a[l], CONV_HALO),
                                    _pad_rows_front(state_pool[l], POOL_HALO), conv_a_w[l], pool_w_b, pool_scale[l],
                                    w_out_b, l, bs, ss, PAST_LEN, U_TILE0)
        xs, ffn_o, w_gate_t, w_up_t, w_down_t = _ffn_sample(
            xs, norm2[l], w_gate, w_up, ffn_conv_w[l], w_down, _pad_rows_front(state_ffn_conv[l], CONV_HALO),
            final_norm, l, final)
        small_s.append((conv_o[:, CONV_HALO - 2:], pool_o[:, POOL_HALO - POOL_BUF:], ffn_o[:, CONV_HALO - 2:]))

        nat, qkv1, qkv2, *kv_p = _inproj_prompt(xp.reshape(bp * sp, D_MODEL), norm1[l], w_in_t, bp, sp, tm, l, kv_p)
        yb = _attn_prompt(nat, qkv1, qkv2, bp, sp)
        xp, conv_o, pool_o = _mixer(nat, yb, xp, None, None, conv_a_w[l], pool_w_b, pool_scale[l], w_out_b,
                                    l, 1, tt, 0, NAT_TILES - 2)
        conv_w_t = jnp.pad(ffn_conv_w[l], ((0, 0), (0, FF_TILES * FF_TILE - D_FF)))
        conv_w_t = conv_w_t.reshape(3, FF_TILES, FF_TILE).transpose(1, 0, 2)
        xp, ffn_o, *kv_s = _ffn_prompt(xp, norm2[l], w_gate_t, w_up_t, conv_w_t, w_down_t, final_norm,
                                       caches, kv_s, l, ss, tm, final)
        ffn_tail = ffn_o[:, -1, :, CONV_HALO - 2:].transpose(0, 2, 1, 3).reshape(bp, 2, FF_TILES * FF_TILE)
        small_p.append((conv_o[:, CONV_HALO - 2:], pool_o[:, POOL_HALO - POOL_BUF:], ffn_tail[:, :, :D_FF]))

    out = [xp, xs]
    for g, (win, _) in enumerate(DIL_GROUPS):
        out += [kv_p[g].reshape(DEPTH, bp, win, 2, HEADS, HEAD_DIM), kv_s[g].reshape(DEPTH, bs, win, 2, HEADS, HEAD_DIM)]
    for i in range(3):
        out += [jnp.stack([s[i] for s in small_p]), jnp.stack([s[i] for s in small_s])]
    return tuple(out)
```

```python
import functools

import numpy as np
import jax
import jax.numpy as jnp
from jax import lax
from jax.experimental import pallas as pl
from jax.experimental.pallas import tpu as pltpu

D_MODEL = 2048
DEPTH = 2
PAST_LEN = 16384
CONV_WIDTH = 512
ATT_WIDTH = 512
HEADS = 4
HEAD_DIM = 128
DIL_GROUPS = ((128, 1), (512, 4), (2048, 16))
POOL_WIDTH = 1024
POOL_WINDOWS = (2, 4, 8, 16)
POOL_GROUP = 256
POOL_BUF = 15
IN_COLS = 7168
D_FF = 5504
RMS_EPS = 1e-6
ATT_SCALE = HEAD_DIM ** -0.5

COL_TILE = 512
IN_TILES = IN_COLS // COL_TILE
Q_TILE = (3, 6, 9)
U_TILE0 = 12
NAT_TILES = 8
KV_ROWS = 2 * HEADS
SUB_ROWS = 512
IN_STEP_TILES = 2
REGROUP_STRIDE = 4
FF_TILE = 512
FF_TILES = -(-D_FF // FF_TILE)
FF_LAST = D_FF - (FF_TILES - 1) * FF_TILE
FFN_SUB_ROWS = 512
CONV_HALO = 8
POOL_HALO = 16
POOL_BLOCK = 128
PROMPT_ROW_TILE = 1024
MIXER_ROW_TILE = 512
V7X_VMEM_BYTES = 64 * 1024 * 1024
VMEM_LIMIT = V7X_VMEM_BYTES - 8 * 1024 * 1024
VMEM_LIMIT_HIGH = V7X_VMEM_BYTES - 3 * 1024 * 1024

_BF16 = jnp.bfloat16
_F32 = jnp.float32


def _alibi_slopes():
    h = np.arange(1, 3 * HEADS + 1, dtype=np.float32)
    return [float(v) for v in np.power(np.float32(2.0), -8.0 * h / (3 * HEADS))]


SLOPES = _alibi_slopes()


def _rms_norm(x, g):
    return (x * lax.rsqrt(jnp.mean(x * x, axis=-1, keepdims=True) + RMS_EPS)) * g


def _head_cols(h, base=0):
    return slice(base + h * HEAD_DIM, base + (h + 1) * HEAD_DIM)


def _inproj_sample_kernel(x_ref, g_ref, w_ref, proj_ref, kv0_ref, kv1_ref, kv2_ref, wt_ref, h_s):
    j = pl.program_id(1)

    @pl.when(j == 0)
    def _():
        h_s[...] = _rms_norm(x_ref[...], g_ref[...]).astype(_BF16)

    w = w_ref[...].astype(_BF16)
    wt_ref[...] = w
    acc = jnp.dot(h_s[...], w, preferred_element_type=_F32)
    proj_ref[...] = acc.astype(_BF16)
    for g, ref in enumerate((kv0_ref, kv1_ref, kv2_ref)):
        @pl.when((j == Q_TILE[g] + 1) | (j == Q_TILE[g] + 2))
        def _(ref=ref):
            ref[...] = acc


def _inproj_sample(x2d, g, w, layer):
    m = x2d.shape[0]

    def kv_spec(grp):
        return pl.BlockSpec((m, COL_TILE), lambda i, j: (i, jnp.clip(j - Q_TILE[grp] - 1, 0, 1)))

    return pl.pallas_call(
        _inproj_sample_kernel,
        grid=(1, IN_TILES),
        in_specs=[pl.BlockSpec((m, D_MODEL), lambda i, j: (i, 0)),
                  pl.BlockSpec((1, D_MODEL), lambda i, j: (0, 0)),
                  pl.BlockSpec((None, D_MODEL, COL_TILE), lambda i, j: (layer, 0, j))],
        out_specs=[pl.BlockSpec((m, COL_TILE), lambda i, j: (i, j)), kv_spec(0), kv_spec(1), kv_spec(2),
                   pl.BlockSpec((None, D_MODEL, COL_TILE), lambda i, j: (j, 0, 0))],
        out_shape=[jax.ShapeDtypeStruct((m, IN_COLS), _BF16)] +
                  [jax.ShapeDtypeStruct((m, 2 * ATT_WIDTH), _F32)] * 3 +
                  [jax.ShapeDtypeStruct((IN_TILES, D_MODEL, COL_TILE), _BF16)],
        scratch_shapes=[pltpu.VMEM((m, D_MODEL), _BF16)],
        compiler_params=pltpu.CompilerParams(dimension_semantics=("parallel", "arbitrary"),
                                             vmem_limit_bytes=VMEM_LIMIT),
        name="inproj_sample",
    )(x2d, g.reshape(1, D_MODEL), w)


def _inproj_prompt_kernel(*refs, tm, aliased):
    n_in = 6 if aliased else 3
    x_ref, g_ref, w_ref = refs[:3]
    nat_ref, g1_ref, g2_ref, kv0_ref, kv1_ref, kv2_ref, h_s, de_s, de2_s = refs[n_in:]
    kv_refs = (kv0_ref, kv1_ref, kv2_ref)
    res_refs = (None, g1_ref, g2_ref)
    j = pl.program_id(1)

    @pl.when(j == 0)
    def _():
        h_s[...] = _rms_norm(x_ref[...], g_ref[...]).astype(_BF16)

    def column_tiles(tiles):
        for c in range(tm // SUB_ROWS):
            r0 = c * SUB_ROWS
            for half, tile in enumerate(tiles):
                acc = jnp.dot(h_s[r0:r0 + SUB_ROWS, :], w_ref[half], preferred_element_type=_F32)
                grp = (tile - Q_TILE[0]) // 3 if Q_TILE[0] <= tile < U_TILE0 else None
                part = None if grp is None or tile == Q_TILE[grp] else tile - Q_TILE[grp] - 1
                dil = 1 if grp is None else DIL_GROUPS[grp][1]
                if dil == 1:
                    nat_ref[r0:r0 + SUB_ROWS, half * COL_TILE:(half + 1) * COL_TILE] = acc.astype(_BF16)
                else:
                    for h in range(HEADS):
                        de_s[c, half, h] = acc[:, _head_cols(h)]
                    n = SUB_ROWS // dil
                    col0 = (tile - Q_TILE[grp]) * COL_TILE
                    if dil == REGROUP_STRIDE:
                        for res in range(dil):
                            for h in range(HEADS):
                                res_refs[grp][res, c * n:(c + 1) * n, _head_cols(h, col0)] = (
                                    de_s[c, half, h, pl.ds(res, n, stride=dil), :].astype(_BF16))
                    else:
                        assert dil == REGROUP_STRIDE ** 2
                        q = SUB_ROWS // REGROUP_STRIDE
                        for h in range(HEADS):
                            for g in range(REGROUP_STRIDE):
                                de2_s[h, g * q:(g + 1) * q, :] = de_s[c, half, h, pl.ds(g, q, stride=REGROUP_STRIDE), :]
                        for g in range(REGROUP_STRIDE):
                            for r in range(REGROUP_STRIDE):
                                for h in range(HEADS):
                                    res_refs[grp][g + REGROUP_STRIDE * r, c * n:(c + 1) * n, _head_cols(h, col0)] = (
                                        de2_s[h, pl.ds(g * q + r, n, stride=REGROUP_STRIDE), :].astype(_BF16))
                if part is not None:
                    keep = min(DIL_GROUPS[grp][0], tm)
                    lo = max(r0, tm - keep)
                    n = r0 + SUB_ROWS - lo
                    if n > 0:
                        for h in range(HEADS):
                            dst = pl.ds((lo - (tm - keep)) * KV_ROWS + part * HEADS + h, n, stride=KV_ROWS)
                            kv_refs[grp][dst, :] = acc[lo - r0:lo - r0 + n, _head_cols(h)]

    for step in range(IN_TILES // IN_STEP_TILES):
        @pl.when(j == step)
        def _(step=step):
            column_tiles(tuple(range(step * IN_STEP_TILES, (step + 1) * IN_STEP_TILES)))


def _inproj_prompt(x2d, g, w, batch, seq, tm, layer, prev_kv):
    m = x2d.shape[0]
    tps = seq // tm
    aliased = prev_kv is not None
    (w0, _), (w1, d1), (w2, d2) = DIL_GROUPS
    assert w2 == seq and w0 <= tm and w1 <= tm and tm % SUB_ROWS == 0

    in_specs = [pl.BlockSpec((tm, D_MODEL), lambda i, j: (i, 0)),
                pl.BlockSpec((1, D_MODEL), lambda i, j: (0, 0)),
                pl.BlockSpec((IN_STEP_TILES, D_MODEL, COL_TILE), lambda i, j: (j, 0, 0))]
    args = [x2d, g.reshape(1, D_MODEL), w]
    if aliased:
        in_specs += [pl.BlockSpec(memory_space=pl.ANY)] * 3
        args += list(prev_kv)

    def res_spec(dil):
        return pl.BlockSpec((None, dil, tm // dil, 3 * ATT_WIDTH), lambda i, j: (i // tps, 0, i % tps, 0))

    assert Q_TILE[1] % IN_STEP_TILES == 0 and U_TILE0 % IN_STEP_TILES == 0 and IN_TILES - U_TILE0 == IN_STEP_TILES
    nat_lo, u_step = Q_TILE[1] // IN_STEP_TILES, U_TILE0 // IN_STEP_TILES
    out_specs = [
        pl.BlockSpec((tm, IN_STEP_TILES * COL_TILE),
                     lambda i, j: (i, jnp.clip(j, 0, nat_lo - 1) + jnp.clip(j - u_step + 1, 0, 1))),
        res_spec(d1), res_spec(d2),
        pl.BlockSpec((None, None, w0 * KV_ROWS, HEAD_DIM), lambda i, j: (layer, i // tps, 0, 0),
                     pipeline_mode=pl.Buffered(1)),
        pl.BlockSpec((None, None, w1 * KV_ROWS, HEAD_DIM), lambda i, j: (layer, i // tps, 0, 0),
                     pipeline_mode=pl.Buffered(1)),
        pl.BlockSpec((None, None, tm * KV_ROWS, HEAD_DIM), lambda i, j: (layer, i // tps, i % tps, 0)),
    ]
    out_shape = [
        jax.ShapeDtypeStruct((m, NAT_TILES * COL_TILE), _BF16),
        jax.ShapeDtypeStruct((batch, d1, seq // d1, 3 * ATT_WIDTH), _BF16),
        jax.ShapeDtypeStruct((batch, d2, seq // d2, 3 * ATT_WIDTH), _BF16),
    ] + [jax.ShapeDtypeStruct((DEPTH, batch, win * KV_ROWS, HEAD_DIM), _F32) for win, _ in DIL_GROUPS]
    return pl.pallas_call(
        functools.partial(_inproj_prompt_kernel, tm=tm, aliased=aliased),
        grid=(m // tm, IN_TILES // IN_STEP_TILES),
        in_specs=in_specs, out_specs=out_specs, out_shape=out_shape,
        scratch_shapes=[pltpu.VMEM((tm, D_MODEL), _BF16),
                        pltpu.VMEM((tm // SUB_ROWS, IN_STEP_TILES, HEADS, SUB_ROWS, HEAD_DIM), _F32),
                        pltpu.VMEM((HEADS, SUB_ROWS, HEAD_DIM), _F32)],
        input_output_aliases={3: 3, 4: 4, 5: 5} if aliased else {},
        compiler_params=pltpu.CompilerParams(dimension_semantics=("arbitrary", "arbitrary"),
                                             vmem_limit_bytes=VMEM_LIMIT_HIGH),
        name="inproj_prompt",
    )(*args)


QB = 128
G0_UNROLL = 3
G2_RES_PER_STEP = 8


def _attn_prompt_kernel(q0, k0, v0, q1, k1, v1, q2, k2, v2, o_ref, acc_s, m_s, l_s, shuf_s, *, seq):
    step = pl.program_id(1)
    n1 = DIL_GROUPS[1][1]
    n2_steps = DIL_GROUPS[2][1] // G2_RES_PER_STEP

    row0 = lax.broadcasted_iota(jnp.int32, (QB, QB), 0)
    col0 = lax.broadcasted_iota(jnp.int32, (QB, QB), 1)
    dist_first = jnp.where(col0 <= row0, (row0 - col0).astype(_F32), jnp.inf)
    row1 = lax.broadcasted_iota(jnp.int32, (QB, 2 * QB), 0)
    col1 = lax.broadcasted_iota(jnp.int32, (QB, 2 * QB), 1)
    d1 = QB + row1 - col1
    dist_next = jnp.where((d1 >= 0) & (d1 <= QB), d1.astype(_F32), jnp.inf)
    ones = jnp.ones((2 * QB, HEAD_DIM), _BF16)

    seg = seq // n1
    chunk = QB // n1
    assert n1 == 4 and DIL_GROUPS[2][1] == 16

    def blocks(specs):
        tiles = []
        for slot, (qr, kr, vr, grp, res, i, is_first_block) in enumerate(specs):
            dil = DIL_GROUPS[grp][1]
            for h in range(HEADS):
                hs = _head_cols(h)
                slope_step = SLOPES[grp * HEADS + h] * dil
                if is_first_block:
                    q, k, v = qr[0:QB, hs], kr[0:QB, hs], vr[0:QB, hs]
                    bias, one = dist_first * slope_step, ones[0:QB]
                else:
                    qs = pl.ds(pl.multiple_of(i * QB, QB), QB)
                    ks = pl.ds(pl.multiple_of(i * QB - QB, QB), 2 * QB)
                    q, k, v = qr[qs, hs], kr[ks, hs], vr[ks, hs]
                    bias, one = dist_next * slope_step, ones
                tiles.append((q, k, jnp.concatenate([v, one], axis=1), bias, grp, res, i, is_first_block, slot, h))
        scores = [lax.dot_general(t[0], t[1], (((1,), (1,)), ((), ())), preferred_element_type=_F32) for t in tiles]
        probs, maxes = [], []
        for t, s in zip(tiles, scores):
            s = s * ATT_SCALE - t[3]
            m = jnp.max(s, axis=1, keepdims=True)
            probs.append(jnp.exp(s - m).astype(_BF16))
            maxes.append(jnp.broadcast_to(m, (QB, HEAD_DIM)))
        outs = [jnp.dot(p, t[2], preferred_element_type=_F32) for t, p in zip(tiles, probs)]
        def merge(h, rows, acc, m, l):
            m_old, l_old, a_old = m_s[h, rows, :], l_s[h, rows, :], acc_s[h, rows, :]
            m_new = jnp.maximum(m_old, m)
            w_old = jnp.exp(m_old - m_new)
            w_new = jnp.exp(m - m_new)
            acc_s[h, rows, :] = a_old * w_old + acc * w_new
            l_s[h, rows, :] = l_old * w_old + l * w_new
            m_s[h, rows, :] = m_new

        for (_, _, _, _, grp, res, i, is_first_block, slot, h), m, acc_l in zip(tiles, maxes, outs):
            acc, l = acc_l[:, :HEAD_DIM], acc_l[:, HEAD_DIM:]
            if grp == 1:
                rows = pl.ds(pl.multiple_of(res * seg + i * QB, QB), QB)
                acc_s[h, rows, :], m_s[h, rows, :], l_s[h, rows, :] = acc, m, l
            elif grp == 2:
                merge(h, pl.ds((res & (n1 - 1)) * seg + (res >> 2), QB, stride=n1), acc, m, l)
            else:
                tmps = [shuf_s.at[(slot * HEADS + h) * 3 + a] for a in range(3)]
                for tmp, val in zip(tmps, (acc, m, l)):
                    tmp[...] = val
                for r4 in range(n1):
                    dst = r4 * seg + i * chunk
                    dst = pl.ds(dst if is_first_block else pl.multiple_of(dst, chunk), chunk)
                    merge(h, dst, *(tmp[pl.ds(r4, chunk, stride=n1), :] for tmp in tmps))

    @pl.when(step < n1)
    def _():
        blocks([(q1, k1, v1, 1, step, i, i == 0) for i in range(seg // QB)])

    @pl.when((step >= n1) & (step < n1 + n2_steps))
    def _():
        assert seq // DIL_GROUPS[2][1] == QB
        blocks([(q2.at[r], k2.at[r], v2.at[r], 2, (step - n1) * G2_RES_PER_STEP + r, 0, True)
                for r in range(G2_RES_PER_STEP)])

    @pl.when(step == n1 + n2_steps)
    def _():
        n_blk = seq // QB
        assert (n_blk - 1) % G0_UNROLL == 0
        blocks([(q0, k0, v0, 0, 0, 0, True)])

        def g0_body(it, carry):
            blocks([(q0, k0, v0, 0, 0, 1 + it * G0_UNROLL + u, False) for u in range(G0_UNROLL)])
            return carry
        lax.fori_loop(0, (n_blk - 1) // G0_UNROLL, g0_body, 0)

        def body(c, carry):
            for h in range(HEADS):
                tmp = shuf_s.at[h]
                for r4 in range(n1):
                    src = pl.ds(pl.multiple_of(r4 * seg + c * chunk, chunk), chunk)
                    tmp[pl.ds(r4, chunk, stride=n1), :] = acc_s[h, src, :] / l_s[h, src, :]
                o_ref[pl.ds(pl.multiple_of(c * QB, QB), QB), _head_cols(h)] = tmp[...].astype(_BF16)
            return carry
        lax.fori_loop(0, seq // QB, body, 0)


def _attn_prompt(nat, qkv1, qkv2, batch, seq):
    n1, n2 = DIL_GROUPS[1][1], DIL_GROUPS[2][1]
    n2_steps = n2 // G2_RES_PER_STEP
    nat3 = nat.reshape(batch, seq, NAT_TILES * COL_TILE)
    in_specs = [pl.BlockSpec((None, seq, COL_TILE), lambda b, s, t=t: (b, 0, Q_TILE[0] + t)) for t in range(3)]
    in_specs += [pl.BlockSpec((None, None, seq // n1, COL_TILE),
                              lambda b, s, t=t: (b, jnp.clip(s, 0, n1 - 1), 0, t)) for t in range(3)]
    in_specs += [pl.BlockSpec((None, G2_RES_PER_STEP, seq // n2, COL_TILE),
                              lambda b, s, t=t: (b, jnp.clip(s - n1, 0, n2_steps - 1), 0, t)) for t in range(3)]
    args = [nat3] * 3 + [qkv1] * 3 + [qkv2] * 3
    return pl.pallas_call(
        functools.partial(_attn_prompt_kernel, seq=seq),
        grid=(batch, 1 + n1 + n2_steps),
        in_specs=in_specs,
        out_specs=pl.BlockSpec((None, seq, ATT_WIDTH), lambda b, s: (b, 0, 0)),
        out_shape=jax.ShapeDtypeStruct((batch, seq, ATT_WIDTH), _BF16),
        scratch_shapes=[pltpu.VMEM((HEADS, seq, HEAD_DIM), _F32)] * 3
                       + [pltpu.VMEM((G0_UNROLL * HEADS * 3, QB, HEAD_DIM), _F32)],
        compiler_params=pltpu.CompilerParams(dimension_semantics=("parallel", "arbitrary"),
                                             vmem_limit_bytes=VMEM_LIMIT),
        name="attn_prompt",
    )(*args)


def _attn_sample_kernel(*refs, tdec, aliased):
    n_in = 10 if aliased else 7
    proj_ref, n0, n1, n2, c0, c1, c2 = refs[:7]
    yb_ref, o0, o1, o2 = refs[n_in:]
    news, caches, outs = (n0, n1, n2), (c0, c1, c2), (o0, o1, o2)
    for h in range(HEADS):
        pieces = []
        for grp, (win, dil) in enumerate(DIL_GROUPS):
            slope = SLOPES[grp * HEADS + h]
            q = proj_ref[:, _head_cols(h, Q_TILE[grp] * COL_TILE)]
            k_new = news[grp][:, _head_cols(h)]
            v_new = news[grp][:, _head_cols(h, ATT_WIDTH)]
            k_old = caches[grp][pl.ds(h, win, stride=KV_ROWS), :]
            v_old = caches[grp][pl.ds(HEADS + h, win, stride=KV_ROWS), :]
            for k, v, n_keys, base in ((k_old, v_old, win, win), (k_new, v_new, tdec, 0)):
                s = lax.dot_general(q, k.astype(_BF16), (((1,), (1,)), ((), ())), preferred_element_type=_F32)
                t = lax.broadcasted_iota(jnp.int32, (tdec, n_keys), 0)
                c = lax.broadcasted_iota(jnp.int32, (tdec, n_keys), 1)
                dist = base + t - c
                ok = (dist >= 0) & (dist <= win) & ((dist & (dil - 1)) == 0)
                s = jnp.where(ok, s * ATT_SCALE - slope * dist.astype(_F32), -jnp.inf)
                pieces.append((s, v.astype(_BF16)))
        m = functools.reduce(jnp.maximum, [jnp.max(s, axis=1, keepdims=True) for s, _ in pieces])
        l = jnp.zeros((tdec, 1), _F32)
        acc = jnp.zeros((tdec, HEAD_DIM), _F32)
        for s, v in pieces:
            p = jnp.exp(s - m)
            l = l + jnp.sum(p, axis=1, keepdims=True)
            acc = acc + jnp.dot(p.astype(_BF16), v, preferred_element_type=_F32)
        yb_ref[:, _head_cols(h)] = (acc / l).astype(_BF16)
    for grp, (win, _) in enumerate(DIL_GROUPS):
        kept = (win - tdec) * KV_ROWS
        outs[grp][0:kept, :] = caches[grp][tdec * KV_ROWS:win * KV_ROWS, :]
        for part in range(2):
            for h in range(HEADS):
                outs[grp][pl.ds(kept + part * HEADS + h, tdec, stride=KV_ROWS), :] = (
                    news[grp][:, _head_cols(h, part * ATT_WIDTH)])


def _attn_sample(proj, kv_new, caches, batch, tdec, layer, prev_out):
    aliased = prev_out is not None
    in_specs = [pl.BlockSpec((None, tdec, IN_COLS), lambda b: (b, 0, 0))]
    in_specs += [pl.BlockSpec((None, tdec, 2 * ATT_WIDTH), lambda b: (b, 0, 0))] * 3
    cache_specs = [pl.BlockSpec((None, None, win * KV_ROWS, HEAD_DIM), lambda b: (layer, b, 0, 0)) for win, _ in DIL_GROUPS]
    in_specs += cache_specs
    args = [proj.reshape(batch, tdec, IN_COLS)] + [k.reshape(batch, tdec, 2 * ATT_WIDTH) for k in kv_new] + list(caches)
    if aliased:
        in_specs += [pl.BlockSpec(memory_space=pl.ANY)] * 3
        args += list(prev_out)
    out_specs = [pl.BlockSpec((None, tdec, ATT_WIDTH), lambda b: (b, 0, 0))] + cache_specs
    out_shape = [jax.ShapeDtypeStruct((batch, tdec, ATT_WIDTH), _BF16)]
    out_shape += [jax.ShapeDtypeStruct((DEPTH, batch, win * KV_ROWS, HEAD_DIM), _F32) for win, _ in DIL_GROUPS]
    return pl.pallas_call(
        functools.partial(_attn_sample_kernel, tdec=tdec, aliased=aliased),
        grid=(batch,),
        in_specs=in_specs, out_specs=out_specs, out_shape=out_shape,
        input_output_aliases={7: 1, 8: 2, 9: 3} if aliased else {},
        compiler_params=pltpu.CompilerParams(dimension_semantics=("arbitrary",), vmem_limit_bytes=VMEM_LIMIT),
        name="attn_sample",
    )(*args)


def _mixer_kernel(*refs, nb, tt, start, from_zero):
    if from_zero:
        pa_ref, pu_ref, yb_ref, x_ref, cw_ref, pw_ref, psc_ref, wo_ref = refs[:8]
    else:
        pa_ref, pu_ref, yb_ref, x_ref, cs0_ref, ps0_ref, cw_ref, pw_ref, psc_ref, wo_ref = refs[:10]
    out_ref, cs_out, ps_out, ext_a, ext_u, yc = refs[-6:]
    ti = pl.program_id(1)
    rows = nb * tt
    pool_halo = POOL_BLOCK if from_zero else POOL_HALO

    @pl.when(ti == 0)
    def _():
        if from_zero:
            ext_a[:, 0:CONV_HALO, :] = jnp.zeros((nb, CONV_HALO, CONV_WIDTH), _F32)
            ext_u[:, 0:pool_halo, :] = jnp.zeros((nb, pool_halo, POOL_WIDTH), _BF16)
        else:
            ext_a[:, 0:CONV_HALO, :] = cs0_ref[...]
            ext_u[:, 0:pool_halo, :] = ps0_ref[...]

    yb = yb_ref[...].astype(_F32).reshape(rows, ATT_WIDTH).astype(_BF16)
    mixed = jnp.dot(yb, wo_ref[CONV_WIDTH:CONV_WIDTH + ATT_WIDTH, :], preferred_element_type=_F32)

    pa = pa_ref[...].astype(_F32)
    xa, gate_b, gate_c = (pa[:, :, k * CONV_WIDTH:(k + 1) * CONV_WIDTH] for k in range(3))
    prod = gate_c * xa
    ext_a[:, CONV_HALO:CONV_HALO + tt, :] = prod
    cw = cw_ref[...]
    cu = (cw[0] * ext_a[:, CONV_HALO - 2:CONV_HALO - 2 + tt, :]
          + cw[1] * ext_a[:, CONV_HALO - 1:CONV_HALO - 1 + tt, :] + cw[2] * prod)
    ya = (gate_b * cu).reshape(rows, CONV_WIDTH).astype(_BF16)
    mixed = mixed + jnp.dot(ya, wo_ref[0:CONV_WIDTH, :], preferred_element_type=_F32)

    ext_u[:, pool_halo:pool_halo + tt, :] = pu_ref[...].astype(ext_u.dtype)
    if from_zero:
        pb = POOL_BLOCK
        row = lax.broadcasted_iota(jnp.int32, (pb, 2 * pb), 0)
        col = lax.broadcasted_iota(jnp.int32, (pb, 2 * pb), 1)
        lag = pb + row - col
        for gi, w in enumerate(POOL_WINDOWS):
            cols = slice(gi * POOL_GROUP, (gi + 1) * POOL_GROUP)
            band = ((lag >= 0) & (lag < w)).astype(_F32).astype(_BF16)
            for r in range(tt // pb):
                tot = jnp.dot(band, ext_u[0, r * pb:(r + 2) * pb, cols], preferred_element_type=_F32)
                pos = start + ti * tt + r * pb + lax.broadcasted_iota(jnp.int32, (pb, 1), 0)
                cnt = jnp.minimum(w, pos + 1).astype(_F32)
                ug = pu_ref[0, r * pb:(r + 1) * pb, cols].astype(_F32)
                diff = (tot / cnt - ug).astype(_BF16)
                z = jnp.dot(diff, pw_ref[gi], preferred_element_type=_F32) * psc_ref[:, cols]
                yc[r * pb:(r + 1) * pb, cols] = z.astype(_BF16)
    else:
        u = pu_ref[...].astype(_F32)
        pos = start + ti * tt + lax.broadcasted_iota(jnp.int32, (1, tt, 1), 1)
        for gi, w in enumerate(POOL_WINDOWS):
            cols = slice(gi * POOL_GROUP, (gi + 1) * POOL_GROUP)
            ug = u[:, :, cols]
            tot = ug
            for k in range(1, w):
                tot = tot + ext_u[:, pool_halo - k:pool_halo - k + tt, cols]
            cnt = jnp.minimum(w, pos + 1).astype(_F32)
            diff = (tot / cnt - ug).reshape(rows, POOL_GROUP).astype(_BF16)
            z = jnp.dot(diff, pw_ref[gi], preferred_element_type=_F32) * psc_ref[:, cols]
            yc[:, cols] = z.astype(_BF16)

    mixed = mixed + jnp.dot(yc[...], wo_ref[CONV_WIDTH + ATT_WIDTH:, :], preferred_element_type=_F32)
    out_ref[...] = x_ref[...] + mixed.reshape(nb, tt, D_MODEL)

    tail_a = ext_a[:, tt:tt + CONV_HALO, :]
    ext_a[:, 0:CONV_HALO, :] = tail_a
    cs_out[...] = tail_a
    ps_out[...] = ext_u[:, pool_halo + tt - POOL_HALO:pool_halo + tt, :].astype(_F32)
    ext_u[:, 0:pool_halo, :] = ext_u[:, tt:tt + pool_halo, :]


def _mixer(proj, yb, x, conv_state, pool_state, conv_w, pool_w, pool_scale, w_out, layer, nb, tt, start, u_tile0):
    batch, seq, _ = x.shape
    proj3 = proj.reshape(batch, seq, proj.shape[-1])
    u_blk = u_tile0 * COL_TILE // POOL_WIDTH
    const2 = lambda b, t: (0, 0)
    from_zero = conv_state is None and pool_state is None
    assert from_zero or (conv_state is not None and pool_state is not None)
    assert not from_zero or (nb == 1 and start == 0 and tt % POOL_BLOCK == 0)
    state_specs = [] if from_zero else [pl.BlockSpec((nb, CONV_HALO, CONV_WIDTH), lambda b, t: (b, 0, 0)),
                                        pl.BlockSpec((nb, POOL_HALO, POOL_WIDTH), lambda b, t: (b, 0, 0))]
    state_args = [] if from_zero else [conv_state, pool_state]
    ext_u = (pltpu.VMEM((nb, tt + POOL_BLOCK, POOL_WIDTH), _BF16) if from_zero
             else pltpu.VMEM((nb, tt + POOL_HALO, POOL_WIDTH), _F32))
    return pl.pallas_call(
        functools.partial(_mixer_kernel, nb=nb, tt=tt, start=start, from_zero=from_zero),
        grid=(batch // nb, seq // tt),
        in_specs=[pl.BlockSpec((nb, tt, 3 * CONV_WIDTH), lambda b, t: (b, t, 0)),
                  pl.BlockSpec((nb, tt, POOL_WIDTH), lambda b, t: (b, t, u_blk)),
                  pl.BlockSpec((nb, tt, ATT_WIDTH), lambda b, t: (b, t, 0)),
                  pl.BlockSpec((nb, tt, D_MODEL), lambda b, t: (b, t, 0))] + state_specs + [
                  pl.BlockSpec((3, CONV_WIDTH), const2),
                  pl.BlockSpec((None, len(POOL_WINDOWS), POOL_GROUP, POOL_GROUP), lambda b, t: (layer, 0, 0, 0)),
                  pl.BlockSpec((1, POOL_WIDTH), const2),
                  pl.BlockSpec((None, D_MODEL, D_MODEL), lambda b, t: (layer, 0, 0))],
        out_specs=[pl.BlockSpec((nb, tt, D_MODEL), lambda b, t: (b, t, 0)),
                   pl.BlockSpec((nb, CONV_HALO, CONV_WIDTH), lambda b, t: (b, 0, 0)),
                   pl.BlockSpec((nb, POOL_HALO, POOL_WIDTH), lambda b, t: (b, 0, 0))],
        out_shape=[jax.ShapeDtypeStruct((batch, seq, D_MODEL), _F32),
                   jax.ShapeDtypeStruct((batch, CONV_HALO, CONV_WIDTH), _F32),
                   jax.ShapeDtypeStruct((batch, POOL_HALO, POOL_WIDTH), _F32)],
        scratch_shapes=[pltpu.VMEM((nb, tt + CONV_HALO, CONV_WIDTH), _F32), ext_u,
                        pltpu.VMEM((nb * tt, POOL_WIDTH), _BF16)],
        compiler_params=pltpu.CompilerParams(dimension_semantics=("parallel", "arbitrary"),
                                             vmem_limit_bytes=VMEM_LIMIT),
        name="mixer",
    )(proj3, proj3, yb, x, *state_args, conv_w, pool_w, pool_scale.reshape(1, POOL_WIDTH), w_out)


def _ffn_sample_kernel(x_ref, n2_ref, wg_ref, wu_ref, cw_ref, wd_ref, st0_ref, fn_ref,
                       out_ref, st_out, wgt_ref, wut_ref, wdt_ref, h_s, ext_s, *, nb, tt, final):
    f = pl.program_id(0)
    rows = nb * tt

    @pl.when(f == 0)
    def _():
        x = x_ref[...]
        h_s[...] = _rms_norm(x, n2_ref[...]).reshape(rows, D_MODEL).astype(_BF16)
        out_ref[...] = x

    def tile(width):
        cols = slice(0, width)
        wg, wu, wd = wg_ref[:, cols].astype(_BF16), wu_ref[:, cols].astype(_BF16), wd_ref[cols, :].astype(_BF16)
        wgt_ref[:, cols], wut_ref[:, cols], wdt_ref[cols, :] = wg, wu, wd
        if width < FF_TILE:
            wgt_ref[:, width:] = jnp.zeros((D_MODEL, FF_TILE - width), _BF16)
            wut_ref[:, width:] = jnp.zeros((D_MODEL, FF_TILE - width), _BF16)
            wdt_ref[width:, :] = jnp.zeros((FF_TILE - width, D_MODEL), _BF16)
        h = h_s[...]
        gate = jnp.dot(h, wg, preferred_element_type=_F32).reshape(nb, tt, width)
        up = jnp.dot(h, wu, preferred_element_type=_F32).reshape(nb, tt, width)
        ext_s[:, 0:CONV_HALO, cols] = st0_ref[:, :, cols]
        ext_s[:, CONV_HALO:CONV_HALO + tt, cols] = gate
        cw = cw_ref[:, cols]
        gc = (cw[0] * ext_s[:, CONV_HALO - 2:CONV_HALO - 2 + tt, cols]
              + cw[1] * ext_s[:, CONV_HALO - 1:CONV_HALO - 1 + tt, cols] + cw[2] * gate)
        st_out[:, :, cols] = ext_s[:, tt:tt + CONV_HALO, cols]
        act = (jax.nn.silu(gc) * up).reshape(rows, width).astype(_BF16)
        out_ref[...] += jnp.dot(act, wd, preferred_element_type=_F32).reshape(nb, tt, D_MODEL)

    @pl.when(f < FF_TILES - 1)
    def _():
        tile(FF_TILE)

    @pl.when(f == FF_TILES - 1)
    def _():
        tile(FF_LAST)
        if final:
            out_ref[...] = _rms_norm(out_ref[...], fn_ref[...])


def _ffn_prompt_kernel(x_ref, n2_ref, wg_ref, wu_ref, cw_ref, wd_ref, fn_ref,
                       out_ref, st_out, h_s, ext_s, carry_s, *, tt, final):
    ti = pl.program_id(1)
    f = pl.program_id(2)

    def tile(first, last):
        if first:
            @pl.when(ti == 0)
            def _():
                carry_s[...] = jnp.zeros(carry_s.shape, _F32)
            x = x_ref[...]
            h_s[...] = _rms_norm(x, n2_ref[...]).astype(_BF16)
            out_ref[...] = x
        ext_s[0:CONV_HALO, :] = carry_s[f]

        wg, wu, wd, cw = wg_ref[...], wu_ref[...], wd_ref[...], cw_ref[f]
        chunks = [slice(r0, r0 + FFN_SUB_ROWS) for r0 in range(0, tt, FFN_SUB_ROWS)]
        for rs in chunks:
            gate = jnp.dot(h_s[rs, :], wg, preferred_element_type=_F32)
            ext_s[CONV_HALO + rs.start:CONV_HALO + rs.stop, :] = gate
        ups = [jnp.dot(h_s[rs, :], wu, preferred_element_type=_F32) for rs in chunks]
        for rs, up in zip(chunks, ups):
            gc = (cw[0] * ext_s[CONV_HALO - 2 + rs.start:CONV_HALO - 2 + rs.stop, :]
                  + cw[1] * ext_s[CONV_HALO - 1 + rs.start:CONV_HALO - 1 + rs.stop, :]
                  + cw[2] * ext_s[CONV_HALO + rs.start:CONV_HALO + rs.stop, :])
            act = (jax.nn.silu(gc) * up).astype(_BF16)
            out_ref[rs, :] += jnp.dot(act, wd, preferred_element_type=_F32)
        tail = ext_s[tt:tt + CONV_HALO, :]
        carry_s[f] = tail
        st_out[f] = tail
        if last:
            out_ref[...] = _rms_norm(out_ref[...], fn_ref[...])

    @pl.when(f == 0)
    def _():
        tile(True, False)

    if final:
        @pl.when((f > 0) & (f < FF_TILES - 1))
        def _():
            tile(False, False)

        @pl.when(f == FF_TILES - 1)
        def _():
            tile(False, True)
    else:
        @pl.when(f > 0)
        def _():
            tile(False, False)


def _ffn_sample(x, norm2, w_gate, w_up, conv_w, w_down, state, final_norm, layer, final):
    nb, tt, _ = x.shape
    vec = lambda f: (0, 0)
    return pl.pallas_call(
        functools.partial(_ffn_sample_kernel, nb=nb, tt=tt, final=final),
        grid=(FF_TILES,),
        in_specs=[pl.BlockSpec((nb, tt, D_MODEL), lambda f: (0, 0, 0)),
                  pl.BlockSpec((1, D_MODEL), vec),
                  pl.BlockSpec((None, D_MODEL, FF_TILE), lambda f: (layer, 0, f)),
                  pl.BlockSpec((None, D_MODEL, FF_TILE), lambda f: (layer, 0, f)),
                  pl.BlockSpec((3, FF_TILE), lambda f: (0, f)),
                  pl.BlockSpec((None, FF_TILE, D_MODEL), lambda f: (layer, f, 0)),
                  pl.BlockSpec((nb, CONV_HALO, FF_TILE), lambda f: (0, 0, f)),
                  pl.BlockSpec((1, D_MODEL), vec)],
        out_specs=[pl.BlockSpec((nb, tt, D_MODEL), lambda f: (0, 0, 0)),
                   pl.BlockSpec((nb, CONV_HALO, FF_TILE), lambda f: (0, 0, f)),
                   pl.BlockSpec((None, D_MODEL, FF_TILE), lambda f: (f, 0, 0)),
                   pl.BlockSpec((None, D_MODEL, FF_TILE), lambda f: (f, 0, 0)),
                   pl.BlockSpec((None, FF_TILE, D_MODEL), lambda f: (f, 0, 0))],
        out_shape=[jax.ShapeDtypeStruct((nb, tt, D_MODEL), _F32),
                   jax.ShapeDtypeStruct((nb, CONV_HALO, D_FF), _F32),
                   jax.ShapeDtypeStruct((FF_TILES, D_MODEL, FF_TILE), _BF16),
                   jax.ShapeDtypeStruct((FF_TILES, D_MODEL, FF_TILE), _BF16),
                   jax.ShapeDtypeStruct((FF_TILES, FF_TILE, D_MODEL), _BF16)],
        scratch_shapes=[pltpu.VMEM((nb * tt, D_MODEL), _BF16),
                        pltpu.VMEM((nb, tt + CONV_HALO, FF_TILE), _F32)],
        compiler_params=pltpu.CompilerParams(dimension_semantics=("arbitrary",), vmem_limit_bytes=VMEM_LIMIT),
        name="ffn_sample",
    )(x, norm2.reshape(1, D_MODEL), w_gate, w_up, conv_w, w_down, state, final_norm.reshape(1, D_MODEL))


def _ffn_prompt(x, norm2, w_gate_t, w_up_t, conv_w, w_down_t, final_norm, tt, final):
    batch, seq, _ = x.shape
    vec = lambda b, t, f: (0, 0)
    return pl.pallas_call(
        functools.partial(_ffn_prompt_kernel, tt=tt, final=final),
        grid=(batch, seq // tt, FF_TILES),
        in_specs=[pl.BlockSpec((None, tt, D_MODEL), lambda b, t, f: (b, t, 0)),
                  pl.BlockSpec((1, D_MODEL), vec),
                  pl.BlockSpec((None, D_MODEL, FF_TILE), lambda b, t, f: (f, 0, 0)),
                  pl.BlockSpec((None, D_MODEL, FF_TILE), lambda b, t, f: (f, 0, 0)),
                  pl.BlockSpec((FF_TILES, 3, FF_TILE), lambda b, t, f: (0, 0, 0)),
                  pl.BlockSpec((None, FF_TILE, D_MODEL), lambda b, t, f: (f, 0, 0)),
                  pl.BlockSpec((1, D_MODEL), vec)],
        out_specs=[pl.BlockSpec((None, tt, D_MODEL), lambda b, t, f: (b, t, 0)),
                   pl.BlockSpec((None, None, FF_TILES, CONV_HALO, FF_TILE), lambda b, t, f: (b, t, 0, 0, 0))],
        out_shape=[jax.ShapeDtypeStruct((batch, seq, D_MODEL), _F32),
                   jax.ShapeDtypeStruct((batch, seq // tt, FF_TILES, CONV_HALO, FF_TILE), _F32)],
        scratch_shapes=[pltpu.VMEM((tt, D_MODEL), _BF16),
                        pltpu.VMEM((tt + CONV_HALO, FF_TILE), _F32),
                        pltpu.VMEM((FF_TILES, CONV_HALO, FF_TILE), _F32)],
        compiler_params=pltpu.CompilerParams(dimension_semantics=("parallel", "arbitrary", "arbitrary"),
                                             vmem_limit_bytes=VMEM_LIMIT),
        name="ffn_prompt",
    )(x, norm2.reshape(1, D_MODEL), w_gate_t, w_up_t, conv_w, w_down_t, final_norm.reshape(1, D_MODEL))


def _pad_rows_front(a, rows):
    return jnp.pad(a, ((0, 0), (rows - a.shape[1], 0), (0, 0)))


def kernel(x_prompt, x_sample, cache_kv_w128, cache_kv_w512, cache_kv_w2048, state_conv_a, state_pool, state_ffn_conv, norm1, w_in, conv_a_w, w_out, pool_w, pool_scale, norm2, w_gate, w_up, ffn_conv_w, w_down, final_norm):
    bp, sp, _ = x_prompt.shape
    bs, ss, _ = x_sample.shape
    caches = tuple(c.reshape(DEPTH, bs, win * KV_ROWS, HEAD_DIM)
                   for c, (win, _) in zip((cache_kv_w128, cache_kv_w512, cache_kv_w2048), DIL_GROUPS))
    tm, tt = PROMPT_ROW_TILE, MIXER_ROW_TILE
    w_out_b, pool_w_b = w_out.astype(_BF16), pool_w.astype(_BF16)
    xp, xs = x_prompt, x_sample
    kv_p = kv_s = None
    small_p, small_s = [], []
    for l in range(DEPTH):
        final = l == DEPTH - 1

        proj, *kv_new, w_in_t = _inproj_sample(xs.reshape(bs * ss, D_MODEL), norm1[l], w_in, l)
        yb, *kv_s = _attn_sample(proj, kv_new, caches, bs, ss, l, kv_s)
        xs, conv_o, pool_o = _mixer(proj, yb, xs, _pad_rows_front(state_conv_a[l], CONV_HALO),
                                    _pad_rows_front(state_pool[l], POOL_HALO), conv_a_w[l], pool_w_b, pool_scale[l],
                                    w_out_b, l, bs, ss, PAST_LEN, U_TILE0)
        xs, ffn_o, w_gate_t, w_up_t, w_down_t = _ffn_sample(
            xs, norm2[l], w_gate, w_up, ffn_conv_w[l], w_down, _pad_rows_front(state_ffn_conv[l], CONV_HALO),
            final_norm, l, final)
        small_s.append((conv_o[:, CONV_HALO - 2:], pool_o[:, POOL_HALO - POOL_BUF:], ffn_o[:, CONV_HALO - 2:]))

        nat, qkv1, qkv2, *kv_p = _inproj_prompt(xp.reshape(bp * sp, D_MODEL), norm1[l], w_in_t, bp, sp, tm, l, kv_p)
        yb = _attn_prompt(nat, qkv1, qkv2, bp, sp)
        xp, conv_o, pool_o = _mixer(nat, yb, xp, None, None, conv_a_w[l], pool_w_b, pool_scale[l], w_out_b,
                                    l, 1, tt, 0, NAT_TILES - 2)
        conv_w_t = jnp.pad(ffn_conv_w[l], ((0, 0), (0, FF_TILES * FF_TILE - D_FF)))
        conv_w_t = conv_w_t.reshape(3, FF_TILES, FF_TILE).transpose(1, 0, 2)
        xp, ffn_o = _ffn_prompt(xp, norm2[l], w_gate_t, w_up_t, conv_w_t, w_down_t, final_norm, tm, final)
        ffn_tail = ffn_o[:, -1, :, CONV_HALO - 2:].transpose(0, 2, 1, 3).reshape(bp, 2, FF_TILES * FF_TILE)
        small_p.append((conv_o[:, CONV_HALO - 2:], pool_o[:, POOL_HALO - POOL_BUF:], ffn_tail[:, :, :D_FF]))

    out = [xp, xs]
    for g, (win, _) in enumerate(DIL_GROUPS):
        out += [kv_p[g].reshape(DEPTH, bp, win, 2, HEADS, HEAD_DIM), kv_s[g].reshape(DEPTH, bs, win, 2, HEADS, HEAD_DIM)]
    for i in range(3):
        out += [jnp.stack([s[i] for s in small_p]), jnp.stack([s[i] for s in small_s])]
    return tuple(out)
```

```python
import functools

import numpy as np
import jax
import jax.numpy as jnp
from jax import lax
from jax.experimental import pallas as pl
from jax.experimental.pallas import tpu as pltpu

D_MODEL = 2048
DEPTH = 2
PAST_LEN = 16384
CONV_WIDTH = 512
ATT_WIDTH = 512
HEADS = 4
HEAD_DIM = 128
DIL_GROUPS = ((128, 1), (512, 4), (2048, 16))
POOL_WIDTH = 1024
POOL_WINDOWS = (2, 4, 8, 16)
POOL_GROUP = 256
POOL_BUF = 15
IN_COLS = 7168
D_FF = 5504
RMS_EPS = 1e-6
ATT_SCALE = HEAD_DIM ** -0.5

COL_TILE = 512
IN_TILES = IN_COLS // COL_TILE
Q_TILE = (3, 6, 9)
U_TILE0 = 12
NAT_TILES = 8
KV_ROWS = 2 * HEADS
SUB_ROWS = 512
IN_STEP_TILES = 2
REGROUP_STRIDE = 4
FF_TILE = 512
FF_TILES = -(-D_FF // FF_TILE)
FF_LAST = D_FF - (FF_TILES - 1) * FF_TILE
FFN_SUB_ROWS = 512
CONV_HALO = 8
POOL_HALO = 16
POOL_BLOCK = 128
PROMPT_ROW_TILE = 1024
MIXER_ROW_TILE = 512
V7X_VMEM_BYTES = 64 * 1024 * 1024
VMEM_LIMIT = V7X_VMEM_BYTES - 8 * 1024 * 1024
VMEM_LIMIT_HIGH = V7X_VMEM_BYTES - 3 * 1024 * 1024

_BF16 = jnp.bfloat16
_F32 = jnp.float32


def _alibi_slopes():
    h = np.arange(1, 3 * HEADS + 1, dtype=np.float32)
    return [float(v) for v in np.power(np.float32(2.0), -8.0 * h / (3 * HEADS))]


SLOPES = _alibi_slopes()


def _rms_norm(x, g):
    return (x * lax.rsqrt(jnp.mean(x * x, axis=-1, keepdims=True) + RMS_EPS)) * g


def _head_cols(h, base=0):
    return slice(base + h * HEAD_DIM, base + (h + 1) * HEAD_DIM)


def _inproj_sample_kernel(x_ref, g_ref, w_ref, proj_ref, kv0_ref, kv1_ref, kv2_ref, wt_ref, h_s):
    j = pl.program_id(1)

    @pl.when(j == 0)
    def _():
        h_s[...] = _rms_norm(x_ref[...], g_ref[...]).astype(_BF16)

    w = w_ref[...].astype(_BF16)
    wt_ref[...] = w
    acc = jnp.dot(h_s[...], w, preferred_element_type=_F32)
    proj_ref[...] = acc.astype(_BF16)
    for g, ref in enumerate((kv0_ref, kv1_ref, kv2_ref)):
        @pl.when((j == Q_TILE[g] + 1) | (j == Q_TILE[g] + 2))
        def _(ref=ref):
            ref[...] = acc


def _inproj_sample(x2d, g, w, layer):
    m = x2d.shape[0]

    def kv_spec(grp):
        return pl.BlockSpec((m, COL_TILE), lambda i, j: (i, jnp.clip(j - Q_TILE[grp] - 1, 0, 1)))

    return pl.pallas_call(
        _inproj_sample_kernel,
        grid=(1, IN_TILES),
        in_specs=[pl.BlockSpec((m, D_MODEL), lambda i, j: (i, 0)),
                  pl.BlockSpec((1, D_MODEL), lambda i, j: (0, 0)),
                  pl.BlockSpec((None, D_MODEL, COL_TILE), lambda i, j: (layer, 0, j))],
        out_specs=[pl.BlockSpec((m, COL_TILE), lambda i, j: (i, j)), kv_spec(0), kv_spec(1), kv_spec(2),
                   pl.BlockSpec((None, D_MODEL, COL_TILE), lambda i, j: (j, 0, 0))],
        out_shape=[jax.ShapeDtypeStruct((m, IN_COLS), _BF16)] +
                  [jax.ShapeDtypeStruct((m, 2 * ATT_WIDTH), _F32)] * 3 +
                  [jax.ShapeDtypeStruct((IN_TILES, D_MODEL, COL_TILE), _BF16)],
        scratch_shapes=[pltpu.VMEM((m, D_MODEL), _BF16)],
        compiler_params=pltpu.CompilerParams(dimension_semantics=("parallel", "arbitrary"),
                                             vmem_limit_bytes=VMEM_LIMIT),
        name="inproj_sample",
    )(x2d, g.reshape(1, D_MODEL), w)


def _inproj_prompt_kernel(*refs, tm, aliased):
    n_in = 6 if aliased else 3
    x_ref, g_ref, w_ref = refs[:3]
    nat_ref, g1_ref, g2_ref, kv0_ref, kv1_ref, kv2_ref, h_s, de_s, de2_s = refs[n_in:]
    kv_refs = (kv0_ref, kv1_ref, kv2_ref)
    res_refs = (None, g1_ref, g2_ref)
    j = pl.program_id(1)

    @pl.when(j == 0)
    def _():
        h_s[...] = _rms_norm(x_ref[...], g_ref[...]).astype(_BF16)

    def column_tiles(tiles):
        for c in range(tm // SUB_ROWS):
            r0 = c * SUB_ROWS
            for half, tile in enumerate(tiles):
                acc = jnp.dot(h_s[r0:r0 + SUB_ROWS, :], w_ref[half], preferred_element_type=_F32)
                grp = (tile - Q_TILE[0]) // 3 if Q_TILE[0] <= tile < U_TILE0 else None
                part = None if grp is None or tile == Q_TILE[grp] else tile - Q_TILE[grp] - 1
                dil = 1 if grp is None else DIL_GROUPS[grp][1]
                if dil == 1:
                    nat_ref[r0:r0 + SUB_ROWS, half * COL_TILE:(half + 1) * COL_TILE] = acc.astype(_BF16)
                else:
                    for h in range(HEADS):
                        de_s[c, half, h] = acc[:, _head_cols(h)]
                    n = SUB_ROWS // dil
                    col0 = (tile - Q_TILE[grp]) * COL_TILE
                    if dil == REGROUP_STRIDE:
                        for res in range(dil):
                            for h in range(HEADS):
                                res_refs[grp][res, c * n:(c + 1) * n, _head_cols(h, col0)] = (
                                    de_s[c, half, h, pl.ds(res, n, stride=dil), :].astype(_BF16))
                    else:
                        assert dil == REGROUP_STRIDE ** 2
                        q = SUB_ROWS // REGROUP_STRIDE
                        for h in range(HEADS):
                            for g in range(REGROUP_STRIDE):
                                de2_s[h, g * q:(g + 1) * q, :] = de_s[c, half, h, pl.ds(g, q, stride=REGROUP_STRIDE), :]
                        for g in range(REGROUP_STRIDE):
                            for r in range(REGROUP_STRIDE):
                                for h in range(HEADS):
                                    res_refs[grp][g + REGROUP_STRIDE * r, c * n:(c + 1) * n, _head_cols(h, col0)] = (
                                        de2_s[h, pl.ds(g * q + r, n, stride=REGROUP_STRIDE), :].astype(_BF16))
                if part is not None:
                    keep = min(DIL_GROUPS[grp][0], tm)
                    lo = max(r0, tm - keep)
                    n = r0 + SUB_ROWS - lo
                    if n > 0:
                        for h in range(HEADS):
                            dst = pl.ds((lo - (tm - keep)) * KV_ROWS + part * HEADS + h, n, stride=KV_ROWS)
                            kv_refs[grp][dst, :] = acc[lo - r0:lo - r0 + n, _head_cols(h)]

    for step in range(IN_TILES // IN_STEP_TILES):
        @pl.when(j == step)
        def _(step=step):
            column_tiles(tuple(range(step * IN_STEP_TILES, (step + 1) * IN_STEP_TILES)))


def _inproj_prompt(x2d, g, w, batch, seq, tm, layer, prev_kv):
    m = x2d.shape[0]
    tps = seq // tm
    aliased = prev_kv is not None
    (w0, _), (w1, d1), (w2, d2) = DIL_GROUPS
    assert w2 == seq and w0 <= tm and w1 <= tm and tm % SUB_ROWS == 0

    in_specs = [pl.BlockSpec((tm, D_MODEL), lambda i, j: (i, 0)),
                pl.BlockSpec((1, D_MODEL), lambda i, j: (0, 0)),
                pl.BlockSpec((IN_STEP_TILES, D_MODEL, COL_TILE), lambda i, j: (j, 0, 0))]
    args = [x2d, g.reshape(1, D_MODEL), w]
    if aliased:
        in_specs += [pl.BlockSpec(memory_space=pl.ANY)] * 3
        args += list(prev_kv)

    def res_spec(dil):
        return pl.BlockSpec((None, dil, tm // dil, 3 * ATT_WIDTH), lambda i, j: (i // tps, 0, i % tps, 0))

    assert Q_TILE[1] % IN_STEP_TILES == 0 and U_TILE0 % IN_STEP_TILES == 0 and IN_TILES - U_TILE0 == IN_STEP_TILES
    nat_lo, u_step = Q_TILE[1] // IN_STEP_TILES, U_TILE0 // IN_STEP_TILES
    out_specs = [
        pl.BlockSpec((tm, IN_STEP_TILES * COL_TILE),
                     lambda i, j: (i, jnp.clip(j, 0, nat_lo - 1) + jnp.clip(j - u_step + 1, 0, 1))),
        res_spec(d1), res_spec(d2),
        pl.BlockSpec((None, None, w0 * KV_ROWS, HEAD_DIM), lambda i, j: (layer, i // tps, 0, 0),
                     pipeline_mode=pl.Buffered(1)),
        pl.BlockSpec((None, None, w1 * KV_ROWS, HEAD_DIM), lambda i, j: (layer, i // tps, 0, 0),
                     pipeline_mode=pl.Buffered(1)),
        pl.BlockSpec((None, None, tm * KV_ROWS, HEAD_DIM), lambda i, j: (layer, i // tps, i % tps, 0)),
    ]
    out_shape = [
        jax.ShapeDtypeStruct((m, NAT_TILES * COL_TILE), _BF16),
        jax.ShapeDtypeStruct((batch, d1, seq // d1, 3 * ATT_WIDTH), _BF16),
        jax.ShapeDtypeStruct((batch, d2, seq // d2, 3 * ATT_WIDTH), _BF16),
    ] + [jax.ShapeDtypeStruct((DEPTH, batch, win * KV_ROWS, HEAD_DIM), _F32) for win, _ in DIL_GROUPS]
    return pl.pallas_call(
        functools.partial(_inproj_prompt_kernel, tm=tm, aliased=aliased),
        grid=(m // tm, IN_TILES // IN_STEP_TILES),
        in_specs=in_specs, out_specs=out_specs, out_shape=out_shape,
        scratch_shapes=[pltpu.VMEM((tm, D_MODEL), _BF16),
                        pltpu.VMEM((tm // SUB_ROWS, IN_STEP_TILES, HEADS, SUB_ROWS, HEAD_DIM), _F32),
                        pltpu.VMEM((HEADS, SUB_ROWS, HEAD_DIM), _F32)],
        input_output_aliases={3: 3, 4: 4, 5: 5} if aliased else {},
        compiler_params=pltpu.CompilerParams(dimension_semantics=("arbitrary", "arbitrary"),
                                             vmem_limit_bytes=VMEM_LIMIT_HIGH),
        name="inproj_prompt",
    )(*args)


QB = 128
G0_UNROLL = 3
G1_RES_PER_STEP = 4
G2_RES_PER_STEP = 8


def _attn_prompt_kernel(q0, k0, v0, q1, k1, v1, q2, k2, v2, o_ref, acc_s, m_s, l_s, shuf_s, *, seq):
    step = pl.program_id(1)
    n1 = DIL_GROUPS[1][1]
    n1_steps = n1 // G1_RES_PER_STEP
    n2_steps = DIL_GROUPS[2][1] // G2_RES_PER_STEP

    row0 = lax.broadcasted_iota(jnp.int32, (QB, QB), 0)
    col0 = lax.broadcasted_iota(jnp.int32, (QB, QB), 1)
    dist_first = jnp.where(col0 <= row0, (row0 - col0).astype(_F32), jnp.inf)
    row1 = lax.broadcasted_iota(jnp.int32, (QB, 2 * QB), 0)
    col1 = lax.broadcasted_iota(jnp.int32, (QB, 2 * QB), 1)
    d1 = QB + row1 - col1
    dist_next = jnp.where((d1 >= 0) & (d1 <= QB), d1.astype(_F32), jnp.inf)
    ones = jnp.ones((2 * QB, HEAD_DIM), _BF16)

    seg = seq // n1
    chunk = QB // n1
    assert n1 == 4 and DIL_GROUPS[2][1] == 16

    def blocks(specs):
        tiles = []
        for slot, (qr, kr, vr, grp, res, i, is_first_block) in enumerate(specs):
            dil = DIL_GROUPS[grp][1]
            for h in range(HEADS):
                hs = _head_cols(h)
                slope_step = SLOPES[grp * HEADS + h] * dil
                if is_first_block:
                    q, k, v = qr[0:QB, hs], kr[0:QB, hs], vr[0:QB, hs]
                    bias, one = dist_first * slope_step, ones[0:QB]
                else:
                    qs = pl.ds(pl.multiple_of(i * QB, QB), QB)
                    ks = pl.ds(pl.multiple_of(i * QB - QB, QB), 2 * QB)
                    q, k, v = qr[qs, hs], kr[ks, hs], vr[ks, hs]
                    bias, one = dist_next * slope_step, ones
                tiles.append((q, k, jnp.concatenate([v, one], axis=1), bias, grp, res, i, is_first_block, slot, h))
        scores = [lax.dot_general(t[0], t[1], (((1,), (1,)), ((), ())), preferred_element_type=_F32) for t in tiles]
        probs, maxes = [], []
        for t, s in zip(tiles, scores):
            s = s * ATT_SCALE - t[3]
            m = jnp.max(s, axis=1, keepdims=True)
            probs.append(jnp.exp(s - m).astype(_BF16))
            maxes.append(jnp.broadcast_to(m, (QB, HEAD_DIM)))
        outs = [jnp.dot(p, t[2], preferred_element_type=_F32) for t, p in zip(tiles, probs)]
        def merge(h, rows, acc, m, l):
            m_old, l_old, a_old = m_s[h, rows, :], l_s[h, rows, :], acc_s[h, rows, :]
            m_new = jnp.maximum(m_old, m)
            w_old = jnp.exp(m_old - m_new)
            w_new = jnp.exp(m - m_new)
            acc_s[h, rows, :] = a_old * w_old + acc * w_new
            l_s[h, rows, :] = l_old * w_old + l * w_new
            m_s[h, rows, :] = m_new

        for (_, _, _, _, grp, res, i, is_first_block, slot, h), m, acc_l in zip(tiles, maxes, outs):
            acc, l = acc_l[:, :HEAD_DIM], acc_l[:, HEAD_DIM:]
            if grp == 1:
                rows = pl.ds(pl.multiple_of(res * seg + i * QB, QB), QB)
                acc_s[h, rows, :], m_s[h, rows, :], l_s[h, rows, :] = acc, m, l
            elif grp == 2:
                merge(h, pl.ds((res & (n1 - 1)) * seg + (res >> 2), QB, stride=n1), acc, m, l)
            else:
                tmps = [shuf_s.at[(slot * HEADS + h) * 3 + a] for a in range(3)]
                for tmp, val in zip(tmps, (acc, m, l)):
                    tmp[...] = val
                for r4 in range(n1):
                    dst = r4 * seg + i * chunk
                    dst = pl.ds(dst if is_first_block else pl.multiple_of(dst, chunk), chunk)
                    merge(h, dst, *(tmp[pl.ds(r4, chunk, stride=n1), :] for tmp in tmps))

    @pl.when(step < n1_steps)
    def _():
        blocks([(q1.at[r], k1.at[r], v1.at[r], 1, step * G1_RES_PER_STEP + r, i, i == 0)
                for r in range(G1_RES_PER_STEP) for i in range(seg // QB)])

    @pl.when((step >= n1_steps) & (step < n1_steps + n2_steps))
    def _():
        assert seq // DIL_GROUPS[2][1] == QB
        blocks([(q2.at[r], k2.at[r], v2.at[r], 2, (step - n1_steps) * G2_RES_PER_STEP + r, 0, True)
                for r in range(G2_RES_PER_STEP)])

    @pl.when(step == n1_steps + n2_steps)
    def _():
        n_blk = seq // QB
        assert (n_blk - 1) % G0_UNROLL == 0
        blocks([(q0, k0, v0, 0, 0, 0, True)])

        def g0_body(it, carry):
            blocks([(q0, k0, v0, 0, 0, 1 + it * G0_UNROLL + u, False) for u in range(G0_UNROLL)])
            return carry
        lax.fori_loop(0, (n_blk - 1) // G0_UNROLL, g0_body, 0)

        def body(c, carry):
            for h in range(HEADS):
                tmp = shuf_s.at[h]
                for r4 in range(n1):
                    src = pl.ds(pl.multiple_of(r4 * seg + c * chunk, chunk), chunk)
                    tmp[pl.ds(r4, chunk, stride=n1), :] = acc_s[h, src, :] / l_s[h, src, :]
                o_ref[pl.ds(pl.multiple_of(c * QB, QB), QB), _head_cols(h)] = tmp[...].astype(_BF16)
            return carry
        lax.fori_loop(0, seq // QB, body, 0)


def _attn_prompt(nat, qkv1, qkv2, batch, seq):
    n1, n2 = DIL_GROUPS[1][1], DIL_GROUPS[2][1]
    n1_steps, n2_steps = n1 // G1_RES_PER_STEP, n2 // G2_RES_PER_STEP
    nat3 = nat.reshape(batch, seq, NAT_TILES * COL_TILE)
    in_specs = [pl.BlockSpec((None, seq, COL_TILE), lambda b, s, t=t: (b, 0, Q_TILE[0] + t)) for t in range(3)]
    in_specs += [pl.BlockSpec((None, G1_RES_PER_STEP, seq // n1, COL_TILE),
                              lambda b, s, t=t: (b, jnp.clip(s, 0, n1_steps - 1), 0, t)) for t in range(3)]
    in_specs += [pl.BlockSpec((None, G2_RES_PER_STEP, seq // n2, COL_TILE),
                              lambda b, s, t=t: (b, jnp.clip(s - n1_steps, 0, n2_steps - 1), 0, t)) for t in range(3)]
    args = [nat3] * 3 + [qkv1] * 3 + [qkv2] * 3
    return pl.pallas_call(
        functools.partial(_attn_prompt_kernel, seq=seq),
        grid=(batch, 1 + n1_steps + n2_steps),
        in_specs=in_specs,
        out_specs=pl.BlockSpec((None, seq, ATT_WIDTH), lambda b, s: (b, 0, 0)),
        out_shape=jax.ShapeDtypeStruct((batch, seq, ATT_WIDTH), _BF16),
        scratch_shapes=[pltpu.VMEM((HEADS, seq, HEAD_DIM), _F32)] * 3
                       + [pltpu.VMEM((G0_UNROLL * HEADS * 3, QB, HEAD_DIM), _F32)],
        compiler_params=pltpu.CompilerParams(dimension_semantics=("parallel", "arbitrary"),
                                             vmem_limit_bytes=VMEM_LIMIT),
        name="attn_prompt",
    )(*args)


def _attn_sample_kernel(*refs, tdec, aliased):
    n_in = 10 if aliased else 7
    proj_ref, n0, n1, n2, c0, c1, c2 = refs[:7]
    yb_ref, o0, o1, o2 = refs[n_in:]
    news, caches, outs = (n0, n1, n2), (c0, c1, c2), (o0, o1, o2)
    for h in range(HEADS):
        pieces = []
        for grp, (win, dil) in enumerate(DIL_GROUPS):
            slope = SLOPES[grp * HEADS + h]
            q = proj_ref[:, _head_cols(h, Q_TILE[grp] * COL_TILE)]
            k_new = news[grp][:, _head_cols(h)]
            v_new = news[grp][:, _head_cols(h, ATT_WIDTH)]
            k_old = caches[grp][pl.ds(h, win, stride=KV_ROWS), :]
            v_old = caches[grp][pl.ds(HEADS + h, win, stride=KV_ROWS), :]
            for k, v, n_keys, base in ((k_old, v_old, win, win), (k_new, v_new, tdec, 0)):
                s = lax.dot_general(q, k.astype(_BF16), (((1,), (1,)), ((), ())), preferred_element_type=_F32)
                t = lax.broadcasted_iota(jnp.int32, (tdec, n_keys), 0)
                c = lax.broadcasted_iota(jnp.int32, (tdec, n_keys), 1)
                dist = base + t - c
                ok = (dist >= 0) & (dist <= win) & ((dist & (dil - 1)) == 0)
                s = jnp.where(ok, s * ATT_SCALE - slope * dist.astype(_F32), -jnp.inf)
                pieces.append((s, v.astype(_BF16)))
        m = functools.reduce(jnp.maximum, [jnp.max(s, axis=1, keepdims=True) for s, _ in pieces])
        l = jnp.zeros((tdec, 1), _F32)
        acc = jnp.zeros((tdec, HEAD_DIM), _F32)
        for s, v in pieces:
            p = jnp.exp(s - m)
            l = l + jnp.sum(p, axis=1, keepdims=True)
            acc = acc + jnp.dot(p.astype(_BF16), v, preferred_element_type=_F32)
        yb_ref[:, _head_cols(h)] = (acc / l).astype(_BF16)
    for grp, (win, _) in enumerate(DIL_GROUPS):
        kept = (win - tdec) * KV_ROWS
        outs[grp][0:kept, :] = caches[grp][tdec * KV_ROWS:win * KV_ROWS, :]
        for part in range(2):
            for h in range(HEADS):
                outs[grp][pl.ds(kept + part * HEADS + h, tdec, stride=KV_ROWS), :] = (
                    news[grp][:, _head_cols(h, part * ATT_WIDTH)])


def _attn_sample(proj, kv_new, caches, batch, tdec, layer, prev_out):
    aliased = prev_out is not None
    in_specs = [pl.BlockSpec((None, tdec, IN_COLS), lambda b: (b, 0, 0))]
    in_specs += [pl.BlockSpec((None, tdec, 2 * ATT_WIDTH), lambda b: (b, 0, 0))] * 3
    cache_specs = [pl.BlockSpec((None, None, win * KV_ROWS, HEAD_DIM), lambda b: (layer, b, 0, 0)) for win, _ in DIL_GROUPS]
    in_specs += cache_specs
    args = [proj.reshape(batch, tdec, IN_COLS)] + [k.reshape(batch, tdec, 2 * ATT_WIDTH) for k in kv_new] + list(caches)
    if aliased:
        in_specs += [pl.BlockSpec(memory_space=pl.ANY)] * 3
        args += list(prev_out)
    out_specs = [pl.BlockSpec((None, tdec, ATT_WIDTH), lambda b: (b, 0, 0))] + cache_specs
    out_shape = [jax.ShapeDtypeStruct((batch, tdec, ATT_WIDTH), _BF16)]
    out_shape += [jax.ShapeDtypeStruct((DEPTH, batch, win * KV_ROWS, HEAD_DIM), _F32) for win, _ in DIL_GROUPS]
    return pl.pallas_call(
        functools.partial(_attn_sample_kernel, tdec=tdec, aliased=aliased),
        grid=(batch,),
        in_specs=in_specs, out_specs=out_specs, out_shape=out_shape,
        input_output_aliases={7: 1, 8: 2, 9: 3} if aliased else {},
        compiler_params=pltpu.CompilerParams(dimension_semantics=("arbitrary",), vmem_limit_bytes=VMEM_LIMIT),
        name="attn_sample",
    )(*args)


def _mixer_kernel(*refs, nb, tt, start, from_zero):
    if from_zero:
        pa_ref, pu_ref, yb_ref, x_ref, cw_ref, pw_ref, psc_ref, wo_ref = refs[:8]
    else:
        pa_ref, pu_ref, yb_ref, x_ref, cs0_ref, ps0_ref, cw_ref, pw_ref, psc_ref, wo_ref = refs[:10]
    out_ref, cs_out, ps_out, ext_a, ext_u, yc = refs[-6:]
    ti = pl.program_id(1)
    rows = nb * tt
    pool_halo = POOL_BLOCK if from_zero else POOL_HALO

    @pl.when(ti == 0)
    def _():
        if from_zero:
            ext_a[:, 0:CONV_HALO, :] = jnp.zeros((nb, CONV_HALO, CONV_WIDTH), _F32)
            ext_u[:, 0:pool_halo, :] = jnp.zeros((nb, pool_halo, POOL_WIDTH), _BF16)
        else:
            ext_a[:, 0:CONV_HALO, :] = cs0_ref[...]
            ext_u[:, 0:pool_halo, :] = ps0_ref[...]

    yb = yb_ref[...].astype(_F32).reshape(rows, ATT_WIDTH).astype(_BF16)
    mixed = jnp.dot(yb, wo_ref[CONV_WIDTH:CONV_WIDTH + ATT_WIDTH, :], preferred_element_type=_F32)

    pa = pa_ref[...].astype(_F32)
    xa, gate_b, gate_c = (pa[:, :, k * CONV_WIDTH:(k + 1) * CONV_WIDTH] for k in range(3))
    prod = gate_c * xa
    ext_a[:, CONV_HALO:CONV_HALO + tt, :] = prod
    cw = cw_ref[...]
    cu = (cw[0] * ext_a[:, CONV_HALO - 2:CONV_HALO - 2 + tt, :]
          + cw[1] * ext_a[:, CONV_HALO - 1:CONV_HALO - 1 + tt, :] + cw[2] * prod)
    ya = (gate_b * cu).reshape(rows, CONV_WIDTH).astype(_BF16)
    mixed = mixed + jnp.dot(ya, wo_ref[0:CONV_WIDTH, :], preferred_element_type=_F32)

    ext_u[:, pool_halo:pool_halo + tt, :] = pu_ref[...].astype(ext_u.dtype)
    if from_zero:
        pb = POOL_BLOCK
        row = lax.broadcasted_iota(jnp.int32, (pb, 2 * pb), 0)
        col = lax.broadcasted_iota(jnp.int32, (pb, 2 * pb), 1)
        lag = pb + row - col
        for gi, w in enumerate(POOL_WINDOWS):
            cols = slice(gi * POOL_GROUP, (gi + 1) * POOL_GROUP)
            band = ((lag >= 0) & (lag < w)).astype(_F32).astype(_BF16)
            for r in range(tt // pb):
                tot = jnp.dot(band, ext_u[0, r * pb:(r + 2) * pb, cols], preferred_element_type=_F32)
                pos = start + ti * tt + r * pb + lax.broadcasted_iota(jnp.int32, (pb, 1), 0)
                cnt = jnp.minimum(w, pos + 1).astype(_F32)
                ug = pu_ref[0, r * pb:(r + 1) * pb, cols].astype(_F32)
                diff = (tot / cnt - ug).astype(_BF16)
                z = jnp.dot(diff, pw_ref[gi], preferred_element_type=_F32) * psc_ref[:, cols]
                yc[r * pb:(r + 1) * pb, cols] = z.astype(_BF16)
    else:
        u = pu_ref[...].astype(_F32)
        pos = start + ti * tt + lax.broadcasted_iota(jnp.int32, (1, tt, 1), 1)
        for gi, w in enumerate(POOL_WINDOWS):
            cols = slice(gi * POOL_GROUP, (gi + 1) * POOL_GROUP)
            ug = u[:, :, cols]
            tot = ug
            for k in range(1, w):
                tot = tot + ext_u[:, pool_halo - k:pool_halo - k + tt, cols]
            cnt = jnp.minimum(w, pos + 1).astype(_F32)
            diff = (tot / cnt - ug).reshape(rows, POOL_GROUP).astype(_BF16)
            z = jnp.dot(diff, pw_ref[gi], preferred_element_type=_F32) * psc_ref[:, cols]
            yc[:, cols] = z.astype(_BF16)

    mixed = mixed + jnp.dot(yc[...], wo_ref[CONV_WIDTH + ATT_WIDTH:, :], preferred_element_type=_F32)
    out_ref[...] = x_ref[...] + mixed.reshape(nb, tt, D_MODEL)

    tail_a = ext_a[:, tt:tt + CONV_HALO, :]
    ext_a[:, 0:CONV_HALO, :] = tail_a
    cs_out[...] = tail_a
    ps_out[...] = ext_u[:, pool_halo + tt - POOL_HALO:pool_halo + tt, :].astype(_F32)
    ext_u[:, 0:pool_halo, :] = ext_u[:, tt:tt + pool_halo, :]


def _mixer(proj, yb, x, conv_state, pool_state, conv_w, pool_w, pool_scale, w_out, layer, nb, tt, start, u_tile0):
    batch, seq, _ = x.shape
    proj3 = proj.reshape(batch, seq, proj.shape[-1])
    u_blk = u_tile0 * COL_TILE // POOL_WIDTH
    const2 = lambda b, t: (0, 0)
    from_zero = conv_state is None and pool_state is None
    assert from_zero or (conv_state is not None and pool_state is not None)
    assert not from_zero or (nb == 1 and start == 0 and tt % POOL_BLOCK == 0)
    state_specs = [] if from_zero else [pl.BlockSpec((nb, CONV_HALO, CONV_WIDTH), lambda b, t: (b, 0, 0)),
                                        pl.BlockSpec((nb, POOL_HALO, POOL_WIDTH), lambda b, t: (b, 0, 0))]
    state_args = [] if from_zero else [conv_state, pool_state]
    ext_u = (pltpu.VMEM((nb, tt + POOL_BLOCK, POOL_WIDTH), _BF16) if from_zero
             else pltpu.VMEM((nb, tt + POOL_HALO, POOL_WIDTH), _F32))
    return pl.pallas_call(
        functools.partial(_mixer_kernel, nb=nb, tt=tt, start=start, from_zero=from_zero),
        grid=(batch // nb, seq // tt),
        in_specs=[pl.BlockSpec((nb, tt, 3 * CONV_WIDTH), lambda b, t: (b, t, 0)),
                  pl.BlockSpec((nb, tt, POOL_WIDTH), lambda b, t: (b, t, u_blk)),
                  pl.BlockSpec((nb, tt, ATT_WIDTH), lambda b, t: (b, t, 0)),
                  pl.BlockSpec((nb, tt, D_MODEL), lambda b, t: (b, t, 0))] + state_specs + [
                  pl.BlockSpec((3, CONV_WIDTH), const2),
                  pl.BlockSpec((None, len(POOL_WINDOWS), POOL_GROUP, POOL_GROUP), lambda b, t: (layer, 0, 0, 0)),
                  pl.BlockSpec((1, POOL_WIDTH), const2),
                  pl.BlockSpec((None, D_MODEL, D_MODEL), lambda b, t: (layer, 0, 0))],
        out_specs=[pl.BlockSpec((nb, tt, D_MODEL), lambda b, t: (b, t, 0)),
                   pl.BlockSpec((nb, CONV_HALO, CONV_WIDTH), lambda b, t: (b, 0, 0)),
                   pl.BlockSpec((nb, POOL_HALO, POOL_WIDTH), lambda b, t: (b, 0, 0))],
        out_shape=[jax.ShapeDtypeStruct((batch, seq, D_MODEL), _F32),
                   jax.ShapeDtypeStruct((batch, CONV_HALO, CONV_WIDTH), _F32),
                   jax.ShapeDtypeStruct((batch, POOL_HALO, POOL_WIDTH), _F32)],
        scratch_shapes=[pltpu.VMEM((nb, tt + CONV_HALO, CONV_WIDTH), _F32), ext_u,
                        pltpu.VMEM((nb * tt, POOL_WIDTH), _BF16)],
        compiler_params=pltpu.CompilerParams(dimension_semantics=("parallel", "arbitrary"),
                                             vmem_limit_bytes=VMEM_LIMIT),
        name="mixer",
    )(proj3, proj3, yb, x, *state_args, conv_w, pool_w, pool_scale.reshape(1, POOL_WIDTH), w_out)


def _ffn_sample_kernel(x_ref, n2_ref, wg_ref, wu_ref, cw_ref, wd_ref, st0_ref, fn_ref,
                       out_ref, st_out, wgt_ref, wut_ref, wdt_ref, h_s, ext_s, *, nb, tt, final):
    f = pl.program_id(0)
    rows = nb * tt

    @pl.when(f == 0)
    def _():
        x = x_ref[...]
        h_s[...] = _rms_norm(x, n2_ref[...]).reshape(rows, D_MODEL).astype(_BF16)
        out_ref[...] = x

    def tile(width):
        cols = slice(0, width)
        wg, wu, wd = wg_ref[:, cols].astype(_BF16), wu_ref[:, cols].astype(_BF16), wd_ref[cols, :].astype(_BF16)
        wgt_ref[:, cols], wut_ref[:, cols], wdt_ref[cols, :] = wg, wu, wd
        if width < FF_TILE:
            wgt_ref[:, width:] = jnp.zeros((D_MODEL, FF_TILE - width), _BF16)
            wut_ref[:, width:] = jnp.zeros((D_MODEL, FF_TILE - width), _BF16)
            wdt_ref[width:, :] = jnp.zeros((FF_TILE - width, D_MODEL), _BF16)
        h = h_s[...]
        gate = jnp.dot(h, wg, preferred_element_type=_F32).reshape(nb, tt, width)
        up = jnp.dot(h, wu, preferred_element_type=_F32).reshape(nb, tt, width)
        ext_s[:, 0:CONV_HALO, cols] = st0_ref[:, :, cols]
        ext_s[:, CONV_HALO:CONV_HALO + tt, cols] = gate
        cw = cw_ref[:, cols]
        gc = (cw[0] * ext_s[:, CONV_HALO - 2:CONV_HALO - 2 + tt, cols]
              + cw[1] * ext_s[:, CONV_HALO - 1:CONV_HALO - 1 + tt, cols] + cw[2] * gate)
        st_out[:, :, cols] = ext_s[:, tt:tt + CONV_HALO, cols]
        act = (jax.nn.silu(gc) * up).reshape(rows, width).astype(_BF16)
        out_ref[...] += jnp.dot(act, wd, preferred_element_type=_F32).reshape(nb, tt, D_MODEL)

    @pl.when(f < FF_TILES - 1)
    def _():
        tile(FF_TILE)

    @pl.when(f == FF_TILES - 1)
    def _():
        tile(FF_LAST)
        if final:
            out_ref[...] = _rms_norm(out_ref[...], fn_ref[...])


def _ffn_prompt_kernel(x_ref, n2_ref, wg_ref, wu_ref, cw_ref, wd_ref, fn_ref,
                       out_ref, st_out, h_s, ext_s, carry_s, *, tt, final):
    ti = pl.program_id(1)
    f = pl.program_id(2)

    def tile(first, last):
        if first:
            @pl.when(ti == 0)
            def _():
                carry_s[...] = jnp.zeros(carry_s.shape, _F32)
            x = x_ref[...]
            h_s[...] = _rms_norm(x, n2_ref[...]).astype(_BF16)
            out_ref[...] = x
        ext_s[0:CONV_HALO, :] = carry_s[f]

        wg, wu, wd, cw = wg_ref[...], wu_ref[...], wd_ref[...], cw_ref[f]
        chunks = [slice(r0, r0 + FFN_SUB_ROWS) for r0 in range(0, tt, FFN_SUB_ROWS)]
        for rs in chunks:
            gate = jnp.dot(h_s[rs, :], wg, preferred_element_type=_F32)
            ext_s[CONV_HALO + rs.start:CONV_HALO + rs.stop, :] = gate
        ups = [jnp.dot(h_s[rs, :], wu, preferred_element_type=_F32) for rs in chunks]
        for rs, up in zip(chunks, ups):
            gc = (cw[0] * ext_s[CONV_HALO - 2 + rs.start:CONV_HALO - 2 + rs.stop, :]
                  + cw[1] * ext_s[CONV_HALO - 1 + rs.start:CONV_HALO - 1 + rs.stop, :]
                  + cw[2] * ext_s[CONV_HALO + rs.start:CONV_HALO + rs.stop, :])
            act = (jax.nn.silu(gc) * up).astype(_BF16)
            out_ref[rs, :] += jnp.dot(act, wd, preferred_element_type=_F32)
        tail = ext_s[tt:tt + CONV_HALO, :]
        carry_s[f] = tail
        st_out[f] = tail
        if last:
            out_ref[...] = _rms_norm(out_ref[...], fn_ref[...])

    @pl.when(f == 0)
    def _():
        tile(True, False)

    if final:
        @pl.when((f > 0) & (f < FF_TILES - 1))
        def _():
            tile(False, False)

        @pl.when(f == FF_TILES - 1)
        def _():
            tile(False, True)
    else:
        @pl.when(f > 0)
        def _():
            tile(False, False)


def _ffn_sample(x, norm2, w_gate, w_up, conv_w, w_down, state, final_norm, layer, final):
    nb, tt, _ = x.shape
    vec = lambda f: (0, 0)
    return pl.pallas_call(
        functools.partial(_ffn_sample_kernel, nb=nb, tt=tt, final=final),
        grid=(FF_TILES,),
        in_specs=[pl.BlockSpec((nb, tt, D_MODEL), lambda f: (0, 0, 0)),
                  pl.BlockSpec((1, D_MODEL), vec),
                  pl.BlockSpec((None, D_MODEL, FF_TILE), lambda f: (layer, 0, f)),
                  pl.BlockSpec((None, D_MODEL, FF_TILE), lambda f: (layer, 0, f)),
                  pl.BlockSpec((3, FF_TILE), lambda f: (0, f)),
                  pl.BlockSpec((None, FF_TILE, D_MODEL), lambda f: (layer, f, 0)),
                  pl.BlockSpec((nb, CONV_HALO, FF_TILE), lambda f: (0, 0, f)),
                  pl.BlockSpec((1, D_MODEL), vec)],
        out_specs=[pl.BlockSpec((nb, tt, D_MODEL), lambda f: (0, 0, 0)),
                   pl.BlockSpec((nb, CONV_HALO, FF_TILE), lambda f: (0, 0, f)),
                   pl.BlockSpec((None, D_MODEL, FF_TILE), lambda f: (f, 0, 0)),
                   pl.BlockSpec((None, D_MODEL, FF_TILE), lambda f: (f, 0, 0)),
                   pl.BlockSpec((None, FF_TILE, D_MODEL), lambda f: (f, 0, 0))],
        out_shape=[jax.ShapeDtypeStruct((nb, tt, D_MODEL), _F32),
                   jax.ShapeDtypeStruct((nb, CONV_HALO, D_FF), _F32),
                   jax.ShapeDtypeStruct((FF_TILES, D_MODEL, FF_TILE), _BF16),
                   jax.ShapeDtypeStruct((FF_TILES, D_MODEL, FF_TILE), _BF16),
                   jax.ShapeDtypeStruct((FF_TILES, FF_TILE, D_MODEL), _BF16)],
        scratch_shapes=[pltpu.VMEM((nb * tt, D_MODEL), _BF16),
                        pltpu.VMEM((nb, tt + CONV_HALO, FF_TILE), _F32)],
        compiler_params=pltpu.CompilerParams(dimension_semantics=("arbitrary",), vmem_limit_bytes=VMEM_LIMIT),
        name="ffn_sample",
    )(x, norm2.reshape(1, D_MODEL), w_gate, w_up, conv_w, w_down, state, final_norm.reshape(1, D_MODEL))


def _ffn_prompt(x, norm2, w_gate_t, w_up_t, conv_w, w_down_t, final_norm, tt, final):
    batch, seq, _ = x.shape
    vec = lambda b, t, f: (0, 0)
    return pl.pallas_call(
        functools.partial(_ffn_prompt_kernel, tt=tt, final=final),
        grid=(batch, seq // tt, FF_TILES),
        in_specs=[pl.BlockSpec((None, tt, D_MODEL), lambda b, t, f: (b, t, 0)),
                  pl.BlockSpec((1, D_MODEL), vec),
                  pl.BlockSpec((None, D_MODEL, FF_TILE), lambda b, t, f: (f, 0, 0)),
                  pl.BlockSpec((None, D_MODEL, FF_TILE), lambda b, t, f: (f, 0, 0)),
                  pl.BlockSpec((FF_TILES, 3, FF_TILE), lambda b, t, f: (0, 0, 0)),
                  pl.BlockSpec((None, FF_TILE, D_MODEL), lambda b, t, f: (f, 0, 0)),
                  pl.BlockSpec((1, D_MODEL), vec)],
        out_specs=[pl.BlockSpec((None, tt, D_MODEL), lambda b, t, f: (b, t, 0)),
                   pl.BlockSpec((None, None, FF_TILES, CONV_HALO, FF_TILE), lambda b, t, f: (b, t, 0, 0, 0))],
        out_shape=[jax.ShapeDtypeStruct((batch, seq, D_MODEL), _F32),
                   jax.ShapeDtypeStruct((batch, seq // tt, FF_TILES, CONV_HALO, FF_TILE), _F32)],
        scratch_shapes=[pltpu.VMEM((tt, D_MODEL), _BF16),
                        pltpu.VMEM((tt + CONV_HALO, FF_TILE), _F32),
                        pltpu.VMEM((FF_TILES, CONV_HALO, FF_TILE), _F32)],
        compiler_params=pltpu.CompilerParams(dimension_semantics=("parallel", "arbitrary", "arbitrary"),
                                             vmem_limit_bytes=VMEM_LIMIT),
        name="ffn_prompt",
    )(x, norm2.reshape(1, D_MODEL), w_gate_t, w_up_t, conv_w, w_down_t, final_norm.reshape(1, D_MODEL))


def _pad_rows_front(a, rows):
    return jnp.pad(a, ((0, 0), (rows - a.shape[1], 0), (0, 0)))


def kernel(x_prompt, x_sample, cache_kv_w128, cache_kv_w512, cache_kv_w2048, state_conv_a, state_pool, state_ffn_conv, norm1, w_in, conv_a_w, w_out, pool_w, pool_scale, norm2, w_gate, w_up, ffn_conv_w, w_down, final_norm):
    bp, sp, _ = x_prompt.shape
    bs, ss, _ = x_sample.shape
    caches = tuple(c.reshape(DEPTH, bs, win * KV_ROWS, HEAD_DIM)
                   for c, (win, _) in zip((cache_kv_w128, cache_kv_w512, cache_kv_w2048), DIL_GROUPS))
    tm, tt = PROMPT_ROW_TILE, MIXER_ROW_TILE
    w_out_b, pool_w_b = w_out.astype(_BF16), pool_w.astype(_BF16)
    xp, xs = x_prompt, x_sample
    kv_p = kv_s = None
    small_p, small_s = [], []
    for l in range(DEPTH):
        final = l == DEPTH - 1

        proj, *kv_new, w_in_t = _inproj_sample(xs.reshape(bs * ss, D_MODEL), norm1[l], w_in, l)
        yb, *kv_s = _attn_sample(proj, kv_new, caches, bs, ss, l, kv_s)
        xs, conv_o, pool_o = _mixer(proj, yb, xs, _pad_rows_front(state_conv_a[l], CONV_HALO),
                                    _pad_rows_front(state_pool[l], POOL_HALO), conv_a_w[l], pool_w_b, pool_scale[l],
                                    w_out_b, l, bs, ss, PAST_LEN, U_TILE0)
        xs, ffn_o, w_gate_t, w_up_t, w_down_t = _ffn_sample(
            xs, norm2[l], w_gate, w_up, ffn_conv_w[l], w_down, _pad_rows_front(state_ffn_conv[l], CONV_HALO),
            final_norm, l, final)
        small_s.append((conv_o[:, CONV_HALO - 2:], pool_o[:, POOL_HALO - POOL_BUF:], ffn_o[:, CONV_HALO - 2:]))

        nat, qkv1, qkv2, *kv_p = _inproj_prompt(xp.reshape(bp * sp, D_MODEL), norm1[l], w_in_t, bp, sp, tm, l, kv_p)
        yb = _attn_prompt(nat, qkv1, qkv2, bp, sp)
        xp, conv_o, pool_o = _mixer(nat, yb, xp, None, None, conv_a_w[l], pool_w_b, pool_scale[l], w_out_b,
                                    l, 1, tt, 0, NAT_TILES - 2)
        conv_w_t = jnp.pad(ffn_conv_w[l], ((0, 0), (0, FF_TILES * FF_TILE - D_FF)))
        conv_w_t = conv_w_t.reshape(3, FF_TILES, FF_TILE).transpose(1, 0, 2)
        xp, ffn_o = _ffn_prompt(xp, norm2[l], w_gate_t, w_up_t, conv_w_t, w_down_t, final_norm, tm, final)
        ffn_tail = ffn_o[:, -1, :, CONV_HALO - 2:].transpose(0, 2, 1, 3).reshape(bp, 2, FF_TILES * FF_TILE)
        small_p.append((conv_o[:, CONV_HALO - 2:], pool_o[:, POOL_HALO - POOL_BUF:], ffn_tail[:, :, :D_FF]))

    out = [xp, xs]
    for g, (win, _) in enumerate(DIL_GROUPS):
        out += [kv_p[g].reshape(DEPTH, bp, win, 2, HEADS, HEAD_DIM), kv_s[g].reshape(DEPTH, bs, win, 2, HEADS, HEAD_DIM)]
    for i in range(3):
        out += [jnp.stack([s[i] for s in small_p]), jnp.stack([s[i] for s in small_s])]
    return tuple(out)
```

```python
import functools

import numpy as np
import jax
import jax.numpy as jnp
from jax import lax
from jax.experimental import pallas as pl
from jax.experimental.pallas import tpu as pltpu

D_MODEL = 2048
DEPTH = 2
PAST_LEN = 16384
CONV_WIDTH = 512
ATT_WIDTH = 512
HEADS = 4
HEAD_DIM = 128
DIL_GROUPS = ((128, 1), (512, 4), (2048, 16))
POOL_WIDTH = 1024
POOL_WINDOWS = (2, 4, 8, 16)
POOL_GROUP = 256
POOL_BUF = 15
IN_COLS = 7168
D_FF = 5504
RMS_EPS = 1e-6
ATT_SCALE = HEAD_DIM ** -0.5

COL_TILE = 512
IN_TILES = IN_COLS // COL_TILE
Q_TILE = (3, 6, 9)
U_TILE0 = 12
NAT_TILES = 8
KV_ROWS = 2 * HEADS
SUB_ROWS = 512
IN_STEP_TILES = 2
REGROUP_STRIDE = 4
FF_TILE = 512
FF_TILES = -(-D_FF // FF_TILE)
FF_LAST = D_FF - (FF_TILES - 1) * FF_TILE
FFN_SUB_ROWS = 512
CONV_HALO = 8
POOL_HALO = 16
POOL_BLOCK = 128
PROMPT_ROW_TILE = 1024
MIXER_ROW_TILE = 512
V7X_VMEM_BYTES = 64 * 1024 * 1024
VMEM_LIMIT = V7X_VMEM_BYTES - 8 * 1024 * 1024
VMEM_LIMIT_HIGH = V7X_VMEM_BYTES - 3 * 1024 * 1024

_BF16 = jnp.bfloat16
_F32 = jnp.float32


def _alibi_slopes():
    h = np.arange(1, 3 * HEADS + 1, dtype=np.float32)
    return [float(v) for v in np.power(np.float32(2.0), -8.0 * h / (3 * HEADS))]


SLOPES = _alibi_slopes()


def _rms_norm(x, g):
    return (x * lax.rsqrt(jnp.mean(x * x, axis=-1, keepdims=True) + RMS_EPS)) * g


def _head_cols(h, base=0):
    return slice(base + h * HEAD_DIM, base + (h + 1) * HEAD_DIM)


def _inproj_sample_kernel(x_ref, g_ref, w_ref, proj_ref, kv0_ref, kv1_ref, kv2_ref, wt_ref, h_s):
    j = pl.program_id(1)

    @pl.when(j == 0)
    def _():
        h_s[...] = _rms_norm(x_ref[...], g_ref[...]).astype(_BF16)

    w = w_ref[...].astype(_BF16)
    wt_ref[...] = w
    acc = jnp.dot(h_s[...], w, preferred_element_type=_F32)
    proj_ref[...] = acc.astype(_BF16)
    for g, ref in enumerate((kv0_ref, kv1_ref, kv2_ref)):
        @pl.when((j == Q_TILE[g] + 1) | (j == Q_TILE[g] + 2))
        def _(ref=ref):
            ref[...] = acc


def _inproj_sample(x2d, g, w, layer):
    m = x2d.shape[0]

    def kv_spec(grp):
        return pl.BlockSpec((m, COL_TILE), lambda i, j: (i, jnp.clip(j - Q_TILE[grp] - 1, 0, 1)))

    return pl.pallas_call(
        _inproj_sample_kernel,
        grid=(1, IN_TILES),
        in_specs=[pl.BlockSpec((m, D_MODEL), lambda i, j: (i, 0)),
                  pl.BlockSpec((None, 1, D_MODEL), lambda i, j: (layer, 0, 0)),
                  pl.BlockSpec((None, D_MODEL, COL_TILE), lambda i, j: (layer, 0, j))],
        out_specs=[pl.BlockSpec((m, COL_TILE), lambda i, j: (i, j)), kv_spec(0), kv_spec(1), kv_spec(2),
                   pl.BlockSpec((None, D_MODEL, COL_TILE), lambda i, j: (j, 0, 0))],
        out_shape=[jax.ShapeDtypeStruct((m, IN_COLS), _BF16)] +
                  [jax.ShapeDtypeStruct((m, 2 * ATT_WIDTH), _F32)] * 3 +
                  [jax.ShapeDtypeStruct((IN_TILES, D_MODEL, COL_TILE), _BF16)],
        scratch_shapes=[pltpu.VMEM((m, D_MODEL), _BF16)],
        compiler_params=pltpu.CompilerParams(dimension_semantics=("parallel", "arbitrary"),
                                             vmem_limit_bytes=VMEM_LIMIT),
        name="inproj_sample",
    )(x2d, g.reshape(DEPTH, 1, D_MODEL), w)


def _inproj_prompt_kernel(*refs, tm, aliased):
    n_in = 6 if aliased else 3
    x_ref, g_ref, w_ref = refs[:3]
    nat_ref, g1_ref, g2_ref, kv0_ref, kv1_ref, kv2_ref, h_s, de_s, de2_s = refs[n_in:]
    kv_refs = (kv0_ref, kv1_ref, kv2_ref)
    res_refs = (None, g1_ref, g2_ref)
    j = pl.program_id(1)

    @pl.when(j == 0)
    def _():
        h_s[...] = _rms_norm(x_ref[...], g_ref[...]).astype(_BF16)

    def column_tiles(tiles):
        for c in range(tm // SUB_ROWS):
            r0 = c * SUB_ROWS
            for half, tile in enumerate(tiles):
                acc = jnp.dot(h_s[r0:r0 + SUB_ROWS, :], w_ref[half], preferred_element_type=_F32)
                grp = (tile - Q_TILE[0]) // 3 if Q_TILE[0] <= tile < U_TILE0 else None
                part = None if grp is None or tile == Q_TILE[grp] else tile - Q_TILE[grp] - 1
                dil = 1 if grp is None else DIL_GROUPS[grp][1]
                if dil == 1:
                    nat_ref[r0:r0 + SUB_ROWS, half * COL_TILE:(half + 1) * COL_TILE] = acc.astype(_BF16)
                else:
                    for h in range(HEADS):
                        de_s[c, half, h] = acc[:, _head_cols(h)]
                    n = SUB_ROWS // dil
                    col0 = (tile - Q_TILE[grp]) * COL_TILE
                    if dil == REGROUP_STRIDE:
                        for res in range(dil):
                            for h in range(HEADS):
                                res_refs[grp][res, c * n:(c + 1) * n, _head_cols(h, col0)] = (
                                    de_s[c, half, h, pl.ds(res, n, stride=dil), :].astype(_BF16))
                    else:
                        assert dil == REGROUP_STRIDE ** 2
                        q = SUB_ROWS // REGROUP_STRIDE
                        for h in range(HEADS):
                            for g in range(REGROUP_STRIDE):
                                de2_s[h, g * q:(g + 1) * q, :] = de_s[c, half, h, pl.ds(g, q, stride=REGROUP_STRIDE), :]
                        for g in range(REGROUP_STRIDE):
                            for r in range(REGROUP_STRIDE):
                                for h in range(HEADS):
                                    res_refs[grp][g + REGROUP_STRIDE * r, c * n:(c + 1) * n, _head_cols(h, col0)] = (
                                        de2_s[h, pl.ds(g * q + r, n, stride=REGROUP_STRIDE), :].astype(_BF16))
                if part is not None:
                    keep = min(DIL_GROUPS[grp][0], tm)
                    lo = max(r0, tm - keep)
                    n = r0 + SUB_ROWS - lo
                    if n > 0:
                        for h in range(HEADS):
                            dst = pl.ds((lo - (tm - keep)) * KV_ROWS + part * HEADS + h, n, stride=KV_ROWS)
                            kv_refs[grp][dst, :] = acc[lo - r0:lo - r0 + n, _head_cols(h)]

    for step in range(IN_TILES // IN_STEP_TILES):
        @pl.when(j == step)
        def _(step=step):
            column_tiles(tuple(range(step * IN_STEP_TILES, (step + 1) * IN_STEP_TILES)))


def _inproj_prompt(x2d, g, w, batch, seq, tm, layer, prev_kv):
    m = x2d.shape[0]
    tps = seq // tm
    aliased = prev_kv is not None
    (w0, _), (w1, d1), (w2, d2) = DIL_GROUPS
    assert w2 == seq and w0 <= tm and w1 <= tm and tm % SUB_ROWS == 0

    in_specs = [pl.BlockSpec((tm, D_MODEL), lambda i, j: (i, 0)),
                pl.BlockSpec((None, 1, D_MODEL), lambda i, j: (layer, 0, 0)),
                pl.BlockSpec((IN_STEP_TILES, D_MODEL, COL_TILE), lambda i, j: (j, 0, 0))]
    args = [x2d, g.reshape(DEPTH, 1, D_MODEL), w]
    if aliased:
        in_specs += [pl.BlockSpec(memory_space=pl.ANY)] * 3
        args += list(prev_kv)

    def res_spec(dil):
        return pl.BlockSpec((None, dil, tm // dil, 3 * ATT_WIDTH), lambda i, j: (i // tps, 0, i % tps, 0))

    assert Q_TILE[1] % IN_STEP_TILES == 0 and U_TILE0 % IN_STEP_TILES == 0 and IN_TILES - U_TILE0 == IN_STEP_TILES
    nat_lo, u_step = Q_TILE[1] // IN_STEP_TILES, U_TILE0 // IN_STEP_TILES
    out_specs = [
        pl.BlockSpec((tm, IN_STEP_TILES * COL_TILE),
                     lambda i, j: (i, jnp.clip(j, 0, nat_lo - 1) + jnp.clip(j - u_step + 1, 0, 1))),
        res_spec(d1), res_spec(d2),
        pl.BlockSpec((None, None, w0 * KV_ROWS, HEAD_DIM), lambda i, j: (layer, i // tps, 0, 0),
                     pipeline_mode=pl.Buffered(1)),
        pl.BlockSpec((None, None, w1 * KV_ROWS, HEAD_DIM), lambda i, j: (layer, i // tps, 0, 0),
                     pipeline_mode=pl.Buffered(1)),
        pl.BlockSpec((None, None, tm * KV_ROWS, HEAD_DIM), lambda i, j: (layer, i // tps, i % tps, 0)),
    ]
    out_shape = [
        jax.ShapeDtypeStruct((m, NAT_TILES * COL_TILE), _BF16),
        jax.ShapeDtypeStruct((batch, d1, seq // d1, 3 * ATT_WIDTH), _BF16),
        jax.ShapeDtypeStruct((batch, d2, seq // d2, 3 * ATT_WIDTH), _BF16),
    ] + [jax.ShapeDtypeStruct((DEPTH, batch, win * KV_ROWS, HEAD_DIM), _F32) for win, _ in DIL_GROUPS]
    return pl.pallas_call(
        functools.partial(_inproj_prompt_kernel, tm=tm, aliased=aliased),
        grid=(m // tm, IN_TILES // IN_STEP_TILES),
        in_specs=in_specs, out_specs=out_specs, out_shape=out_shape,
        scratch_shapes=[pltpu.VMEM((tm, D_MODEL), _BF16),
                        pltpu.VMEM((tm // SUB_ROWS, IN_STEP_TILES, HEADS, SUB_ROWS, HEAD_DIM), _F32),
                        pltpu.VMEM((HEADS, SUB_ROWS, HEAD_DIM), _F32)],
        input_output_aliases={3: 3, 4: 4, 5: 5} if aliased else {},
        compiler_params=pltpu.CompilerParams(dimension_semantics=("arbitrary", "arbitrary"),
                                             vmem_limit_bytes=VMEM_LIMIT_HIGH),
        name="inproj_prompt",
    )(*args)


QB = 128
G0_UNROLL = 3
G1_RES_PER_STEP = 4
G2_RES_PER_STEP = 8


def _attn_prompt_kernel(q0, k0, v0, q1, k1, v1, q2, k2, v2, o_ref, acc_s, m_s, l_s, shuf_s, *, seq):
    step = pl.program_id(1)
    n1 = DIL_GROUPS[1][1]
    n1_steps = n1 // G1_RES_PER_STEP
    n2_steps = DIL_GROUPS[2][1] // G2_RES_PER_STEP

    row0 = lax.broadcasted_iota(jnp.int32, (QB, QB), 0)
    col0 = lax.broadcasted_iota(jnp.int32, (QB, QB), 1)
    dist_first = jnp.where(col0 <= row0, (row0 - col0).astype(_F32), jnp.inf)
    row1 = lax.broadcasted_iota(jnp.int32, (QB, 2 * QB), 0)
    col1 = lax.broadcasted_iota(jnp.int32, (QB, 2 * QB), 1)
    d1 = QB + row1 - col1
    dist_next = jnp.where((d1 >= 0) & (d1 <= QB), d1.astype(_F32), jnp.inf)
    ones = jnp.ones((2 * QB, HEAD_DIM), _BF16)

    seg = seq // n1
    chunk = QB // n1
    assert n1 == 4 and DIL_GROUPS[2][1] == 16

    def blocks(specs):
        tiles = []
        for slot, (qr, kr, vr, grp, res, i, is_first_block) in enumerate(specs):
            dil = DIL_GROUPS[grp][1]
            for h in range(HEADS):
                hs = _head_cols(h)
                slope_step = SLOPES[grp * HEADS + h] * dil
                if is_first_block:
                    q, k, v = qr[0:QB, hs], kr[0:QB, hs], vr[0:QB, hs]
                    bias, one = dist_first * slope_step, ones[0:QB]
                else:
                    qs = pl.ds(pl.multiple_of(i * QB, QB), QB)
                    ks = pl.ds(pl.multiple_of(i * QB - QB, QB), 2 * QB)
                    q, k, v = qr[qs, hs], kr[ks, hs], vr[ks, hs]
                    bias, one = dist_next * slope_step, ones
                tiles.append((q, k, jnp.concatenate([v, one], axis=1), bias, grp, res, i, is_first_block, slot, h))
        scores = [lax.dot_general(t[0], t[1], (((1,), (1,)), ((), ())), preferred_element_type=_F32) for t in tiles]
        probs, maxes = [], []
        for t, s in zip(tiles, scores):
            s = s * ATT_SCALE - t[3]
            m = jnp.max(s, axis=1, keepdims=True)
            probs.append(jnp.exp(s - m).astype(_BF16))
            maxes.append(jnp.broadcast_to(m, (QB, HEAD_DIM)))
        outs = [jnp.dot(p, t[2], preferred_element_type=_F32) for t, p in zip(tiles, probs)]
        def merge(h, rows, acc, m, l):
            m_old, l_old, a_old = m_s[h, rows, :], l_s[h, rows, :], acc_s[h, rows, :]
            m_new = jnp.maximum(m_old, m)
            w_old = jnp.exp(m_old - m_new)
            w_new = jnp.exp(m - m_new)
            acc_s[h, rows, :] = a_old * w_old + acc * w_new
            l_s[h, rows, :] = l_old * w_old + l * w_new
            m_s[h, rows, :] = m_new

        for (_, _, _, _, grp, res, i, is_first_block, slot, h), m, acc_l in zip(tiles, maxes, outs):
            acc, l = acc_l[:, :HEAD_DIM], acc_l[:, HEAD_DIM:]
            if grp == 1:
                rows = pl.ds(pl.multiple_of(res * seg + i * QB, QB), QB)
                acc_s[h, rows, :], m_s[h, rows, :], l_s[h, rows, :] = acc, m, l
            elif grp == 2:
                merge(h, pl.ds((res & (n1 - 1)) * seg + (res >> 2), QB, stride=n1), acc, m, l)
            else:
                tmps = [shuf_s.at[(slot * HEADS + h) * 3 + a] for a in range(3)]
                for tmp, val in zip(tmps, (acc, m, l)):
                    tmp[...] = val
                for r4 in range(n1):
                    dst = r4 * seg + i * chunk
                    dst = pl.ds(dst if is_first_block else pl.multiple_of(dst, chunk), chunk)
                    merge(h, dst, *(tmp[pl.ds(r4, chunk, stride=n1), :] for tmp in tmps))

    @pl.when(step < n1_steps)
    def _():
        blocks([(q1.at[r], k1.at[r], v1.at[r], 1, step * G1_RES_PER_STEP + r, i, i == 0)
                for r in range(G1_RES_PER_STEP) for i in range(seg // QB)])

    @pl.when((step >= n1_steps) & (step < n1_steps + n2_steps))
    def _():
        assert seq // DIL_GROUPS[2][1] == QB
        blocks([(q2.at[r], k2.at[r], v2.at[r], 2, (step - n1_steps) * G2_RES_PER_STEP + r, 0, True)
                for r in range(G2_RES_PER_STEP)])

    @pl.when(step == n1_steps + n2_steps)
    def _():
        n_blk = seq // QB
        assert (n_blk - 1) % G0_UNROLL == 0
        blocks([(q0, k0, v0, 0, 0, 0, True)])

        def g0_body(it, carry):
            blocks([(q0, k0, v0, 0, 0, 1 + it * G0_UNROLL + u, False) for u in range(G0_UNROLL)])
            return carry
        lax.fori_loop(0, (n_blk - 1) // G0_UNROLL, g0_body, 0)

        def body(c, carry):
            for h in range(HEADS):
                tmp = shuf_s.at[h]
                for r4 in range(n1):
                    src = pl.ds(pl.multiple_of(r4 * seg + c * chunk, chunk), chunk)
                    tmp[pl.ds(r4, chunk, stride=n1), :] = acc_s[h, src, :] / l_s[h, src, :]
                o_ref[pl.ds(pl.multiple_of(c * QB, QB), QB), _head_cols(h)] = tmp[...].astype(_BF16)
            return carry
        lax.fori_loop(0, seq // QB, body, 0)


def _attn_prompt(nat, qkv1, qkv2, batch, seq):
    n1, n2 = DIL_GROUPS[1][1], DIL_GROUPS[2][1]
    n1_steps, n2_steps = n1 // G1_RES_PER_STEP, n2 // G2_RES_PER_STEP
    nat3 = nat.reshape(batch, seq, NAT_TILES * COL_TILE)
    in_specs = [pl.BlockSpec((None, seq, COL_TILE), lambda b, s, t=t: (b, 0, Q_TILE[0] + t)) for t in range(3)]
    in_specs += [pl.BlockSpec((None, G1_RES_PER_STEP, seq // n1, COL_TILE),
                              lambda b, s, t=t: (b, jnp.clip(s, 0, n1_steps - 1), 0, t)) for t in range(3)]
    in_specs += [pl.BlockSpec((None, G2_RES_PER_STEP, seq // n2, COL_TILE),
                              lambda b, s, t=t: (b, jnp.clip(s - n1_steps, 0, n2_steps - 1), 0, t)) for t in range(3)]
    args = [nat3] * 3 + [qkv1] * 3 + [qkv2] * 3
    return pl.pallas_call(
        functools.partial(_attn_prompt_kernel, seq=seq),
        grid=(batch, 1 + n1_steps + n2_steps),
        in_specs=in_specs,
        out_specs=pl.BlockSpec((None, seq, ATT_WIDTH), lambda b, s: (b, 0, 0)),
        out_shape=jax.ShapeDtypeStruct((batch, seq, ATT_WIDTH), _BF16),
        scratch_shapes=[pltpu.VMEM((HEADS, seq, HEAD_DIM), _F32)] * 3
                       + [pltpu.VMEM((G0_UNROLL * HEADS * 3, QB, HEAD_DIM), _F32)],
        compiler_params=pltpu.CompilerParams(dimension_semantics=("parallel", "arbitrary"),
                                             vmem_limit_bytes=VMEM_LIMIT),
        name="attn_prompt",
    )(*args)


def _attn_sample_kernel(*refs, tdec, aliased):
    n_in = 10 if aliased else 7
    proj_ref, n0, n1, n2, c0, c1, c2 = refs[:7]
    yb_ref, o0, o1, o2 = refs[n_in:]
    news, caches, outs = (n0, n1, n2), (c0, c1, c2), (o0, o1, o2)
    for h in range(HEADS):
        pieces = []
        for grp, (win, dil) in enumerate(DIL_GROUPS):
            slope = SLOPES[grp * HEADS + h]
            q = proj_ref[:, _head_cols(h, Q_TILE[grp] * COL_TILE)]
            k_new = news[grp][:, _head_cols(h)]
            v_new = news[grp][:, _head_cols(h, ATT_WIDTH)]
            k_old = caches[grp][pl.ds(h, win, stride=KV_ROWS), :]
            v_old = caches[grp][pl.ds(HEADS + h, win, stride=KV_ROWS), :]
            for k, v, n_keys, base in ((k_old, v_old, win, win), (k_new, v_new, tdec, 0)):
                s = lax.dot_general(q, k.astype(_BF16), (((1,), (1,)), ((), ())), preferred_element_type=_F32)
                t = lax.broadcasted_iota(jnp.int32, (tdec, n_keys), 0)
                c = lax.broadcasted_iota(jnp.int32, (tdec, n_keys), 1)
                dist = base + t - c
                ok = (dist >= 0) & (dist <= win) & ((dist & (dil - 1)) == 0)
                s = jnp.where(ok, s * ATT_SCALE - slope * dist.astype(_F32), -jnp.inf)
                pieces.append((s, v.astype(_BF16)))
        m = functools.reduce(jnp.maximum, [jnp.max(s, axis=1, keepdims=True) for s, _ in pieces])
        l = jnp.zeros((tdec, 1), _F32)
        acc = jnp.zeros((tdec, HEAD_DIM), _F32)
        for s, v in pieces:
            p = jnp.exp(s - m)
            l = l + jnp.sum(p, axis=1, keepdims=True)
            acc = acc + jnp.dot(p.astype(_BF16), v, preferred_element_type=_F32)
        yb_ref[:, _head_cols(h)] = (acc / l).astype(_BF16)
    for grp, (win, _) in enumerate(DIL_GROUPS):
        kept = (win - tdec) * KV_ROWS
        outs[grp][0:kept, :] = caches[grp][tdec * KV_ROWS:win * KV_ROWS, :]
        for part in range(2):
            for h in range(HEADS):
                outs[grp][pl.ds(kept + part * HEADS + h, tdec, stride=KV_ROWS), :] = (
                    news[grp][:, _head_cols(h, part * ATT_WIDTH)])


def _attn_sample(proj, kv_new, caches, batch, tdec, layer, prev_out):
    aliased = prev_out is not None
    in_specs = [pl.BlockSpec((None, tdec, IN_COLS), lambda b: (b, 0, 0))]
    in_specs += [pl.BlockSpec((None, tdec, 2 * ATT_WIDTH), lambda b: (b, 0, 0))] * 3
    cache_specs = [pl.BlockSpec((None, None, win * KV_ROWS, HEAD_DIM), lambda b: (layer, b, 0, 0)) for win, _ in DIL_GROUPS]
    in_specs += cache_specs
    args = [proj.reshape(batch, tdec, IN_COLS)] + [k.reshape(batch, tdec, 2 * ATT_WIDTH) for k in kv_new] + list(caches)
    if aliased:
        in_specs += [pl.BlockSpec(memory_space=pl.ANY)] * 3
        args += list(prev_out)
    out_specs = [pl.BlockSpec((None, tdec, ATT_WIDTH), lambda b: (b, 0, 0))] + cache_specs
    out_shape = [jax.ShapeDtypeStruct((batch, tdec, ATT_WIDTH), _BF16)]
    out_shape += [jax.ShapeDtypeStruct((DEPTH, batch, win * KV_ROWS, HEAD_DIM), _F32) for win, _ in DIL_GROUPS]
    return pl.pallas_call(
        functools.partial(_attn_sample_kernel, tdec=tdec, aliased=aliased),
        grid=(batch,),
        in_specs=in_specs, out_specs=out_specs, out_shape=out_shape,
        input_output_aliases={7: 1, 8: 2, 9: 3} if aliased else {},
        compiler_params=pltpu.CompilerParams(dimension_semantics=("arbitrary",), vmem_limit_bytes=VMEM_LIMIT),
        name="attn_sample",
    )(*args)


def _mixer_kernel(*refs, nb, tt, start, from_zero):
    if from_zero:
        pa_ref, pu_ref, yb_ref, x_ref, cw_ref, pw_ref, psc_ref, wo_ref = refs[:8]
    else:
        pa_ref, pu_ref, yb_ref, x_ref, cs0_ref, ps0_ref, cw_ref, pw_ref, psc_ref, wo_ref = refs[:10]
    out_ref, cs_out, ps_out, ext_a, ext_u, yc = refs[-6:]
    ti = pl.program_id(1)
    rows = nb * tt
    pool_halo = POOL_BLOCK if from_zero else POOL_HALO

    @pl.when(ti == 0)
    def _():
        if from_zero:
            ext_a[:, 0:CONV_HALO, :] = jnp.zeros((nb, CONV_HALO, CONV_WIDTH), _F32)
            ext_u[:, 0:pool_halo, :] = jnp.zeros((nb, pool_halo, POOL_WIDTH), _BF16)
        else:
            ext_a[:, 0:CONV_HALO, :] = cs0_ref[...]
            ext_u[:, 0:pool_halo, :] = ps0_ref[...]

    yb = yb_ref[...].astype(_F32).reshape(rows, ATT_WIDTH).astype(_BF16)
    mixed = jnp.dot(yb, wo_ref[CONV_WIDTH:CONV_WIDTH + ATT_WIDTH, :], preferred_element_type=_F32)

    pa = pa_ref[...].astype(_F32)
    xa, gate_b, gate_c = (pa[:, :, k * CONV_WIDTH:(k + 1) * CONV_WIDTH] for k in range(3))
    prod = gate_c * xa
    ext_a[:, CONV_HALO:CONV_HALO + tt, :] = prod
    cw = cw_ref[...]
    cu = (cw[0] * ext_a[:, CONV_HALO - 2:CONV_HALO - 2 + tt, :]
          + cw[1] * ext_a[:, CONV_HALO - 1:CONV_HALO - 1 + tt, :] + cw[2] * prod)
    ya = (gate_b * cu).reshape(rows, CONV_WIDTH).astype(_BF16)
    mixed = mixed + jnp.dot(ya, wo_ref[0:CONV_WIDTH, :], preferred_element_type=_F32)

    ext_u[:, pool_halo:pool_halo + tt, :] = pu_ref[...].astype(ext_u.dtype)
    if from_zero:
        pb = POOL_BLOCK
        row = lax.broadcasted_iota(jnp.int32, (pb, 2 * pb), 0)
        col = lax.broadcasted_iota(jnp.int32, (pb, 2 * pb), 1)
        lag = pb + row - col
        for gi, w in enumerate(POOL_WINDOWS):
            cols = slice(gi * POOL_GROUP, (gi + 1) * POOL_GROUP)
            band = ((lag >= 0) & (lag < w)).astype(_F32).astype(_BF16)
            for r in range(tt // pb):
                tot = jnp.dot(band, ext_u[0, r * pb:(r + 2) * pb, cols], preferred_element_type=_F32)
                pos = start + ti * tt + r * pb + lax.broadcasted_iota(jnp.int32, (pb, 1), 0)
                cnt = jnp.minimum(w, pos + 1).astype(_F32)
                ug = pu_ref[0, r * pb:(r + 1) * pb, cols].astype(_F32)
                diff = (tot / cnt - ug).astype(_BF16)
                z = jnp.dot(diff, pw_ref[gi], preferred_element_type=_F32) * psc_ref[:, cols]
                yc[r * pb:(r + 1) * pb, cols] = z.astype(_BF16)
    else:
        u = pu_ref[...].astype(_F32)
        pos = start + ti * tt + lax.broadcasted_iota(jnp.int32, (1, tt, 1), 1)
        for gi, w in enumerate(POOL_WINDOWS):
            cols = slice(gi * POOL_GROUP, (gi + 1) * POOL_GROUP)
            ug = u[:, :, cols]
            tot = ug
            for k in range(1, w):
                tot = tot + ext_u[:, pool_halo - k:pool_halo - k + tt, cols]
            cnt = jnp.minimum(w, pos + 1).astype(_F32)
            diff = (tot / cnt - ug).reshape(rows, POOL_GROUP).astype(_BF16)
            z = jnp.dot(diff, pw_ref[gi], preferred_element_type=_F32) * psc_ref[:, cols]
            yc[:, cols] = z.astype(_BF16)

    mixed = mixed + jnp.dot(yc[...], wo_ref[CONV_WIDTH + ATT_WIDTH:, :], preferred_element_type=_F32)
    out_ref[...] = x_ref[...] + mixed.reshape(nb, tt, D_MODEL)

    tail_a = ext_a[:, tt:tt + CONV_HALO, :]
    ext_a[:, 0:CONV_HALO, :] = tail_a
    cs_out[...] = tail_a
    ps_out[...] = ext_u[:, pool_halo + tt - POOL_HALO:pool_halo + tt, :].astype(_F32)
    ext_u[:, 0:pool_halo, :] = ext_u[:, tt:tt + pool_halo, :]


def _mixer(proj, yb, x, conv_state, pool_state, conv_w, pool_w, pool_scale, w_out, layer, nb, tt, start, u_tile0):
    batch, seq, _ = x.shape
    proj3 = proj.reshape(batch, seq, proj.shape[-1])
    u_blk = u_tile0 * COL_TILE // POOL_WIDTH
    const2 = lambda b, t: (0, 0)
    from_zero = conv_state is None and pool_state is None
    assert from_zero or (conv_state is not None and pool_state is not None)
    assert not from_zero or (nb == 1 and start == 0 and tt % POOL_BLOCK == 0)
    state_specs = [] if from_zero else [pl.BlockSpec((nb, CONV_HALO, CONV_WIDTH), lambda b, t: (b, 0, 0)),
                                        pl.BlockSpec((nb, POOL_HALO, POOL_WIDTH), lambda b, t: (b, 0, 0))]
    state_args = [] if from_zero else [conv_state, pool_state]
    ext_u = (pltpu.VMEM((nb, tt + POOL_BLOCK, POOL_WIDTH), _BF16) if from_zero
             else pltpu.VMEM((nb, tt + POOL_HALO, POOL_WIDTH), _F32))
    return pl.pallas_call(
        functools.partial(_mixer_kernel, nb=nb, tt=tt, start=start, from_zero=from_zero),
        grid=(batch // nb, seq // tt),
        in_specs=[pl.BlockSpec((nb, tt, 3 * CONV_WIDTH), lambda b, t: (b, t, 0)),
                  pl.BlockSpec((nb, tt, POOL_WIDTH), lambda b, t: (b, t, u_blk)),
                  pl.BlockSpec((nb, tt, ATT_WIDTH), lambda b, t: (b, t, 0)),
                  pl.BlockSpec((nb, tt, D_MODEL), lambda b, t: (b, t, 0))] + state_specs + [
                  pl.BlockSpec((3, CONV_WIDTH), const2),
                  pl.BlockSpec((None, len(POOL_WINDOWS), POOL_GROUP, POOL_GROUP), lambda b, t: (layer, 0, 0, 0)),
                  pl.BlockSpec((1, POOL_WIDTH), const2),
                  pl.BlockSpec((None, D_MODEL, D_MODEL), lambda b, t: (layer, 0, 0))],
        out_specs=[pl.BlockSpec((nb, tt, D_MODEL), lambda b, t: (b, t, 0)),
                   pl.BlockSpec((nb, CONV_HALO, CONV_WIDTH), lambda b, t: (b, 0, 0)),
                   pl.BlockSpec((nb, POOL_HALO, POOL_WIDTH), lambda b, t: (b, 0, 0))],
        out_shape=[jax.ShapeDtypeStruct((batch, seq, D_MODEL), _F32),
                   jax.ShapeDtypeStruct((batch, CONV_HALO, CONV_WIDTH), _F32),
                   jax.ShapeDtypeStruct((batch, POOL_HALO, POOL_WIDTH), _F32)],
        scratch_shapes=[pltpu.VMEM((nb, tt + CONV_HALO, CONV_WIDTH), _F32), ext_u,
                        pltpu.VMEM((nb * tt, POOL_WIDTH), _BF16)],
        compiler_params=pltpu.CompilerParams(dimension_semantics=("parallel", "arbitrary"),
                                             vmem_limit_bytes=VMEM_LIMIT),
        name="mixer",
    )(proj3, proj3, yb, x, *state_args, conv_w, pool_w, pool_scale.reshape(1, POOL_WIDTH), w_out)


def _ffn_sample_kernel(x_ref, n2_ref, wg_ref, wu_ref, cw_ref, wd_ref, st0_ref, fn_ref,
                       out_ref, st_out, wgt_ref, wut_ref, wdt_ref, h_s, ext_s, *, nb, tt, final):
    f = pl.program_id(0)
    rows = nb * tt

    @pl.when(f == 0)
    def _():
        x = x_ref[...]
        h_s[...] = _rms_norm(x, n2_ref[...]).reshape(rows, D_MODEL).astype(_BF16)
        out_ref[...] = x

    def tile(width):
        cols = slice(0, width)
        wg, wu, wd = wg_ref[:, cols].astype(_BF16), wu_ref[:, cols].astype(_BF16), wd_ref[cols, :].astype(_BF16)
        wgt_ref[:, cols], wut_ref[:, cols], wdt_ref[cols, :] = wg, wu, wd
        if width < FF_TILE:
            wgt_ref[:, width:] = jnp.zeros((D_MODEL, FF_TILE - width), _BF16)
            wut_ref[:, width:] = jnp.zeros((D_MODEL, FF_TILE - width), _BF16)
            wdt_ref[width:, :] = jnp.zeros((FF_TILE - width, D_MODEL), _BF16)
        h = h_s[...]
        gate = jnp.dot(h, wg, preferred_element_type=_F32).reshape(nb, tt, width)
        up = jnp.dot(h, wu, preferred_element_type=_F32).reshape(nb, tt, width)
        ext_s[:, 0:CONV_HALO, cols] = st0_ref[:, :, cols]
        ext_s[:, CONV_HALO:CONV_HALO + tt, cols] = gate
        cw = cw_ref[:, cols]
        gc = (cw[0] * ext_s[:, CONV_HALO - 2:CONV_HALO - 2 + tt, cols]
              + cw[1] * ext_s[:, CONV_HALO - 1:CONV_HALO - 1 + tt, cols] + cw[2] * gate)
        st_out[:, :, cols] = ext_s[:, tt:tt + CONV_HALO, cols]
        act = (jax.nn.silu(gc) * up).reshape(rows, width).astype(_BF16)
        out_ref[...] += jnp.dot(act, wd, preferred_element_type=_F32).reshape(nb, tt, D_MODEL)

    @pl.when(f < FF_TILES - 1)
    def _():
        tile(FF_TILE)

    @pl.when(f == FF_TILES - 1)
    def _():
        tile(FF_LAST)
        if final:
            out_ref[...] = _rms_norm(out_ref[...], fn_ref[...])


def _ffn_prompt_kernel(x_ref, n2_ref, wg_ref, wu_ref, cw_ref, wd_ref, fn_ref,
                       out_ref, st_out, h_s, ext_s, carry_s, *, tt, final):
    ti = pl.program_id(1)
    f = pl.program_id(2)

    def tile(first, last):
        if first:
            @pl.when(ti == 0)
            def _():
                carry_s[...] = jnp.zeros(carry_s.shape, _F32)
            x = x_ref[...]
            h_s[...] = _rms_norm(x, n2_ref[...]).astype(_BF16)
            out_ref[...] = x
        ext_s[0:CONV_HALO, :] = carry_s[f]

        wg, wu, wd, cw = wg_ref[...], wu_ref[...], wd_ref[...], cw_ref[f]
        chunks = [slice(r0, r0 + FFN_SUB_ROWS) for r0 in range(0, tt, FFN_SUB_ROWS)]
        for rs in chunks:
            gate = jnp.dot(h_s[rs, :], wg, preferred_element_type=_F32)
            ext_s[CONV_HALO + rs.start:CONV_HALO + rs.stop, :] = gate
        ups = [jnp.dot(h_s[rs, :], wu, preferred_element_type=_F32) for rs in chunks]
        for rs, up in zip(chunks, ups):
            gc = (cw[0] * ext_s[CONV_HALO - 2 + rs.start:CONV_HALO - 2 + rs.stop, :]
                  + cw[1] * ext_s[CONV_HALO - 1 + rs.start:CONV_HALO - 1 + rs.stop, :]
                  + cw[2] * ext_s[CONV_HALO + rs.start:CONV_HALO + rs.stop, :])
            act = (jax.nn.silu(gc) * up).astype(_BF16)
            out_ref[rs, :] += jnp.dot(act, wd, preferred_element_type=_F32)
        tail = ext_s[tt:tt + CONV_HALO, :]
        carry_s[f] = tail
        st_out[f] = tail
        if last:
            out_ref[...] = _rms_norm(out_ref[...], fn_ref[...])

    @pl.when(f == 0)
    def _():
        tile(True, False)

    if final:
        @pl.when((f > 0) & (f < FF_TILES - 1))
        def _():
            tile(False, False)

        @pl.when(f == FF_TILES - 1)
        def _():
            tile(False, True)
    else:
        @pl.when(f > 0)
        def _():
            tile(False, False)


def _ffn_sample(x, norm2, w_gate, w_up, conv_w, w_down, state, final_norm, layer, final):
    nb, tt, _ = x.shape
    vec = lambda f: (0, 0)
    return pl.pallas_call(
        functools.partial(_ffn_sample_kernel, nb=nb, tt=tt, final=final),
        grid=(FF_TILES,),
        in_specs=[pl.BlockSpec((nb, tt, D_MODEL), lambda f: (0, 0, 0)),
                  pl.BlockSpec((None, 1, D_MODEL), lambda f: (layer, 0, 0)),
                  pl.BlockSpec((None, D_MODEL, FF_TILE), lambda f: (layer, 0, f)),
                  pl.BlockSpec((None, D_MODEL, FF_TILE), lambda f: (layer, 0, f)),
                  pl.BlockSpec((3, FF_TILE), lambda f: (0, f)),
                  pl.BlockSpec((None, FF_TILE, D_MODEL), lambda f: (layer, f, 0)),
                  pl.BlockSpec((nb, CONV_HALO, FF_TILE), lambda f: (0, 0, f)),
                  pl.BlockSpec((1, D_MODEL), vec)],
        out_specs=[pl.BlockSpec((nb, tt, D_MODEL), lambda f: (0, 0, 0)),
                   pl.BlockSpec((nb, CONV_HALO, FF_TILE), lambda f: (0, 0, f)),
                   pl.BlockSpec((None, D_MODEL, FF_TILE), lambda f: (f, 0, 0)),
                   pl.BlockSpec((None, D_MODEL, FF_TILE), lambda f: (f, 0, 0)),
                   pl.BlockSpec((None, FF_TILE, D_MODEL), lambda f: (f, 0, 0))],
        out_shape=[jax.ShapeDtypeStruct((nb, tt, D_MODEL), _F32),
                   jax.ShapeDtypeStruct((nb, CONV_HALO, D_FF), _F32),
                   jax.ShapeDtypeStruct((FF_TILES, D_MODEL, FF_TILE), _BF16),
                   jax.ShapeDtypeStruct((FF_TILES, D_MODEL, FF_TILE), _BF16),
                   jax.ShapeDtypeStruct((FF_TILES, FF_TILE, D_MODEL), _BF16)],
        scratch_shapes=[pltpu.VMEM((nb * tt, D_MODEL), _BF16),
                        pltpu.VMEM((nb, tt + CONV_HALO, FF_TILE), _F32)],
        compiler_params=pltpu.CompilerParams(dimension_semantics=("arbitrary",), vmem_limit_bytes=VMEM_LIMIT),
        name="ffn_sample",
    )(x, norm2.reshape(DEPTH, 1, D_MODEL), w_gate, w_up, conv_w, w_down, state, final_norm.reshape(1, D_MODEL))


def _ffn_prompt(x, norm2, w_gate_t, w_up_t, conv_w, w_down_t, final_norm, layer, tt, final):
    batch, seq, _ = x.shape
    vec = lambda b, t, f: (0, 0)
    return pl.pallas_call(
        functools.partial(_ffn_prompt_kernel, tt=tt, final=final),
        grid=(batch, seq // tt, FF_TILES),
        in_specs=[pl.BlockSpec((None, tt, D_MODEL), lambda b, t, f: (b, t, 0)),
                  pl.BlockSpec((None, 1, D_MODEL), lambda b, t, f: (layer, 0, 0)),
                  pl.BlockSpec((None, D_MODEL, FF_TILE), lambda b, t, f: (f, 0, 0)),
                  pl.BlockSpec((None, D_MODEL, FF_TILE), lambda b, t, f: (f, 0, 0)),
                  pl.BlockSpec((None, FF_TILES, 3, FF_TILE), lambda b, t, f: (layer, 0, 0, 0)),
                  pl.BlockSpec((None, FF_TILE, D_MODEL), lambda b, t, f: (f, 0, 0)),
                  pl.BlockSpec((1, D_MODEL), vec)],
        out_specs=[pl.BlockSpec((None, tt, D_MODEL), lambda b, t, f: (b, t, 0)),
                   pl.BlockSpec((None, None, FF_TILES, CONV_HALO, FF_TILE), lambda b, t, f: (b, t, 0, 0, 0))],
        out_shape=[jax.ShapeDtypeStruct((batch, seq, D_MODEL), _F32),
                   jax.ShapeDtypeStruct((batch, seq // tt, FF_TILES, CONV_HALO, FF_TILE), _F32)],
        scratch_shapes=[pltpu.VMEM((tt, D_MODEL), _BF16),
                        pltpu.VMEM((tt + CONV_HALO, FF_TILE), _F32),
                        pltpu.VMEM((FF_TILES, CONV_HALO, FF_TILE), _F32)],
        compiler_params=pltpu.CompilerParams(dimension_semantics=("parallel", "arbitrary", "arbitrary"),
                                             vmem_limit_bytes=VMEM_LIMIT),
        name="ffn_prompt",
    )(x, norm2.reshape(DEPTH, 1, D_MODEL), w_gate_t, w_up_t, conv_w, w_down_t, final_norm.reshape(1, D_MODEL))


def _pad_rows_front(a, rows):
    return jnp.pad(a, ((0, 0), (rows - a.shape[1], 0), (0, 0)))


def kernel(x_prompt, x_sample, cache_kv_w128, cache_kv_w512, cache_kv_w2048, state_conv_a, state_pool, state_ffn_conv, norm1, w_in, conv_a_w, w_out, pool_w, pool_scale, norm2, w_gate, w_up, ffn_conv_w, w_down, final_norm):
    bp, sp, _ = x_prompt.shape
    bs, ss, _ = x_sample.shape
    caches = tuple(c.reshape(DEPTH, bs, win * KV_ROWS, HEAD_DIM)
                   for c, (win, _) in zip((cache_kv_w128, cache_kv_w512, cache_kv_w2048), DIL_GROUPS))
    tm, tt = PROMPT_ROW_TILE, MIXER_ROW_TILE
    w_out_b, pool_w_b = w_out.astype(_BF16), pool_w.astype(_BF16)
    conv_w_t = jnp.pad(ffn_conv_w, ((0, 0), (0, 0), (0, FF_TILES * FF_TILE - D_FF)))
    conv_w_t = conv_w_t.reshape(DEPTH, 3, FF_TILES, FF_TILE).transpose(0, 2, 1, 3)
    xp, xs = x_prompt, x_sample
    kv_p = kv_s = None
    small_p, small_s = [], []
    for l in range(DEPTH):
        final = l == DEPTH - 1

        proj, *kv_new, w_in_t = _inproj_sample(xs.reshape(bs * ss, D_MODEL), norm1, w_in, l)
        yb, *kv_s = _attn_sample(proj, kv_new, caches, bs, ss, l, kv_s)
        xs, conv_o, pool_o = _mixer(proj, yb, xs, _pad_rows_front(state_conv_a[l], CONV_HALO),
                                    _pad_rows_front(state_pool[l], POOL_HALO), conv_a_w[l], pool_w_b, pool_scale[l],
                                    w_out_b, l, bs, ss, PAST_LEN, U_TILE0)
        xs, ffn_o, w_gate_t, w_up_t, w_down_t = _ffn_sample(
            xs, norm2, w_gate, w_up, ffn_conv_w[l], w_down, _pad_rows_front(state_ffn_conv[l], CONV_HALO),
            final_norm, l, final)
        small_s.append((conv_o[:, CONV_HALO - 2:], pool_o[:, POOL_HALO - POOL_BUF:], ffn_o[:, CONV_HALO - 2:]))

        nat, qkv1, qkv2, *kv_p = _inproj_prompt(xp.reshape(bp * sp, D_MODEL), norm1, w_in_t, bp, sp, tm, l, kv_p)
        yb = _attn_prompt(nat, qkv1, qkv2, bp, sp)
        xp, conv_o, pool_o = _mixer(nat, yb, xp, None, None, conv_a_w[l], pool_w_b, pool_scale[l], w_out_b,
                                    l, 1, tt, 0, NAT_TILES - 2)
        xp, ffn_o = _ffn_prompt(xp, norm2, w_gate_t, w_up_t, conv_w_t, w_down_t, final_norm, l, tm, final)
        ffn_tail = ffn_o[:, -1, :, CONV_HALO - 2:].transpose(0, 2, 1, 3).reshape(bp, 2, FF_TILES * FF_TILE)
        small_p.append((conv_o[:, CONV_HALO - 2:], pool_o[:, POOL_HALO - POOL_BUF:], ffn_tail[:, :, :D_FF]))

    out = [xp, xs]
    for g, (win, _) in enumerate(DIL_GROUPS):
        out += [kv_p[g].reshape(DEPTH, bp, win, 2, HEADS, HEAD_DIM), kv_s[g].reshape(DEPTH, bs, win, 2, HEADS, HEAD_DIM)]
    for i in range(3):
        out += [jnp.stack([s[i] for s in small_p]), jnp.stack([s[i] for s in small_s])]
    return tuple(out)
```

```python
import functools

import numpy as np
import jax
import jax.numpy as jnp
from jax import lax
from jax.experimental import pallas as pl
from jax.experimental.pallas import tpu as pltpu

D_MODEL = 2048
DEPTH = 2
PAST_LEN = 16384
CONV_WIDTH = 512
ATT_WIDTH = 512
HEADS = 4
HEAD_DIM = 128
DIL_GROUPS = ((128, 1), (512, 4), (2048, 16))
POOL_WIDTH = 1024
POOL_WINDOWS = (2, 4, 8, 16)
POOL_GROUP = 256
POOL_BUF = 15
IN_COLS = 7168
D_FF = 5504
RMS_EPS = 1e-6
ATT_SCALE = HEAD_DIM ** -0.5

COL_TILE = 512
IN_TILES = IN_COLS // COL_TILE
Q_TILE = (3, 6, 9)
U_TILE0 = 12
NAT_TILES = 8
KV_ROWS = 2 * HEADS
SUB_ROWS = 512
IN_STEP_TILES = 2
REGROUP_STRIDE = 4
FF_TILE = 512
FF_TILES = -(-D_FF // FF_TILE)
FF_LAST = D_FF - (FF_TILES - 1) * FF_TILE
FFN_SUB_ROWS = 512
SHIFT_ROWS = 960
CONV_HALO = 8
POOL_HALO = 16
POOL_BLOCK = 128
PROMPT_ROW_TILE = 1024
MIXER_ROW_TILE = 512
V7X_VMEM_BYTES = 64 * 1024 * 1024
VMEM_LIMIT = V7X_VMEM_BYTES - 8 * 1024 * 1024
VMEM_LIMIT_HIGH = V7X_VMEM_BYTES - 3 * 1024 * 1024

_BF16 = jnp.bfloat16
_F32 = jnp.float32


def _alibi_slopes():
    h = np.arange(1, 3 * HEADS + 1, dtype=np.float32)
    return [float(v) for v in np.power(np.float32(2.0), -8.0 * h / (3 * HEADS))]


SLOPES = _alibi_slopes()


def _rms_norm(x, g):
    return (x * lax.rsqrt(jnp.mean(x * x, axis=-1, keepdims=True) + RMS_EPS)) * g


def _head_cols(h, base=0):
    return slice(base + h * HEAD_DIM, base + (h + 1) * HEAD_DIM)


def _inproj_sample_kernel(x_ref, g_ref, w_ref, proj_ref, kv0_ref, kv1_ref, kv2_ref, wt_ref, h_s):
    j = pl.program_id(1)

    @pl.when(j == 0)
    def _():
        h_s[...] = _rms_norm(x_ref[...], g_ref[...]).astype(_BF16)

    w = w_ref[...].astype(_BF16)
    wt_ref[...] = w
    acc = jnp.dot(h_s[...], w, preferred_element_type=_F32)
    proj_ref[...] = acc.astype(_BF16)
    for g, ref in enumerate((kv0_ref, kv1_ref, kv2_ref)):
        @pl.when((j == Q_TILE[g] + 1) | (j == Q_TILE[g] + 2))
        def _(ref=ref):
            ref[...] = acc


def _inproj_sample(x2d, g, w, layer):
    m = x2d.shape[0]

    def kv_spec(grp):
        return pl.BlockSpec((m, COL_TILE), lambda i, j: (i, jnp.clip(j - Q_TILE[grp] - 1, 0, 1)))

    return pl.pallas_call(
        _inproj_sample_kernel,
        grid=(1, IN_TILES),
        in_specs=[pl.BlockSpec((m, D_MODEL), lambda i, j: (i, 0)),
                  pl.BlockSpec((1, D_MODEL), lambda i, j: (0, 0)),
                  pl.BlockSpec((None, D_MODEL, COL_TILE), lambda i, j: (layer, 0, j))],
        out_specs=[pl.BlockSpec((m, COL_TILE), lambda i, j: (i, j)), kv_spec(0), kv_spec(1), kv_spec(2),
                   pl.BlockSpec((None, D_MODEL, COL_TILE), lambda i, j: (j, 0, 0))],
        out_shape=[jax.ShapeDtypeStruct((m, IN_COLS), _BF16)] +
                  [jax.ShapeDtypeStruct((m, 2 * ATT_WIDTH), _F32)] * 3 +
                  [jax.ShapeDtypeStruct((IN_TILES, D_MODEL, COL_TILE), _BF16)],
        scratch_shapes=[pltpu.VMEM((m, D_MODEL), _BF16)],
        compiler_params=pltpu.CompilerParams(dimension_semantics=("parallel", "arbitrary"),
                                             vmem_limit_bytes=VMEM_LIMIT),
        name="inproj_sample",
    )(x2d, g.reshape(1, D_MODEL), w)


def _inproj_prompt_kernel(*refs, tm, aliased):
    n_in = 6 if aliased else 3
    x_ref, g_ref, w_ref = refs[:3]
    nat_ref, g1_ref, g2_ref, kv0_ref, kv1_ref, kv2_ref, h_s, de_s, de2_s = refs[n_in:]
    kv_refs = (kv0_ref, kv1_ref, kv2_ref)
    res_refs = (None, g1_ref, g2_ref)
    j = pl.program_id(1)

    @pl.when(j == 0)
    def _():
        h_s[...] = _rms_norm(x_ref[...], g_ref[...]).astype(_BF16)

    def column_tiles(tiles):
        for c in range(tm // SUB_ROWS):
            r0 = c * SUB_ROWS
            for half, tile in enumerate(tiles):
                acc = jnp.dot(h_s[r0:r0 + SUB_ROWS, :], w_ref[half], preferred_element_type=_F32)
                grp = (tile - Q_TILE[0]) // 3 if Q_TILE[0] <= tile < U_TILE0 else None
                part = None if grp is None or tile == Q_TILE[grp] else tile - Q_TILE[grp] - 1
                dil = 1 if grp is None else DIL_GROUPS[grp][1]
                if dil == 1:
                    nat_ref[r0:r0 + SUB_ROWS, half * COL_TILE:(half + 1) * COL_TILE] = acc.astype(_BF16)
                else:
                    for h in range(HEADS):
                        de_s[c, half, h] = acc[:, _head_cols(h)]
                    n = SUB_ROWS // dil
                    col0 = (tile - Q_TILE[grp]) * COL_TILE
                    if dil == REGROUP_STRIDE:
                        for res in range(dil):
                            for h in range(HEADS):
                                res_refs[grp][res, c * n:(c + 1) * n, _head_cols(h, col0)] = (
                                    de_s[c, half, h, pl.ds(res, n, stride=dil), :].astype(_BF16))
                    else:
                        assert dil == REGROUP_STRIDE ** 2
                        q = SUB_ROWS // REGROUP_STRIDE
                        for h in range(HEADS):
                            for g in range(REGROUP_STRIDE):
                                de2_s[h, g * q:(g + 1) * q, :] = de_s[c, half, h, pl.ds(g, q, stride=REGROUP_STRIDE), :]
                        for g in range(REGROUP_STRIDE):
                            for r in range(REGROUP_STRIDE):
                                for h in range(HEADS):
                                    res_refs[grp][g + REGROUP_STRIDE * r, c * n:(c + 1) * n, _head_cols(h, col0)] = (
                                        de2_s[h, pl.ds(g * q + r, n, stride=REGROUP_STRIDE), :].astype(_BF16))
                if part is not None:
                    keep = min(DIL_GROUPS[grp][0], tm)
                    lo = max(r0, tm - keep)
                    n = r0 + SUB_ROWS - lo
                    if n > 0:
                        for h in range(HEADS):
                            dst = pl.ds((lo - (tm - keep)) * KV_ROWS + part * HEADS + h, n, stride=KV_ROWS)
                            kv_refs[grp][dst, :] = acc[lo - r0:lo - r0 + n, _head_cols(h)]

    for step in range(IN_TILES // IN_STEP_TILES):
        @pl.when(j == step)
        def _(step=step):
            column_tiles(tuple(range(step * IN_STEP_TILES, (step + 1) * IN_STEP_TILES)))


def _inproj_prompt(x2d, g, w, batch, seq, tm, layer, prev_kv):
    m = x2d.shape[0]
    tps = seq // tm
    aliased = prev_kv is not None
    (w0, _), (w1, d1), (w2, d2) = DIL_GROUPS
    assert w2 == seq and w0 <= tm and w1 <= tm and tm % SUB_ROWS == 0

    in_specs = [pl.BlockSpec((tm, D_MODEL), lambda i, j: (i, 0)),
                pl.BlockSpec((1, D_MODEL), lambda i, j: (0, 0)),
                pl.BlockSpec((IN_STEP_TILES, D_MODEL, COL_TILE), lambda i, j: (j, 0, 0))]
    args = [x2d, g.reshape(1, D_MODEL), w]
    if aliased:
        in_specs += [pl.BlockSpec(memory_space=pl.ANY)] * 3
        args += list(prev_kv)

    def res_spec(dil):
        return pl.BlockSpec((None, dil, tm // dil, 3 * ATT_WIDTH), lambda i, j: (i // tps, 0, i % tps, 0))

    assert Q_TILE[1] % IN_STEP_TILES == 0 and U_TILE0 % IN_STEP_TILES == 0 and IN_TILES - U_TILE0 == IN_STEP_TILES
    nat_lo, u_step = Q_TILE[1] // IN_STEP_TILES, U_TILE0 // IN_STEP_TILES
    out_specs = [
        pl.BlockSpec((tm, IN_STEP_TILES * COL_TILE),
                     lambda i, j: (i, jnp.clip(j, 0, nat_lo - 1) + jnp.clip(j - u_step + 1, 0, 1))),
        res_spec(d1), res_spec(d2),
        pl.BlockSpec((None, None, w0 * KV_ROWS, HEAD_DIM), lambda i, j: (layer, i // tps, 0, 0),
                     pipeline_mode=pl.Buffered(1)),
        pl.BlockSpec((None, None, w1 * KV_ROWS, HEAD_DIM), lambda i, j: (layer, i // tps, 0, 0),
                     pipeline_mode=pl.Buffered(1)),
        pl.BlockSpec((None, None, tm * KV_ROWS, HEAD_DIM), lambda i, j: (layer, i // tps, i % tps, 0)),
    ]
    out_shape = [
        jax.ShapeDtypeStruct((m, NAT_TILES * COL_TILE), _BF16),
        jax.ShapeDtypeStruct((batch, d1, seq // d1, 3 * ATT_WIDTH), _BF16),
        jax.ShapeDtypeStruct((batch, d2, seq // d2, 3 * ATT_WIDTH), _BF16),
    ] + [jax.ShapeDtypeStruct((DEPTH, batch, win * KV_ROWS, HEAD_DIM), _F32) for win, _ in DIL_GROUPS]
    return pl.pallas_call(
        functools.partial(_inproj_prompt_kernel, tm=tm, aliased=aliased),
        grid=(m // tm, IN_TILES // IN_STEP_TILES),
        in_specs=in_specs, out_specs=out_specs, out_shape=out_shape,
        scratch_shapes=[pltpu.VMEM((tm, D_MODEL), _BF16),
                        pltpu.VMEM((tm // SUB_ROWS, IN_STEP_TILES, HEADS, SUB_ROWS, HEAD_DIM), _F32),
                        pltpu.VMEM((HEADS, SUB_ROWS, HEAD_DIM), _F32)],
        input_output_aliases={3: 3, 4: 4, 5: 5} if aliased else {},
        compiler_params=pltpu.CompilerParams(dimension_semantics=("arbitrary", "arbitrary"),
                                             vmem_limit_bytes=VMEM_LIMIT_HIGH),
        name="inproj_prompt",
    )(*args)


QB = 128
G0_UNROLL = 3
G1_RES_PER_STEP = 4
G2_RES_PER_STEP = 8


def _attn_prompt_kernel(q0, k0, v0, q1, k1, v1, q2, k2, v2, o_ref, acc_s, m_s, l_s, shuf_s, *, seq):
    step = pl.program_id(1)
    n1 = DIL_GROUPS[1][1]
    n1_steps = n1 // G1_RES_PER_STEP
    n2_steps = DIL_GROUPS[2][1] // G2_RES_PER_STEP

    row0 = lax.broadcasted_iota(jnp.int32, (QB, QB), 0)
    col0 = lax.broadcasted_iota(jnp.int32, (QB, QB), 1)
    dist_first = jnp.where(col0 <= row0, (row0 - col0).astype(_F32), jnp.inf)
    row1 = lax.broadcasted_iota(jnp.int32, (QB, 2 * QB), 0)
    col1 = lax.broadcasted_iota(jnp.int32, (QB, 2 * QB), 1)
    d1 = QB + row1 - col1
    dist_next = jnp.where((d1 >= 0) & (d1 <= QB), d1.astype(_F32), jnp.inf)
    ones = jnp.ones((2 * QB, HEAD_DIM), _BF16)

    seg = seq // n1
    chunk = QB // n1
    assert n1 == 4 and DIL_GROUPS[2][1] == 16

    def blocks(specs):
        tiles = []
        for slot, (qr, kr, vr, grp, res, i, is_first_block) in enumerate(specs):
            dil = DIL_GROUPS[grp][1]
            for h in range(HEADS):
                hs = _head_cols(h)
                slope_step = SLOPES[grp * HEADS + h] * dil
                if is_first_block:
                    q, k, v = qr[0:QB, hs], kr[0:QB, hs], vr[0:QB, hs]
                    bias, one = dist_first * slope_step, ones[0:QB]
                else:
                    qs = pl.ds(pl.multiple_of(i * QB, QB), QB)
                    ks = pl.ds(pl.multiple_of(i * QB - QB, QB), 2 * QB)
                    q, k, v = qr[qs, hs], kr[ks, hs], vr[ks, hs]
                    bias, one = dist_next * slope_step, ones
                tiles.append((q, k, jnp.concatenate([v, one], axis=1), bias, grp, res, i, is_first_block, slot, h))
        scores = [lax.dot_general(t[0], t[1], (((1,), (1,)), ((), ())), preferred_element_type=_F32) for t in tiles]
        probs, maxes = [], []
        for t, s in zip(tiles, scores):
            s = s * ATT_SCALE - t[3]
            m = jnp.max(s, axis=1, keepdims=True)
            probs.append(jnp.exp(s - m).astype(_BF16))
            maxes.append(jnp.broadcast_to(m, (QB, HEAD_DIM)))
        outs = [jnp.dot(p, t[2], preferred_element_type=_F32) for t, p in zip(tiles, probs)]
        def merge(h, rows, acc, m, l):
            m_old, l_old, a_old = m_s[h, rows, :], l_s[h, rows, :], acc_s[h, rows, :]
            m_new = jnp.maximum(m_old, m)
            w_old = jnp.exp(m_old - m_new)
            w_new = jnp.exp(m - m_new)
            acc_s[h, rows, :] = a_old * w_old + acc * w_new
            l_s[h, rows, :] = l_old * w_old + l * w_new
            m_s[h, rows, :] = m_new

        for (_, _, _, _, grp, res, i, is_first_block, slot, h), m, acc_l in zip(tiles, maxes, outs):
            acc, l = acc_l[:, :HEAD_DIM], acc_l[:, HEAD_DIM:]
            if grp == 1:
                rows = pl.ds(pl.multiple_of(res * seg + i * QB, QB), QB)
                acc_s[h, rows, :], m_s[h, rows, :], l_s[h, rows, :] = acc, m, l
            elif grp == 2:
                merge(h, pl.ds((res & (n1 - 1)) * seg + (res >> 2), QB, stride=n1), acc, m, l)
            else:
                tmps = [shuf_s.at[(slot * HEADS + h) * 3 + a] for a in range(3)]
                for tmp, val in zip(tmps, (acc, m, l)):
                    tmp[...] = val
                for r4 in range(n1):
                    dst = r4 * seg + i * chunk
                    dst = pl.ds(dst if is_first_block else pl.multiple_of(dst, chunk), chunk)
                    merge(h, dst, *(tmp[pl.ds(r4, chunk, stride=n1), :] for tmp in tmps))

    @pl.when(step < n1_steps)
    def _():
        blocks([(q1.at[r], k1.at[r], v1.at[r], 1, step * G1_RES_PER_STEP + r, i, i == 0)
                for r in range(G1_RES_PER_STEP) for i in range(seg // QB)])

    @pl.when((step >= n1_steps) & (step < n1_steps + n2_steps))
    def _():
        assert seq // DIL_GROUPS[2][1] == QB
        blocks([(q2.at[r], k2.at[r], v2.at[r], 2, (step - n1_steps) * G2_RES_PER_STEP + r, 0, True)
                for r in range(G2_RES_PER_STEP)])

    @pl.when(step == n1_steps + n2_steps)
    def _():
        n_blk = seq // QB
        assert (n_blk - 1) % G0_UNROLL == 0
        blocks([(q0, k0, v0, 0, 0, 0, True)])

        def g0_body(it, carry):
            blocks([(q0, k0, v0, 0, 0, 1 + it * G0_UNROLL + u, False) for u in range(G0_UNROLL)])
            return carry
        lax.fori_loop(0, (n_blk - 1) // G0_UNROLL, g0_body, 0)

        def body(c, carry):
            for h in range(HEADS):
                tmp = shuf_s.at[h]
                for r4 in range(n1):
                    src = pl.ds(pl.multiple_of(r4 * seg + c * chunk, chunk), chunk)
                    tmp[pl.ds(r4, chunk, stride=n1), :] = acc_s[h, src, :] / l_s[h, src, :]
                o_ref[pl.ds(pl.multiple_of(c * QB, QB), QB), _head_cols(h)] = tmp[...].astype(_BF16)
            return carry
        lax.fori_loop(0, seq // QB, body, 0)


def _attn_prompt(nat, qkv1, qkv2, batch, seq):
    n1, n2 = DIL_GROUPS[1][1], DIL_GROUPS[2][1]
    n1_steps, n2_steps = n1 // G1_RES_PER_STEP, n2 // G2_RES_PER_STEP
    nat3 = nat.reshape(batch, seq, NAT_TILES * COL_TILE)
    in_specs = [pl.BlockSpec((None, seq, COL_TILE), lambda b, s, t=t: (b, 0, Q_TILE[0] + t)) for t in range(3)]
    in_specs += [pl.BlockSpec((None, G1_RES_PER_STEP, seq // n1, COL_TILE),
                              lambda b, s, t=t: (b, jnp.clip(s, 0, n1_steps - 1), 0, t)) for t in range(3)]
    in_specs += [pl.BlockSpec((None, G2_RES_PER_STEP, seq // n2, COL_TILE),
                              lambda b, s, t=t: (b, jnp.clip(s - n1_steps, 0, n2_steps - 1), 0, t)) for t in range(3)]
    args = [nat3] * 3 + [qkv1] * 3 + [qkv2] * 3
    return pl.pallas_call(
        functools.partial(_attn_prompt_kernel, seq=seq),
        grid=(batch, 1 + n1_steps + n2_steps),
        in_specs=in_specs,
        out_specs=pl.BlockSpec((None, seq, ATT_WIDTH), lambda b, s: (b, 0, 0)),
        out_shape=jax.ShapeDtypeStruct((batch, seq, ATT_WIDTH), _BF16),
        scratch_shapes=[pltpu.VMEM((HEADS, seq, HEAD_DIM), _F32)] * 3
                       + [pltpu.VMEM((G0_UNROLL * HEADS * 3, QB, HEAD_DIM), _F32)],
        compiler_params=pltpu.CompilerParams(dimension_semantics=("parallel", "arbitrary"),
                                             vmem_limit_bytes=VMEM_LIMIT),
        name="attn_prompt",
    )(*args)


def _attn_sample_kernel(*refs, tdec, aliased):
    n_in = 10 if aliased else 7
    proj_ref, n0, n1, n2, c0, c1, c2 = refs[:7]
    yb_ref, o0, o1, o2 = refs[n_in:]
    news, caches, outs = (n0, n1, n2), (c0, c1, c2), (o0, o1, o2)
    for h in range(HEADS):
        pieces = []
        for grp, (win, dil) in enumerate(DIL_GROUPS):
            slope = SLOPES[grp * HEADS + h]
            q = proj_ref[:, _head_cols(h, Q_TILE[grp] * COL_TILE)]
            k_new = news[grp][:, _head_cols(h)]
            v_new = news[grp][:, _head_cols(h, ATT_WIDTH)]
            k_old = caches[grp][pl.ds(h, win, stride=KV_ROWS), :]
            v_old = caches[grp][pl.ds(HEADS + h, win, stride=KV_ROWS), :]
            for k, v, n_keys, base in ((k_old, v_old, win, win), (k_new, v_new, tdec, 0)):
                s = lax.dot_general(q, k.astype(_BF16), (((1,), (1,)), ((), ())), preferred_element_type=_F32)
                t = lax.broadcasted_iota(jnp.int32, (tdec, n_keys), 0)
                c = lax.broadcasted_iota(jnp.int32, (tdec, n_keys), 1)
                dist = base + t - c
                ok = (dist >= 0) & (dist <= win) & ((dist & (dil - 1)) == 0)
                s = jnp.where(ok, s * ATT_SCALE - slope * dist.astype(_F32), -jnp.inf)
                pieces.append((s, v.astype(_BF16)))
        m = functools.reduce(jnp.maximum, [jnp.max(s, axis=1, keepdims=True) for s, _ in pieces])
        l = jnp.zeros((tdec, 1), _F32)
        acc = jnp.zeros((tdec, HEAD_DIM), _F32)
        for s, v in pieces:
            p = jnp.exp(s - m)
            l = l + jnp.sum(p, axis=1, keepdims=True)
            acc = acc + jnp.dot(p.astype(_BF16), v, preferred_element_type=_F32)
        yb_ref[:, _head_cols(h)] = (acc / l).astype(_BF16)
    for grp in range(len(DIL_GROUPS)):
        for part in range(2):
            for h in range(HEADS):
                outs[grp][pl.ds(part * HEADS + h, tdec, stride=KV_ROWS), :] = (
                    news[grp][:, _head_cols(h, part * ATT_WIDTH)])


def _attn_sample(proj, kv_new, caches, batch, tdec, layer, prev_out):
    aliased = prev_out is not None
    in_specs = [pl.BlockSpec((None, tdec, IN_COLS), lambda b: (b, 0, 0))]
    in_specs += [pl.BlockSpec((None, tdec, 2 * ATT_WIDTH), lambda b: (b, 0, 0))] * 3
    cache_specs = [pl.BlockSpec((None, None, win * KV_ROWS, HEAD_DIM), lambda b: (layer, b, 0, 0)) for win, _ in DIL_GROUPS]
    in_specs += cache_specs
    args = [proj.reshape(batch, tdec, IN_COLS)] + [k.reshape(batch, tdec, 2 * ATT_WIDTH) for k in kv_new] + list(caches)
    if aliased:
        in_specs += [pl.BlockSpec(memory_space=pl.ANY)] * 3
        args += list(prev_out)
    out_specs = [pl.BlockSpec((None, tdec, ATT_WIDTH), lambda b: (b, 0, 0))]
    out_specs += [pl.BlockSpec((None, None, tdec * KV_ROWS, HEAD_DIM), lambda b, win=win: (layer, b, win // tdec - 1, 0))
                  for win, _ in DIL_GROUPS]
    out_shape = [jax.ShapeDtypeStruct((batch, tdec, ATT_WIDTH), _BF16)]
    out_shape += [jax.ShapeDtypeStruct((DEPTH, batch, win * KV_ROWS, HEAD_DIM), _F32) for win, _ in DIL_GROUPS]
    return pl.pallas_call(
        functools.partial(_attn_sample_kernel, tdec=tdec, aliased=aliased),
        grid=(batch,),
        in_specs=in_specs, out_specs=out_specs, out_shape=out_shape,
        input_output_aliases={7: 1, 8: 2, 9: 3} if aliased else {},
        compiler_params=pltpu.CompilerParams(dimension_semantics=("arbitrary",), vmem_limit_bytes=VMEM_LIMIT),
        name="attn_sample",
    )(*args)


def _mixer_kernel(*refs, nb, tt, start, from_zero):
    if from_zero:
        pa_ref, pu_ref, yb_ref, x_ref, cw_ref, pw_ref, psc_ref, wo_ref = refs[:8]
    else:
        pa_ref, pu_ref, yb_ref, x_ref, cs0_ref, ps0_ref, cw_ref, pw_ref, psc_ref, wo_ref = refs[:10]
    out_ref, cs_out, ps_out, ext_a, ext_u, yc = refs[-6:]
    ti = pl.program_id(1)
    rows = nb * tt
    pool_halo = POOL_BLOCK if from_zero else POOL_HALO

    @pl.when(ti == 0)
    def _():
        if from_zero:
            ext_a[:, 0:CONV_HALO, :] = jnp.zeros((nb, CONV_HALO, CONV_WIDTH), _F32)
            ext_u[:, 0:pool_halo, :] = jnp.zeros((nb, pool_halo, POOL_WIDTH), _BF16)
        else:
            ext_a[:, 0:CONV_HALO, :] = cs0_ref[...]
            ext_u[:, 0:pool_halo, :] = ps0_ref[...]

    yb = yb_ref[...].astype(_F32).reshape(rows, ATT_WIDTH).astype(_BF16)
    mixed = jnp.dot(yb, wo_ref[CONV_WIDTH:CONV_WIDTH + ATT_WIDTH, :], preferred_element_type=_F32)

    pa = pa_ref[...].astype(_F32)
    xa, gate_b, gate_c = (pa[:, :, k * CONV_WIDTH:(k + 1) * CONV_WIDTH] for k in range(3))
    prod = gate_c * xa
    ext_a[:, CONV_HALO:CONV_HALO + tt, :] = prod
    cw = cw_ref[...]
    cu = (cw[0] * ext_a[:, CONV_HALO - 2:CONV_HALO - 2 + tt, :]
          + cw[1] * ext_a[:, CONV_HALO - 1:CONV_HALO - 1 + tt, :] + cw[2] * prod)
    ya = (gate_b * cu).reshape(rows, CONV_WIDTH).astype(_BF16)
    mixed = mixed + jnp.dot(ya, wo_ref[0:CONV_WIDTH, :], preferred_element_type=_F32)

    ext_u[:, pool_halo:pool_halo + tt, :] = pu_ref[...].astype(ext_u.dtype)
    if from_zero:
        pb = POOL_BLOCK
        row = lax.broadcasted_iota(jnp.int32, (pb, 2 * pb), 0)
        col = lax.broadcasted_iota(jnp.int32, (pb, 2 * pb), 1)
        lag = pb + row - col
        for gi, w in enumerate(POOL_WINDOWS):
            cols = slice(gi * POOL_GROUP, (gi + 1) * POOL_GROUP)
            band = ((lag >= 0) & (lag < w)).astype(_F32).astype(_BF16)
            for r in range(tt // pb):
                tot = jnp.dot(band, ext_u[0, r * pb:(r + 2) * pb, cols], preferred_element_type=_F32)
                pos = start + ti * tt + r * pb + lax.broadcasted_iota(jnp.int32, (pb, 1), 0)
                cnt = jnp.minimum(w, pos + 1).astype(_F32)
                ug = pu_ref[0, r * pb:(r + 1) * pb, cols].astype(_F32)
                diff = (tot / cnt - ug).astype(_BF16)
                z = jnp.dot(diff, pw_ref[gi], preferred_element_type=_F32) * psc_ref[:, cols]
                yc[r * pb:(r + 1) * pb, cols] = z.astype(_BF16)
    else:
        u = pu_ref[...].astype(_F32)
        pos = start + ti * tt + lax.broadcasted_iota(jnp.int32, (1, tt, 1), 1)
        for gi, w in enumerate(POOL_WINDOWS):
            cols = slice(gi * POOL_GROUP, (gi + 1) * POOL_GROUP)
            ug = u[:, :, cols]
            tot = ug
            for k in range(1, w):
                tot = tot + ext_u[:, pool_halo - k:pool_halo - k + tt, cols]
            cnt = jnp.minimum(w, pos + 1).astype(_F32)
            diff = (tot / cnt - ug).reshape(rows, POOL_GROUP).astype(_BF16)
            z = jnp.dot(diff, pw_ref[gi], preferred_element_type=_F32) * psc_ref[:, cols]
            yc[:, cols] = z.astype(_BF16)

    mixed = mixed + jnp.dot(yc[...], wo_ref[CONV_WIDTH + ATT_WIDTH:, :], preferred_element_type=_F32)
    out_ref[...] = x_ref[...] + mixed.reshape(nb, tt, D_MODEL)

    tail_a = ext_a[:, tt:tt + CONV_HALO, :]
    ext_a[:, 0:CONV_HALO, :] = tail_a
    cs_out[...] = tail_a
    ps_out[...] = ext_u[:, pool_halo + tt - POOL_HALO:pool_halo + tt, :].astype(_F32)
    ext_u[:, 0:pool_halo, :] = ext_u[:, tt:tt + pool_halo, :]


def _mixer(proj, yb, x, conv_state, pool_state, conv_w, pool_w, pool_scale, w_out, layer, nb, tt, start, u_tile0):
    batch, seq, _ = x.shape
    proj3 = proj.reshape(batch, seq, proj.shape[-1])
    u_blk = u_tile0 * COL_TILE // POOL_WIDTH
    const2 = lambda b, t: (0, 0)
    from_zero = conv_state is None and pool_state is None
    assert from_zero or (conv_state is not None and pool_state is not None)
    assert not from_zero or (nb == 1 and start == 0 and tt % POOL_BLOCK == 0)
    state_specs = [] if from_zero else [pl.BlockSpec((nb, CONV_HALO, CONV_WIDTH), lambda b, t: (b, 0, 0)),
                                        pl.BlockSpec((nb, POOL_HALO, POOL_WIDTH), lambda b, t: (b, 0, 0))]
    state_args = [] if from_zero else [conv_state, pool_state]
    ext_u = (pltpu.VMEM((nb, tt + POOL_BLOCK, POOL_WIDTH), _BF16) if from_zero
             else pltpu.VMEM((nb, tt + POOL_HALO, POOL_WIDTH), _F32))
    return pl.pallas_call(
        functools.partial(_mixer_kernel, nb=nb, tt=tt, start=start, from_zero=from_zero),
        grid=(batch // nb, seq // tt),
        in_specs=[pl.BlockSpec((nb, tt, 3 * CONV_WIDTH), lambda b, t: (b, t, 0)),
                  pl.BlockSpec((nb, tt, POOL_WIDTH), lambda b, t: (b, t, u_blk)),
                  pl.BlockSpec((nb, tt, ATT_WIDTH), lambda b, t: (b, t, 0)),
                  pl.BlockSpec((nb, tt, D_MODEL), lambda b, t: (b, t, 0))] + state_specs + [
                  pl.BlockSpec((3, CONV_WIDTH), const2),
                  pl.BlockSpec((None, len(POOL_WINDOWS), POOL_GROUP, POOL_GROUP), lambda b, t: (layer, 0, 0, 0)),
                  pl.BlockSpec((1, POOL_WIDTH), const2),
                  pl.BlockSpec((None, D_MODEL, D_MODEL), lambda b, t: (layer, 0, 0))],
        out_specs=[pl.BlockSpec((nb, tt, D_MODEL), lambda b, t: (b, t, 0)),
                   pl.BlockSpec((nb, CONV_HALO, CONV_WIDTH), lambda b, t: (b, 0, 0)),
                   pl.BlockSpec((nb, POOL_HALO, POOL_WIDTH), lambda b, t: (b, 0, 0))],
        out_shape=[jax.ShapeDtypeStruct((batch, seq, D_MODEL), _F32),
                   jax.ShapeDtypeStruct((batch, CONV_HALO, CONV_WIDTH), _F32),
                   jax.ShapeDtypeStruct((batch, POOL_HALO, POOL_WIDTH), _F32)],
        scratch_shapes=[pltpu.VMEM((nb, tt + CONV_HALO, CONV_WIDTH), _F32), ext_u,
                        pltpu.VMEM((nb * tt, POOL_WIDTH), _BF16)],
        compiler_params=pltpu.CompilerParams(dimension_semantics=("parallel", "arbitrary"),
                                             vmem_limit_bytes=VMEM_LIMIT),
        name="mixer",
    )(proj3, proj3, yb, x, *state_args, conv_w, pool_w, pool_scale.reshape(1, POOL_WIDTH), w_out)


def _ffn_sample_kernel(x_ref, n2_ref, wg_ref, wu_ref, cw_ref, wd_ref, st0_ref, fn_ref,
                       out_ref, st_out, wgt_ref, wut_ref, wdt_ref, h_s, ext_s, *, nb, tt, final):
    f = pl.program_id(0)
    rows = nb * tt

    @pl.when(f == 0)
    def _():
        x = x_ref[...]
        h_s[...] = _rms_norm(x, n2_ref[...]).reshape(rows, D_MODEL).astype(_BF16)
        out_ref[...] = x

    def tile(width):
        cols = slice(0, width)
        wg, wu, wd = wg_ref[:, cols].astype(_BF16), wu_ref[:, cols].astype(_BF16), wd_ref[cols, :].astype(_BF16)
        wgt_ref[:, cols], wut_ref[:, cols], wdt_ref[cols, :] = wg, wu, wd
        if width < FF_TILE:
            wgt_ref[:, width:] = jnp.zeros((D_MODEL, FF_TILE - width), _BF16)
            wut_ref[:, width:] = jnp.zeros((D_MODEL, FF_TILE - width), _BF16)
            wdt_ref[width:, :] = jnp.zeros((FF_TILE - width, D_MODEL), _BF16)
        h = h_s[...]
        gate = jnp.dot(h, wg, preferred_element_type=_F32).reshape(nb, tt, width)
        up = jnp.dot(h, wu, preferred_element_type=_F32).reshape(nb, tt, width)
        ext_s[:, 0:CONV_HALO, cols] = st0_ref[:, :, cols]
        ext_s[:, CONV_HALO:CONV_HALO + tt, cols] = gate
        cw = cw_ref[:, cols]
        gc = (cw[0] * ext_s[:, CONV_HALO - 2:CONV_HALO - 2 + tt, cols]
              + cw[1] * ext_s[:, CONV_HALO - 1:CONV_HALO - 1 + tt, cols] + cw[2] * gate)
        st_out[:, :, cols] = ext_s[:, tt:tt + CONV_HALO, cols]
        act = (jax.nn.silu(gc) * up).reshape(rows, width).astype(_BF16)
        out_ref[...] += jnp.dot(act, wd, preferred_element_type=_F32).reshape(nb, tt, D_MODEL)

    @pl.when(f < FF_TILES - 1)
    def _():
        tile(FF_TILE)

    @pl.when(f == FF_TILES - 1)
    def _():
        tile(FF_LAST)
        if final:
            out_ref[...] = _rms_norm(out_ref[...], fn_ref[...])


def _ffn_prompt_kernel(x_ref, n2_ref, wg_ref, wu_ref, cw_ref, wd_ref, fn_ref, c0, c1, c2, _b0, _b1, _b2,
                       out_ref, st_out, nc0, nc1, nc2, h_s, ext_s, carry_s, stage, rsem, wsem,
                       *, tt, final, layer, tdec):
    ti = pl.program_id(1)
    f = pl.program_id(2)
    seq_i = pl.program_id(0) * pl.num_programs(1) + ti
    first_step = (pl.program_id(0) == 0) & (ti == 0) & (f == 0)
    last_step = ((pl.program_id(0) == pl.num_programs(0) - 1) & (ti == pl.num_programs(1) - 1)
                 & (f == FF_TILES - 1))
    skip = tdec * KV_ROWS
    n_big = (DIL_GROUPS[2][0] * KV_ROWS - skip) // SHIFT_ROWS
    mid_len = DIL_GROUPS[1][0] * KV_ROWS - skip
    n_mid = -(-mid_len // SHIFT_ROWS)
    assert (DIL_GROUPS[2][0] * KV_ROWS - skip) % SHIFT_ROWS == 0 and DIL_GROUPS[0][0] * KV_ROWS - skip == SHIFT_ROWS
    assert n_big + n_mid + 1 == 2 * FF_TILES + 1

    def shift(slot, phase):
        chunk = 2 * f + slot if slot < 2 else None

        def run(src, dst, row0):
            rd = pltpu.make_async_copy(src.at[layer, seq_i, pl.ds(row0 + skip, SHIFT_ROWS), :], stage.at[slot],
                                       rsem.at[slot])
            wr = pltpu.make_async_copy(stage.at[slot], dst.at[layer, seq_i, pl.ds(row0, SHIFT_ROWS), :],
                                       wsem.at[slot])
            if phase == 0:
                rd.start()
            else:
                rd.wait()
                wr.start()

        if slot < 2:
            @pl.when(chunk < n_big)
            def _():
                run(c2, nc2, chunk * SHIFT_ROWS)

            @pl.when(chunk >= n_big)
            def _():
                j = chunk - n_big
                run(c1, nc1, jnp.where(j == 0, mid_len - SHIFT_ROWS, (j - 1) * SHIFT_ROWS))
        else:
            @pl.when(f == FF_TILES - 1)
            def _():
                run(c0, nc0, 0)

    def wait_write(slot):
        pltpu.make_async_copy(stage.at[slot], nc2.at[layer, 0, pl.ds(0, SHIFT_ROWS), :], wsem.at[slot]).wait()

    @pl.when(jnp.logical_not(first_step))
    def _():
        wait_write(0)
        wait_write(1)

    @pl.when(jnp.logical_not(first_step) & (f == 0))
    def _():
        wait_write(2)

    for slot in range(3):
        shift(slot, 0)

    def tile(first, last):
        if first:
            @pl.when(ti == 0)
            def _():
                carry_s[...] = jnp.zeros(carry_s.shape, _F32)
            x = x_ref[...]
            h_s[...] = _rms_norm(x, n2_ref[...]).astype(_BF16)
            out_ref[...] = x
        ext_s[0:CONV_HALO, :] = carry_s[f]

        wg, wu, wd, cw = wg_ref[...], wu_ref[...], wd_ref[...], cw_ref[f]
        chunks = [slice(r0, r0 + FFN_SUB_ROWS) for r0 in range(0, tt, FFN_SUB_ROWS)]
        for rs in chunks:
            gate = jnp.dot(h_s[rs, :], wg, preferred_element_type=_F32)
            ext_s[CONV_HALO + rs.start:CONV_HALO + rs.stop, :] = gate
        ups = [jnp.dot(h_s[rs, :], wu, preferred_element_type=_F32) for rs in chunks]
        for rs, up in zip(chunks, ups):
            gc = (cw[0] * ext_s[CONV_HALO - 2 + rs.start:CONV_HALO - 2 + rs.stop, :]
                  + cw[1] * ext_s[CONV_HALO - 1 + rs.start:CONV_HALO - 1 + rs.stop, :]
                  + cw[2] * ext_s[CONV_HALO + rs.start:CONV_HALO + rs.stop, :])
            act = (jax.nn.silu(gc) * up).astype(_BF16)
            out_ref[rs, :] += jnp.dot(act, wd, preferred_element_type=_F32)
        tail = ext_s[tt:tt + CONV_HALO, :]
        carry_s[f] = tail
        st_out[f] = tail
        if last:
            out_ref[...] = _rms_norm(out_ref[...], fn_ref[...])

    @pl.when(f == 0)
    def _():
        tile(True, False)

    if final:
        @pl.when((f > 0) & (f < FF_TILES - 1))
        def _():
            tile(False, False)

        @pl.when(f == FF_TILES - 1)
        def _():
            tile(False, True)
    else:
        @pl.when(f > 0)
        def _():
            tile(False, False)

    for slot in range(3):
        shift(slot, 1)

    @pl.when(last_step)
    def _():
        for slot in range(3):
            wait_write(slot)


def _ffn_sample(x, norm2, w_gate, w_up, conv_w, w_down, state, final_norm, layer, final):
    nb, tt, _ = x.shape
    vec = lambda f: (0, 0)
    return pl.pallas_call(
        functools.partial(_ffn_sample_kernel, nb=nb, tt=tt, final=final),
        grid=(FF_TILES,),
        in_specs=[pl.BlockSpec((nb, tt, D_MODEL), lambda f: (0, 0, 0)),
                  pl.BlockSpec((1, D_MODEL), vec),
                  pl.BlockSpec((None, D_MODEL, FF_TILE), lambda f: (layer, 0, f)),
                  pl.BlockSpec((None, D_MODEL, FF_TILE), lambda f: (layer, 0, f)),
                  pl.BlockSpec((3, FF_TILE), lambda f: (0, f)),
                  pl.BlockSpec((None, FF_TILE, D_MODEL), lambda f: (layer, f, 0)),
                  pl.BlockSpec((nb, CONV_HALO, FF_TILE), lambda f: (0, 0, f)),
                  pl.BlockSpec((1, D_MODEL), vec)],
        out_specs=[pl.BlockSpec((nb, tt, D_MODEL), lambda f: (0, 0, 0)),
                   pl.BlockSpec((nb, CONV_HALO, FF_TILE), lambda f: (0, 0, f)),
                   pl.BlockSpec((None, D_MODEL, FF_TILE), lambda f: (f, 0, 0)),
                   pl.BlockSpec((None, D_MODEL, FF_TILE), lambda f: (f, 0, 0)),
                   pl.BlockSpec((None, FF_TILE, D_MODEL), lambda f: (f, 0, 0))],
        out_shape=[jax.ShapeDtypeStruct((nb, tt, D_MODEL), _F32),
                   jax.ShapeDtypeStruct((nb, CONV_HALO, D_FF), _F32),
                   jax.ShapeDtypeStruct((FF_TILES, D_MODEL, FF_TILE), _BF16),
                   jax.ShapeDtypeStruct((FF_TILES, D_MODEL, FF_TILE), _BF16),
                   jax.ShapeDtypeStruct((FF_TILES, FF_TILE, D_MODEL), _BF16)],
        scratch_shapes=[pltpu.VMEM((nb * tt, D_MODEL), _BF16),
                        pltpu.VMEM((nb, tt + CONV_HALO, FF_TILE), _F32)],
        compiler_params=pltpu.CompilerParams(dimension_semantics=("arbitrary",), vmem_limit_bytes=VMEM_LIMIT),
        name="ffn_sample",
    )(x, norm2.reshape(1, D_MODEL), w_gate, w_up, conv_w, w_down, state, final_norm.reshape(1, D_MODEL))


def _ffn_prompt(x, norm2, w_gate_t, w_up_t, conv_w, w_down_t, final_norm, caches, new_caches, layer, tdec, tt, final):
    batch, seq, _ = x.shape
    vec = lambda b, t, f: (0, 0)
    any_spec = pl.BlockSpec(memory_space=pl.ANY)
    assert caches[0].shape[1] == batch * (seq // tt)
    return pl.pallas_call(
        functools.partial(_ffn_prompt_kernel, tt=tt, final=final, layer=layer, tdec=tdec),
        grid=(batch, seq // tt, FF_TILES),
        in_specs=[pl.BlockSpec((None, tt, D_MODEL), lambda b, t, f: (b, t, 0)),
                  pl.BlockSpec((1, D_MODEL), vec),
                  pl.BlockSpec((None, D_MODEL, FF_TILE), lambda b, t, f: (f, 0, 0)),
                  pl.BlockSpec((None, D_MODEL, FF_TILE), lambda b, t, f: (f, 0, 0)),
                  pl.BlockSpec((FF_TILES, 3, FF_TILE), lambda b, t, f: (0, 0, 0)),
                  pl.BlockSpec((None, FF_TILE, D_MODEL), lambda b, t, f: (f, 0, 0)),
                  pl.BlockSpec((1, D_MODEL), vec)] + [any_spec] * 6,
        out_specs=[pl.BlockSpec((None, tt, D_MODEL), lambda b, t, f: (b, t, 0)),
                   pl.BlockSpec((None, None, FF_TILES, CONV_HALO, FF_TILE), lambda b, t, f: (b, t, 0, 0, 0))]
                  + [any_spec] * 3,
        out_shape=[jax.ShapeDtypeStruct((batch, seq, D_MODEL), _F32),
                   jax.ShapeDtypeStruct((batch, seq // tt, FF_TILES, CONV_HALO, FF_TILE), _F32)]
                  + [jax.ShapeDtypeStruct(c.shape, c.dtype) for c in new_caches],
        scratch_shapes=[pltpu.VMEM((tt, D_MODEL), _BF16),
                        pltpu.VMEM((tt + CONV_HALO, FF_TILE), _F32),
                        pltpu.VMEM((FF_TILES, CONV_HALO, FF_TILE), _F32),
                        pltpu.VMEM((3, SHIFT_ROWS, HEAD_DIM), _F32),
                        pltpu.SemaphoreType.DMA((3,)), pltpu.SemaphoreType.DMA((3,))],
        input_output_aliases={10: 2, 11: 3, 12: 4},
        compiler_params=pltpu.CompilerParams(dimension_semantics=("arbitrary", "arbitrary", "arbitrary"),
                                             vmem_limit_bytes=VMEM_LIMIT),
        name="ffn_prompt",
    )(x, norm2.reshape(1, D_MODEL), w_gate_t, w_up_t, conv_w, w_down_t, final_norm.reshape(1, D_MODEL),
      *caches, *new_caches)


def _pad_rows_front(a, rows):
    return jnp.pad(a, ((0, 0), (rows - a.shape[1], 0), (0, 0)))


def kernel(x_prompt, x_sample, cache_kv_w128, cache_kv_w512, cache_kv_w2048, state_conv_a, state_pool, state_ffn_conv, norm1, w_in, conv_a_w, w_out, pool_w, pool_scale, norm2, w_gate, w_up, ffn_conv_w, w_down, final_norm):
    bp, sp, _ = x_prompt.shape
    bs, ss, _ = x_sample.shape
    caches = tuple(c.reshape(DEPTH, bs, win * KV_ROWS, HEAD_DIM)
                   for c, (win, _) in zip((cache_kv_w128, cache_kv_w512, cache_kv_w2048), DIL_GROUPS))
    tm, tt = PROMPT_ROW_TILE, MIXER_ROW_TILE
    w_out_b, pool_w_b = w_out.astype(_BF16), pool_w.astype(_BF16)
    xp, xs = x_prompt, x_sample
    kv_p = kv_s = None
    small_p, small_s = [], []
    for l in range(DEPTH):
        final = l == DEPTH - 1

        proj, *kv_new, w_in_t = _inproj_sample(xs.reshape(bs * ss, D_MODEL), norm1[l], w_in, l)
        yb, *kv_s = _attn_sample(proj, kv_new, caches, bs, ss, l, kv_s)
        xs, conv_o, pool_o = _mixer(proj, yb, xs, _pad_rows_front(state_conv_a[l], CONV_HALO),
                                    _pad_rows_front(state_pool[l], POOL_HALO), conv_a_w[l], pool_w_b, pool_scale[l],
                                    w_out_b, l, bs, ss, PAST_LEN, U_TILE0)
        xs, ffn_o, w_gate_t, w_up_t, w_down_t = _ffn_sample(
            xs, norm2[l], w_gate, w_up, ffn_conv_w[l], w_down, _pad_rows_front(state_ffn_conv[l], CONV_HALO),
            final_norm, l, final)
        small_s.append((conv_o[:, CONV_HALO - 2:], pool_o[:, POOL_HALO - POOL_BUF:], ffn_o[:, CONV_HALO - 2:]))

        nat, qkv1, qkv2, *kv_p = _inproj_prompt(xp.reshape(bp * sp, D_MODEL), norm1[l], w_in_t, bp, sp, tm, l, kv_p)
        yb = _attn_prompt(nat, qkv1, qkv2, bp, sp)
        xp, conv_o, pool_o = _mixer(nat, yb, xp, None, None, conv_a_w[l], pool_w_b, pool_scale[l], w_out_b,
                                    l, 1, tt, 0, NAT_TILES - 2)
        conv_w_t = jnp.pad(ffn_conv_w[l], ((0, 0), (0, FF_TILES * FF_TILE - D_FF)))
        conv_w_t = conv_w_t.reshape(3, FF_TILES, FF_TILE).transpose(1, 0, 2)
        xp, ffn_o, *kv_s = _ffn_prompt(xp, norm2[l], w_gate_t, w_up_t, conv_w_t, w_down_t, final_norm,
                                       caches, kv_s, l, ss, tm, final)
        ffn_tail = ffn_o[:, -1, :, CONV_HALO - 2:].transpose(0, 2, 1, 3).reshape(bp, 2, FF_TILES * FF_TILE)
        small_p.append((conv_o[:, CONV_HALO - 2:], pool_o[:, POOL_HALO - POOL_BUF:], ffn_tail[:, :, :D_FF]))

    out = [xp, xs]
    for g, (win, _) in enumerate(DIL_GROUPS):
        out += [kv_p[g].reshape(DEPTH, bp, win, 2, HEADS, HEAD_DIM), kv_s[g].reshape(DEPTH, bs, win, 2, HEADS, HEAD_DIM)]
    for i in range(3):
        out += [jnp.stack([s[i] for s in small_p]), jnp.stack([s[i] for s in small_s])]
    return tuple(out)
```

```python
import functools

import numpy as np
import jax
import jax.numpy as jnp
from jax import lax
from jax.experimental import pallas as pl
from jax.experimental.pallas import tpu as pltpu

D_MODEL = 2048
DEPTH = 2
PAST_LEN = 16384
CONV_WIDTH = 512
ATT_WIDTH = 512
HEADS = 4
HEAD_DIM = 128
DIL_GROUPS = ((128, 1), (512, 4), (2048, 16))
POOL_WIDTH = 1024
POOL_WINDOWS = (2, 4, 8, 16)
POOL_GROUP = 256
POOL_BUF = 15
IN_COLS = 7168
D_FF = 5504
RMS_EPS = 1e-6
ATT_SCALE = HEAD_DIM ** -0.5

COL_TILE = 512
IN_TILES = IN_COLS // COL_TILE
Q_TILE = (3, 6, 9)
U_TILE0 = 12
NAT_TILES = 8
KV_ROWS = 2 * HEADS
SUB_ROWS = 512
IN_STEP_TILES = 2
REGROUP_STRIDE = 4
FF_TILE = 512
FF_TILES = -(-D_FF // FF_TILE)
FF_LAST = D_FF - (FF_TILES - 1) * FF_TILE
FFN_SUB_ROWS = 512
CONV_HALO = 8
POOL_HALO = 16
POOL_BLOCK = 128
PROMPT_ROW_TILE = 1024
MIXER_ROW_TILE = 512
V7X_VMEM_BYTES = 64 * 1024 * 1024
VMEM_LIMIT = V7X_VMEM_BYTES - 8 * 1024 * 1024
VMEM_LIMIT_HIGH = V7X_VMEM_BYTES - 3 * 1024 * 1024

_BF16 = jnp.bfloat16
_F32 = jnp.float32


def _alibi_slopes():
    h = np.arange(1, 3 * HEADS + 1, dtype=np.float32)
    return [float(v) for v in np.power(np.float32(2.0), -8.0 * h / (3 * HEADS))]


SLOPES = _alibi_slopes()


def _rms_norm(x, g):
    return (x * lax.rsqrt(jnp.mean(x * x, axis=-1, keepdims=True) + RMS_EPS)) * g


def _head_cols(h, base=0):
    return slice(base + h * HEAD_DIM, base + (h + 1) * HEAD_DIM)


def _inproj_sample_kernel(x_ref, g_ref, w_ref, proj_ref, kv0_ref, kv1_ref, kv2_ref, wt_ref, h_s):
    j = pl.program_id(1)

    @pl.when(j == 0)
    def _():
        h_s[...] = _rms_norm(x_ref[...], g_ref[...]).astype(_BF16)

    accs = []
    for half in range(IN_STEP_TILES):
        cols = slice(half * COL_TILE, (half + 1) * COL_TILE)
        w = w_ref[:, cols].astype(_BF16)
        wt_ref[half] = w
        acc = jnp.dot(h_s[...], w, preferred_element_type=_F32)
        proj_ref[:, cols] = acc.astype(_BF16)
        accs.append(acc)
    for g, ref in enumerate((kv0_ref, kv1_ref, kv2_ref)):
        for part in range(2):
            tile = Q_TILE[g] + 1 + part

            @pl.when(j == tile // IN_STEP_TILES)
            def _(ref=ref, part=part, tile=tile):
                ref[:, part * ATT_WIDTH:(part + 1) * ATT_WIDTH] = accs[tile % IN_STEP_TILES]


def _inproj_sample(x2d, g, w, layer):
    m = x2d.shape[0]

    def kv_spec(grp):
        return pl.BlockSpec((m, 2 * ATT_WIDTH), lambda i, j: (i, 0))

    step_cols = IN_STEP_TILES * COL_TILE
    return pl.pallas_call(
        _inproj_sample_kernel,
        grid=(1, IN_TILES // IN_STEP_TILES),
        in_specs=[pl.BlockSpec((m, D_MODEL), lambda i, j: (i, 0)),
                  pl.BlockSpec((1, D_MODEL), lambda i, j: (0, 0)),
                  pl.BlockSpec((None, D_MODEL, step_cols), lambda i, j: (layer, 0, j))],
        out_specs=[pl.BlockSpec((m, step_cols), lambda i, j: (i, j)), kv_spec(0), kv_spec(1), kv_spec(2),
                   pl.BlockSpec((IN_STEP_TILES, D_MODEL, COL_TILE), lambda i, j: (j, 0, 0))],
        out_shape=[jax.ShapeDtypeStruct((m, IN_COLS), _BF16)] +
                  [jax.ShapeDtypeStruct((m, 2 * ATT_WIDTH), _F32)] * 3 +
                  [jax.ShapeDtypeStruct((IN_TILES, D_MODEL, COL_TILE), _BF16)],
        scratch_shapes=[pltpu.VMEM((m, D_MODEL), _BF16)],
        compiler_params=pltpu.CompilerParams(dimension_semantics=("parallel", "arbitrary"),
                                             vmem_limit_bytes=VMEM_LIMIT),
        name="inproj_sample",
    )(x2d, g.reshape(1, D_MODEL), w)


def _inproj_prompt_kernel(*refs, tm, aliased):
    n_in = 6 if aliased else 3
    x_ref, g_ref, w_ref = refs[:3]
    nat_ref, g1_ref, g2_ref, kv0_ref, kv1_ref, kv2_ref, h_s, de_s, de2_s = refs[n_in:]
    kv_refs = (kv0_ref, kv1_ref, kv2_ref)
    res_refs = (None, g1_ref, g2_ref)
    j = pl.program_id(1)

    @pl.when(j == 0)
    def _():
        h_s[...] = _rms_norm(x_ref[...], g_ref[...]).astype(_BF16)

    def column_tiles(tiles):
        for c in range(tm // SUB_ROWS):
            r0 = c * SUB_ROWS
            for half, tile in enumerate(tiles):
                acc = jnp.dot(h_s[r0:r0 + SUB_ROWS, :], w_ref[half], preferred_element_type=_F32)
                grp = (tile - Q_TILE[0]) // 3 if Q_TILE[0] <= tile < U_TILE0 else None
                part = None if grp is None or tile == Q_TILE[grp] else tile - Q_TILE[grp] - 1
                dil = 1 if grp is None else DIL_GROUPS[grp][1]
                if dil == 1:
                    nat_ref[r0:r0 + SUB_ROWS, half * COL_TILE:(half + 1) * COL_TILE] = acc.astype(_BF16)
                else:
                    for h in range(HEADS):
                        de_s[c, half, h] = acc[:, _head_cols(h)]
                    n = SUB_ROWS // dil
                    col0 = (tile - Q_TILE[grp]) * COL_TILE
                    if dil == REGROUP_STRIDE:
                        for res in range(dil):
                            for h in range(HEADS):
                                res_refs[grp][res, c * n:(c + 1) * n, _head_cols(h, col0)] = (
                                    de_s[c, half, h, pl.ds(res, n, stride=dil), :].astype(_BF16))
                    else:
                        assert dil == REGROUP_STRIDE ** 2
                        q = SUB_ROWS // REGROUP_STRIDE
                        for h in range(HEADS):
                            for g in range(REGROUP_STRIDE):
                                de2_s[h, g * q:(g + 1) * q, :] = de_s[c, half, h, pl.ds(g, q, stride=REGROUP_STRIDE), :]
                        for g in range(REGROUP_STRIDE):
                            for r in range(REGROUP_STRIDE):
                                for h in range(HEADS):
                                    res_refs[grp][g + REGROUP_STRIDE * r, c * n:(c + 1) * n, _head_cols(h, col0)] = (
                                        de2_s[h, pl.ds(g * q + r, n, stride=REGROUP_STRIDE), :].astype(_BF16))
                if part is not None:
                    keep = min(DIL_GROUPS[grp][0], tm)
                    lo = max(r0, tm - keep)
                    n = r0 + SUB_ROWS - lo
                    if n > 0:
                        for h in range(HEADS):
                            dst = pl.ds((lo - (tm - keep)) * KV_ROWS + part * HEADS + h, n, stride=KV_ROWS)
                            kv_refs[grp][dst, :] = acc[lo - r0:lo - r0 + n, _head_cols(h)]

    for step in range(IN_TILES // IN_STEP_TILES):
        @pl.when(j == step)
        def _(step=step):
            column_tiles(tuple(range(step * IN_STEP_TILES, (step + 1) * IN_STEP_TILES)))


def _inproj_prompt(x2d, g, w, batch, seq, tm, layer, prev_kv):
    m = x2d.shape[0]
    tps = seq // tm
    aliased = prev_kv is not None
    (w0, _), (w1, d1), (w2, d2) = DIL_GROUPS
    assert w2 == seq and w0 <= tm and w1 <= tm and tm % SUB_ROWS == 0

    in_specs = [pl.BlockSpec((tm, D_MODEL), lambda i, j: (i, 0)),
                pl.BlockSpec((1, D_MODEL), lambda i, j: (0, 0)),
                pl.BlockSpec((IN_STEP_TILES, D_MODEL, COL_TILE), lambda i, j: (j, 0, 0))]
    args = [x2d, g.reshape(1, D_MODEL), w]
    if aliased:
        in_specs += [pl.BlockSpec(memory_space=pl.ANY)] * 3
        args += list(prev_kv)

    def res_spec(dil):
        return pl.BlockSpec((None, dil, tm // dil, 3 * ATT_WIDTH), lambda i, j: (i // tps, 0, i % tps, 0))

    assert Q_TILE[1] % IN_STEP_TILES == 0 and U_TILE0 % IN_STEP_TILES == 0 and IN_TILES - U_TILE0 == IN_STEP_TILES
    nat_lo, u_step = Q_TILE[1] // IN_STEP_TILES, U_TILE0 // IN_STEP_TILES
    out_specs = [
        pl.BlockSpec((tm, IN_STEP_TILES * COL_TILE),
                     lambda i, j: (i, jnp.clip(j, 0, nat_lo - 1) + jnp.clip(j - u_step + 1, 0, 1))),
        res_spec(d1), res_spec(d2),
        pl.BlockSpec((None, None, w0 * KV_ROWS, HEAD_DIM), lambda i, j: (layer, i // tps, 0, 0),
                     pipeline_mode=pl.Buffered(1)),
        pl.BlockSpec((None, None, w1 * KV_ROWS, HEAD_DIM), lambda i, j: (layer, i // tps, 0, 0),
                     pipeline_mode=pl.Buffered(1)),
        pl.BlockSpec((None, None, tm * KV_ROWS, HEAD_DIM), lambda i, j: (layer, i // tps, i % tps, 0)),
    ]
    out_shape = [
        jax.ShapeDtypeStruct((m, NAT_TILES * COL_TILE), _BF16),
        jax.ShapeDtypeStruct((batch, d1, seq // d1, 3 * ATT_WIDTH), _BF16),
        jax.ShapeDtypeStruct((batch, d2, seq // d2, 3 * ATT_WIDTH), _BF16),
    ] + [jax.ShapeDtypeStruct((DEPTH, batch, win * KV_ROWS, HEAD_DIM), _F32) for win, _ in DIL_GROUPS]
    return pl.pallas_call(
        functools.partial(_inproj_prompt_kernel, tm=tm, aliased=aliased),
        grid=(m // tm, IN_TILES // IN_STEP_TILES),
        in_specs=in_specs, out_specs=out_specs, out_shape=out_shape,
        scratch_shapes=[pltpu.VMEM((tm, D_MODEL), _BF16),
                        pltpu.VMEM((tm // SUB_ROWS, IN_STEP_TILES, HEADS, SUB_ROWS, HEAD_DIM), _F32),
                        pltpu.VMEM((HEADS, SUB_ROWS, HEAD_DIM), _F32)],
        input_output_aliases={3: 3, 4: 4, 5: 5} if aliased else {},
        compiler_params=pltpu.CompilerParams(dimension_semantics=("arbitrary", "arbitrary"),
                                             vmem_limit_bytes=VMEM_LIMIT_HIGH),
        name="inproj_prompt",
    )(*args)


QB = 128
G0_UNROLL = 3
G1_RES_PER_STEP = 4
G2_RES_PER_STEP = 8


def _attn_prompt_kernel(q0, k0, v0, q1, k1, v1, q2, k2, v2, o_ref, acc_s, m_s, l_s, shuf_s, *, seq):
    step = pl.program_id(1)
    n1 = DIL_GROUPS[1][1]
    n1_steps = n1 // G1_RES_PER_STEP
    n2_steps = DIL_GROUPS[2][1] // G2_RES_PER_STEP

    row0 = lax.broadcasted_iota(jnp.int32, (QB, QB), 0)
    col0 = lax.broadcasted_iota(jnp.int32, (QB, QB), 1)
    dist_first = jnp.where(col0 <= row0, (row0 - col0).astype(_F32), jnp.inf)
    row1 = lax.broadcasted_iota(jnp.int32, (QB, 2 * QB), 0)
    col1 = lax.broadcasted_iota(jnp.int32, (QB, 2 * QB), 1)
    d1 = QB + row1 - col1
    dist_next = jnp.where((d1 >= 0) & (d1 <= QB), d1.astype(_F32), jnp.inf)
    ones = jnp.ones((2 * QB, HEAD_DIM), _BF16)

    seg = seq // n1
    chunk = QB // n1
    assert n1 == 4 and DIL_GROUPS[2][1] == 16

    def blocks(specs):
        tiles = []
        for slot, (qr, kr, vr, grp, res, i, is_first_block) in enumerate(specs):
            dil = DIL_GROUPS[grp][1]
            for h in range(HEADS):
                hs = _head_cols(h)
                slope_step = SLOPES[grp * HEADS + h] * dil
                if is_first_block:
                    q, k, v = qr[0:QB, hs], kr[0:QB, hs], vr[0:QB, hs]
                    bias, one = dist_first * slope_step, ones[0:QB]
                else:
                    qs = pl.ds(pl.multiple_of(i * QB, QB), QB)
                    ks = pl.ds(pl.multiple_of(i * QB - QB, QB), 2 * QB)
                    q, k, v = qr[qs, hs], kr[ks, hs], vr[ks, hs]
                    bias, one = dist_next * slope_step, ones
                tiles.append((q, k, jnp.concatenate([v, one], axis=1), bias, grp, res, i, is_first_block, slot, h))
        scores = [lax.dot_general(t[0], t[1], (((1,), (1,)), ((), ())), preferred_element_type=_F32) for t in tiles]
        probs, maxes = [], []
        for t, s in zip(tiles, scores):
            s = s * ATT_SCALE - t[3]
            m = jnp.max(s, axis=1, keepdims=True)
            probs.append(jnp.exp(s - m).astype(_BF16))
            maxes.append(jnp.broadcast_to(m, (QB, HEAD_DIM)))
        outs = [jnp.dot(p, t[2], preferred_element_type=_F32) for t, p in zip(tiles, probs)]
        def merge(h, rows, acc, m, l):
            m_old, l_old, a_old = m_s[h, rows, :], l_s[h, rows, :], acc_s[h, rows, :]
            m_new = jnp.maximum(m_old, m)
            w_old = jnp.exp(m_old - m_new)
            w_new = jnp.exp(m - m_new)
            acc_s[h, rows, :] = a_old * w_old + acc * w_new
            l_s[h, rows, :] = l_old * w_old + l * w_new
            m_s[h, rows, :] = m_new

        for (_, _, _, _, grp, res, i, is_first_block, slot, h), m, acc_l in zip(tiles, maxes, outs):
            acc, l = acc_l[:, :HEAD_DIM], acc_l[:, HEAD_DIM:]
            if grp == 1:
                rows = pl.ds(pl.multiple_of(res * seg + i * QB, QB), QB)
                acc_s[h, rows, :], m_s[h, rows, :], l_s[h, rows, :] = acc, m, l
            elif grp == 2:
                merge(h, pl.ds((res & (n1 - 1)) * seg + (res >> 2), QB, stride=n1), acc, m, l)
            else:
                tmps = [shuf_s.at[(slot * HEADS + h) * 3 + a] for a in range(3)]
                for tmp, val in zip(tmps, (acc, m, l)):
                    tmp[...] = val
                for r4 in range(n1):
                    dst = r4 * seg + i * chunk
                    dst = pl.ds(dst if is_first_block else pl.multiple_of(dst, chunk), chunk)
                    merge(h, dst, *(tmp[pl.ds(r4, chunk, stride=n1), :] for tmp in tmps))

    @pl.when(step < n1_steps)
    def _():
        blocks([(q1.at[r], k1.at[r], v1.at[r], 1, step * G1_RES_PER_STEP + r, i, i == 0)
                for r in range(G1_RES_PER_STEP) for i in range(seg // QB)])

    @pl.when((step >= n1_steps) & (step < n1_steps + n2_steps))
    def _():
        assert seq // DIL_GROUPS[2][1] == QB
        blocks([(q2.at[r], k2.at[r], v2.at[r], 2, (step - n1_steps) * G2_RES_PER_STEP + r, 0, True)
                for r in range(G2_RES_PER_STEP)])

    @pl.when(step == n1_steps + n2_steps)
    def _():
        n_blk = seq // QB
        assert (n_blk - 1) % G0_UNROLL == 0
        blocks([(q0, k0, v0, 0, 0, 0, True)])

        def g0_body(it, carry):
            blocks([(q0, k0, v0, 0, 0, 1 + it * G0_UNROLL + u, False) for u in range(G0_UNROLL)])
            return carry
        lax.fori_loop(0, (n_blk - 1) // G0_UNROLL, g0_body, 0)

        def body(c, carry):
            for h in range(HEADS):
                tmp = shuf_s.at[h]
                for r4 in range(n1):
                    src = pl.ds(pl.multiple_of(r4 * seg + c * chunk, chunk), chunk)
                    tmp[pl.ds(r4, chunk, stride=n1), :] = acc_s[h, src, :] / l_s[h, src, :]
                o_ref[pl.ds(pl.multiple_of(c * QB, QB), QB), _head_cols(h)] = tmp[...].astype(_BF16)
            return carry
        lax.fori_loop(0, seq // QB, body, 0)


def _attn_prompt(nat, qkv1, qkv2, batch, seq):
    n1, n2 = DIL_GROUPS[1][1], DIL_GROUPS[2][1]
    n1_steps, n2_steps = n1 // G1_RES_PER_STEP, n2 // G2_RES_PER_STEP
    nat3 = nat.reshape(batch, seq, NAT_TILES * COL_TILE)
    in_specs = [pl.BlockSpec((None, seq, COL_TILE), lambda b, s, t=t: (b, 0, Q_TILE[0] + t)) for t in range(3)]
    in_specs += [pl.BlockSpec((None, G1_RES_PER_STEP, seq // n1, COL_TILE),
                              lambda b, s, t=t: (b, jnp.clip(s, 0, n1_steps - 1), 0, t)) for t in range(3)]
    in_specs += [pl.BlockSpec((None, G2_RES_PER_STEP, seq // n2, COL_TILE),
                              lambda b, s, t=t: (b, jnp.clip(s - n1_steps, 0, n2_steps - 1), 0, t)) for t in range(3)]
    args = [nat3] * 3 + [qkv1] * 3 + [qkv2] * 3
    return pl.pallas_call(
        functools.partial(_attn_prompt_kernel, seq=seq),
        grid=(batch, 1 + n1_steps + n2_steps),
        in_specs=in_specs,
        out_specs=pl.BlockSpec((None, seq, ATT_WIDTH), lambda b, s: (b, 0, 0)),
        out_shape=jax.ShapeDtypeStruct((batch, seq, ATT_WIDTH), _BF16),
        scratch_shapes=[pltpu.VMEM((HEADS, seq, HEAD_DIM), _F32)] * 3
                       + [pltpu.VMEM((G0_UNROLL * HEADS * 3, QB, HEAD_DIM), _F32)],
        compiler_params=pltpu.CompilerParams(dimension_semantics=("parallel", "arbitrary"),
                                             vmem_limit_bytes=VMEM_LIMIT),
        name="attn_prompt",
    )(*args)


def _attn_sample_kernel(*refs, tdec, aliased):
    n_in = 10 if aliased else 7
    proj_ref, n0, n1, n2, c0, c1, c2 = refs[:7]
    yb_ref, o0, o1, o2 = refs[n_in:]
    news, caches, outs = (n0, n1, n2), (c0, c1, c2), (o0, o1, o2)
    for h in range(HEADS):
        pieces = []
        for grp, (win, dil) in enumerate(DIL_GROUPS):
            slope = SLOPES[grp * HEADS + h]
            q = proj_ref[:, _head_cols(h, Q_TILE[grp] * COL_TILE)]
            k_new = news[grp][:, _head_cols(h)]
            v_new = news[grp][:, _head_cols(h, ATT_WIDTH)]
            k_old = caches[grp][pl.ds(h, win, stride=KV_ROWS), :]
            v_old = caches[grp][pl.ds(HEADS + h, win, stride=KV_ROWS), :]
            for k, v, n_keys, base in ((k_old, v_old, win, win), (k_new, v_new, tdec, 0)):
                s = lax.dot_general(q, k.astype(_BF16), (((1,), (1,)), ((), ())), preferred_element_type=_F32)
                t = lax.broadcasted_iota(jnp.int32, (tdec, n_keys), 0)
                c = lax.broadcasted_iota(jnp.int32, (tdec, n_keys), 1)
                dist = base + t - c
                ok = (dist >= 0) & (dist <= win) & ((dist & (dil - 1)) == 0)
                s = jnp.where(ok, s * ATT_SCALE - slope * dist.astype(_F32), -jnp.inf)
                pieces.append((s, v.astype(_BF16)))
        m = functools.reduce(jnp.maximum, [jnp.max(s, axis=1, keepdims=True) for s, _ in pieces])
        l = jnp.zeros((tdec, 1), _F32)
        acc = jnp.zeros((tdec, HEAD_DIM), _F32)
        for s, v in pieces:
            p = jnp.exp(s - m)
            l = l + jnp.sum(p, axis=1, keepdims=True)
            acc = acc + jnp.dot(p.astype(_BF16), v, preferred_element_type=_F32)
        yb_ref[:, _head_cols(h)] = (acc / l).astype(_BF16)
    for grp, (win, _) in enumerate(DIL_GROUPS):
        kept = (win - tdec) * KV_ROWS
        outs[grp][0:kept, :] = caches[grp][tdec * KV_ROWS:win * KV_ROWS, :]
        for part in range(2):
            for h in range(HEADS):
                outs[grp][pl.ds(kept + part * HEADS + h, tdec, stride=KV_ROWS), :] = (
                    news[grp][:, _head_cols(h, part * ATT_WIDTH)])


def _attn_sample(proj, kv_new, caches, batch, tdec, layer, prev_out):
    aliased = prev_out is not None
    in_specs = [pl.BlockSpec((None, tdec, IN_COLS), lambda b: (b, 0, 0))]
    in_specs += [pl.BlockSpec((None, tdec, 2 * ATT_WIDTH), lambda b: (b, 0, 0))] * 3
    cache_specs = [pl.BlockSpec((None, None, win * KV_ROWS, HEAD_DIM), lambda b: (layer, b, 0, 0)) for win, _ in DIL_GROUPS]
    in_specs += cache_specs
    args = [proj.reshape(batch, tdec, IN_COLS)] + [k.reshape(batch, tdec, 2 * ATT_WIDTH) for k in kv_new] + list(caches)
    if aliased:
        in_specs += [pl.BlockSpec(memory_space=pl.ANY)] * 3
        args += list(prev_out)
    out_specs = [pl.BlockSpec((None, tdec, ATT_WIDTH), lambda b: (b, 0, 0))] + cache_specs
    out_shape = [jax.ShapeDtypeStruct((batch, tdec, ATT_WIDTH), _BF16)]
    out_shape += [jax.ShapeDtypeStruct((DEPTH, batch, win * KV_ROWS, HEAD_DIM), _F32) for win, _ in DIL_GROUPS]
    return pl.pallas_call(
        functools.partial(_attn_sample_kernel, tdec=tdec, aliased=aliased),
        grid=(batch,),
        in_specs=in_specs, out_specs=out_specs, out_shape=out_shape,
        input_output_aliases={7: 1, 8: 2, 9: 3} if aliased else {},
        compiler_params=pltpu.CompilerParams(dimension_semantics=("arbitrary",), vmem_limit_bytes=VMEM_LIMIT),
        name="attn_sample",
    )(*args)


def _mixer_kernel(*refs, nb, tt, start, from_zero):
    if from_zero:
        pa_ref, pu_ref, yb_ref, x_ref, cw_ref, pw_ref, psc_ref, wo_ref = refs[:8]
    else:
        pa_ref, pu_ref, yb_ref, x_ref, cs0_ref, ps0_ref, cw_ref, pw_ref, psc_ref, wo_ref = refs[:10]
    out_ref, cs_out, ps_out, ext_a, ext_u, yc = refs[-6:]
    ti = pl.program_id(1)
    rows = nb * tt
    pool_halo = POOL_BLOCK if from_zero else POOL_HALO

    @pl.when(ti == 0)
    def _():
        if from_zero:
            ext_a[:, 0:CONV_HALO, :] = jnp.zeros((nb, CONV_HALO, CONV_WIDTH), _F32)
            ext_u[:, 0:pool_halo, :] = jnp.zeros((nb, pool_halo, POOL_WIDTH), _BF16)
        else:
            ext_a[:, 0:CONV_HALO, :] = cs0_ref[...]
            ext_u[:, 0:pool_halo, :] = ps0_ref[...]

    yb = yb_ref[...].astype(_F32).reshape(rows, ATT_WIDTH).astype(_BF16)
    mixed = jnp.dot(yb, wo_ref[CONV_WIDTH:CONV_WIDTH + ATT_WIDTH, :], preferred_element_type=_F32)

    pa = pa_ref[...].astype(_F32)
    xa, gate_b, gate_c = (pa[:, :, k * CONV_WIDTH:(k + 1) * CONV_WIDTH] for k in range(3))
    prod = gate_c * xa
    ext_a[:, CONV_HALO:CONV_HALO + tt, :] = prod
    cw = cw_ref[...]
    cu = (cw[0] * ext_a[:, CONV_HALO - 2:CONV_HALO - 2 + tt, :]
          + cw[1] * ext_a[:, CONV_HALO - 1:CONV_HALO - 1 + tt, :] + cw[2] * prod)
    ya = (gate_b * cu).reshape(rows, CONV_WIDTH).astype(_BF16)
    mixed = mixed + jnp.dot(ya, wo_ref[0:CONV_WIDTH, :], preferred_element_type=_F32)

    ext_u[:, pool_halo:pool_halo + tt, :] = pu_ref[...].astype(ext_u.dtype)
    if from_zero:
        pb = POOL_BLOCK
        row = lax.broadcasted_iota(jnp.int32, (pb, 2 * pb), 0)
        col = lax.broadcasted_iota(jnp.int32, (pb, 2 * pb), 1)
        lag = pb + row - col
        for gi, w in enumerate(POOL_WINDOWS):
            cols = slice(gi * POOL_GROUP, (gi + 1) * POOL_GROUP)
            band = ((lag >= 0) & (lag < w)).astype(_F32).astype(_BF16)
            for r in range(tt // pb):
                tot = jnp.dot(band, ext_u[0, r * pb:(r + 2) * pb, cols], preferred_element_type=_F32)
                pos = start + ti * tt + r * pb + lax.broadcasted_iota(jnp.int32, (pb, 1), 0)
                cnt = jnp.minimum(w, pos + 1).astype(_F32)
                ug = pu_ref[0, r * pb:(r + 1) * pb, cols].astype(_F32)
                diff = (tot / cnt - ug).astype(_BF16)
                z = jnp.dot(diff, pw_ref[gi], preferred_element_type=_F32) * psc_ref[:, cols]
                yc[r * pb:(r + 1) * pb, cols] = z.astype(_BF16)
    else:
        u = pu_ref[...].astype(_F32)
        pos = start + ti * tt + lax.broadcasted_iota(jnp.int32, (1, tt, 1), 1)
        for gi, w in enumerate(POOL_WINDOWS):
            cols = slice(gi * POOL_GROUP, (gi + 1) * POOL_GROUP)
            ug = u[:, :, cols]
            tot = ug
            for k in range(1, w):
                tot = tot + ext_u[:, pool_halo - k:pool_halo - k + tt, cols]
            cnt = jnp.minimum(w, pos + 1).astype(_F32)
            diff = (tot / cnt - ug).reshape(rows, POOL_GROUP).astype(_BF16)
            z = jnp.dot(diff, pw_ref[gi], preferred_element_type=_F32) * psc_ref[:, cols]
            yc[:, cols] = z.astype(_BF16)

    mixed = mixed + jnp.dot(yc[...], wo_ref[CONV_WIDTH + ATT_WIDTH:, :], preferred_element_type=_F32)
    out_ref[...] = x_ref[...] + mixed.reshape(nb, tt, D_MODEL)

    tail_a = ext_a[:, tt:tt + CONV_HALO, :]
    ext_a[:, 0:CONV_HALO, :] = tail_a
    cs_out[...] = tail_a
    ps_out[...] = ext_u[:, pool_halo + tt - POOL_HALO:pool_halo + tt, :].astype(_F32)
    ext_u[:, 0:pool_halo, :] = ext_u[:, tt:tt + pool_halo, :]


def _mixer(proj, yb, x, conv_state, pool_state, conv_w, pool_w, pool_scale, w_out, layer, nb, tt, start, u_tile0):
    batch, seq, _ = x.shape
    proj3 = proj.reshape(batch, seq, proj.shape[-1])
    u_blk = u_tile0 * COL_TILE // POOL_WIDTH
    const2 = lambda b, t: (0, 0)
    from_zero = conv_state is None and pool_state is None
    assert from_zero or (conv_state is not None and pool_state is not None)
    assert not from_zero or (nb == 1 and start == 0 and tt % POOL_BLOCK == 0)
    state_specs = [] if from_zero else [pl.BlockSpec((nb, CONV_HALO, CONV_WIDTH), lambda b, t: (b, 0, 0)),
                                        pl.BlockSpec((nb, POOL_HALO, POOL_WIDTH), lambda b, t: (b, 0, 0))]
    state_args = [] if from_zero else [conv_state, pool_state]
    ext_u = (pltpu.VMEM((nb, tt + POOL_BLOCK, POOL_WIDTH), _BF16) if from_zero
             else pltpu.VMEM((nb, tt + POOL_HALO, POOL_WIDTH), _F32))
    return pl.pallas_call(
        functools.partial(_mixer_kernel, nb=nb, tt=tt, start=start, from_zero=from_zero),
        grid=(batch // nb, seq // tt),
        in_specs=[pl.BlockSpec((nb, tt, 3 * CONV_WIDTH), lambda b, t: (b, t, 0)),
                  pl.BlockSpec((nb, tt, POOL_WIDTH), lambda b, t: (b, t, u_blk)),
                  pl.BlockSpec((nb, tt, ATT_WIDTH), lambda b, t: (b, t, 0)),
                  pl.BlockSpec((nb, tt, D_MODEL), lambda b, t: (b, t, 0))] + state_specs + [
                  pl.BlockSpec((3, CONV_WIDTH), const2),
                  pl.BlockSpec((None, len(POOL_WINDOWS), POOL_GROUP, POOL_GROUP), lambda b, t: (layer, 0, 0, 0)),
                  pl.BlockSpec((1, POOL_WIDTH), const2),
                  pl.BlockSpec((None, D_MODEL, D_MODEL), lambda b, t: (layer, 0, 0))],
        out_specs=[pl.BlockSpec((nb, tt, D_MODEL), lambda b, t: (b, t, 0)),
                   pl.BlockSpec((nb, CONV_HALO, CONV_WIDTH), lambda b, t: (b, 0, 0)),
                   pl.BlockSpec((nb, POOL_HALO, POOL_WIDTH), lambda b, t: (b, 0, 0))],
        out_shape=[jax.ShapeDtypeStruct((batch, seq, D_MODEL), _F32),
                   jax.ShapeDtypeStruct((batch, CONV_HALO, CONV_WIDTH), _F32),
                   jax.ShapeDtypeStruct((batch, POOL_HALO, POOL_WIDTH), _F32)],
        scratch_shapes=[pltpu.VMEM((nb, tt + CONV_HALO, CONV_WIDTH), _F32), ext_u,
                        pltpu.VMEM((nb * tt, POOL_WIDTH), _BF16)],
        compiler_params=pltpu.CompilerParams(dimension_semantics=("parallel", "arbitrary"),
                                             vmem_limit_bytes=VMEM_LIMIT),
        name="mixer",
    )(proj3, proj3, yb, x, *state_args, conv_w, pool_w, pool_scale.reshape(1, POOL_WIDTH), w_out)


def _ffn_sample_kernel(x_ref, n2_ref, wg_ref, wu_ref, cw_ref, wd_ref, st0_ref, fn_ref,
                       out_ref, st_out, wgt_ref, wut_ref, wdt_ref, h_s, ext_s, *, nb, tt, final):
    f = pl.program_id(0)
    rows = nb * tt

    @pl.when(f == 0)
    def _():
        x = x_ref[...]
        h_s[...] = _rms_norm(x, n2_ref[...]).reshape(rows, D_MODEL).astype(_BF16)
        out_ref[...] = x

    def tile(width):
        cols = slice(0, width)
        wg, wu, wd = wg_ref[:, cols].astype(_BF16), wu_ref[:, cols].astype(_BF16), wd_ref[cols, :].astype(_BF16)
        wgt_ref[:, cols], wut_ref[:, cols], wdt_ref[cols, :] = wg, wu, wd
        if width < FF_TILE:
            wgt_ref[:, width:] = jnp.zeros((D_MODEL, FF_TILE - width), _BF16)
            wut_ref[:, width:] = jnp.zeros((D_MODEL, FF_TILE - width), _BF16)
            wdt_ref[width:, :] = jnp.zeros((FF_TILE - width, D_MODEL), _BF16)
        h = h_s[...]
        gate = jnp.dot(h, wg, preferred_element_type=_F32).reshape(nb, tt, width)
        up = jnp.dot(h, wu, preferred_element_type=_F32).reshape(nb, tt, width)
        ext_s[:, 0:CONV_HALO, cols] = st0_ref[:, :, cols]
        ext_s[:, CONV_HALO:CONV_HALO + tt, cols] = gate
        cw = cw_ref[:, cols]
        gc = (cw[0] * ext_s[:, CONV_HALO - 2:CONV_HALO - 2 + tt, cols]
              + cw[1] * ext_s[:, CONV_HALO - 1:CONV_HALO - 1 + tt, cols] + cw[2] * gate)
        st_out[:, :, cols] = ext_s[:, tt:tt + CONV_HALO, cols]
        act = (jax.nn.silu(gc) * up).reshape(rows, width).astype(_BF16)
        out_ref[...] += jnp.dot(act, wd, preferred_element_type=_F32).reshape(nb, tt, D_MODEL)

    @pl.when(f < FF_TILES - 1)
    def _():
        tile(FF_TILE)

    @pl.when(f == FF_TILES - 1)
    def _():
        tile(FF_LAST)
        if final:
            out_ref[...] = _rms_norm(out_ref[...], fn_ref[...])


def _ffn_prompt_kernel(x_ref, n2_ref, wg_ref, wu_ref, cw_ref, wd_ref, fn_ref,
                       out_ref, st_out, h_s, ext_s, carry_s, *, tt, final):
    ti = pl.program_id(1)
    f = pl.program_id(2)

    def tile(first, last):
        if first:
            @pl.when(ti == 0)
            def _():
                carry_s[...] = jnp.zeros(carry_s.shape, _F32)
            x = x_ref[...]
            h_s[...] = _rms_norm(x, n2_ref[...]).astype(_BF16)
            out_ref[...] = x
        ext_s[0:CONV_HALO, :] = carry_s[f]

        wg, wu, wd, cw = wg_ref[...], wu_ref[...], wd_ref[...], cw_ref[f]
        chunks = [slice(r0, r0 + FFN_SUB_ROWS) for r0 in range(0, tt, FFN_SUB_ROWS)]
        for rs in chunks:
            gate = jnp.dot(h_s[rs, :], wg, preferred_element_type=_F32)
            ext_s[CONV_HALO + rs.start:CONV_HALO + rs.stop, :] = gate
        ups = [jnp.dot(h_s[rs, :], wu, preferred_element_type=_F32) for rs in chunks]
        for rs, up in zip(chunks, ups):
            gc = (cw[0] * ext_s[CONV_HALO - 2 + rs.start:CONV_HALO - 2 + rs.stop, :]
                  + cw[1] * ext_s[CONV_HALO - 1 + rs.start:CONV_HALO - 1 + rs.stop, :]
                  + cw[2] * ext_s[CONV_HALO + rs.start:CONV_HALO + rs.stop, :])
            act = (jax.nn.silu(gc) * up).astype(_BF16)
            out_ref[rs, :] += jnp.dot(act, wd, preferred_element_type=_F32)
        tail = ext_s[tt:tt + CONV_HALO, :]
        carry_s[f] = tail
        st_out[f] = tail
        if last:
            out_ref[...] = _rms_norm(out_ref[...], fn_ref[...])

    @pl.when(f == 0)
    def _():
        tile(True, False)

    if final:
        @pl.when((f > 0) & (f < FF_TILES - 1))
        def _():
            tile(False, False)

        @pl.when(f == FF_TILES - 1)
        def _():
            tile(False, True)
    else:
        @pl.when(f > 0)
        def _():
            tile(False, False)


def _ffn_sample(x, norm2, w_gate, w_up, conv_w, w_down, state, final_norm, layer, final):
    nb, tt, _ = x.shape
    vec = lambda f: (0, 0)
    return pl.pallas_call(
        functools.partial(_ffn_sample_kernel, nb=nb, tt=tt, final=final),
        grid=(FF_TILES,),
        in_specs=[pl.BlockSpec((nb, tt, D_MODEL), lambda f: (0, 0, 0)),
                  pl.BlockSpec((1, D_MODEL), vec),
                  pl.BlockSpec((None, D_MODEL, FF_TILE), lambda f: (layer, 0, f)),
                  pl.BlockSpec((None, D_MODEL, FF_TILE), lambda f: (layer, 0, f)),
                  pl.BlockSpec((3, FF_TILE), lambda f: (0, f)),
                  pl.BlockSpec((None, FF_TILE, D_MODEL), lambda f: (layer, f, 0)),
                  pl.BlockSpec((nb, CONV_HALO, FF_TILE), lambda f: (0, 0, f)),
                  pl.BlockSpec((1, D_MODEL), vec)],
        out_specs=[pl.BlockSpec((nb, tt, D_MODEL), lambda f: (0, 0, 0)),
                   pl.BlockSpec((nb, CONV_HALO, FF_TILE), lambda f: (0, 0, f)),
                   pl.BlockSpec((None, D_MODEL, FF_TILE), lambda f: (f, 0, 0)),
                   pl.BlockSpec((None, D_MODEL, FF_TILE), lambda f: (f, 0, 0)),
                   pl.BlockSpec((None, FF_TILE, D_MODEL), lambda f: (f, 0, 0))],
        out_shape=[jax.ShapeDtypeStruct((nb, tt, D_MODEL), _F32),
                   jax.ShapeDtypeStruct((nb, CONV_HALO, D_FF), _F32),
                   jax.ShapeDtypeStruct((FF_TILES, D_MODEL, FF_TILE), _BF16),
                   jax.ShapeDtypeStruct((FF_TILES, D_MODEL, FF_TILE), _BF16),
                   jax.ShapeDtypeStruct((FF_TILES, FF_TILE, D_MODEL), _BF16)],
        scratch_shapes=[pltpu.VMEM((nb * tt, D_MODEL), _BF16),
                        pltpu.VMEM((nb, tt + CONV_HALO, FF_TILE), _F32)],
        compiler_params=pltpu.CompilerParams(dimension_semantics=("arbitrary",), vmem_limit_bytes=VMEM_LIMIT),
        name="ffn_sample",
    )(x, norm2.reshape(1, D_MODEL), w_gate, w_up, conv_w, w_down, state, final_norm.reshape(1, D_MODEL))


def _ffn_prompt(x, norm2, w_gate_t, w_up_t, conv_w, w_down_t, final_norm, tt, final):
    batch, seq, _ = x.shape
    vec = lambda b, t, f: (0, 0)
    return pl.pallas_call(
        functools.partial(_ffn_prompt_kernel, tt=tt, final=final),
        grid=(batch, seq // tt, FF_TILES),
        in_specs=[pl.BlockSpec((None, tt, D_MODEL), lambda b, t, f: (b, t, 0)),
                  pl.BlockSpec((1, D_MODEL), vec),
                  pl.BlockSpec((None, D_MODEL, FF_TILE), lambda b, t, f: (f, 0, 0)),
                  pl.BlockSpec((None, D_MODEL, FF_TILE), lambda b, t, f: (f, 0, 0)),
                  pl.BlockSpec((FF_TILES, 3, FF_TILE), lambda b, t, f: (0, 0, 0)),
                  pl.BlockSpec((None, FF_TILE, D_MODEL), lambda b, t, f: (f, 0, 0)),
                  pl.BlockSpec((1, D_MODEL), vec)],
        out_specs=[pl.BlockSpec((None, tt, D_MODEL), lambda b, t, f: (b, t, 0)),
                   pl.BlockSpec((None, None, FF_TILES, CONV_HALO, FF_TILE), lambda b, t, f: (b, t, 0, 0, 0))],
        out_shape=[jax.ShapeDtypeStruct((batch, seq, D_MODEL), _F32),
                   jax.ShapeDtypeStruct((batch, seq // tt, FF_TILES, CONV_HALO, FF_TILE), _F32)],
        scratch_shapes=[pltpu.VMEM((tt, D_MODEL), _BF16),
                        pltpu.VMEM((tt + CONV_HALO, FF_TILE), _F32),
                        pltpu.VMEM((FF_TILES, CONV_HALO, FF_TILE), _F32)],
        compiler_params=pltpu.CompilerParams(dimension_semantics=("parallel", "arbitrary", "arbitrary"),
                                             vmem_limit_bytes=VMEM_LIMIT),
        name="ffn_prompt",
    )(x, norm2.reshape(1, D_MODEL), w_gate_t, w_up_t, conv_w, w_down_t, final_norm.reshape(1, D_MODEL))


def _pad_rows_front(a, rows):
    return jnp.pad(a, ((0, 0), (rows - a.shape[1], 0), (0, 0)))


def kernel(x_prompt, x_sample, cache_kv_w128, cache_kv_w512, cache_kv_w2048, state_conv_a, state_pool, state_ffn_conv, norm1, w_in, conv_a_w, w_out, pool_w, pool_scale, norm2, w_gate, w_up, ffn_conv_w, w_down, final_norm):
    bp, sp, _ = x_prompt.shape
    bs, ss, _ = x_sample.shape
    caches = tuple(c.reshape(DEPTH, bs, win * KV_ROWS, HEAD_DIM)
                   for c, (win, _) in zip((cache_kv_w128, cache_kv_w512, cache_kv_w2048), DIL_GROUPS))
    tm, tt = PROMPT_ROW_TILE, MIXER_ROW_TILE
    w_out_b, pool_w_b = w_out.astype(_BF16), pool_w.astype(_BF16)
    xp, xs = x_prompt, x_sample
    kv_p = kv_s = None
    small_p, small_s = [], []
    for l in range(DEPTH):
        final = l == DEPTH - 1

        proj, *kv_new, w_in_t = _inproj_sample(xs.reshape(bs * ss, D_MODEL), norm1[l], w_in, l)
        yb, *kv_s = _attn_sample(proj, kv_new, caches, bs, ss, l, kv_s)
        xs, conv_o, pool_o = _mixer(proj, yb, xs, _pad_rows_front(state_conv_a[l], CONV_HALO),
                                    _pad_rows_front(state_pool[l], POOL_HALO), conv_a_w[l], pool_w_b, pool_scale[l],
                                    w_out_b, l, bs, ss, PAST_LEN, U_TILE0)
        xs, ffn_o, w_gate_t, w_up_t, w_down_t = _ffn_sample(
            xs, norm2[l], w_gate, w_up, ffn_conv_w[l], w_down, _pad_rows_front(state_ffn_conv[l], CONV_HALO),
            final_norm, l, final)
        small_s.append((conv_o[:, CONV_HALO - 2:], pool_o[:, POOL_HALO - POOL_BUF:], ffn_o[:, CONV_HALO - 2:]))

        nat, qkv1, qkv2, *kv_p = _inproj_prompt(xp.reshape(bp * sp, D_MODEL), norm1[l], w_in_t, bp, sp, tm, l, kv_p)
        yb = _attn_prompt(nat, qkv1, qkv2, bp, sp)
        xp, conv_o, pool_o = _mixer(nat, yb, xp, None, None, conv_a_w[l], pool_w_b, pool_scale[l], w_out_b,
                                    l, 1, tt, 0, NAT_TILES - 2)
        conv_w_t = jnp.pad(ffn_conv_w[l], ((0, 0), (0, FF_TILES * FF_TILE - D_FF)))
        conv_w_t = conv_w_t.reshape(3, FF_TILES, FF_TILE).transpose(1, 0, 2)
        xp, ffn_o = _ffn_prompt(xp, norm2[l], w_gate_t, w_up_t, conv_w_t, w_down_t, final_norm, tm, final)
        ffn_tail = ffn_o[:, -1, :, CONV_HALO - 2:].transpose(0, 2, 1, 3).reshape(bp, 2, FF_TILES * FF_TILE)
        small_p.append((conv_o[:, CONV_HALO - 2:], pool_o[:, POOL_HALO - POOL_BUF:], ffn_tail[:, :, :D_FF]))

    out = [xp, xs]
    for g, (win, _) in enumerate(DIL_GROUPS):
        out += [kv_p[g].reshape(DEPTH, bp, win, 2, HEADS, HEAD_DIM), kv_s[g].reshape(DEPTH, bs, win, 2, HEADS, HEAD_DIM)]
    for i in range(3):
        out += [jnp.stack([s[i] for s in small_p]), jnp.stack([s[i] for s in small_s])]
    return tuple(out)
```
